```python
import math
import jax, jax.numpy as jnp
from jax import lax
import numpy as np

D_MODEL = 2048
BATCH = 4
SEQ = 2048
DEPTH = 4
DEC_BATCH = 32
DEC_SEQ = 4
PAST_LEN = 16384
PAGE_SIZE = 128

A_HEADS = 8
A_DK = 128
A_DV = 128
A_KEY_WIDTH = A_HEADS * A_DK
A_WIDTH = A_HEADS * A_DV
HGRN_CHUNK = 64
B_HEADS = 16
B_KV_HEADS = 4
B_GROUP = B_HEADS // B_KV_HEADS
B_HD = 64
B_WIDTH = B_HEADS * B_HD
B_KV_WIDTH = B_KV_HEADS * B_HD
WINDOW = 128
ATTN_SCALE = 1.0 / math.sqrt(B_HD)
NEG_LOGIT = -1e30
LB_FLOOR = 1e-20
N_BUCKETS = 32
MAX_DISTANCE = 128
MIX_WIDTH = A_WIDTH + B_WIDTH
SPLIT_POINTS = (A_KEY_WIDTH,
                2 * A_KEY_WIDTH,
                2 * A_KEY_WIDTH + A_WIDTH,
                2 * A_KEY_WIDTH + 2 * A_WIDTH,
                2 * A_KEY_WIDTH + 2 * A_WIDTH + B_WIDTH,
                2 * A_KEY_WIDTH + 2 * A_WIDTH + B_WIDTH + B_KV_WIDTH)
IN_WIDTH = 2 * A_KEY_WIDTH + 2 * A_WIDTH + B_WIDTH + 2 * B_KV_WIDTH
D_FF = ((8 * D_MODEL // 3 + 255) // 256) * 256
RMS_EPS = 1e-6

kernel_name = 'hymba_hgrn2_swa_sink_decoder_step'


def rms_norm(x, g):
    xf = x.astype(jnp.float32)
    y = xf * lax.rsqrt(jnp.mean(xf * xf, axis=-1, keepdims=True) + RMS_EPS)
    return (y * g.astype(jnp.float32)).astype(x.dtype)


def t5_bucket(dist):
    n = np.maximum(dist, 0)
    max_exact = N_BUCKETS // 2
    nf = np.maximum(n, 1).astype(np.float32)
    large = max_exact + (np.log(nf / max_exact) / math.log(MAX_DISTANCE / max_exact)
                         * (N_BUCKETS - max_exact)).astype(np.int32)
    large = np.minimum(large, N_BUCKETS - 1)
    return np.where(n < max_exact, n, large).astype(np.int32)


def relative_bias(table, dist):
    b = table.astype(jnp.float32)[t5_bucket(dist)]
    return jnp.transpose(b, (2, 0, 1)).reshape(B_KV_HEADS, B_GROUP, dist.shape[0], dist.shape[1])


def sink_softmax(s, sink):
    sk = sink.astype(jnp.float32).reshape(B_KV_HEADS, B_GROUP, 1, 1)
    m = jnp.maximum(jnp.max(s, axis=-1, keepdims=True), sk)
    e = jnp.exp(s - m)
    return e / (jnp.sum(e, axis=-1, keepdims=True) + jnp.exp(sk - m))


def hgrn2_scan(q, log_f, k, v, s0):
    bsz, length = q.shape[0], q.shape[1]
    c = min(HGRN_CHUNK, length)
    n = -(-length // c)
    pad = n * c - length

    def blocks(t):
        t = jnp.pad(t, ((0, 0), (0, pad), (0, 0), (0, 0)))
        return t.reshape(bsz, n, c, t.shape[2], t.shape[3]).transpose(1, 0, 3, 2, 4)

    causal = jnp.asarray(np.tril(np.ones((c, c), dtype=bool)))[:, :, None]

    def step(state, inp):
        qc, lfc, kc, vc = inp
        b = jnp.cumsum(lfc, axis=2)
        o_inter = jnp.einsum('bhtk,bhkv->bhtv', qc * jnp.exp(b), state)
        diff = b[:, :, :, None, :] - b[:, :, None, :, :]
        decay = jnp.where(causal, jnp.exp(jnp.where(causal, diff, 0.0)), 0.0)
        a = jnp.einsum('bhtk,bhtsk,bhsk->bhts', qc, decay, kc)
        o_intra = jnp.einsum('bhts,bhsv->bhtv', a, vc)
        b_last = b[:, :, -1:, :]
        new_state = (jnp.exp(b_last[:, :, 0, :])[..., None] * state
                     + jnp.einsum('bhsk,bhsv->bhkv', kc * jnp.exp(b_last - b), vc))
        return new_state, o_inter + o_intra

    s_final, o = lax.scan(step, s0, (blocks(q), blocks(log_f), blocks(k), blocks(v)))
    o = o.transpose(1, 0, 3, 2, 4).reshape(bsz, n * c, q.shape[2], v.shape[3])[:, :length]
    return o, s_final


def mixer_inputs(x, norm_g, w_in, lb, qn_g, kn_g):
    bsz, length, _ = x.shape
    u = rms_norm(x, norm_g) @ w_in
    qa, fa, ia, ga, qb, kb, vb = jnp.split(u, SPLIT_POINTS, axis=-1)
    z = fa.astype(jnp.float32)
    log_lb = jnp.log(jnp.maximum(lb, LB_FLOOR))
    log_f = jnp.logaddexp(log_lb, jnp.log1p(-lb) + jax.nn.log_sigmoid(z))
    ka = (1.0 - lb) * jax.nn.sigmoid(-z)
    qa = jax.nn.silu(qa.astype(jnp.float32))
    a_shape = (bsz, length, A_HEADS, A_DK)
    hgrn = (qa.reshape(a_shape), log_f.reshape(a_shape), ka.reshape(a_shape),
            ia.astype(jnp.float32).reshape(bsz, length, A_HEADS, A_DV), ga)
    qb = rms_norm(qb.reshape(bsz, length, B_HEADS, B_HD), qn_g)
    kb = rms_norm(kb.reshape(bsz, length, B_KV_HEADS, B_HD), kn_g)
    vb = vb.reshape(bsz, length, B_KV_HEADS, B_HD)
    return hgrn, (qb, kb, vb)


def hgrn2_output(o, gn, ga, dtype):
    bsz, length = o.shape[0], o.shape[1]
    gate = jax.nn.silu(ga.astype(jnp.float32)).reshape(bsz, length, A_HEADS, A_DV)
    return (rms_norm(o, gn) * gate).reshape(bsz, length, A_WIDTH).astype(dtype)


def swa_prompt(q, k, v, sink, bias, mask):
    bsz, length = q.shape[0], q.shape[1]
    nb = length // WINDOW
    qb = q.reshape(bsz, nb, WINDOW, B_KV_HEADS, B_GROUP, B_HD)

    def band(t):
        t = t.reshape(bsz, nb, WINDOW, B_KV_HEADS, B_HD)
        prev = jnp.concatenate([jnp.zeros_like(t[:, :1]), t[:, :-1]], axis=1)
        return jnp.concatenate([prev, t], axis=2)

    kb, vb = band(k), band(v)
    s = jnp.einsum('bnqhgd,bnshd->bnhgqs', qb, kb, preferred_element_type=jnp.float32) * ATTN_SCALE + bias
    s = jnp.where(mask[None, :, None, None], s, NEG_LOGIT)
    p = sink_softmax(s, sink)
    o = jnp.einsum('bnhgqs,bnshd->bnqhgd', p.astype(v.dtype), vb)
    return o.reshape(bsz, length, B_WIDTH)


def swa_sample(q, k, v, ck, cv, sink, bias, mask):
    bsz, length = q.shape[0], q.shape[1]
    w = ck.shape[1]
    kall = jnp.concatenate([ck.astype(k.dtype), k], axis=1)
    vall = jnp.concatenate([cv.astype(v.dtype), v], axis=1)
    q5 = q.reshape(bsz, length, B_KV_HEADS, B_GROUP, B_HD)
    s = jnp.einsum('blhgd,bshd->bhgls', q5, kall, preferred_element_type=jnp.float32) * ATTN_SCALE + bias
    s = jnp.where(mask, s, NEG_LOGIT)
    p = sink_softmax(s, sink)
    o = jnp.einsum('bhgls,bshd->blhgd', p.astype(vall.dtype), vall).reshape(bsz, length, B_WIDTH)
    return o, kall[:, -w:], vall[:, -w:]


def swiglu(x, g, wg, wu, wd):
    h = rms_norm(x, g)
    return x + (jax.nn.silu(h @ wg) * (h @ wu)) @ wd


def setup_inputs(seed: int = 0) -> dict:
    key = jax.random.key(seed)
    ks = jax.random.split(key, 20)
    w = min(WINDOW, PAST_LEN)

    def nrm(k, shape, scale):
        return scale * jax.random.normal(k, shape, jnp.float32)

    return {
        'x_prompt': nrm(ks[0], (BATCH, SEQ, D_MODEL), 1.0),
        'x_sample': nrm(ks[1], (DEC_BATCH, DEC_SEQ, D_MODEL), 1.0),
        'cache_k': nrm(ks[2], (DEPTH, DEC_BATCH, w, B_KV_HEADS, B_HD), 1.0),
        'cache_v': nrm(ks[3], (DEPTH, DEC_BATCH, w, B_KV_HEADS, B_HD), 1.0),
        'state_hgrn': nrm(ks[4], (DEPTH, DEC_BATCH, A_HEADS, A_DK, A_DV), 0.5),
        'norm_mix': 1.0 + nrm(ks[5], (DEPTH, D_MODEL), 0.02),
        'w_in': nrm(ks[6], (DEPTH, D_MODEL, IN_WIDTH), D_MODEL ** -0.5),
        'lower_bounds': 1.0 + nrm(ks[7], (DEPTH, A_KEY_WIDTH), 0.1),
        'hgrn_norm': 1.0 + nrm(ks[8], (DEPTH, A_DV), 0.02),
        'q_norm': 1.0 + nrm(ks[9], (DEPTH, B_HD), 0.02),
        'k_norm': 1.0 + nrm(ks[10], (DEPTH, B_HD), 0.02),
        'attn_sinks': nrm(ks[11], (DEPTH, B_HEADS), 0.5),
        'rel_bias_table': nrm(ks[12], (N_BUCKETS, B_HEADS), 0.5),
        'w_out': nrm(ks[13], (DEPTH, MIX_WIDTH, D_MODEL), MIX_WIDTH ** -0.5),
        'norm_ffn': 1.0 + nrm(ks[14], (DEPTH, D_MODEL), 0.02),
        'w_gate': nrm(ks[15], (DEPTH, D_MODEL, D_FF), D_MODEL ** -0.5),
        'w_up': nrm(ks[16], (DEPTH, D_MODEL, D_FF), D_MODEL ** -0.5),
        'w_down': nrm(ks[17], (DEPTH, D_FF, D_MODEL), D_FF ** -0.5),
    }


def reference(x_prompt, x_sample, cache_k, cache_v, state_hgrn, norm_mix, w_in, lower_bounds,
              hgrn_norm, q_norm, k_norm, attn_sinks, rel_bias_table, w_out, norm_ffn,
              w_gate, w_up, w_down):
    p_lb = jax.nn.softmax(lower_bounds.astype(jnp.float32), axis=0)
    lb_all = jnp.cumsum(p_lb, axis=0) - p_lb[0:1]

    bp, seq = x_prompt.shape[0], x_prompt.shape[1]
    nb = seq // WINDOW
    dist_p = (WINDOW + np.arange(WINDOW))[:, None] - np.arange(2 * WINDOW)[None, :]
    key_pos = np.arange(nb)[:, None, None] * WINDOW - WINDOW + np.arange(2 * WINDOW)[None, None, :]
    mask_p = jnp.asarray(((dist_p >= 0) & (dist_p < WINDOW))[None] & (key_pos >= 0))
    bias_p = relative_bias(rel_bias_table, dist_p)
    wp = min(WINDOW, seq)

    bd, ld = x_sample.shape[0], x_sample.shape[1]
    wd = cache_k.shape[2]
    dist_s = (wd + np.arange(ld))[:, None] - np.arange(wd + ld)[None, :]
    mask_s = jnp.asarray((dist_s >= 0) & (dist_s < WINDOW))
    bias_s = relative_bias(rel_bias_table, dist_s)

    xp, xs = x_prompt, x_sample
    pk, pv, ps, sk, sv, ss = [], [], [], [], [], []
    for l in range(DEPTH):
        lb = lb_all[l]
        (qa, lf, ka, va, ga), (qb, kb, vb) = mixer_inputs(xp, norm_mix[l], w_in[l], lb, q_norm[l], k_norm[l])
        oa, st_p = hgrn2_scan(qa, lf, ka, va, jnp.zeros((bp, A_HEADS, A_DK, A_DV), jnp.float32))
        oa = hgrn2_output(oa, hgrn_norm[l], ga, xp.dtype)
        ob = swa_prompt(qb, kb, vb, attn_sinks[l], bias_p, mask_p)
        xp = xp + jnp.concatenate([oa, ob.astype(xp.dtype)], axis=-1) @ w_out[l]
        xp = swiglu(xp, norm_ffn[l], w_gate[l], w_up[l], w_down[l])
        pk.append(kb[:, -wp:])
        pv.append(vb[:, -wp:])
        ps.append(st_p)
        (qa, lf, ka, va, ga), (qb, kb, vb) = mixer_inputs(xs, norm_mix[l], w_in[l], lb, q_norm[l], k_norm[l])
        oa, st_s = hgrn2_scan(qa, lf, ka, va, state_hgrn[l].astype(jnp.float32))
        oa = hgrn2_output(oa, hgrn_norm[l], ga, xs.dtype)
        ob, nk, nv = swa_sample(qb, kb, vb, cache_k[l], cache_v[l], attn_sinks[l], bias_s, mask_s)
        xs = xs + jnp.concatenate([oa, ob.astype(xs.dtype)], axis=-1) @ w_out[l]
        xs = swiglu(xs, norm_ffn[l], w_gate[l], w_up[l], w_down[l])
        sk.append(nk)
        sv.append(nv)
        ss.append(st_s)

    prompt_k, prompt_v, prompt_state = jnp.stack(pk), jnp.stack(pv), jnp.stack(ps)
    sample_k, sample_v, sample_state = jnp.stack(sk), jnp.stack(sv), jnp.stack(ss)
    return (xp, xs, prompt_k, prompt_v, prompt_state, sample_k, sample_v, sample_state)
```

```python
import functools
import math

import jax
import jax.numpy as jnp
import numpy as np
from jax import lax
from jax.experimental import pallas as pl
from jax.experimental.pallas import tpu as pltpu

F32 = jnp.float32
BF16 = jnp.bfloat16

LANES = 128
SUBLANES = 8
VMEM_BYTES_V7X = 64 * 1024 * 1024
VMEM_CAP = VMEM_BYTES_V7X - 8 * 1024 * 1024

A_HEADS = 8
A_DK = 128
A_DV = 128
A_KEY_WIDTH = A_HEADS * A_DK
A_WIDTH = A_HEADS * A_DV
B_HEADS = 16
B_KV_HEADS = 4
B_HD = 64
B_WIDTH = B_HEADS * B_HD
B_KV_WIDTH = B_KV_HEADS * B_HD
WINDOW = 128
ATTN_SCALE = 1.0 / math.sqrt(B_HD)
NEG_LOGIT = -1e30
LB_FLOOR = 1e-20
N_BUCKETS = 32
MAX_DISTANCE = 128
RMS_EPS = 1e-6

Q_A, F_A, I_A, G_A = 0, A_KEY_WIDTH, 2 * A_KEY_WIDTH, 2 * A_KEY_WIDTH + A_WIDTH
Q_B = 2 * A_KEY_WIDTH + 2 * A_WIDTH
K_B = Q_B + B_WIDTH
V_B = K_B + B_KV_WIDTH

HGRN_CHUNK = 128
HGRN_HEADS_PER_STEP = 2
SAMPLE_ROWS = SUBLANES
KEY_SLOTS = 2 * WINDOW


def _vmem_limit(nbytes):
    return int(min(VMEM_CAP, nbytes * 5 // 4 + (4 << 20)))


def _params(sem, nbytes):
    return pltpu.CompilerParams(dimension_semantics=sem, vmem_limit_bytes=_vmem_limit(nbytes))


def _sig_pair(z):
    t = jnp.exp(-jnp.abs(z))
    r = 1.0 / (1.0 + t)
    tr = t * r
    pos = z >= 0
    return jnp.where(pos, r, tr), jnp.where(pos, tr, r)


def _silu(x):
    return x * _sig_pair(x)[0]


def _rms_rows(x, g):
    ms = jnp.mean(x * x, axis=-1, keepdims=True)
    return x * lax.rsqrt(ms + RMS_EPS) * g


def _split_bf16(x):
    hi = x.astype(BF16)
    lo = (x - hi.astype(F32)).astype(BF16)
    return hi, lo


def _dot(a, b):
    return jnp.dot(a, b, preferred_element_type=F32)


def _dot_nt(a, b):
    return lax.dot_general(a, b, (((1,), (1,)), ((), ())), preferred_element_type=F32)


def _lb_kernel(lb_ref, lbf_ref, oml_ref):
    depth = lb_ref.shape[0]
    rows = [lb_ref[i:i + 1, :] for i in range(depth)]
    m = functools.reduce(jnp.maximum, rows)
    e = [jnp.exp(r - m) for r in rows]
    s = functools.reduce(lambda a, b: a + b, e)
    p = [ei / s for ei in e]
    cum = p[0]
    for i in range(depth):
        if i > 0:
            cum = cum + p[i]
        lb = cum - p[0]
        lbf_ref[i:i + 1, :] = jnp.maximum(lb, LB_FLOOR)
        oml_ref[i:i + 1, :] = 1.0 - lb


def _lower_bounds(lower_bounds):
    shp = jax.ShapeDtypeStruct(lower_bounds.shape, F32)
    return pl.pallas_call(_lb_kernel, out_shape=(shp, shp), name="lb_prep")(lower_bounds.astype(F32))


def _norm_matmul_kernel(x_ref, g_ref, w_ref, o_ref, h_ref):
    @pl.when(pl.program_id(1) == 0)
    def _():
        h_ref[...] = _rms_rows(x_ref[...], g_ref[...]).astype(BF16)

    o_ref[...] = _dot(h_ref[...], w_ref[...])


def _norm_matmul(x, g, w, tm, tn):
    m, d = x.shape
    n = w.shape[1]
    nbytes = 2 * tm * d * 4 + tm * d * 2 + 2 * d * tn * 2 + 2 * tm * tn * 4
    return pl.pallas_call(
        _norm_matmul_kernel,
        grid=(m // tm, n // tn),
        in_specs=[pl.BlockSpec((tm, d), lambda i, j: (i, 0)),
                  pl.BlockSpec((1, d), lambda i, j: (0, 0)),
                  pl.BlockSpec((d, tn), lambda i, j: (0, j))],
        out_specs=pl.BlockSpec((tm, tn), lambda i, j: (i, j)),
        out_shape=jax.ShapeDtypeStruct((m, n), F32),
        scratch_shapes=[pltpu.VMEM((tm, d), BF16)],
        compiler_params=_params(("parallel", "arbitrary"), nbytes),
        name="norm_matmul",
    )(x, g.reshape(1, d), w)


def _outproj_kernel(x_ref, oa_ref, ob_ref, w_ref, o_ref):
    ka = oa_ref.shape[1]
    acc = _dot(oa_ref[...], w_ref[0:ka, :])
    acc = acc + _dot(ob_ref[...], w_ref[ka:, :])
    o_ref[...] = x_ref[...] + acc


def _outproj(x, oa, ob, w, tm):
    m, d = x.shape
    ka, kb = oa.shape[1], ob.shape[1]
    nbytes = 4 * tm * d * 4 + 2 * tm * (ka + kb) * 2 + 2 * (ka + kb) * d * 2
    return pl.pallas_call(
        _outproj_kernel,
        grid=(m // tm,),
        in_specs=[pl.BlockSpec((tm, d), lambda i: (i, 0)),
                  pl.BlockSpec((tm, ka), lambda i: (i, 0)),
                  pl.BlockSpec((tm, kb), lambda i: (i, 0)),
                  pl.BlockSpec((ka + kb, d), lambda i: (0, 0))],
        out_specs=pl.BlockSpec((tm, d), lambda i: (i, 0)),
        out_shape=jax.ShapeDtypeStruct((m, d), F32),
        compiler_params=_params(("parallel",), nbytes),
        name="outproj",
    )(x, oa, ob, w)


def _ffn_kernel(x_ref, g_ref, wg_ref, wu_ref, wd_ref, o_ref, h_ref):
    @pl.when(pl.program_id(1) == 0)
    def _():
        x = x_ref[...]
        h_ref[...] = _rms_rows(x, g_ref[...]).astype(BF16)
        o_ref[...] = x

    h = h_ref[...]
    act = (_silu(_dot(h, wg_ref[...])) * _dot(h, wu_ref[...])).astype(BF16)
    o_ref[...] += _dot(act, wd_ref[...])


def _ffn(x, g, wg, wu, wd, tm, tf):
    m, d = x.shape
    f = wg.shape[1]
    nbytes = 4 * tm * d * 4 + tm * d * 2 + 2 * 3 * d * tf * 2 + 3 * tm * tf * 4
    return pl.pallas_call(
        _ffn_kernel,
        grid=(m // tm, f // tf),
        in_specs=[pl.BlockSpec((tm, d), lambda i, j: (i, 0)),
                  pl.BlockSpec((1, d), lambda i, j: (0, 0)),
                  pl.BlockSpec((d, tf), lambda i, j: (0, j)),
                  pl.BlockSpec((d, tf), lambda i, j: (0, j)),
                  pl.BlockSpec((tf, d), lambda i, j: (j, 0))],
        out_specs=pl.BlockSpec((tm, d), lambda i, j: (i, 0)),
        out_shape=jax.ShapeDtypeStruct((m, d), F32),
        scratch_shapes=[pltpu.VMEM((tm, d), BF16)],
        compiler_params=_params(("parallel", "arbitrary"), nbytes),
        name="ffn",
    )(x, g.reshape(1, d), wg, wu, wd)


def _hgrn_static(c):
    t = np.arange(c)
    blocks = [(t[None, :] <= t[:, None]), (t[None, :] > t[:, None])]
    masks = []
    h = c // 2
    while h >= 1:
        grp = t // (2 * h)
        mid = grp * 2 * h + h
        upper = (t % (2 * h)) >= h
        col = t[None, :]
        p_up = (col >= mid[:, None]) & (col <= t[:, None])
        p_lo = (col > t[:, None]) & (col < mid[:, None])
        blocks.append(np.where(upper[:, None], p_up, p_lo))
        masks.append((grp[:, None] == grp[None, :]) & upper[:, None] & ~upper[None, :])
        h //= 2
    masks.append(np.eye(c, dtype=bool))
    p_all = np.concatenate(blocks, axis=0).astype(np.float32)
    p2 = np.concatenate([p_all, p_all], axis=1)
    return p2, np.stack(masks).astype(np.float32)


def _hgrn_prompt_kernel(q_ref, f_ref, i_ref, g_ref, lbf_ref, oml_ref, gn_ref, p2_ref, m_ref,
                        oa_ref, st_ref, s_scr, *, chunk, heads, n_levels):
    c_idx = pl.program_id(2)

    @pl.when(c_idx == 0)
    def _():
        s_scr[...] = jnp.zeros_like(s_scr)

    sz, snz = _sig_pair(f_ref[0])
    oml = oml_ref[...]
    lf = jnp.log(lbf_ref[...] + oml * sz)
    ka = oml * snz
    hi, lo = _split_bf16(lf)
    expo = _dot(p2_ref[...], jnp.concatenate([hi, lo], axis=0))
    q = _silu(q_ref[0])
    v = i_ref[0]
    gate = _silu(g_ref[0])
    gn = gn_ref[...]

    for hh in range(heads):
        sl = slice(hh * A_DK, (hh + 1) * A_DK)
        qh, kh, vh = q[:, sl], ka[:, sl], v[:, sl]
        b = expo[0:chunk, sl]
        st = s_scr[hh]
        o = _dot_nt((qh * jnp.exp(b)).astype(BF16), st.astype(BF16))
        qs, ks = [], []
        for li in range(n_levels):
            e = jnp.exp(expo[(2 + li) * chunk:(3 + li) * chunk, sl])
            qs.append((qh * e).astype(BF16))
            ks.append((kh * e).astype(BF16))
        qs.append(qh.astype(BF16))
        ks.append(kh.astype(BF16))
        a = jnp.zeros((chunk, chunk), F32)
        for pi in range(0, n_levels + 1, 2):
            gq = jnp.concatenate([qs[pi], qs[pi + 1]], axis=0)
            gk = jnp.concatenate([ks[pi], ks[pi + 1]], axis=0)
            gm = _dot_nt(gq, gk)
            a = a + gm[0:chunk, 0:chunk] * m_ref[pi] + gm[chunk:, chunk:] * m_ref[pi + 1]
        o = o + _dot(a.astype(BF16), vh.astype(BF16))
        k_tail = (kh * jnp.exp(expo[chunk:2 * chunk, sl])).astype(BF16)
        s_scr[hh] = st * jnp.exp(b[chunk - 1:chunk, :]) + _dot(vh.T.astype(BF16), k_tail)
        oa_ref[0, :, sl] = (_rms_rows(o, gn) * gate[:, sl]).astype(BF16)

    @pl.when(c_idx == pl.num_programs(2) - 1)
    def _():
        for hh in range(heads):
            st_ref[0, hh] = s_scr[hh].T


def _hgrn_prompt(u3, lbf, oml, gn):
    bsz, length, _ = u3.shape
    chunk, heads = HGRN_CHUNK, HGRN_HEADS_PER_STEP
    cols = heads * A_DK
    n_levels = int(math.log2(chunk))
    assert (n_levels + 1) % 2 == 0 and length % chunk == 0 and A_HEADS % heads == 0
    p2, masks = _hgrn_static(chunk)
    p2 = jnp.asarray(p2, BF16)
    masks = jnp.asarray(masks, F32)

    def col_spec(offset):
        base = offset // cols
        return pl.BlockSpec((1, chunk, cols), lambda b, h, c: (b, c, base + h))

    vec_spec = pl.BlockSpec((1, cols), lambda b, h, c: (0, h))
    nbytes = (2 * 5 * chunk * cols * 4 + 2 * (p2.size * 2 + masks.size * 4)
              + 6 * p2.shape[0] * cols * 4)
    return pl.pallas_call(
        functools.partial(_hgrn_prompt_kernel, chunk=chunk, heads=heads, n_levels=n_levels),
        grid=(bsz, A_HEADS // heads, length // chunk),
        in_specs=[col_spec(Q_A), col_spec(F_A), col_spec(I_A), col_spec(G_A),
                  vec_spec, vec_spec,
                  pl.BlockSpec((1, A_DV), lambda b, h, c: (0, 0)),
                  pl.BlockSpec(p2.shape, lambda b, h, c: (0, 0)),
                  pl.BlockSpec(masks.shape, lambda b, h, c: (0, 0, 0))],
        out_specs=(pl.BlockSpec((1, chunk, cols), lambda b, h, c: (b, c, h)),
                   pl.BlockSpec((1, heads, A_DK, A_DV), lambda b, h, c: (b, h, 0, 0))),
        out_shape=(jax.ShapeDtypeStruct((bsz, length, A_WIDTH), BF16),
                   jax.ShapeDtypeStruct((bsz, A_HEADS, A_DK, A_DV), F32)),
        scratch_shapes=[pltpu.VMEM((heads, A_DV, A_DK), F32)],
        compiler_params=_params(("parallel", "parallel", "arbitrary"), nbytes),
        name="hgrn_prompt",
    )(u3, u3, u3, u3, lbf, oml, gn.reshape(1, A_DV), p2, masks)


def _hgrn_sample_kernel(u_ref, lbf_ref, oml_ref, gn_ref, s_ref, oa_ref, so_ref, *, steps):
    u = u_ref[0]
    q = _silu(u[:, Q_A:Q_A + A_KEY_WIDTH])
    sz, snz = _sig_pair(u[:, F_A:F_A + A_KEY_WIDTH])
    oml = oml_ref[...]
    f = lbf_ref[...] + oml * sz
    ka = oml * snz
    v = u[:, I_A:I_A + A_WIDTH]
    gate = _silu(u[:, G_A:G_A + A_WIDTH])
    gn = gn_ref[...]
    rows = u.shape[0]
    row_id = lax.broadcasted_iota(jnp.int32, (rows, A_DV), 0)
    pad = jnp.zeros((A_DK - 3 * rows, A_DK), F32)

    for hh in range(A_HEADS):
        sl = slice(hh * A_DK, (hh + 1) * A_DK)
        xt = jnp.concatenate([f[:, sl], ka[:, sl], q[:, sl], pad], axis=0).T
        s = s_ref[0, hh]
        o = jnp.zeros((rows, A_DV), F32)
        for t in range(steps):
            s = s * xt[:, t:t + 1] + xt[:, rows + t:rows + t + 1] * v[t:t + 1, sl]
            o_t = jnp.sum(s * xt[:, 2 * rows + t:2 * rows + t + 1], axis=0, keepdims=True)
            o = jnp.where(row_id == t, o_t, o)
        so_ref[0, hh] = s
        oa_ref[0, :, sl] = (_rms_rows(o, gn) * gate[:, sl]).astype(BF16)


def _hgrn_sample(u3, lbf, oml, gn, state, steps):
    bsz, rows, _ = u3.shape
    width = G_A + A_WIDTH
    nbytes = 2 * rows * width * 4 + 4 * A_HEADS * A_DK * A_DV * 4
    return pl.pallas_call(
        functools.partial(_hgrn_sample_kernel, steps=steps),
        grid=(bsz,),
        in_specs=[pl.BlockSpec((1, rows, width), lambda b: (b, 0, 0)),
                  pl.BlockSpec((1, A_KEY_WIDTH), lambda b: (0, 0)),
                  pl.BlockSpec((1, A_KEY_WIDTH), lambda b: (0, 0)),
                  pl.BlockSpec((1, A_DV), lambda b: (0, 0)),
                  pl.BlockSpec((1, A_HEADS, A_DK, A_DV), lambda b: (b, 0, 0, 0))],
        out_specs=(pl.BlockSpec((1, rows, A_WIDTH), lambda b: (b, 0, 0)),
                   pl.BlockSpec((1, A_HEADS, A_DK, A_DV), lambda b: (b, 0, 0, 0))),
        out_shape=(jax.ShapeDtypeStruct((bsz, rows, A_WIDTH), BF16),
                   jax.ShapeDtypeStruct(state.shape, F32)),
        compiler_params=_params(("parallel",), nbytes),
        name="hgrn_sample",
    )(u3, lbf, oml, gn.reshape(1, A_DV), state)


def _t5_bucket(dist):
    n = np.maximum(dist, 0)
    max_exact = N_BUCKETS // 2
    nf = np.maximum(n, 1).astype(np.float32)
    large = max_exact + (np.log(nf / max_exact) / math.log(MAX_DISTANCE / max_exact)
                         * (N_BUCKETS - max_exact)).astype(np.int32)
    large = np.minimum(large, N_BUCKETS - 1)
    return np.where(n < max_exact, n, large).astype(np.int32)


def _bias_layout(table, dist):
    rows = dist.shape[0]
    b = table.astype(F32)[_t5_bucket(dist)]
    b = jnp.transpose(b, (2, 0, 1)).reshape(B_KV_HEADS, 2, 2, rows, KEY_SLOTS)
    return jnp.transpose(b, (0, 1, 3, 2, 4)).reshape(B_KV_HEADS, 2 * rows, 2 * KEY_SLOTS)


def _tile2(a):
    return np.tile(a, (2, 2))


def _block_diag_ones():
    r = lax.broadcasted_iota(jnp.int32, (LANES, LANES), 0) // B_HD
    c = lax.broadcasted_iota(jnp.int32, (LANES, LANES), 1) // B_HD
    return (r == c).astype(BF16)


def _head_norm(x, g2, bd):
    hi, lo = _split_bf16(x * x)
    ss = _dot(hi, bd) + _dot(lo, bd)
    return x * lax.rsqrt(ss * (1.0 / B_HD) + RMS_EPS) * g2


def _place(tile, half, lo_mask):
    rolled = pltpu.roll(tile, B_HD, axis=1)
    zero = jnp.zeros_like(tile)
    if half == 0:
        return jnp.where(lo_mask, tile, zero), jnp.where(lo_mask, zero, rolled)
    return jnp.where(lo_mask, rolled, zero), jnp.where(lo_mask, zero, tile)


def _swa_core(qn_tiles, k_tiles, v_tiles, bias_ref, valid, sink_ref, ob_ref):
    rows = qn_tiles[0].shape[0]
    lo_mask = lax.broadcasted_iota(jnp.int32, k_tiles[0].shape, 1) < B_HD
    for j in range(B_KV_HEADS):
        t, half = j // 2, j % 2
        k_lo, k_hi = _place(k_tiles[t], half, lo_mask)
        v_lo, v_hi = _place(v_tiles[t], half, lo_mask)
        kk = jnp.concatenate([k_lo, k_hi], axis=0).astype(BF16)
        vv = jnp.concatenate([v_lo, v_hi], axis=0).astype(BF16)
        qq = jnp.concatenate([qn_tiles[2 * j], qn_tiles[2 * j + 1]], axis=0).astype(BF16)
        s = _dot_nt(qq, kk) * ATTN_SCALE + bias_ref[j]
        s = jnp.where(valid, s, NEG_LOGIT)
        p_rows = []
        for r in range(2):
            p_cols = []
            for c in range(2):
                sk = sink_ref[4 * j + 2 * r + c]
                sb = s[r * rows:(r + 1) * rows, c * KEY_SLOTS:(c + 1) * KEY_SLOTS]
                m = jnp.maximum(jnp.max(sb, axis=-1, keepdims=True), sk)
                e = jnp.exp(sb - m)
                p_cols.append(e / (jnp.sum(e, axis=-1, keepdims=True) + jnp.exp(sk - m)))
            p_rows.append(jnp.concatenate(p_cols, axis=1))
        p = jnp.concatenate(p_rows, axis=0).astype(BF16)
        o = _dot(p, vv)
        ob_ref[0, :, (2 * j) * LANES:(2 * j + 1) * LANES] = o[0:rows].astype(BF16)
        ob_ref[0, :, (2 * j + 1) * LANES:(2 * j + 2) * LANES] = o[rows:].astype(BF16)


def _swa_prompt_kernel(sink_ref, q_ref, kc_ref, kp_ref, vc_ref, vp_ref, qg_ref, kg_ref, bias_ref, code_ref,
                       ob_ref, ko_ref, vo_ref):
    n = pl.program_id(1)
    bd = _block_diag_ones()
    k_all = jnp.concatenate([kp_ref[0], kc_ref[0]], axis=0)
    v_all = jnp.concatenate([vp_ref[0], vc_ref[0]], axis=0)
    kg = kg_ref[...]
    k_tiles = [_head_norm(k_all[:, t * LANES:(t + 1) * LANES], kg, bd) for t in range(2)]
    v_tiles = [v_all[:, t * LANES:(t + 1) * LANES] for t in range(2)]
    q = q_ref[0]
    qg = qg_ref[...]
    qn_tiles = [_head_norm(q[:, t * LANES:(t + 1) * LANES], qg, bd) for t in range(B_WIDTH // LANES)]
    code = code_ref[...]
    valid = (code >= 1) & (code <= jnp.where(n > 0, 2, 1))
    _swa_core(qn_tiles, k_tiles, v_tiles, bias_ref, valid, sink_ref, ob_ref)

    @pl.when(n == pl.num_programs(1) - 1)
    def _():
        ko_ref[0] = jnp.concatenate([kt[WINDOW:] for kt in k_tiles], axis=1)
        vo_ref[0] = vc_ref[0]


def _swa_prompt(u3, sinks, qg, kg, table):
    bsz, length, _ = u3.shape
    nb = length // WINDOW
    i = np.arange(WINDOW)[:, None]
    j = np.arange(KEY_SLOTS)[None, :]
    dist = WINDOW + i - j
    code = np.where((j >= WINDOW) & (dist >= 0), 1, np.where((j < WINDOW) & (dist < WINDOW), 2, 0))
    code = jnp.asarray(_tile2(code), jnp.int32)
    bias = _bias_layout(table, dist)
    qb, kb, vb = Q_B // B_WIDTH, K_B // B_KV_WIDTH, V_B // B_KV_WIDTH
    assert Q_B % B_WIDTH == 0 and K_B % B_KV_WIDTH == 0 and V_B % B_KV_WIDTH == 0
    kv_spec = lambda col, prev: pl.BlockSpec(
        (1, WINDOW, B_KV_WIDTH),
        (lambda b, n: (b, jnp.maximum(n - 1, 0), col)) if prev else (lambda b, n: (b, n, col)))
    g_spec = pl.BlockSpec((1, LANES), lambda b, n: (0, 0))
    nbytes = (2 * WINDOW * (B_WIDTH + 4 * B_KV_WIDTH) * 4 + 2 * (bias.size + code.size) * 4
              + 2 * WINDOW * B_WIDTH * 2 + 24 * WINDOW * 2 * KEY_SLOTS * 4)
    return pl.pallas_call(
        _swa_prompt_kernel,
        grid=(bsz, nb),
        in_specs=[pl.BlockSpec(memory_space=pltpu.SMEM),
                  pl.BlockSpec((1, WINDOW, B_WIDTH), lambda b, n: (b, n, qb)),
                  kv_spec(kb, False), kv_spec(kb, True), kv_spec(vb, False), kv_spec(vb, True),
                  g_spec, g_spec,
                  pl.BlockSpec(bias.shape, lambda b, n: (0, 0, 0)),
                  pl.BlockSpec(code.shape, lambda b, n: (0, 0))],
        out_specs=(pl.BlockSpec((1, WINDOW, B_WIDTH), lambda b, n: (b, n, 0)),
                   pl.BlockSpec((1, WINDOW, B_KV_WIDTH), lambda b, n: (b, 0, 0)),
                   pl.BlockSpec((1, WINDOW, B_KV_WIDTH), lambda b, n: (b, 0, 0))),
        out_shape=(jax.ShapeDtypeStruct((bsz, length, B_WIDTH), BF16),
                   jax.ShapeDtypeStruct((bsz, WINDOW, B_KV_WIDTH), F32),
                   jax.ShapeDtypeStruct((bsz, WINDOW, B_KV_WIDTH), F32)),
        compiler_params=_params(("parallel", "arbitrary"), nbytes),
        name="swa_prompt",
    )(sinks, u3, u3, u3, u3, u3, qg, kg, bias, code)


def _swa_sample_kernel(sink_ref, q_ref, kv_ref, ck_ref, cv_ref, qg_ref, kg_ref, bias_ref, code_ref,
                       ob_ref, ko_ref, vo_ref, *, steps):
    bd = _block_diag_ones()
    rows = q_ref.shape[1]
    kv = kv_ref[0]
    kg = kg_ref[...]
    k_new = [_head_norm(kv[:, t * LANES:(t + 1) * LANES], kg, bd) for t in range(2)]
    v_new = [kv[:, B_KV_WIDTH + t * LANES:B_KV_WIDTH + (t + 1) * LANES] for t in range(2)]
    ck, cv = ck_ref[0], cv_ref[0]
    pad = jnp.zeros((KEY_SLOTS - WINDOW - rows, LANES), F32)
    k_tiles = [jnp.concatenate([ck[:, t * LANES:(t + 1) * LANES], k_new[t], pad], axis=0) for t in range(2)]
    v_tiles = [jnp.concatenate([cv[:, t * LANES:(t + 1) * LANES], v_new[t], pad], axis=0) for t in range(2)]
    q = q_ref[0]
    qg = qg_ref[...]
    qn_tiles = [_head_norm(q[:, t * LANES:(t + 1) * LANES], qg, bd) for t in range(B_WIDTH // LANES)]
    valid = code_ref[...] >= 1
    _swa_core(qn_tiles, k_tiles, v_tiles, bias_ref, valid, sink_ref, ob_ref)
    ko_ref[0, 0:WINDOW - steps, :] = ck_ref[0, steps:WINDOW, :]
    vo_ref[0, 0:WINDOW - steps, :] = cv_ref[0, steps:WINDOW, :]
    ko_ref[0, WINDOW - steps:WINDOW, :] = jnp.concatenate([kt[0:steps] for kt in k_new], axis=1)
    vo_ref[0, WINDOW - steps:WINDOW, :] = kv[0:steps, B_KV_WIDTH:]


def _swa_sample(u3, cache_k, cache_v, sinks, qg, kg, table, steps):
    bsz, rows, _ = u3.shape
    w = cache_k.shape[1]
    assert w == WINDOW
    i = np.arange(rows)[:, None]
    j = np.arange(KEY_SLOTS)[None, :]
    dist = w + i - j
    code = np.where((dist >= 0) & (dist < WINDOW) & (j < w + steps), 1, 0)
    code = jnp.asarray(_tile2(code), jnp.int32)
    bias = _bias_layout(table, dist)
    qb, kvb = Q_B // B_WIDTH, K_B // (2 * B_KV_WIDTH)
    assert K_B % (2 * B_KV_WIDTH) == 0
    c_spec = pl.BlockSpec((1, w, B_KV_WIDTH), lambda b: (b, 0, 0))
    g_spec = pl.BlockSpec((1, LANES), lambda b: (0, 0))
    nbytes = 8 * w * B_KV_WIDTH * 4 + 2 * (bias.size + code.size) * 4 + 16 * KEY_SLOTS * LANES * 4
    return pl.pallas_call(
        functools.partial(_swa_sample_kernel, steps=steps),
        grid=(bsz,),
        in_specs=[pl.BlockSpec(memory_space=pltpu.SMEM),
                  pl.BlockSpec((1, rows, B_WIDTH), lambda b: (b, 0, qb)),
                  pl.BlockSpec((1, rows, 2 * B_KV_WIDTH), lambda b: (b, 0, kvb)),
                  c_spec, c_spec, g_spec, g_spec,
                  pl.BlockSpec(bias.shape, lambda b: (0, 0, 0)),
                  pl.BlockSpec(code.shape, lambda b: (0, 0))],
        out_specs=(pl.BlockSpec((1, rows, B_WIDTH), lambda b: (b, 0, 0)), c_spec, c_spec),
        out_shape=(jax.ShapeDtypeStruct((bsz, rows, B_WIDTH), BF16),
                   jax.ShapeDtypeStruct((bsz, w, B_KV_WIDTH), F32),
                   jax.ShapeDtypeStruct((bsz, w, B_KV_WIDTH), F32)),
        compiler_params=_params(("parallel",), nbytes),
        name="swa_sample",
    )(sinks, u3, u3, cache_k, cache_v, qg, kg, bias, code)


def kernel(x_prompt, x_sample, cache_k, cache_v, state_hgrn, norm_mix, w_in, lower_bounds, hgrn_norm,
           q_norm, k_norm, attn_sinks, rel_bias_table, w_out, norm_ffn, w_gate, w_up, w_down):
    depth = w_in.shape[0]
    bp, seq, d = x_prompt.shape
    bd, ld, _ = x_sample.shape
    w = cache_k.shape[2]
    in_width = w_in.shape[2]

    w_in_b, w_out_b = w_in.astype(BF16), w_out.astype(BF16)
    w_gate_b, w_up_b, w_down_b = w_gate.astype(BF16), w_up.astype(BF16), w_down.astype(BF16)
    lbf, oml = _lower_bounds(lower_bounds)
    sinks = attn_sinks.astype(F32)
    qg2 = jnp.tile(q_norm.astype(F32), (1, LANES // B_HD))
    kg2 = jnp.tile(k_norm.astype(F32), (1, LANES // B_HD))
    ck = cache_k.astype(F32).reshape(depth, bd, w, B_KV_WIDTH)
    cv = cache_v.astype(F32).reshape(depth, bd, w, B_KV_WIDTH)

    xp = x_prompt.reshape(bp * seq, d)
    xs = jnp.pad(x_sample, ((0, 0), (0, SAMPLE_ROWS - ld), (0, 0))).reshape(bd * SAMPLE_ROWS, d)
    tm_p, tm_s = 1024, bd * SAMPLE_ROWS

    pk, pv, ps, sk, sv, ss = [], [], [], [], [], []
    for l in range(depth):
        lbf_l, oml_l = lbf[l:l + 1], oml[l:l + 1]
        u = _norm_matmul(xp, norm_mix[l], w_in_b[l], tm_p, 512).reshape(bp, seq, in_width)
        oa, st = _hgrn_prompt(u, lbf_l, oml_l, hgrn_norm[l])
        ob, kn, vn = _swa_prompt(u, sinks[l], qg2[l:l + 1], kg2[l:l + 1], rel_bias_table)
        xp = _outproj(xp, oa.reshape(bp * seq, A_WIDTH), ob.reshape(bp * seq, B_WIDTH), w_out_b[l], 512)
        xp = _ffn(xp, norm_ffn[l], w_gate_b[l], w_up_b[l], w_down_b[l], tm_p, 256)
        pk.append(kn)
        pv.append(vn)
        ps.append(st)
        u = _norm_matmul(xs, norm_mix[l], w_in_b[l], tm_s, 512).reshape(bd, SAMPLE_ROWS, in_width)
        oa, st = _hgrn_sample(u, lbf_l, oml_l, hgrn_norm[l], state_hgrn[l].astype(F32), ld)
        ob, kn, vn = _swa_sample(u, ck[l], cv[l], sinks[l], qg2[l:l + 1], kg2[l:l + 1], rel_bias_table, ld)
        xs = _outproj(xs, oa.reshape(tm_s, A_WIDTH), ob.reshape(tm_s, B_WIDTH), w_out_b[l], tm_s)
        xs = _ffn(xs, norm_ffn[l], w_gate_b[l], w_up_b[l], w_down_b[l], tm_s, 512)
        sk.append(kn)
        sv.append(vn)
        ss.append(st)

    kv_shape = lambda n: (depth, n, WINDOW, B_KV_HEADS, B_HD)
    return (xp.reshape(bp, seq, d),
            xs.reshape(bd, SAMPLE_ROWS, d)[:, :ld],
            jnp.stack(pk).reshape(kv_shape(bp)), jnp.stack(pv).reshape(kv_shape(bp)), jnp.stack(ps),
            jnp.stack(sk).reshape(kv_shape(bd)), jnp.stack(sv).reshape(kv_shape(bd)), jnp.stack(ss))
```

```python
import functools
import math

import jax
import jax.numpy as jnp
import numpy as np
from jax import lax
from jax.experimental import pallas as pl
from jax.experimental.pallas import tpu as pltpu

F32 = jnp.float32
BF16 = jnp.bfloat16

LANES = 128
SUBLANES = 8
VMEM_BYTES_V7X = 64 * 1024 * 1024
VMEM_CAP = VMEM_BYTES_V7X - 8 * 1024 * 1024

A_HEADS = 8
A_DK = 128
A_DV = 128
A_KEY_WIDTH = A_HEADS * A_DK
A_WIDTH = A_HEADS * A_DV
B_HEADS = 16
B_KV_HEADS = 4
B_HD = 64
B_WIDTH = B_HEADS * B_HD
B_KV_WIDTH = B_KV_HEADS * B_HD
WINDOW = 128
ATTN_SCALE = 1.0 / math.sqrt(B_HD)
NEG_LOGIT = -1e30
LB_FLOOR = 1e-20
N_BUCKETS = 32
MAX_DISTANCE = 128
RMS_EPS = 1e-6

Q_A, F_A, I_A, G_A = 0, A_KEY_WIDTH, 2 * A_KEY_WIDTH, 2 * A_KEY_WIDTH + A_WIDTH
Q_B = 2 * A_KEY_WIDTH + 2 * A_WIDTH
K_B = Q_B + B_WIDTH
V_B = K_B + B_KV_WIDTH

HGRN_CHUNK = 128
HGRN_HEADS_PER_STEP = 4
SAMPLE_ROWS = SUBLANES
KEY_SLOTS = 2 * WINDOW


def _vmem_limit(nbytes):
    return int(min(VMEM_CAP, nbytes * 5 // 4 + (4 << 20)))


def _params(sem, nbytes):
    return pltpu.CompilerParams(dimension_semantics=sem, vmem_limit_bytes=_vmem_limit(nbytes))


def _sig_pair(z):
    t = jnp.exp(-jnp.abs(z))
    r = 1.0 / (1.0 + t)
    tr = t * r
    pos = z >= 0
    return jnp.where(pos, r, tr), jnp.where(pos, tr, r)


def _silu(x):
    return x * _sig_pair(x)[0]


def _rms_rows(x, g):
    ms = jnp.mean(x * x, axis=-1, keepdims=True)
    return x * lax.rsqrt(ms + RMS_EPS) * g


def _split_bf16(x):
    hi = x.astype(BF16)
    lo = (x - hi.astype(F32)).astype(BF16)
    return hi, lo


def _dot(a, b):
    return jnp.dot(a, b, preferred_element_type=F32)


def _dot_nt(a, b):
    return lax.dot_general(a, b, (((1,), (1,)), ((), ())), preferred_element_type=F32)


def _lb_kernel(lb_ref, lbf_ref, oml_ref):
    depth = lb_ref.shape[0]
    rows = [lb_ref[i:i + 1, :] for i in range(depth)]
    m = functools.reduce(jnp.maximum, rows)
    e = [jnp.exp(r - m) for r in rows]
    s = functools.reduce(lambda a, b: a + b, e)
    p = [ei / s for ei in e]
    cum = p[0]
    for i in range(depth):
        if i > 0:
            cum = cum + p[i]
        lb = cum - p[0]
        lbf_ref[i:i + 1, :] = jnp.maximum(lb, LB_FLOOR)
        oml_ref[i:i + 1, :] = 1.0 - lb


def _lower_bounds(lower_bounds):
    shp = jax.ShapeDtypeStruct(lower_bounds.shape, F32)
    return pl.pallas_call(_lb_kernel, out_shape=(shp, shp), name="lb_prep")(lower_bounds.astype(F32))


def _norm_matmul_kernel(x_ref, g_ref, w_ref, o_ref, h_ref):
    @pl.when(pl.program_id(1) == 0)
    def _():
        h_ref[...] = _rms_rows(x_ref[...], g_ref[...]).astype(BF16)

    o_ref[...] = _dot(h_ref[...], w_ref[...])


def _norm_matmul(x, g, w, layer, tm, tn):
    m, d = x.shape
    n = w.shape[2]
    nbytes = 2 * tm * d * 4 + tm * d * 2 + 2 * d * tn * 2 + 2 * tm * tn * 4
    return pl.pallas_call(
        _norm_matmul_kernel,
        grid=(m // tm, n // tn),
        in_specs=[pl.BlockSpec((tm, d), lambda i, j: (i, 0)),
                  pl.BlockSpec((1, d), lambda i, j: (0, 0)),
                  pl.BlockSpec((None, d, tn), lambda i, j: (layer, 0, j))],
        out_specs=pl.BlockSpec((tm, tn), lambda i, j: (i, j)),
        out_shape=jax.ShapeDtypeStruct((m, n), F32),
        scratch_shapes=[pltpu.VMEM((tm, d), BF16)],
        compiler_params=_params(("parallel", "arbitrary"), nbytes),
        name="norm_matmul",
    )(x, g.reshape(1, d), w)


def _outproj_kernel(x_ref, oa_ref, ob_ref, w_ref, o_ref):
    ka = oa_ref.shape[1]
    acc = _dot(oa_ref[...], w_ref[0:ka, :])
    acc = acc + _dot(ob_ref[...], w_ref[ka:, :])
    o_ref[...] = x_ref[...] + acc


def _outproj(x, oa, ob, w, layer, tm):
    m, d = x.shape
    ka, kb = oa.shape[1], ob.shape[1]
    nbytes = 4 * tm * d * 4 + 2 * tm * (ka + kb) * 2 + 2 * (ka + kb) * d * 2
    return pl.pallas_call(
        _outproj_kernel,
        grid=(m // tm,),
        in_specs=[pl.BlockSpec((tm, d), lambda i: (i, 0)),
                  pl.BlockSpec((tm, ka), lambda i: (i, 0)),
                  pl.BlockSpec((tm, kb), lambda i: (i, 0)),
                  pl.BlockSpec((None, ka + kb, d), lambda i: (layer, 0, 0))],
        out_specs=pl.BlockSpec((tm, d), lambda i: (i, 0)),
        out_shape=jax.ShapeDtypeStruct((m, d), F32),
        compiler_params=_params(("parallel",), nbytes),
        name="outproj",
    )(x, oa, ob, w)


def _ffn_kernel(x_ref, g_ref, wg_ref, wu_ref, wd_ref, o_ref, h_ref):
    @pl.when(pl.program_id(1) == 0)
    def _():
        x = x_ref[...]
        h_ref[...] = _rms_rows(x, g_ref[...]).astype(BF16)
        o_ref[...] = x

    h = h_ref[...]
    act = (_silu(_dot(h, wg_ref[...])) * _dot(h, wu_ref[...])).astype(BF16)
    o_ref[...] += _dot(act, wd_ref[...])


def _ffn(x, g, wg, wu, wd, layer, tm, tf):
    m, d = x.shape
    f = wg.shape[2]
    nbytes = 4 * tm * d * 4 + tm * d * 2 + 2 * 3 * d * tf * 2 + 3 * tm * tf * 4
    return pl.pallas_call(
        _ffn_kernel,
        grid=(m // tm, f // tf),
        in_specs=[pl.BlockSpec((tm, d), lambda i, j: (i, 0)),
                  pl.BlockSpec((1, d), lambda i, j: (0, 0)),
                  pl.BlockSpec((None, d, tf), lambda i, j: (layer, 0, j)),
                  pl.BlockSpec((None, d, tf), lambda i, j: (layer, 0, j)),
                  pl.BlockSpec((None, tf, d), lambda i, j: (layer, j, 0))],
        out_specs=pl.BlockSpec((tm, d), lambda i, j: (i, 0)),
        out_shape=jax.ShapeDtypeStruct((m, d), F32),
        scratch_shapes=[pltpu.VMEM((tm, d), BF16)],
        compiler_params=_params(("parallel", "arbitrary"), nbytes),
        name="ffn",
    )(x, g.reshape(1, d), wg, wu, wd)


def _hgrn_static(c):
    t = np.arange(c)
    blocks = [(t[None, :] <= t[:, None]), (t[None, :] > t[:, None])]
    masks = []
    h = c // 2
    while h >= 1:
        grp = t // (2 * h)
        mid = grp * 2 * h + h
        upper = (t % (2 * h)) >= h
        col = t[None, :]
        p_up = (col >= mid[:, None]) & (col <= t[:, None])
        p_lo = (col > t[:, None]) & (col < mid[:, None])
        blocks.append(np.where(upper[:, None], p_up, p_lo))
        masks.append((grp[:, None] == grp[None, :]) & upper[:, None] & ~upper[None, :])
        h //= 2
    masks.append(np.eye(c, dtype=bool))
    p_all = np.concatenate(blocks, axis=0).astype(np.float32)
    p2 = np.concatenate([p_all, p_all], axis=1)
    return p2, np.stack(masks).astype(np.float32)


def _hgrn_prompt_kernel(q_ref, f_ref, i_ref, g_ref, lbf_ref, oml_ref, gn_ref, p2_ref, m_ref,
                        oa_ref, st_ref, s_scr, *, chunk, heads, n_levels):
    c_idx = pl.program_id(2)

    @pl.when(c_idx == 0)
    def _():
        s_scr[...] = jnp.zeros_like(s_scr)

    sz, snz = _sig_pair(f_ref[0])
    oml = oml_ref[...]
    lf = jnp.log(lbf_ref[...] + oml * sz)
    ka = oml * snz
    hi, lo = _split_bf16(lf)
    expo = _dot(p2_ref[...], jnp.concatenate([hi, lo], axis=0))
    q = _silu(q_ref[0])
    v = i_ref[0]
    gate = _silu(g_ref[0])
    gn = gn_ref[...]

    for hh in range(heads):
        sl = slice(hh * A_DK, (hh + 1) * A_DK)
        qh, kh, vh = q[:, sl], ka[:, sl], v[:, sl]
        b = expo[0:chunk, sl]
        st = s_scr[hh]
        o = _dot_nt((qh * jnp.exp(b)).astype(BF16), st.astype(BF16))
        qs, ks = [], []
        for li in range(n_levels):
            e = jnp.exp(expo[(2 + li) * chunk:(3 + li) * chunk, sl])
            qs.append((qh * e).astype(BF16))
            ks.append((kh * e).astype(BF16))
        qs.append(qh.astype(BF16))
        ks.append(kh.astype(BF16))
        a = jnp.zeros((chunk, chunk), F32)
        for pi in range(0, n_levels + 1, 2):
            gq = jnp.concatenate([qs[pi], qs[pi + 1]], axis=0)
            gk = jnp.concatenate([ks[pi], ks[pi + 1]], axis=0)
            gm = _dot_nt(gq, gk)
            a = a + gm[0:chunk, 0:chunk] * m_ref[pi] + gm[chunk:, chunk:] * m_ref[pi + 1]
        o = o + _dot(a.astype(BF16), vh.astype(BF16))
        k_tail = (kh * jnp.exp(expo[chunk:2 * chunk, sl])).astype(BF16)
        s_scr[hh] = st * jnp.exp(b[chunk - 1:chunk, :]) + _dot(vh.T.astype(BF16), k_tail)
        oa_ref[0, :, sl] = (_rms_rows(o, gn) * gate[:, sl]).astype(BF16)

    @pl.when(c_idx == pl.num_programs(2) - 1)
    def _():
        for hh in range(heads):
            st_ref[0, hh] = s_scr[hh].T


def _hgrn_prompt(u3, lbf, oml, gn):
    bsz, length, _ = u3.shape
    chunk, heads = HGRN_CHUNK, HGRN_HEADS_PER_STEP
    cols = heads * A_DK
    n_levels = int(math.log2(chunk))
    assert (n_levels + 1) % 2 == 0 and length % chunk == 0 and A_HEADS % heads == 0
    p2, masks = _hgrn_static(chunk)
    p2 = jnp.asarray(p2, BF16)
    masks = jnp.asarray(masks, F32)

    def col_spec(offset):
        base = offset // cols
        return pl.BlockSpec((1, chunk, cols), lambda b, h, c: (b, c, base + h))

    vec_spec = pl.BlockSpec((1, cols), lambda b, h, c: (0, h))
    nbytes = (2 * 5 * chunk * cols * 4 + 2 * (p2.size * 2 + masks.size * 4)
              + 6 * p2.shape[0] * cols * 4)
    return pl.pallas_call(
        functools.partial(_hgrn_prompt_kernel, chunk=chunk, heads=heads, n_levels=n_levels),
        grid=(bsz, A_HEADS // heads, length // chunk),
        in_specs=[col_spec(Q_A), col_spec(F_A), col_spec(I_A), col_spec(G_A),
                  vec_spec, vec_spec,
                  pl.BlockSpec((1, A_DV), lambda b, h, c: (0, 0)),
                  pl.BlockSpec(p2.shape, lambda b, h, c: (0, 0)),
                  pl.BlockSpec(masks.shape, lambda b, h, c: (0, 0, 0))],
        out_specs=(pl.BlockSpec((1, chunk, cols), lambda b, h, c: (b, c, h)),
                   pl.BlockSpec((1, heads, A_DK, A_DV), lambda b, h, c: (b, h, 0, 0))),
        out_shape=(jax.ShapeDtypeStruct((bsz, length, A_WIDTH), BF16),
                   jax.ShapeDtypeStruct((bsz, A_HEADS, A_DK, A_DV), F32)),
        scratch_shapes=[pltpu.VMEM((heads, A_DV, A_DK), F32)],
        compiler_params=_params(("parallel", "parallel", "arbitrary"), nbytes),
        name="hgrn_prompt",
    )(u3, u3, u3, u3, lbf, oml, gn.reshape(1, A_DV), p2, masks)


def _hgrn_sample_kernel(u_ref, lbf_ref, oml_ref, gn_ref, s_ref, oa_ref, so_ref, *, steps):
    u = u_ref[0]
    q = _silu(u[:, Q_A:Q_A + A_KEY_WIDTH])
    sz, snz = _sig_pair(u[:, F_A:F_A + A_KEY_WIDTH])
    oml = oml_ref[...]
    f = lbf_ref[...] + oml * sz
    ka = oml * snz
    v = u[:, I_A:I_A + A_WIDTH]
    gate = _silu(u[:, G_A:G_A + A_WIDTH])
    gn = gn_ref[...]
    rows = u.shape[0]
    row_id = lax.broadcasted_iota(jnp.int32, (rows, A_DV), 0)
    pad = jnp.zeros((A_DK - 3 * rows, A_DK), F32)

    for hh in range(A_HEADS):
        sl = slice(hh * A_DK, (hh + 1) * A_DK)
        xt = jnp.concatenate([f[:, sl], ka[:, sl], q[:, sl], pad], axis=0).T
        s = s_ref[0, hh]
        o = jnp.zeros((rows, A_DV), F32)
        for t in range(steps):
            s = s * xt[:, t:t + 1] + xt[:, rows + t:rows + t + 1] * v[t:t + 1, sl]
            o_t = jnp.sum(s * xt[:, 2 * rows + t:2 * rows + t + 1], axis=0, keepdims=True)
            o = jnp.where(row_id == t, o_t, o)
        so_ref[0, hh] = s
        oa_ref[0, :, sl] = (_rms_rows(o, gn) * gate[:, sl]).astype(BF16)


def _hgrn_sample(u3, lbf, oml, gn, state, layer, steps):
    bsz, rows, _ = u3.shape
    width = G_A + A_WIDTH
    nbytes = 2 * rows * width * 4 + 4 * A_HEADS * A_DK * A_DV * 4
    return pl.pallas_call(
        functools.partial(_hgrn_sample_kernel, steps=steps),
        grid=(bsz,),
        in_specs=[pl.BlockSpec((1, rows, width), lambda b: (b, 0, 0)),
                  pl.BlockSpec((1, A_KEY_WIDTH), lambda b: (0, 0)),
                  pl.BlockSpec((1, A_KEY_WIDTH), lambda b: (0, 0)),
                  pl.BlockSpec((1, A_DV), lambda b: (0, 0)),
                  pl.BlockSpec((None, 1, A_HEADS, A_DK, A_DV), lambda b: (layer, b, 0, 0, 0))],
        out_specs=(pl.BlockSpec((1, rows, A_WIDTH), lambda b: (b, 0, 0)),
                   pl.BlockSpec((1, A_HEADS, A_DK, A_DV), lambda b: (b, 0, 0, 0))),
        out_shape=(jax.ShapeDtypeStruct((bsz, rows, A_WIDTH), BF16),
                   jax.ShapeDtypeStruct(state.shape[1:], F32)),
        compiler_params=_params(("parallel",), nbytes),
        name="hgrn_sample",
    )(u3, lbf, oml, gn.reshape(1, A_DV), state)


def _t5_bucket(dist):
    n = np.maximum(dist, 0)
    max_exact = N_BUCKETS // 2
    nf = np.maximum(n, 1).astype(np.float32)
    large = max_exact + (np.log(nf / max_exact) / math.log(MAX_DISTANCE / max_exact)
                         * (N_BUCKETS - max_exact)).astype(np.int32)
    large = np.minimum(large, N_BUCKETS - 1)
    return np.where(n < max_exact, n, large).astype(np.int32)


def _bias_kernel(tab_ref, bucket_ref, o_ref):
    rows = bucket_ref.shape[0]
    bucket = bucket_ref[...]
    hits = [bucket == b for b in range(N_BUCKETS)]
    for h in range(B_HEADS):
        acc = jnp.zeros(bucket.shape, F32)
        for b in range(N_BUCKETS):
            acc = jnp.where(hits[b], tab_ref[b, h], acc)
        j, r, c = h // 4, (h % 4) // 2, h % 2
        o_ref[j, r * rows:(r + 1) * rows, c * KEY_SLOTS:(c + 1) * KEY_SLOTS] = acc


def _bias_layout(table, dist):
    rows = dist.shape[0]
    return pl.pallas_call(
        _bias_kernel,
        in_specs=[pl.BlockSpec(memory_space=pltpu.SMEM),
                  pl.BlockSpec(memory_space=pltpu.VMEM)],
        out_shape=jax.ShapeDtypeStruct((B_KV_HEADS, 2 * rows, 2 * KEY_SLOTS), F32),
        name="bias_prep",
    )(table.astype(F32), jnp.asarray(_t5_bucket(dist), jnp.int32))


def _tile2(a):
    return np.tile(a, (2, 2))


def _block_diag_ones():
    r = lax.broadcasted_iota(jnp.int32, (LANES, LANES), 0) // B_HD
    c = lax.broadcasted_iota(jnp.int32, (LANES, LANES), 1) // B_HD
    return (r == c).astype(BF16)


def _head_norm(x, g2, bd):
    hi, lo = _split_bf16(x * x)
    ss = _dot(hi, bd) + _dot(lo, bd)
    return x * lax.rsqrt(ss * (1.0 / B_HD) + RMS_EPS) * g2


def _place(tile, half, lo_mask):
    rolled = pltpu.roll(tile, B_HD, axis=1)
    zero = jnp.zeros_like(tile)
    if half == 0:
        return jnp.where(lo_mask, tile, zero), jnp.where(lo_mask, zero, rolled)
    return jnp.where(lo_mask, rolled, zero), jnp.where(lo_mask, zero, tile)


def _swa_core(qn_tiles, k_tiles, v_tiles, bias_ref, valid, sink_ref, ob_ref):
    rows = qn_tiles[0].shape[0]
    lo_mask = lax.broadcasted_iota(jnp.int32, k_tiles[0].shape, 1) < B_HD
    for j in range(B_KV_HEADS):
        t, half = j // 2, j % 2
        k_lo, k_hi = _place(k_tiles[t], half, lo_mask)
        v_lo, v_hi = _place(v_tiles[t], half, lo_mask)
        kk = jnp.concatenate([k_lo, k_hi], axis=0).astype(BF16)
        vv = jnp.concatenate([v_lo, v_hi], axis=0).astype(BF16)
        qq = jnp.concatenate([qn_tiles[2 * j], qn_tiles[2 * j + 1]], axis=0).astype(BF16)
        s = _dot_nt(qq, kk) * ATTN_SCALE + bias_ref[j]
        s = jnp.where(valid, s, NEG_LOGIT)
        p_rows = []
        for r in range(2):
            p_cols = []
            for c in range(2):
                sk = sink_ref[4 * j + 2 * r + c]
                sb = s[r * rows:(r + 1) * rows, c * KEY_SLOTS:(c + 1) * KEY_SLOTS]
                m = jnp.maximum(jnp.max(sb, axis=-1, keepdims=True), sk)
                e = jnp.exp(sb - m)
                p_cols.append(e / (jnp.sum(e, axis=-1, keepdims=True) + jnp.exp(sk - m)))
            p_rows.append(jnp.concatenate(p_cols, axis=1))
        p = jnp.concatenate(p_rows, axis=0).astype(BF16)
        o = _dot(p, vv)
        ob_ref[0, :, (2 * j) * LANES:(2 * j + 1) * LANES] = o[0:rows].astype(BF16)
        ob_ref[0, :, (2 * j + 1) * LANES:(2 * j + 2) * LANES] = o[rows:].astype(BF16)


def _swa_prompt_kernel(sink_ref, q_ref, kc_ref, kp_ref, vc_ref, vp_ref, qg_ref, kg_ref, bias_ref, code_ref,
                       ob_ref, ko_ref, vo_ref):
    n = pl.program_id(1)
    bd = _block_diag_ones()
    k_all = jnp.concatenate([kp_ref[0], kc_ref[0]], axis=0)
    v_all = jnp.concatenate([vp_ref[0], vc_ref[0]], axis=0)
    kg = kg_ref[...]
    k_tiles = [_head_norm(k_all[:, t * LANES:(t + 1) * LANES], kg, bd) for t in range(2)]
    v_tiles = [v_all[:, t * LANES:(t + 1) * LANES] for t in range(2)]
    q = q_ref[0]
    qg = qg_ref[...]
    qn_tiles = [_head_norm(q[:, t * LANES:(t + 1) * LANES], qg, bd) for t in range(B_WIDTH // LANES)]
    code = code_ref[...]
    valid = (code >= 1) & (code <= jnp.where(n > 0, 2, 1))
    _swa_core(qn_tiles, k_tiles, v_tiles, bias_ref, valid, sink_ref, ob_ref)

    @pl.when(n == pl.num_programs(1) - 1)
    def _():
        ko_ref[0] = jnp.concatenate([kt[WINDOW:] for kt in k_tiles], axis=1)
        vo_ref[0] = vc_ref[0]


def _prompt_dist():
    return WINDOW + np.arange(WINDOW)[:, None] - np.arange(KEY_SLOTS)[None, :]


def _swa_prompt(u3, sinks, qg, kg, bias):
    bsz, length, _ = u3.shape
    nb = length // WINDOW
    j = np.arange(KEY_SLOTS)[None, :]
    dist = _prompt_dist()
    code = np.where((j >= WINDOW) & (dist >= 0), 1, np.where((j < WINDOW) & (dist < WINDOW), 2, 0))
    code = jnp.asarray(_tile2(code), jnp.int32)
    qb, kb, vb = Q_B // B_WIDTH, K_B // B_KV_WIDTH, V_B // B_KV_WIDTH
    assert Q_B % B_WIDTH == 0 and K_B % B_KV_WIDTH == 0 and V_B % B_KV_WIDTH == 0
    kv_spec = lambda col, prev: pl.BlockSpec(
        (1, WINDOW, B_KV_WIDTH),
        (lambda b, n: (b, jnp.maximum(n - 1, 0), col)) if prev else (lambda b, n: (b, n, col)))
    g_spec = pl.BlockSpec((1, LANES), lambda b, n: (0, 0))
    nbytes = (2 * WINDOW * (B_WIDTH + 4 * B_KV_WIDTH) * 4 + 2 * (bias.size + code.size) * 4
              + 2 * WINDOW * B_WIDTH * 2 + 24 * WINDOW * 2 * KEY_SLOTS * 4)
    return pl.pallas_call(
        _swa_prompt_kernel,
        grid=(bsz, nb),
        in_specs=[pl.BlockSpec(memory_space=pltpu.SMEM),
                  pl.BlockSpec((1, WINDOW, B_WIDTH), lambda b, n: (b, n, qb)),
                  kv_spec(kb, False), kv_spec(kb, True), kv_spec(vb, False), kv_spec(vb, True),
                  g_spec, g_spec,
                  pl.BlockSpec(bias.shape, lambda b, n: (0, 0, 0)),
                  pl.BlockSpec(code.shape, lambda b, n: (0, 0))],
        out_specs=(pl.BlockSpec((1, WINDOW, B_WIDTH), lambda b, n: (b, n, 0)),
                   pl.BlockSpec((1, WINDOW, B_KV_WIDTH), lambda b, n: (b, 0, 0)),
                   pl.BlockSpec((1, WINDOW, B_KV_WIDTH), lambda b, n: (b, 0, 0))),
        out_shape=(jax.ShapeDtypeStruct((bsz, length, B_WIDTH), BF16),
                   jax.ShapeDtypeStruct((bsz, WINDOW, B_KV_WIDTH), F32),
                   jax.ShapeDtypeStruct((bsz, WINDOW, B_KV_WIDTH), F32)),
        compiler_params=_params(("parallel", "arbitrary"), nbytes),
        name="swa_prompt",
    )(sinks, u3, u3, u3, u3, u3, qg, kg, bias, code)


def _swa_sample_kernel(sink_ref, q_ref, kv_ref, ck_ref, cv_ref, qg_ref, kg_ref, bias_ref, code_ref,
                       ob_ref, ko_ref, vo_ref, *, steps):
    bd = _block_diag_ones()
    rows = q_ref.shape[1]
    kv = kv_ref[0]
    kg = kg_ref[...]
    k_new = [_head_norm(kv[:, t * LANES:(t + 1) * LANES], kg, bd) for t in range(2)]
    v_new = [kv[:, B_KV_WIDTH + t * LANES:B_KV_WIDTH + (t + 1) * LANES] for t in range(2)]
    ck, cv = ck_ref[0], cv_ref[0]
    pad = jnp.zeros((KEY_SLOTS - WINDOW - rows, LANES), F32)
    k_tiles = [jnp.concatenate([ck[:, t * LANES:(t + 1) * LANES], k_new[t], pad], axis=0) for t in range(2)]
    v_tiles = [jnp.concatenate([cv[:, t * LANES:(t + 1) * LANES], v_new[t], pad], axis=0) for t in range(2)]
    q = q_ref[0]
    qg = qg_ref[...]
    qn_tiles = [_head_norm(q[:, t * LANES:(t + 1) * LANES], qg, bd) for t in range(B_WIDTH // LANES)]
    valid = code_ref[...] >= 1
    _swa_core(qn_tiles, k_tiles, v_tiles, bias_ref, valid, sink_ref, ob_ref)
    ko_ref[0, 0:WINDOW - steps, :] = ck_ref[0, steps:WINDOW, :]
    vo_ref[0, 0:WINDOW - steps, :] = cv_ref[0, steps:WINDOW, :]
    ko_ref[0, WINDOW - steps:WINDOW, :] = jnp.concatenate([kt[0:steps] for kt in k_new], axis=1)
    vo_ref[0, WINDOW - steps:WINDOW, :] = kv[0:steps, B_KV_WIDTH:]


def _sample_dist(rows):
    return WINDOW + np.arange(rows)[:, None] - np.arange(KEY_SLOTS)[None, :]


def _swa_sample(u3, cache_k, cache_v, layer, sinks, qg, kg, bias, steps):
    bsz, rows, _ = u3.shape
    w = cache_k.shape[2]
    assert w == WINDOW
    j = np.arange(KEY_SLOTS)[None, :]
    dist = _sample_dist(rows)
    code = np.where((dist >= 0) & (dist < WINDOW) & (j < w + steps), 1, 0)
    code = jnp.asarray(_tile2(code), jnp.int32)
    qb, kvb = Q_B // B_WIDTH, K_B // (2 * B_KV_WIDTH)
    assert K_B % (2 * B_KV_WIDTH) == 0
    c_in_spec = pl.BlockSpec((None, 1, w, B_KV_WIDTH), lambda b: (layer, b, 0, 0))
    c_spec = pl.BlockSpec((1, w, B_KV_WIDTH), lambda b: (b, 0, 0))
    g_spec = pl.BlockSpec((1, LANES), lambda b: (0, 0))
    nbytes = 8 * w * B_KV_WIDTH * 4 + 2 * (bias.size + code.size) * 4 + 16 * KEY_SLOTS * LANES * 4
    return pl.pallas_call(
        functools.partial(_swa_sample_kernel, steps=steps),
        grid=(bsz,),
        in_specs=[pl.BlockSpec(memory_space=pltpu.SMEM),
                  pl.BlockSpec((1, rows, B_WIDTH), lambda b: (b, 0, qb)),
                  pl.BlockSpec((1, rows, 2 * B_KV_WIDTH), lambda b: (b, 0, kvb)),
                  c_in_spec, c_in_spec, g_spec, g_spec,
                  pl.BlockSpec(bias.shape, lambda b: (0, 0, 0)),
                  pl.BlockSpec(code.shape, lambda b: (0, 0))],
        out_specs=(pl.BlockSpec((1, rows, B_WIDTH), lambda b: (b, 0, 0)), c_spec, c_spec),
        out_shape=(jax.ShapeDtypeStruct((bsz, rows, B_WIDTH), BF16),
                   jax.ShapeDtypeStruct((bsz, w, B_KV_WIDTH), F32),
                   jax.ShapeDtypeStruct((bsz, w, B_KV_WIDTH), F32)),
        compiler_params=_params(("parallel",), nbytes),
        name="swa_sample",
    )(sinks, u3, u3, cache_k, cache_v, qg, kg, bias, code)


def kernel(x_prompt, x_sample, cache_k, cache_v, state_hgrn, norm_mix, w_in, lower_bounds, hgrn_norm,
           q_norm, k_norm, attn_sinks, rel_bias_table, w_out, norm_ffn, w_gate, w_up, w_down):
    depth = w_in.shape[0]
    bp, seq, d = x_prompt.shape
    bd, ld, _ = x_sample.shape
    w = cache_k.shape[2]
    in_width = w_in.shape[2]

    w_in_b, w_out_b = w_in.astype(BF16), w_out.astype(BF16)
    w_gate_b, w_up_b, w_down_b = w_gate.astype(BF16), w_up.astype(BF16), w_down.astype(BF16)
    lbf, oml = _lower_bounds(lower_bounds)
    sinks = attn_sinks.astype(F32)
    qg2 = jnp.tile(q_norm.astype(F32), (1, LANES // B_HD))
    kg2 = jnp.tile(k_norm.astype(F32), (1, LANES // B_HD))
    ck = cache_k.astype(F32).reshape(depth, bd, w, B_KV_WIDTH)
    cv = cache_v.astype(F32).reshape(depth, bd, w, B_KV_WIDTH)

    xp = x_prompt.reshape(bp * seq, d)
    xs = jnp.pad(x_sample, ((0, 0), (0, SAMPLE_ROWS - ld), (0, 0))).reshape(bd * SAMPLE_ROWS, d)
    tm_p, tm_s = 1024, bd * SAMPLE_ROWS
    bias_p = _bias_layout(rel_bias_table, _prompt_dist())
    bias_s = _bias_layout(rel_bias_table, _sample_dist(SAMPLE_ROWS))
    state = state_hgrn.astype(F32)

    pk, pv, ps, sk, sv, ss = [], [], [], [], [], []
    for l in range(depth):
        lbf_l, oml_l = lbf[l:l + 1], oml[l:l + 1]
        u = _norm_matmul(xp, norm_mix[l], w_in_b, l, tm_p, 512).reshape(bp, seq, in_width)
        oa, st = _hgrn_prompt(u, lbf_l, oml_l, hgrn_norm[l])
        ob, kn, vn = _swa_prompt(u, sinks[l], qg2[l:l + 1], kg2[l:l + 1], bias_p)
        xp = _outproj(xp, oa.reshape(bp * seq, A_WIDTH), ob.reshape(bp * seq, B_WIDTH), w_out_b, l, 512)
        xp = _ffn(xp, norm_ffn[l], w_gate_b, w_up_b, w_down_b, l, tm_p, 256)
        pk.append(kn)
        pv.append(vn)
        ps.append(st)
        u = _norm_matmul(xs, norm_mix[l], w_in_b, l, tm_s, 512).reshape(bd, SAMPLE_ROWS, in_width)
        oa, st = _hgrn_sample(u, lbf_l, oml_l, hgrn_norm[l], state, l, ld)
        ob, kn, vn = _swa_sample(u, ck, cv, l, sinks[l], qg2[l:l + 1], kg2[l:l + 1], bias_s, ld)
        xs = _outproj(xs, oa.reshape(tm_s, A_WIDTH), ob.reshape(tm_s, B_WIDTH), w_out_b, l, tm_s)
        xs = _ffn(xs, norm_ffn[l], w_gate_b, w_up_b, w_down_b, l, tm_s, 512)
        sk.append(kn)
        sv.append(vn)
        ss.append(st)

    kv_shape = lambda n: (depth, n, WINDOW, B_KV_HEADS, B_HD)
    return (xp.reshape(bp, seq, d),
            xs.reshape(bd, SAMPLE_ROWS, d)[:, :ld],
            jnp.stack(pk).reshape(kv_shape(bp)), jnp.stack(pv).reshape(kv_shape(bp)), jnp.stack(ps),
            jnp.stack(sk).reshape(kv_shape(bd)), jnp.stack(sv).reshape(kv_shape(bd)), jnp.stack(ss))
```

```python
import functools
import math

import jax
import jax.numpy as jnp
import numpy as np
from jax import lax
from jax.experimental import pallas as pl
from jax.experimental.pallas import tpu as pltpu

F32 = jnp.float32
BF16 = jnp.bfloat16

LANES = 128
SUBLANES = 8
VMEM_BYTES_V7X = 64 * 1024 * 1024
VMEM_CAP = VMEM_BYTES_V7X - 8 * 1024 * 1024

A_HEADS = 8
A_DK = 128
A_DV = 128
A_KEY_WIDTH = A_HEADS * A_DK
A_WIDTH = A_HEADS * A_DV
B_HEADS = 16
B_KV_HEADS = 4
B_HD = 64
B_WIDTH = B_HEADS * B_HD
B_KV_WIDTH = B_KV_HEADS * B_HD
WINDOW = 128
ATTN_SCALE = 1.0 / math.sqrt(B_HD)
NEG_LOGIT = -1e30
LB_FLOOR = 1e-20
N_BUCKETS = 32
MAX_DISTANCE = 128
RMS_EPS = 1e-6

Q_A, F_A, I_A, G_A = 0, A_KEY_WIDTH, 2 * A_KEY_WIDTH, 2 * A_KEY_WIDTH + A_WIDTH
Q_B = 2 * A_KEY_WIDTH + 2 * A_WIDTH
K_B = Q_B + B_WIDTH
V_B = K_B + B_KV_WIDTH

HGRN_CHUNK = 128
HGRN_FINE = 4
HGRN_HEADS_PER_STEP = 4
SAMPLE_ROWS = SUBLANES
KEY_SLOTS = 2 * WINDOW


def _vmem_limit(nbytes):
    return int(min(VMEM_CAP, nbytes * 5 // 4 + (4 << 20)))


def _params(sem, nbytes):
    return pltpu.CompilerParams(dimension_semantics=sem, vmem_limit_bytes=_vmem_limit(nbytes))


def _sig_pair(z):
    t = jnp.exp(-jnp.abs(z))
    r = 1.0 / (1.0 + t)
    tr = t * r
    pos = z >= 0
    return jnp.where(pos, r, tr), jnp.where(pos, tr, r)


def _silu(x):
    return x * _sig_pair(x)[0]


def _rms_rows(x, g):
    ms = jnp.mean(x * x, axis=-1, keepdims=True)
    return x * lax.rsqrt(ms + RMS_EPS) * g


def _split_bf16(x):
    hi = x.astype(BF16)
    lo = (x - hi.astype(F32)).astype(BF16)
    return hi, lo


def _dot(a, b):
    return jnp.dot(a, b, preferred_element_type=F32)


def _dot_nt(a, b):
    return lax.dot_general(a, b, (((1,), (1,)), ((), ())), preferred_element_type=F32)


def _lb_kernel(lb_ref, lbf_ref, oml_ref):
    depth = lb_ref.shape[0]
    rows = [lb_ref[i:i + 1, :] for i in range(depth)]
    m = functools.reduce(jnp.maximum, rows)
    e = [jnp.exp(r - m) for r in rows]
    s = functools.reduce(lambda a, b: a + b, e)
    p = [ei / s for ei in e]
    cum = p[0]
    for i in range(depth):
        if i > 0:
            cum = cum + p[i]
        lb = cum - p[0]
        lbf_ref[i:i + 1, :] = jnp.maximum(lb, LB_FLOOR)
        oml_ref[i:i + 1, :] = 1.0 - lb


def _lower_bounds(lower_bounds):
    shp = jax.ShapeDtypeStruct(lower_bounds.shape, F32)
    return pl.pallas_call(_lb_kernel, out_shape=(shp, shp), name="lb_prep")(lower_bounds.astype(F32))


def _norm_matmul_kernel(x_ref, g_ref, w_ref, o_ref, h_ref):
    @pl.when(pl.program_id(1) == 0)
    def _():
        h_ref[...] = _rms_rows(x_ref[...], g_ref[...]).astype(BF16)

    o_ref[...] = _dot(h_ref[...], w_ref[...])


def _norm_matmul(x, g, w, layer, tm, tn):
    m, d = x.shape
    n = w.shape[2]
    nbytes = 2 * tm * d * 4 + tm * d * 2 + 2 * d * tn * 2 + 2 * tm * tn * 4
    return pl.pallas_call(
        _norm_matmul_kernel,
        grid=(m // tm, n // tn),
        in_specs=[pl.BlockSpec((tm, d), lambda i, j: (i, 0)),
                  pl.BlockSpec((1, d), lambda i, j: (0, 0)),
                  pl.BlockSpec((None, d, tn), lambda i, j: (layer, 0, j))],
        out_specs=pl.BlockSpec((tm, tn), lambda i, j: (i, j)),
        out_shape=jax.ShapeDtypeStruct((m, n), F32),
        scratch_shapes=[pltpu.VMEM((tm, d), BF16)],
        compiler_params=_params(("parallel", "arbitrary"), nbytes),
        name="norm_matmul",
    )(x, g.reshape(1, d), w)


def _outproj_kernel(x_ref, oa_ref, ob_ref, w_ref, o_ref):
    ka = oa_ref.shape[1]
    acc = _dot(oa_ref[...], w_ref[0:ka, :])
    acc = acc + _dot(ob_ref[...], w_ref[ka:, :])
    o_ref[...] = x_ref[...] + acc


def _outproj(x, oa, ob, w, layer, tm):
    m, d = x.shape
    ka, kb = oa.shape[1], ob.shape[1]
    nbytes = 4 * tm * d * 4 + 2 * tm * (ka + kb) * 2 + 2 * (ka + kb) * d * 2
    return pl.pallas_call(
        _outproj_kernel,
        grid=(m // tm,),
        in_specs=[pl.BlockSpec((tm, d), lambda i: (i, 0)),
                  pl.BlockSpec((tm, ka), lambda i: (i, 0)),
                  pl.BlockSpec((tm, kb), lambda i: (i, 0)),
                  pl.BlockSpec((None, ka + kb, d), lambda i: (layer, 0, 0))],
        out_specs=pl.BlockSpec((tm, d), lambda i: (i, 0)),
        out_shape=jax.ShapeDtypeStruct((m, d), F32),
        compiler_params=_params(("parallel",), nbytes),
        name="outproj",
    )(x, oa, ob, w)


def _ffn_kernel(x_ref, g_ref, wg_ref, wu_ref, wd_ref, o_ref, h_ref):
    @pl.when(pl.program_id(1) == 0)
    def _():
        x = x_ref[...]
        h_ref[...] = _rms_rows(x, g_ref[...]).astype(BF16)
        o_ref[...] = x

    h = h_ref[...]
    act = (_silu(_dot(h, wg_ref[...])) * _dot(h, wu_ref[...])).astype(BF16)
    o_ref[...] += _dot(act, wd_ref[...])


def _ffn(x, g, wg, wu, wd, layer, tm, tf):
    m, d = x.shape
    f = wg.shape[2]
    nbytes = 4 * tm * d * 4 + tm * d * 2 + 2 * 3 * d * tf * 2 + 3 * tm * tf * 4
    return pl.pallas_call(
        _ffn_kernel,
        grid=(m // tm, f // tf),
        in_specs=[pl.BlockSpec((tm, d), lambda i, j: (i, 0)),
                  pl.BlockSpec((1, d), lambda i, j: (0, 0)),
                  pl.BlockSpec((None, d, tf), lambda i, j: (layer, 0, j)),
                  pl.BlockSpec((None, d, tf), lambda i, j: (layer, 0, j)),
                  pl.BlockSpec((None, tf, d), lambda i, j: (layer, j, 0))],
        out_specs=pl.BlockSpec((tm, d), lambda i, j: (i, 0)),
        out_shape=jax.ShapeDtypeStruct((m, d), F32),
        scratch_shapes=[pltpu.VMEM((tm, d), BF16)],
        compiler_params=_params(("parallel", "arbitrary"), nbytes),
        name="ffn",
    )(x, g.reshape(1, d), wg, wu, wd)


def _hgrn_static(c):
    t = np.arange(c)
    blocks = [(t[None, :] <= t[:, None])]
    masks = []
    h = c // 2
    while h >= 1:
        grp = t // (2 * h)
        mid = grp * 2 * h + h
        upper = (t % (2 * h)) >= h
        if h < HGRN_FINE:
            col = t[None, :]
            p_up = (col >= mid[:, None]) & (col <= t[:, None])
            p_lo = (col > t[:, None]) & (col < mid[:, None])
            blocks.append(np.where(upper[:, None], p_up, p_lo))
        masks.append((grp[:, None] == grp[None, :]) & upper[:, None] & ~upper[None, :])
        h //= 2
    masks.append(np.eye(c, dtype=bool))
    p_all = np.concatenate(blocks, axis=0).astype(np.float32)
    p2 = np.concatenate([p_all, p_all], axis=1)
    masks = np.stack(masks).astype(np.float32)
    return p2, np.concatenate([masks, masks], axis=2)


def _hgrn_prompt_kernel(q_ref, f_ref, i_ref, g_ref, lbf_ref, oml_ref, gn_ref, p2_ref, m_ref,
                        oa_ref, st_ref, s_scr, *, chunk, heads):
    c_idx = pl.program_id(2)

    @pl.when(c_idx == 0)
    def _():
        s_scr[...] = jnp.zeros_like(s_scr)

    width = heads * A_DK
    pw = 2 * A_DK
    sz, snz = _sig_pair(f_ref[0])
    oml = oml_ref[...]
    lf = jnp.log(lbf_ref[...] + oml * sz)
    ka = oml * snz
    hi, lo = _split_bf16(lf)
    expo = _dot(p2_ref[...], jnp.concatenate([hi, lo], axis=0))
    b = expo[0:chunk]
    q = _silu(q_ref[0])
    v = i_ref[0]
    gate = _silu(g_ref[0])
    gn = gn_ref[...]

    row = lax.broadcasted_iota(jnp.int32, (chunk, width), 0)
    qs, ks = [], []
    h, fine = chunk // 2, 0
    while h >= 1:
        if h >= HGRN_FINE:
            groups = chunk // (2 * h)
            mid = jnp.broadcast_to(b.reshape(groups, 2 * h, width)[:, h - 1:h, :], (groups, 2 * h, width))
            d = b - mid.reshape(chunk, width)
            e = jnp.where((row & h) != 0, d, -d)
        else:
            fine += 1
            e = expo[fine * chunk:(fine + 1) * chunk]
        e = jnp.exp(e)
        qs.append((q * e).astype(BF16))
        ks.append((ka * e).astype(BF16))
        h //= 2
    qs.append(q.astype(BF16))
    ks.append(ka.astype(BF16))
    b_last = b[chunk - 1:chunk, :]
    q0 = (q * jnp.exp(b)).astype(BF16)
    k_tail = (ka * jnp.exp(b_last - b)).astype(BF16)
    decay = jnp.exp(b_last)
    vb = v.astype(BF16)

    left = lax.broadcasted_iota(jnp.int32, (chunk, pw), 1) < A_DK

    def bdiag(x):
        z = jnp.zeros_like(x)
        return jnp.concatenate([jnp.where(left, x, z), jnp.where(left, z, x)], axis=0)

    for p in range(heads // 2):
        sl = slice(p * pw, (p + 1) * pw)
        st = s_scr[p]
        o = _dot_nt(q0[:, sl], bdiag(st.astype(BF16)))
        a = jnp.zeros((chunk, pw), F32)
        for li in range(len(qs)):
            a = a + _dot_nt(qs[li][:, sl], bdiag(ks[li][:, sl])) * m_ref[li]
        o = o + _dot(a.astype(BF16), bdiag(vb[:, sl]))
        vp = v[:, sl]
        vt = jnp.concatenate([vp[:, :A_DV].T, vp[:, A_DV:].T], axis=1).astype(BF16)
        s_scr[p] = st * decay[:, sl] + _dot(vt, bdiag(k_tail[:, sl]))
        for hh in range(2):
            hs = slice(p * pw + hh * A_DV, p * pw + (hh + 1) * A_DV)
            oa_ref[0, :, hs] = (_rms_rows(o[:, hh * A_DV:(hh + 1) * A_DV], gn) * gate[:, hs]).astype(BF16)

    @pl.when(c_idx == pl.num_programs(2) - 1)
    def _():
        for hh in range(heads):
            st_ref[0, hh] = s_scr[hh // 2][:, (hh % 2) * A_DK:(hh % 2 + 1) * A_DK].T


def _hgrn_prompt(u3, lbf, oml, gn):
    bsz, length, _ = u3.shape
    chunk, heads = HGRN_CHUNK, HGRN_HEADS_PER_STEP
    cols = heads * A_DK
    assert length % chunk == 0 and A_HEADS % heads == 0 and heads % 2 == 0 and chunk == A_DV
    p2, masks = _hgrn_static(chunk)
    p2 = jnp.asarray(p2, BF16)
    masks = jnp.asarray(masks, F32)

    def col_spec(offset):
        base = offset // cols
        return pl.BlockSpec((1, chunk, cols), lambda b, h, c: (b, c, base + h))

    vec_spec = pl.BlockSpec((1, cols), lambda b, h, c: (0, h))
    nbytes = (2 * 5 * chunk * cols * 4 + 2 * (p2.size * 2 + masks.size * 4)
              + 6 * p2.shape[0] * cols * 4)
    return pl.pallas_call(
        functools.partial(_hgrn_prompt_kernel, chunk=chunk, heads=heads),
        grid=(bsz, A_HEADS // heads, length // chunk),
        in_specs=[col_spec(Q_A), col_spec(F_A), col_spec(I_A), col_spec(G_A),
                  vec_spec, vec_spec,
                  pl.BlockSpec((1, A_DV), lambda b, h, c: (0, 0)),
                  pl.BlockSpec(p2.shape, lambda b, h, c: (0, 0)),
                  pl.BlockSpec(masks.shape, lambda b, h, c: (0, 0, 0))],
        out_specs=(pl.BlockSpec((1, chunk, cols), lambda b, h, c: (b, c, h)),
                   pl.BlockSpec((1, heads, A_DK, A_DV), lambda b, h, c: (b, h, 0, 0))),
        out_shape=(jax.ShapeDtypeStruct((bsz, length, A_WIDTH), BF16),
                   jax.ShapeDtypeStruct((bsz, A_HEADS, A_DK, A_DV), F32)),
        scratch_shapes=[pltpu.VMEM((heads // 2, A_DV, 2 * A_DK), F32)],
        compiler_params=_params(("parallel", "parallel", "arbitrary"), nbytes),
        name="hgrn_prompt",
    )(u3, u3, u3, u3, lbf, oml, gn.reshape(1, A_DV), p2, masks)


def _hgrn_sample_kernel(u_ref, lbf_ref, oml_ref, gn_ref, s_ref, oa_ref, so_ref, *, steps):
    u = u_ref[0]
    q = _silu(u[:, Q_A:Q_A + A_KEY_WIDTH])
    sz, snz = _sig_pair(u[:, F_A:F_A + A_KEY_WIDTH])
    oml = oml_ref[...]
    f = lbf_ref[...] + oml * sz
    ka = oml * snz
    v = u[:, I_A:I_A + A_WIDTH]
    gate = _silu(u[:, G_A:G_A + A_WIDTH])
    gn = gn_ref[...]
    rows = u.shape[0]
    row = lax.broadcasted_iota(jnp.int32, f.shape, 0)
    live = row < steps
    f = jnp.where(live, f, 1.0)
    ka = jnp.where(live, ka, 0.0)

    def down(x, d, fill):
        return x if d == 0 else jnp.where(row >= d, pltpu.roll(x, d, axis=0), fill)

    def up(x, d, fill):
        return jnp.where(row < rows - d, pltpu.roll(x, rows - d, axis=0), fill)

    decay = [jnp.ones_like(f)]
    for d in range(1, steps + 1):
        decay.append(decay[-1] * down(f, d - 1, 1.0))
    head = decay[steps]
    tail = jnp.ones_like(f)
    for d in range(1, steps):
        tail = tail * up(f, d, 1.0)
    q_in = (q * head).astype(BF16)
    k_tail = ka * tail
    total = head[steps - 1:steps, :]
    t_hi = total.astype(BF16).astype(F32)
    t_mid = (total - t_hi).astype(BF16).astype(F32)
    t_lo = (total - t_hi - t_mid).astype(BF16).astype(F32)
    t_rows = jnp.where(row == 0, t_hi, jnp.where(row == 1, t_mid, jnp.where(row == 2, t_lo, 0.0)))
    prods = [q * down(ka, d, 0.0) * decay[d] for d in range(steps)]
    v_sh = [down(v, d, 0.0) for d in range(steps)]

    zsq = jnp.zeros((A_DK, A_DV), BF16)
    zrows = jnp.zeros((rows, A_DV), F32)
    r2 = lax.broadcasted_iota(jnp.int32, (rows, 2 * A_DV), 0)
    l2 = lax.broadcasted_iota(jnp.int32, (rows, 2 * A_DV), 1)
    ones_right = jnp.where((r2 < 3) & (l2 >= A_DV), 1.0, 0.0)

    for p in range(A_HEADS // 2):
        ps = slice(2 * p * A_DK, (2 * p + 2) * A_DK)
        s0 = [s_ref[0, 2 * p + hh] for hh in range(2)]
        s_bd = jnp.concatenate([jnp.concatenate([s0[0].astype(BF16), zsq], axis=1),
                                jnp.concatenate([zsq, s0[1].astype(BF16)], axis=1)], axis=0)
        o_pair = _dot(q_in[:, ps], s_bd)
        for hh in range(2):
            head_i = 2 * p + hh
            sl = slice(head_i * A_DK, (head_i + 1) * A_DK)
            o = o_pair[:, hh * A_DV:(hh + 1) * A_DV]
            for d in range(steps):
                o = o + jnp.sum(prods[d][:, sl], axis=-1, keepdims=True) * v_sh[d][:, sl]
            lhs = jnp.concatenate([k_tail[:, sl], t_rows[:, sl]], axis=0).astype(BF16)
            rhs = jnp.concatenate([jnp.concatenate([v[:, sl], zrows], axis=1), ones_right],
                                  axis=0).astype(BF16)
            upd = lax.dot_general(lhs, rhs, (((0,), (0,)), ((), ())), preferred_element_type=F32)
            so_ref[0, head_i] = s0[hh] * upd[:, A_DV:] + upd[:, :A_DV]
            oa_ref[0, :, sl] = (_rms_rows(o, gn) * gate[:, sl]).astype(BF16)


def _hgrn_sample(u3, lbf, oml, gn, state, layer, steps):
    bsz, rows, _ = u3.shape
    width = G_A + A_WIDTH
    nbytes = 2 * rows * width * 4 + 4 * A_HEADS * A_DK * A_DV * 4
    return pl.pallas_call(
        functools.partial(_hgrn_sample_kernel, steps=steps),
        grid=(bsz,),
        in_specs=[pl.BlockSpec((1, rows, width), lambda b: (b, 0, 0)),
                  pl.BlockSpec((1, A_KEY_WIDTH), lambda b: (0, 0)),
                  pl.BlockSpec((1, A_KEY_WIDTH), lambda b: (0, 0)),
                  pl.BlockSpec((1, A_DV), lambda b: (0, 0)),
                  pl.BlockSpec((None, 1, A_HEADS, A_DK, A_DV), lambda b: (layer, b, 0, 0, 0))],
        out_specs=(pl.BlockSpec((1, rows, A_WIDTH), lambda b: (b, 0, 0)),
                   pl.BlockSpec((1, A_HEADS, A_DK, A_DV), lambda b: (b, 0, 0, 0))),
        out_shape=(jax.ShapeDtypeStruct((bsz, rows, A_WIDTH), BF16),
                   jax.ShapeDtypeStruct(state.shape[1:], F32)),
        compiler_params=_params(("parallel",), nbytes),
        name="hgrn_sample",
    )(u3, lbf, oml, gn.reshape(1, A_DV), state)


def _t5_bucket(dist):
    n = np.maximum(dist, 0)
    max_exact = N_BUCKETS // 2
    nf = np.maximum(n, 1).astype(np.float32)
    large = max_exact + (np.log(nf / max_exact) / math.log(MAX_DISTANCE / max_exact)
                         * (N_BUCKETS - max_exact)).astype(np.int32)
    large = np.minimum(large, N_BUCKETS - 1)
    return np.where(n < max_exact, n, large).astype(np.int32)


def _bias_kernel(tab_ref, bucket_ref, o_ref):
    rows = bucket_ref.shape[0]
    bucket = bucket_ref[...]
    hits = [bucket == b for b in range(N_BUCKETS)]
    for h in range(B_HEADS):
        acc = jnp.zeros(bucket.shape, F32)
        for b in range(N_BUCKETS):
            acc = jnp.where(hits[b], tab_ref[b, h], acc)
        j, r, c = h // 4, (h % 4) // 2, h % 2
        o_ref[j, r * rows:(r + 1) * rows, c * KEY_SLOTS:(c + 1) * KEY_SLOTS] = acc


def _bias_layout(table, dist):
    rows = dist.shape[0]
    return pl.pallas_call(
        _bias_kernel,
        in_specs=[pl.BlockSpec(memory_space=pltpu.SMEM),
                  pl.BlockSpec(memory_space=pltpu.VMEM)],
        out_shape=jax.ShapeDtypeStruct((B_KV_HEADS, 2 * rows, 2 * KEY_SLOTS), F32),
        name="bias_prep",
    )(table.astype(F32), jnp.asarray(_t5_bucket(dist), jnp.int32))


def _tile2(a):
    return np.tile(a, (2, 2))


def _block_diag_ones():
    r = lax.broadcasted_iota(jnp.int32, (LANES, LANES), 0) // B_HD
    c = lax.broadcasted_iota(jnp.int32, (LANES, LANES), 1) // B_HD
    return (r == c).astype(BF16)


def _head_norm(x, g2, bd):
    hi, lo = _split_bf16(x * x)
    ss = _dot(hi, bd) + _dot(lo, bd)
    return x * lax.rsqrt(ss * (1.0 / B_HD) + RMS_EPS) * g2


def _place(tile, half, lo_mask):
    rolled = pltpu.roll(tile, B_HD, axis=1)
    zero = jnp.zeros_like(tile)
    if half == 0:
        return jnp.where(lo_mask, tile, zero), jnp.where(lo_mask, zero, rolled)
    return jnp.where(lo_mask, rolled, zero), jnp.where(lo_mask, zero, tile)


def _swa_core(qn_tiles, k_tiles, v_tiles, bias_ref, valid, sink_ref, ob_ref):
    rows = qn_tiles[0].shape[0]
    lo_mask = lax.broadcasted_iota(jnp.int32, k_tiles[0].shape, 1) < B_HD
    for j in range(B_KV_HEADS):
        t, half = j // 2, j % 2
        k_lo, k_hi = _place(k_tiles[t], half, lo_mask)
        v_lo, v_hi = _place(v_tiles[t], half, lo_mask)
        kk = jnp.concatenate([k_lo, k_hi], axis=0).astype(BF16)
        vv = jnp.concatenate([v_lo, v_hi], axis=0).astype(BF16)
        qq = jnp.concatenate([qn_tiles[2 * j], qn_tiles[2 * j + 1]], axis=0).astype(BF16)
        s = _dot_nt(qq, kk) * ATTN_SCALE + bias_ref[j]
        s = jnp.where(valid, s, NEG_LOGIT)
        p_rows = []
        for r in range(2):
            p_cols = []
            for c in range(2):
                sk = sink_ref[4 * j + 2 * r + c]
                sb = s[r * rows:(r + 1) * rows, c * KEY_SLOTS:(c + 1) * KEY_SLOTS]
                m = jnp.maximum(jnp.max(sb, axis=-1, keepdims=True), sk)
                e = jnp.exp(sb - m)
                p_cols.append(e / (jnp.sum(e, axis=-1, keepdims=True) + jnp.exp(sk - m)))
            p_rows.append(jnp.concatenate(p_cols, axis=1))
        p = jnp.concatenate(p_rows, axis=0).astype(BF16)
        o = _dot(p, vv)
        ob_ref[0, :, (2 * j) * LANES:(2 * j + 1) * LANES] = o[0:rows].astype(BF16)
        ob_ref[0, :, (2 * j + 1) * LANES:(2 * j + 2) * LANES] = o[rows:].astype(BF16)


def _swa_prompt_kernel(sink_ref, q_ref, kc_ref, kp_ref, vc_ref, vp_ref, qg_ref, kg_ref, bias_ref, code_ref,
                       ob_ref, ko_ref, vo_ref):
    n = pl.program_id(1)
    bd = _block_diag_ones()
    k_all = jnp.concatenate([kp_ref[0], kc_ref[0]], axis=0)
    v_all = jnp.concatenate([vp_ref[0], vc_ref[0]], axis=0)
    kg = kg_ref[...]
    k_tiles = [_head_norm(k_all[:, t * LANES:(t + 1) * LANES], kg, bd) for t in range(2)]
    v_tiles = [v_all[:, t * LANES:(t + 1) * LANES] for t in range(2)]
    q = q_ref[0]
    qg = qg_ref[...]
    qn_tiles = [_head_norm(q[:, t * LANES:(t + 1) * LANES], qg, bd) for t in range(B_WIDTH // LANES)]
    code = code_ref[...]
    valid = (code >= 1) & (code <= jnp.where(n > 0, 2, 1))
    _swa_core(qn_tiles, k_tiles, v_tiles, bias_ref, valid, sink_ref, ob_ref)

    @pl.when(n == pl.num_programs(1) - 1)
    def _():
        ko_ref[0] = jnp.concatenate([kt[WINDOW:] for kt in k_tiles], axis=1)
        vo_ref[0] = vc_ref[0]


def _prompt_dist():
    return WINDOW + np.arange(WINDOW)[:, None] - np.arange(KEY_SLOTS)[None, :]


def _swa_prompt(u3, sinks, qg, kg, bias):
    bsz, length, _ = u3.shape
    nb = length // WINDOW
    j = np.arange(KEY_SLOTS)[None, :]
    dist = _prompt_dist()
    code = np.where((j >= WINDOW) & (dist >= 0), 1, np.where((j < WINDOW) & (dist < WINDOW), 2, 0))
    code = jnp.asarray(_tile2(code), jnp.int32)
    qb, kb, vb = Q_B // B_WIDTH, K_B // B_KV_WIDTH, V_B // B_KV_WIDTH
    assert Q_B % B_WIDTH == 0 and K_B % B_KV_WIDTH == 0 and V_B % B_KV_WIDTH == 0
    kv_spec = lambda col, prev: pl.BlockSpec(
        (1, WINDOW, B_KV_WIDTH),
        (lambda b, n: (b, jnp.maximum(n - 1, 0), col)) if prev else (lambda b, n: (b, n, col)))
    g_spec = pl.BlockSpec((1, LANES), lambda b, n: (0, 0))
    nbytes = (2 * WINDOW * (B_WIDTH + 4 * B_KV_WIDTH) * 4 + 2 * (bias.size + code.size) * 4
              + 2 * WINDOW * B_WIDTH * 2 + 24 * WINDOW * 2 * KEY_SLOTS * 4)
    return pl.pallas_call(
        _swa_prompt_kernel,
        grid=(bsz, nb),
        in_specs=[pl.BlockSpec(memory_space=pltpu.SMEM),
                  pl.BlockSpec((1, WINDOW, B_WIDTH), lambda b, n: (b, n, qb)),
                  kv_spec(kb, False), kv_spec(kb, True), kv_spec(vb, False), kv_spec(vb, True),
                  g_spec, g_spec,
                  pl.BlockSpec(bias.shape, lambda b, n: (0, 0, 0)),
                  pl.BlockSpec(code.shape, lambda b, n: (0, 0))],
        out_specs=(pl.BlockSpec((1, WINDOW, B_WIDTH), lambda b, n: (b, n, 0)),
                   pl.BlockSpec((1, WINDOW, B_KV_WIDTH), lambda b, n: (b, 0, 0)),
                   pl.BlockSpec((1, WINDOW, B_KV_WIDTH), lambda b, n: (b, 0, 0))),
        out_shape=(jax.ShapeDtypeStruct((bsz, length, B_WIDTH), BF16),
                   jax.ShapeDtypeStruct((bsz, WINDOW, B_KV_WIDTH), F32),
                   jax.ShapeDtypeStruct((bsz, WINDOW, B_KV_WIDTH), F32)),
        compiler_params=_params(("parallel", "arbitrary"), nbytes),
        name="swa_prompt",
    )(sinks, u3, u3, u3, u3, u3, qg, kg, bias, code)


def _swa_sample_kernel(sink_ref, q_ref, kv_ref, ck_ref, cv_ref, qg_ref, kg_ref, bias_ref, code_ref,
                       ob_ref, ko_ref, vo_ref, *, steps):
    bd = _block_diag_ones()
    rows = q_ref.shape[1]
    kv = kv_ref[0]
    kg = kg_ref[...]
    k_new = [_head_norm(kv[:, t * LANES:(t + 1) * LANES], kg, bd) for t in range(2)]
    v_new = [kv[:, B_KV_WIDTH + t * LANES:B_KV_WIDTH + (t + 1) * LANES] for t in range(2)]
    ck, cv = ck_ref[0], cv_ref[0]
    pad = jnp.zeros((KEY_SLOTS - WINDOW - rows, LANES), F32)
    k_tiles = [jnp.concatenate([ck[:, t * LANES:(t + 1) * LANES], k_new[t], pad], axis=0) for t in range(2)]
    v_tiles = [jnp.concatenate([cv[:, t * LANES:(t + 1) * LANES], v_new[t], pad], axis=0) for t in range(2)]
    q = q_ref[0]
    qg = qg_ref[...]
    qn_tiles = [_head_norm(q[:, t * LANES:(t + 1) * LANES], qg, bd) for t in range(B_WIDTH // LANES)]
    valid = code_ref[...] >= 1
    _swa_core(qn_tiles, k_tiles, v_tiles, bias_ref, valid, sink_ref, ob_ref)
    ko_ref[0, 0:WINDOW - steps, :] = ck_ref[0, steps:WINDOW, :]
    vo_ref[0, 0:WINDOW - steps, :] = cv_ref[0, steps:WINDOW, :]
    ko_ref[0, WINDOW - steps:WINDOW, :] = jnp.concatenate([kt[0:steps] for kt in k_new], axis=1)
    vo_ref[0, WINDOW - steps:WINDOW, :] = kv[0:steps, B_KV_WIDTH:]


def _sample_dist(rows):
    return WINDOW + np.arange(rows)[:, None] - np.arange(KEY_SLOTS)[None, :]


def _swa_sample(u3, cache_k, cache_v, layer, sinks, qg, kg, bias, steps):
    bsz, rows, _ = u3.shape
    w = cache_k.shape[2]
    assert w == WINDOW
    j = np.arange(KEY_SLOTS)[None, :]
    dist = _sample_dist(rows)
    code = np.where((dist >= 0) & (dist < WINDOW) & (j < w + steps), 1, 0)
    code = jnp.asarray(_tile2(code), jnp.int32)
    qb, kvb = Q_B // B_WIDTH, K_B // (2 * B_KV_WIDTH)
    assert K_B % (2 * B_KV_WIDTH) == 0
    c_in_spec = pl.BlockSpec((None, 1, w, B_KV_WIDTH), lambda b: (layer, b, 0, 0))
    c_spec = pl.BlockSpec((1, w, B_KV_WIDTH), lambda b: (b, 0, 0))
    g_spec = pl.BlockSpec((1, LANES), lambda b: (0, 0))
    nbytes = 8 * w * B_KV_WIDTH * 4 + 2 * (bias.size + code.size) * 4 + 16 * KEY_SLOTS * LANES * 4
    return pl.pallas_call(
        functools.partial(_swa_sample_kernel, steps=steps),
        grid=(bsz,),
        in_specs=[pl.BlockSpec(memory_space=pltpu.SMEM),
                  pl.BlockSpec((1, rows, B_WIDTH), lambda b: (b, 0, qb)),
                  pl.BlockSpec((1, rows, 2 * B_KV_WIDTH), lambda b: (b, 0, kvb)),
                  c_in_spec, c_in_spec, g_spec, g_spec,
                  pl.BlockSpec(bias.shape, lambda b: (0, 0, 0)),
                  pl.BlockSpec(code.shape, lambda b: (0, 0))],
        out_specs=(pl.BlockSpec((1, rows, B_WIDTH), lambda b: (b, 0, 0)), c_spec, c_spec),
        out_shape=(jax.ShapeDtypeStruct((bsz, rows, B_WIDTH), BF16),
                   jax.ShapeDtypeStruct((bsz, w, B_KV_WIDTH), F32),
                   jax.ShapeDtypeStruct((bsz, w, B_KV_WIDTH), F32)),
        compiler_params=_params(("parallel",), nbytes),
        name="swa_sample",
    )(sinks, u3, u3, cache_k, cache_v, qg, kg, bias, code)


def kernel(x_prompt, x_sample, cache_k, cache_v, state_hgrn, norm_mix, w_in, lower_bounds, hgrn_norm,
           q_norm, k_norm, attn_sinks, rel_bias_table, w_out, norm_ffn, w_gate, w_up, w_down):
    depth = w_in.shape[0]
    bp, seq, d = x_prompt.shape
    bd, ld, _ = x_sample.shape
    w = cache_k.shape[2]
    in_width = w_in.shape[2]

    w_in_b, w_out_b = w_in.astype(BF16), w_out.astype(BF16)
    w_gate_b, w_up_b, w_down_b = w_gate.astype(BF16), w_up.astype(BF16), w_down.astype(BF16)
    lbf, oml = _lower_bounds(lower_bounds)
    sinks = attn_sinks.astype(F32)
    qg2 = jnp.tile(q_norm.astype(F32), (1, LANES // B_HD))
    kg2 = jnp.tile(k_norm.astype(F32), (1, LANES // B_HD))
    ck = cache_k.astype(F32).reshape(depth, bd, w, B_KV_WIDTH)
    cv = cache_v.astype(F32).reshape(depth, bd, w, B_KV_WIDTH)

    xp = x_prompt.reshape(bp * seq, d)
    xs = jnp.pad(x_sample, ((0, 0), (0, SAMPLE_ROWS - ld), (0, 0))).reshape(bd * SAMPLE_ROWS, d)
    tm_p, tm_s = 1024, bd * SAMPLE_ROWS
    bias_p = _bias_layout(rel_bias_table, _prompt_dist())
    bias_s = _bias_layout(rel_bias_table, _sample_dist(SAMPLE_ROWS))
    state = state_hgrn.astype(F32)

    pk, pv, ps, sk, sv, ss = [], [], [], [], [], []
    for l in range(depth):
        lbf_l, oml_l = lbf[l:l + 1], oml[l:l + 1]
        u = _norm_matmul(xp, norm_mix[l], w_in_b, l, tm_p, 512).reshape(bp, seq, in_width)
        oa, st = _hgrn_prompt(u, lbf_l, oml_l, hgrn_norm[l])
        ob, kn, vn = _swa_prompt(u, sinks[l], qg2[l:l + 1], kg2[l:l + 1], bias_p)
        xp = _outproj(xp, oa.reshape(bp * seq, A_WIDTH), ob.reshape(bp * seq, B_WIDTH), w_out_b, l, 512)
        xp = _ffn(xp, norm_ffn[l], w_gate_b, w_up_b, w_down_b, l, tm_p, 256)
        pk.append(kn)
        pv.append(vn)
        ps.append(st)
        u = _norm_matmul(xs, norm_mix[l], w_in_b, l, tm_s, 512).reshape(bd, SAMPLE_ROWS, in_width)
        oa, st = _hgrn_sample(u, lbf_l, oml_l, hgrn_norm[l], state, l, ld)
        ob, kn, vn = _swa_sample(u, ck, cv, l, sinks[l], qg2[l:l + 1], kg2[l:l + 1], bias_s, ld)
        xs = _outproj(xs, oa.reshape(tm_s, A_WIDTH), ob.reshape(tm_s, B_WIDTH), w_out_b, l, tm_s)
        xs = _ffn(xs, norm_ffn[l], w_gate_b, w_up_b, w_down_b, l, tm_s, 512)
        sk.append(kn)
        sv.append(vn)
        ss.append(st)

    kv_shape = lambda n: (depth, n, WINDOW, B_KV_HEADS, B_HD)
    return (xp.reshape(bp, seq, d),
            xs.reshape(bd, SAMPLE_ROWS, d)[:, :ld],
            jnp.stack(pk).reshape(kv_shape(bp)), jnp.stack(pv).reshape(kv_shape(bp)), jnp.stack(ps),
            jnp.stack(sk).reshape(kv_shape(bd)), jnp.stack(sv).reshape(kv_shape(bd)), jnp.stack(ss))
```

```python
import functools
import math

import jax
import jax.numpy as jnp
import numpy as np
from jax import lax
from jax.experimental import pallas as pl
from jax.experimental.pallas import tpu as pltpu

F32 = jnp.float32
BF16 = jnp.bfloat16

LANES = 128
SUBLANES = 8
VMEM_BYTES_V7X = 64 * 1024 * 1024
VMEM_CAP = VMEM_BYTES_V7X - 8 * 1024 * 1024

A_HEADS = 8
A_DK = 128
A_DV = 128
A_KEY_WIDTH = A_HEADS * A_DK
A_WIDTH = A_HEADS * A_DV
B_HEADS = 16
B_KV_HEADS = 4
B_HD = 64
B_WIDTH = B_HEADS * B_HD
B_KV_WIDTH = B_KV_HEADS * B_HD
WINDOW = 128
ATTN_SCALE = 1.0 / math.sqrt(B_HD)
LOG2E = math.log2(math.e)
NEG_LOGIT = -1e30
LB_FLOOR = 1e-20
N_BUCKETS = 32
MAX_DISTANCE = 128
RMS_EPS = 1e-6

Q_A, F_A, I_A, G_A = 0, A_KEY_WIDTH, 2 * A_KEY_WIDTH, 2 * A_KEY_WIDTH + A_WIDTH
Q_B = 2 * A_KEY_WIDTH + 2 * A_WIDTH
K_B = Q_B + B_WIDTH
V_B = K_B + B_KV_WIDTH

HGRN_CHUNK = 128
HGRN_FINE = 4
HGRN_HEADS_PER_STEP = 4
SAMPLE_ROWS = SUBLANES
SWA_SAMPLE_SEQS = 4
KEY_SLOTS = 2 * WINDOW


def _vmem_limit(nbytes):
    return int(min(VMEM_CAP, nbytes * 5 // 4 + (4 << 20)))


def _params(sem, nbytes):
    return pltpu.CompilerParams(dimension_semantics=sem, vmem_limit_bytes=_vmem_limit(nbytes))


def _sig_pair(z):
    t = jnp.exp(-jnp.abs(z))
    r = 1.0 / (1.0 + t)
    tr = t * r
    pos = z >= 0
    return jnp.where(pos, r, tr), jnp.where(pos, tr, r)


def _silu(x):
    return x * _sig_pair(x)[0]


def _rms_rows(x, g):
    ms = jnp.mean(x * x, axis=-1, keepdims=True)
    return x * lax.rsqrt(ms + RMS_EPS) * g


def _split_bf16(x):
    hi = x.astype(BF16)
    lo = (x - hi.astype(F32)).astype(BF16)
    return hi, lo


def _dot(a, b):
    return jnp.dot(a, b, preferred_element_type=F32)


def _dot_nt(a, b):
    return lax.dot_general(a, b, (((1,), (1,)), ((), ())), preferred_element_type=F32)


def _lb_kernel(lb_ref, lbf_ref, oml_ref):
    depth = lb_ref.shape[0]
    rows = [lb_ref[i:i + 1, :] for i in range(depth)]
    m = functools.reduce(jnp.maximum, rows)
    e = [jnp.exp(r - m) for r in rows]
    s = functools.reduce(lambda a, b: a + b, e)
    p = [ei / s for ei in e]
    cum = p[0]
    for i in range(depth):
        if i > 0:
            cum = cum + p[i]
        lb = cum - p[0]
        lbf_ref[i:i + 1, :] = jnp.maximum(lb, LB_FLOOR)
        oml_ref[i:i + 1, :] = 1.0 - lb


def _lower_bounds(lower_bounds):
    shp = jax.ShapeDtypeStruct(lower_bounds.shape, F32)
    return pl.pallas_call(_lb_kernel, out_shape=(shp, shp), name="lb_prep")(lower_bounds.astype(F32))


def _norm_matmul_kernel(x_ref, g_ref, w_ref, o_ref, h_ref):
    @pl.when(pl.program_id(1) == 0)
    def _():
        h_ref[...] = _rms_rows(x_ref[...], g_ref[...]).astype(BF16)

    o_ref[...] = _dot(h_ref[...], w_ref[...])


def _norm_matmul(x, g, w, layer, tm, tn):
    m, d = x.shape
    n = w.shape[2]
    nbytes = 2 * tm * d * 4 + tm * d * 2 + 2 * d * tn * 2 + 2 * tm * tn * 4
    return pl.pallas_call(
        _norm_matmul_kernel,
        grid=(m // tm, n // tn),
        in_specs=[pl.BlockSpec((tm, d), lambda i, j: (i, 0)),
                  pl.BlockSpec((1, d), lambda i, j: (0, 0)),
                  pl.BlockSpec((None, d, tn), lambda i, j: (layer, 0, j))],
        out_specs=pl.BlockSpec((tm, tn), lambda i, j: (i, j)),
        out_shape=jax.ShapeDtypeStruct((m, n), F32),
        scratch_shapes=[pltpu.VMEM((tm, d), BF16)],
        compiler_params=_params(("parallel", "arbitrary"), nbytes),
        name="norm_matmul",
    )(x, g.reshape(1, d), w)


def _outproj_kernel(x_ref, oa_ref, ob_ref, w_ref, o_ref):
    ka = oa_ref.shape[1]
    acc = _dot(oa_ref[...], w_ref[0:ka, :])
    acc = acc + _dot(ob_ref[...], w_ref[ka:, :])
    o_ref[...] = x_ref[...] + acc


def _outproj(x, oa, ob, w, layer, tm):
    m, d = x.shape
    ka, kb = oa.shape[1], ob.shape[1]
    nbytes = 4 * tm * d * 4 + 2 * tm * (ka + kb) * 2 + 2 * (ka + kb) * d * 2
    return pl.pallas_call(
        _outproj_kernel,
        grid=(m // tm,),
        in_specs=[pl.BlockSpec((tm, d), lambda i: (i, 0)),
                  pl.BlockSpec((tm, ka), lambda i: (i, 0)),
                  pl.BlockSpec((tm, kb), lambda i: (i, 0)),
                  pl.BlockSpec((None, ka + kb, d), lambda i: (layer, 0, 0))],
        out_specs=pl.BlockSpec((tm, d), lambda i: (i, 0)),
        out_shape=jax.ShapeDtypeStruct((m, d), F32),
        compiler_params=_params(("parallel",), nbytes),
        name="outproj",
    )(x, oa, ob, w)


def _ffn_kernel(x_ref, g_ref, wg_ref, wu_ref, wd_ref, o_ref, h_ref):
    @pl.when(pl.program_id(1) == 0)
    def _():
        x = x_ref[...]
        h_ref[...] = _rms_rows(x, g_ref[...]).astype(BF16)
        o_ref[...] = x

    h = h_ref[...]
    act = (_silu(_dot(h, wg_ref[...])) * _dot(h, wu_ref[...])).astype(BF16)
    o_ref[...] += _dot(act, wd_ref[...])


def _ffn(x, g, wg, wu, wd, layer, tm, tf):
    m, d = x.shape
    f = wg.shape[2]
    nbytes = 4 * tm * d * 4 + tm * d * 2 + 2 * 3 * d * tf * 2 + 3 * tm * tf * 4
    return pl.pallas_call(
        _ffn_kernel,
        grid=(m // tm, f // tf),
        in_specs=[pl.BlockSpec((tm, d), lambda i, j: (i, 0)),
                  pl.BlockSpec((1, d), lambda i, j: (0, 0)),
                  pl.BlockSpec((None, d, tf), lambda i, j: (layer, 0, j)),
                  pl.BlockSpec((None, d, tf), lambda i, j: (layer, 0, j)),
                  pl.BlockSpec((None, tf, d), lambda i, j: (layer, j, 0))],
        out_specs=pl.BlockSpec((tm, d), lambda i, j: (i, 0)),
        out_shape=jax.ShapeDtypeStruct((m, d), F32),
        scratch_shapes=[pltpu.VMEM((tm, d), BF16)],
        compiler_params=_params(("parallel", "arbitrary"), nbytes),
        name="ffn",
    )(x, g.reshape(1, d), wg, wu, wd)


def _hgrn_static(c):
    t = np.arange(c)
    blocks = [(t[None, :] <= t[:, None])]
    masks = []
    h = c // 2
    while h >= 1:
        grp = t // (2 * h)
        mid = grp * 2 * h + h
        upper = (t % (2 * h)) >= h
        if h < HGRN_FINE:
            col = t[None, :]
            p_up = (col >= mid[:, None]) & (col <= t[:, None])
            p_lo = (col > t[:, None]) & (col < mid[:, None])
            blocks.append(np.where(upper[:, None], p_up, p_lo))
        masks.append((grp[:, None] == grp[None, :]) & upper[:, None] & ~upper[None, :])
        h //= 2
    masks.append(np.eye(c, dtype=bool))
    p_all = np.concatenate(blocks, axis=0).astype(np.float32)
    p2 = np.concatenate([p_all, p_all], axis=1)
    masks = np.stack(masks).astype(np.float32)
    return p2, np.concatenate([masks, masks], axis=2)


def _hgrn_prompt_kernel(q_ref, f_ref, i_ref, g_ref, lbf_ref, oml_ref, gn_ref, p2_ref, m_ref,
                        oa_ref, st_ref, s_scr, *, chunk, heads):
    c_idx = pl.program_id(2)

    @pl.when(c_idx == 0)
    def _():
        s_scr[...] = jnp.zeros_like(s_scr)

    width = heads * A_DK
    pw = 2 * A_DK
    sz, snz = _sig_pair(f_ref[0])
    oml = oml_ref[...]
    lf = jnp.log(lbf_ref[...] + oml * sz)
    ka = oml * snz
    hi, lo = _split_bf16(lf)
    expo = _dot(p2_ref[...], jnp.concatenate([hi, lo], axis=0))
    b = expo[0:chunk]
    q = _silu(q_ref[0])
    v = i_ref[0]
    gate = _silu(g_ref[0])
    gn = gn_ref[...]

    row = lax.broadcasted_iota(jnp.int32, (chunk, width), 0)
    qs, ks = [], []
    h, fine = chunk // 2, 0
    while h >= 1:
        if h >= HGRN_FINE:
            groups = chunk // (2 * h)
            mid = jnp.broadcast_to(b.reshape(groups, 2 * h, width)[:, h - 1:h, :], (groups, 2 * h, width))
            d = b - mid.reshape(chunk, width)
            e = jnp.where((row & h) != 0, d, -d)
        else:
            fine += 1
            e = expo[fine * chunk:(fine + 1) * chunk]
        e = jnp.exp(e)
        qs.append((q * e).astype(BF16))
        ks.append((ka * e).astype(BF16))
        h //= 2
    qs.append(q.astype(BF16))
    ks.append(ka.astype(BF16))
    b_last = b[chunk - 1:chunk, :]
    q0 = (q * jnp.exp(b)).astype(BF16)
    k_tail = (ka * jnp.exp(b_last - b)).astype(BF16)
    decay = jnp.exp(b_last)
    vb = v.astype(BF16)

    left = lax.broadcasted_iota(jnp.int32, (chunk, pw), 1) < A_DK

    def bdiag(x):
        z = jnp.zeros_like(x)
        return jnp.concatenate([jnp.where(left, x, z), jnp.where(left, z, x)], axis=0)

    pairs = range(heads // 2)
    sls = [slice(p * pw, (p + 1) * pw) for p in pairs]
    sts = [s_scr[p] for p in pairs]
    outs = [_dot_nt(q0[:, sls[p]], bdiag(sts[p].astype(BF16))) for p in pairs]
    accs = [jnp.zeros((chunk, pw), F32) for _ in pairs]
    for li in range(len(qs)):
        for p in pairs:
            accs[p] = accs[p] + _dot_nt(qs[li][:, sls[p]], bdiag(ks[li][:, sls[p]])) * m_ref[li]
    for p in pairs:
        outs[p] = outs[p] + _dot(accs[p].astype(BF16), bdiag(vb[:, sls[p]]))
    for p in pairs:
        vp = v[:, sls[p]]
        vt = jnp.concatenate([vp[:, :A_DV].T, vp[:, A_DV:].T], axis=1).astype(BF16)
        s_scr[p] = sts[p] * decay[:, sls[p]] + _dot(vt, bdiag(k_tail[:, sls[p]]))
    for p in pairs:
        for hh in range(2):
            hs = slice(p * pw + hh * A_DV, p * pw + (hh + 1) * A_DV)
            o_h = outs[p][:, hh * A_DV:(hh + 1) * A_DV]
            oa_ref[0, :, hs] = (_rms_rows(o_h, gn) * gate[:, hs]).astype(BF16)

    @pl.when(c_idx == pl.num_programs(2) - 1)
    def _():
        for hh in range(heads):
            st_ref[0, hh] = s_scr[hh // 2][:, (hh % 2) * A_DK:(hh % 2 + 1) * A_DK].T


def _hgrn_prompt(u3, lbf, oml, gn):
    bsz, length, _ = u3.shape
    chunk, heads = HGRN_CHUNK, HGRN_HEADS_PER_STEP
    cols = heads * A_DK
    assert length % chunk == 0 and A_HEADS % heads == 0 and heads % 2 == 0 and chunk == A_DV
    p2, masks = _hgrn_static(chunk)
    p2 = jnp.asarray(p2, BF16)
    masks = jnp.asarray(masks, F32)

    def col_spec(offset):
        base = offset // cols
        return pl.BlockSpec((1, chunk, cols), lambda b, h, c: (b, c, base + h))

    vec_spec = pl.BlockSpec((1, cols), lambda b, h, c: (0, h))
    nbytes = (2 * 5 * chunk * cols * 4 + 2 * (p2.size * 2 + masks.size * 4)
              + 6 * p2.shape[0] * cols * 4)
    return pl.pallas_call(
        functools.partial(_hgrn_prompt_kernel, chunk=chunk, heads=heads),
        grid=(bsz, A_HEADS // heads, length // chunk),
        in_specs=[col_spec(Q_A), col_spec(F_A), col_spec(I_A), col_spec(G_A),
                  vec_spec, vec_spec,
                  pl.BlockSpec((1, A_DV), lambda b, h, c: (0, 0)),
                  pl.BlockSpec(p2.shape, lambda b, h, c: (0, 0)),
                  pl.BlockSpec(masks.shape, lambda b, h, c: (0, 0, 0))],
        out_specs=(pl.BlockSpec((1, chunk, cols), lambda b, h, c: (b, c, h)),
                   pl.BlockSpec((1, heads, A_DK, A_DV), lambda b, h, c: (b, h, 0, 0))),
        out_shape=(jax.ShapeDtypeStruct((bsz, length, A_WIDTH), BF16),
                   jax.ShapeDtypeStruct((bsz, A_HEADS, A_DK, A_DV), F32)),
        scratch_shapes=[pltpu.VMEM((heads // 2, A_DV, 2 * A_DK), F32)],
        compiler_params=_params(("parallel", "parallel", "arbitrary"), nbytes),
        name="hgrn_prompt",
    )(u3, u3, u3, u3, lbf, oml, gn.reshape(1, A_DV), p2, masks)


def _hgrn_sample_kernel(u_ref, lbf_ref, oml_ref, gn_ref, s_ref, oa_ref, so_ref, *, steps):
    u = u_ref[0]
    q = _silu(u[:, Q_A:Q_A + A_KEY_WIDTH])
    sz, snz = _sig_pair(u[:, F_A:F_A + A_KEY_WIDTH])
    oml = oml_ref[...]
    f = lbf_ref[...] + oml * sz
    ka = oml * snz
    v = u[:, I_A:I_A + A_WIDTH]
    gate = _silu(u[:, G_A:G_A + A_WIDTH])
    gn = gn_ref[...]
    rows = u.shape[0]
    row = lax.broadcasted_iota(jnp.int32, f.shape, 0)
    live = row < steps
    f = jnp.where(live, f, 1.0)
    ka = jnp.where(live, ka, 0.0)

    def down(x, d, fill):
        return x if d == 0 else jnp.where(row >= d, pltpu.roll(x, d, axis=0), fill)

    def up(x, d, fill):
        return jnp.where(row < rows - d, pltpu.roll(x, rows - d, axis=0), fill)

    decay = [jnp.ones_like(f)]
    for d in range(1, steps + 1):
        decay.append(decay[-1] * down(f, d - 1, 1.0))
    head = decay[steps]
    tail = jnp.ones_like(f)
    for d in range(1, steps):
        tail = tail * up(f, d, 1.0)
    q_in = (q * head).astype(BF16)
    k_tail = ka * tail
    total = head[steps - 1:steps, :]
    t_hi = total.astype(BF16).astype(F32)
    t_mid = (total - t_hi).astype(BF16).astype(F32)
    t_lo = (total - t_hi - t_mid).astype(BF16).astype(F32)
    t_rows = jnp.where(row == 0, t_hi, jnp.where(row == 1, t_mid, jnp.where(row == 2, t_lo, 0.0)))
    prods = [q * down(ka, d, 0.0) * decay[d] for d in range(steps)]
    v_sh = [down(v, d, 0.0) for d in range(steps)]

    zsq = jnp.zeros((A_DK, A_DV), BF16)
    zrows = jnp.zeros((rows, A_DV), F32)
    r2 = lax.broadcasted_iota(jnp.int32, (rows, 2 * A_DV), 0)
    l2 = lax.broadcasted_iota(jnp.int32, (rows, 2 * A_DV), 1)
    ones_right = jnp.where((r2 < 3) & (l2 >= A_DV), 1.0, 0.0)

    for p in range(A_HEADS // 2):
        ps = slice(2 * p * A_DK, (2 * p + 2) * A_DK)
        s0 = [s_ref[0, 2 * p + hh] for hh in range(2)]
        s_bd = jnp.concatenate([jnp.concatenate([s0[0].astype(BF16), zsq], axis=1),
                                jnp.concatenate([zsq, s0[1].astype(BF16)], axis=1)], axis=0)
        o_pair = _dot(q_in[:, ps], s_bd)
        for hh in range(2):
            head_i = 2 * p + hh
            sl = slice(head_i * A_DK, (head_i + 1) * A_DK)
            o = o_pair[:, hh * A_DV:(hh + 1) * A_DV]
            for d in range(steps):
                o = o + jnp.sum(prods[d][:, sl], axis=-1, keepdims=True) * v_sh[d][:, sl]
            lhs = jnp.concatenate([k_tail[:, sl], t_rows[:, sl]], axis=0).astype(BF16)
            rhs = jnp.concatenate([jnp.concatenate([v[:, sl], zrows], axis=1), ones_right],
                                  axis=0).astype(BF16)
            upd = lax.dot_general(lhs, rhs, (((0,), (0,)), ((), ())), preferred_element_type=F32)
            so_ref[0, head_i] = s0[hh] * upd[:, A_DV:] + upd[:, :A_DV]
            oa_ref[0, :, sl] = (_rms_rows(o, gn) * gate[:, sl]).astype(BF16)


def _hgrn_sample(u3, lbf, oml, gn, state, layer, steps):
    bsz, rows, _ = u3.shape
    width = G_A + A_WIDTH
    nbytes = 2 * rows * width * 4 + 4 * A_HEADS * A_DK * A_DV * 4
    return pl.pallas_call(
        functools.partial(_hgrn_sample_kernel, steps=steps),
        grid=(bsz,),
        in_specs=[pl.BlockSpec((1, rows, width), lambda b: (b, 0, 0)),
                  pl.BlockSpec((1, A_KEY_WIDTH), lambda b: (0, 0)),
                  pl.BlockSpec((1, A_KEY_WIDTH), lambda b: (0, 0)),
                  pl.BlockSpec((1, A_DV), lambda b: (0, 0)),
                  pl.BlockSpec((None, 1, A_HEADS, A_DK, A_DV), lambda b: (layer, b, 0, 0, 0))],
        out_specs=(pl.BlockSpec((1, rows, A_WIDTH), lambda b: (b, 0, 0)),
                   pl.BlockSpec((1, A_HEADS, A_DK, A_DV), lambda b: (b, 0, 0, 0))),
        out_shape=(jax.ShapeDtypeStruct((bsz, rows, A_WIDTH), BF16),
                   jax.ShapeDtypeStruct(state.shape[1:], F32)),
        compiler_params=_params(("parallel",), nbytes),
        name="hgrn_sample",
    )(u3, lbf, oml, gn.reshape(1, A_DV), state)


def _t5_bucket(dist):
    n = np.maximum(dist, 0)
    max_exact = N_BUCKETS // 2
    nf = np.maximum(n, 1).astype(np.float32)
    large = max_exact + (np.log(nf / max_exact) / math.log(MAX_DISTANCE / max_exact)
                         * (N_BUCKETS - max_exact)).astype(np.int32)
    large = np.minimum(large, N_BUCKETS - 1)
    return np.where(n < max_exact, n, large).astype(np.int32)


def _bias_kernel(tab_ref, bucket_ref, valid_ref, o_ref):
    rows = bucket_ref.shape[0]
    bucket = bucket_ref[...]
    hits = [bucket == b for b in range(N_BUCKETS)]
    for h in range(B_HEADS):
        acc = jnp.zeros(bucket.shape, F32)
        for b in range(N_BUCKETS):
            acc = jnp.where(hits[b], tab_ref[b, h], acc)
        acc = acc * LOG2E
        j, r, c = h // 4, (h % 4) // 2, h % 2
        for m in range(valid_ref.shape[0]):
            o_ref[m, j, r * rows:(r + 1) * rows, c * KEY_SLOTS:(c + 1) * KEY_SLOTS] = jnp.where(
                valid_ref[m] != 0, acc, NEG_LOGIT * LOG2E)


def _bias_layout(table, dist, valids):
    rows = dist.shape[0]
    return pl.pallas_call(
        _bias_kernel,
        in_specs=[pl.BlockSpec(memory_space=pltpu.SMEM),
                  pl.BlockSpec(memory_space=pltpu.VMEM),
                  pl.BlockSpec(memory_space=pltpu.VMEM)],
        out_shape=jax.ShapeDtypeStruct((valids.shape[0], B_KV_HEADS, 2 * rows, 2 * KEY_SLOTS), F32),
        name="bias_prep",
    )(table.astype(F32), jnp.asarray(_t5_bucket(dist), jnp.int32), jnp.asarray(valids, jnp.int32))


def _block_diag_ones():
    r = lax.broadcasted_iota(jnp.int32, (LANES, LANES), 0) // B_HD
    c = lax.broadcasted_iota(jnp.int32, (LANES, LANES), 1) // B_HD
    return (r == c).astype(BF16)


def _head_norm(x, g2, bd):
    hi, lo = _split_bf16(x * x)
    ss = _dot(hi, bd) + _dot(lo, bd)
    return x * lax.rsqrt(ss * (1.0 / B_HD) + RMS_EPS) * g2


def _place(tile, half, lo_mask):
    rolled = pltpu.roll(tile, B_HD, axis=1)
    zero = jnp.zeros_like(tile)
    if half == 0:
        return jnp.where(lo_mask, tile, zero), jnp.where(lo_mask, zero, rolled)
    return jnp.where(lo_mask, rolled, zero), jnp.where(lo_mask, zero, tile)


def _swa_core(qn_tiles, k_tiles, v_tiles, bias_at, sink_ref, store):
    rows = qn_tiles[0].shape[0]
    lo_mask = lax.broadcasted_iota(jnp.int32, k_tiles[0].shape, 1) < B_HD
    out_lo = lax.broadcasted_iota(jnp.int32, (rows, LANES), 1) < B_HD
    scores, values = [], []
    for j in range(B_KV_HEADS):
        t, half = j // 2, j % 2
        k_lo, k_hi = _place(k_tiles[t], half, lo_mask)
        v_lo, v_hi = _place(v_tiles[t], half, lo_mask)
        kk = jnp.concatenate([k_lo, k_hi], axis=0).astype(BF16)
        values.append(jnp.concatenate([v_lo, v_hi], axis=0).astype(BF16))
        qq = jnp.concatenate([qn_tiles[2 * j], qn_tiles[2 * j + 1]], axis=0).astype(BF16)
        scores.append(_dot_nt(qq, kk) + bias_at(j))
    weights, scales = [], []
    for j in range(B_KV_HEADS):
        e_rows, inv = [], []
        for r in range(2):
            e_cols, inv_r = [], []
            for c in range(2):
                sk = sink_ref[4 * j + 2 * r + c] * LOG2E
                sb = scores[j][r * rows:(r + 1) * rows, c * KEY_SLOTS:(c + 1) * KEY_SLOTS]
                m = jnp.maximum(jnp.max(sb, axis=-1, keepdims=True), sk)
                e = jnp.exp2(sb - m)
                inv_r.append(1.0 / (jnp.sum(e, axis=-1, keepdims=True) + jnp.exp2(sk - m)))
                e_cols.append(e.astype(BF16))
            e_rows.append(jnp.concatenate(e_cols, axis=1))
            inv.append(jnp.where(out_lo, inv_r[0], inv_r[1]))
        weights.append(jnp.concatenate(e_rows, axis=0))
        scales.append(inv)
    for j in range(B_KV_HEADS):
        o = _dot(weights[j], values[j])
        for r in range(2):
            store(2 * j + r, (o[r * rows:(r + 1) * rows] * scales[j][r]).astype(BF16))


def _swa_prompt_kernel(sink_ref, q_ref, kc_ref, kp_ref, vc_ref, vp_ref, qg_ref, kg_ref, bias_ref,
                       ob_ref, ko_ref, vo_ref):
    n = pl.program_id(1)
    bd = _block_diag_ones()
    k_all = jnp.concatenate([kp_ref[0], kc_ref[0]], axis=0)
    v_all = jnp.concatenate([vp_ref[0], vc_ref[0]], axis=0)
    kg = kg_ref[...]
    k_tiles = [_head_norm(k_all[:, t * LANES:(t + 1) * LANES], kg, bd) for t in range(2)]
    v_tiles = [v_all[:, t * LANES:(t + 1) * LANES] for t in range(2)]
    q = q_ref[0]
    qg = qg_ref[...]
    qn_tiles = [_head_norm(q[:, t * LANES:(t + 1) * LANES], qg, bd) for t in range(B_WIDTH // LANES)]

    def store(tile, val):
        ob_ref[0, :, tile * LANES:(tile + 1) * LANES] = val

    _swa_core(qn_tiles, k_tiles, v_tiles, lambda j: bias_ref[j], sink_ref, store)

    @pl.when(n == pl.num_programs(1) - 1)
    def _():
        ko_ref[0] = jnp.concatenate([kt[WINDOW:] for kt in k_tiles], axis=1)
        vo_ref[0] = vc_ref[0]


def _prompt_dist():
    return WINDOW + np.arange(WINDOW)[:, None] - np.arange(KEY_SLOTS)[None, :]


def _prompt_valid():
    j = np.arange(KEY_SLOTS)[None, :]
    dist = _prompt_dist()
    cur = (j >= WINDOW) & (dist >= 0)
    prev = (j < WINDOW) & (dist < WINDOW)
    return np.stack([cur, cur | prev]).astype(np.int32)


def _swa_prompt(u3, sinks, qg, kg, bias):
    bsz, length, _ = u3.shape
    nb = length // WINDOW
    qb, kb, vb = Q_B // B_WIDTH, K_B // B_KV_WIDTH, V_B // B_KV_WIDTH
    assert Q_B % B_WIDTH == 0 and K_B % B_KV_WIDTH == 0 and V_B % B_KV_WIDTH == 0
    kv_spec = lambda col, prev: pl.BlockSpec(
        (1, WINDOW, B_KV_WIDTH),
        (lambda b, n: (b, jnp.maximum(n - 1, 0), col)) if prev else (lambda b, n: (b, n, col)))
    g_spec = pl.BlockSpec((1, LANES), lambda b, n: (0, 0))
    nbytes = (2 * WINDOW * (B_WIDTH + 4 * B_KV_WIDTH) * 4 + bias.size * 4
              + 2 * WINDOW * B_WIDTH * 2 + 24 * WINDOW * 2 * KEY_SLOTS * 4)
    return pl.pallas_call(
        _swa_prompt_kernel,
        grid=(bsz, nb),
        in_specs=[pl.BlockSpec(memory_space=pltpu.SMEM),
                  pl.BlockSpec((1, WINDOW, B_WIDTH), lambda b, n: (b, n, qb)),
                  kv_spec(kb, False), kv_spec(kb, True), kv_spec(vb, False), kv_spec(vb, True),
                  g_spec, g_spec,
                  pl.BlockSpec((None,) + bias.shape[1:], lambda b, n: (jnp.minimum(n, 1), 0, 0, 0))],
        out_specs=(pl.BlockSpec((1, WINDOW, B_WIDTH), lambda b, n: (b, n, 0)),
                   pl.BlockSpec((1, WINDOW, B_KV_WIDTH), lambda b, n: (b, 0, 0)),
                   pl.BlockSpec((1, WINDOW, B_KV_WIDTH), lambda b, n: (b, 0, 0))),
        out_shape=(jax.ShapeDtypeStruct((bsz, length, B_WIDTH), BF16),
                   jax.ShapeDtypeStruct((bsz, WINDOW, B_KV_WIDTH), F32),
                   jax.ShapeDtypeStruct((bsz, WINDOW, B_KV_WIDTH), F32)),
        compiler_params=_params(("parallel", "arbitrary"), nbytes),
        name="swa_prompt",
    )(sinks, u3, u3, u3, u3, u3, qg, kg, bias)


def _swa_sample_kernel(sink_ref, q_ref, kv_ref, ck_ref, cv_ref, qg_ref, kg_ref, bias_ref,
                       ob_ref, ko_ref, vo_ref, *, steps):
    bd = _block_diag_ones()
    rows = q_ref.shape[1]
    kg = kg_ref[...]
    qg = qg_ref[...]
    pad = jnp.zeros((KEY_SLOTS - WINDOW - rows, LANES), F32)
    for i in range(q_ref.shape[0]):
        kv = kv_ref[i]
        k_new = [_head_norm(kv[:, t * LANES:(t + 1) * LANES], kg, bd) for t in range(2)]
        v_new = [kv[:, B_KV_WIDTH + t * LANES:B_KV_WIDTH + (t + 1) * LANES] for t in range(2)]
        ck, cv = ck_ref[i], cv_ref[i]
        k_tiles = [jnp.concatenate([ck[:, t * LANES:(t + 1) * LANES], k_new[t], pad], axis=0)
                   for t in range(2)]
        v_tiles = [jnp.concatenate([cv[:, t * LANES:(t + 1) * LANES], v_new[t], pad], axis=0)
                   for t in range(2)]
        q = q_ref[i]
        qn_tiles = [_head_norm(q[:, t * LANES:(t + 1) * LANES], qg, bd) for t in range(B_WIDTH // LANES)]

        def store(tile, val, i=i):
            ob_ref[i, :, tile * LANES:(tile + 1) * LANES] = val

        _swa_core(qn_tiles, k_tiles, v_tiles, lambda j: bias_ref[0, j], sink_ref, store)
        ko_ref[i, 0:WINDOW - steps, :] = ck_ref[i, steps:WINDOW, :]
        vo_ref[i, 0:WINDOW - steps, :] = cv_ref[i, steps:WINDOW, :]
        ko_ref[i, WINDOW - steps:WINDOW, :] = jnp.concatenate([kt[0:steps] for kt in k_new], axis=1)
        vo_ref[i, WINDOW - steps:WINDOW, :] = kv[0:steps, B_KV_WIDTH:]


def _sample_dist(rows):
    return WINDOW + np.arange(rows)[:, None] - np.arange(KEY_SLOTS)[None, :]


def _sample_valid(rows, steps):
    dist = _sample_dist(rows)
    j = np.arange(KEY_SLOTS)[None, :]
    return ((dist >= 0) & (dist < WINDOW) & (j < WINDOW + steps)).astype(np.int32)[None]


def _swa_sample(u3, cache_k, cache_v, layer, sinks, qg, kg, bias, steps):
    bsz, rows, _ = u3.shape
    w = cache_k.shape[2]
    bt = SWA_SAMPLE_SEQS
    assert w == WINDOW and bsz % bt == 0
    qb, kvb = Q_B // B_WIDTH, K_B // (2 * B_KV_WIDTH)
    assert K_B % (2 * B_KV_WIDTH) == 0
    c_in_spec = pl.BlockSpec((None, bt, w, B_KV_WIDTH), lambda b: (layer, b, 0, 0))
    c_spec = pl.BlockSpec((bt, w, B_KV_WIDTH), lambda b: (b, 0, 0))
    g_spec = pl.BlockSpec((1, LANES), lambda b: (0, 0))
    nbytes = bt * 8 * w * B_KV_WIDTH * 4 + 2 * bias.size * 4 + bt * 16 * KEY_SLOTS * LANES * 4
    return pl.pallas_call(
        functools.partial(_swa_sample_kernel, steps=steps),
        grid=(bsz // bt,),
        in_specs=[pl.BlockSpec(memory_space=pltpu.SMEM),
                  pl.BlockSpec((bt, rows, B_WIDTH), lambda b: (b, 0, qb)),
                  pl.BlockSpec((bt, rows, 2 * B_KV_WIDTH), lambda b: (b, 0, kvb)),
                  c_in_spec, c_in_spec, g_spec, g_spec,
                  pl.BlockSpec(bias.shape, lambda b: (0, 0, 0, 0))],
        out_specs=(pl.BlockSpec((bt, rows, B_WIDTH), lambda b: (b, 0, 0)), c_spec, c_spec),
        out_shape=(jax.ShapeDtypeStruct((bsz, rows, B_WIDTH), BF16),
                   jax.ShapeDtypeStruct((bsz, w, B_KV_WIDTH), F32),
                   jax.ShapeDtypeStruct((bsz, w, B_KV_WIDTH), F32)),
        compiler_params=_params(("parallel",), nbytes),
        name="swa_sample",
    )(sinks, u3, u3, cache_k, cache_v, qg, kg, bias)


def kernel(x_prompt, x_sample, cache_k, cache_v, state_hgrn, norm_mix, w_in, lower_bounds, hgrn_norm,
           q_norm, k_norm, attn_sinks, rel_bias_table, w_out, norm_ffn, w_gate, w_up, w_down):
    depth = w_in.shape[0]
    bp, seq, d = x_prompt.shape
    bd, ld, _ = x_sample.shape
    w = cache_k.shape[2]
    in_width = w_in.shape[2]

    w_in_b, w_out_b = w_in.astype(BF16), w_out.astype(BF16)
    w_gate_b, w_up_b, w_down_b = w_gate.astype(BF16), w_up.astype(BF16), w_down.astype(BF16)
    lbf, oml = _lower_bounds(lower_bounds)
    sinks = attn_sinks.astype(F32)
    qg2 = jnp.tile(q_norm.astype(F32), (1, LANES // B_HD)) * (ATTN_SCALE * LOG2E)
    kg2 = jnp.tile(k_norm.astype(F32), (1, LANES // B_HD))
    ck = cache_k.astype(F32).reshape(depth, bd, w, B_KV_WIDTH)
    cv = cache_v.astype(F32).reshape(depth, bd, w, B_KV_WIDTH)

    xp = x_prompt.reshape(bp * seq, d)
    xs = jnp.pad(x_sample, ((0, 0), (0, SAMPLE_ROWS - ld), (0, 0))).reshape(bd * SAMPLE_ROWS, d)
    tm_p, tm_s = 1024, bd * SAMPLE_ROWS
    bias_p = _bias_layout(rel_bias_table, _prompt_dist(), _prompt_valid())
    bias_s = _bias_layout(rel_bias_table, _sample_dist(SAMPLE_ROWS), _sample_valid(SAMPLE_ROWS, ld))
    state = state_hgrn.astype(F32)

    pk, pv, ps, sk, sv, ss = [], [], [], [], [], []
    for l in range(depth):
        lbf_l, oml_l = lbf[l:l + 1], oml[l:l + 1]
        u = _norm_matmul(xp, norm_mix[l], w_in_b, l, tm_p, 512).reshape(bp, seq, in_width)
        oa, st = _hgrn_prompt(u, lbf_l, oml_l, hgrn_norm[l])
        ob, kn, vn = _swa_prompt(u, sinks[l], qg2[l:l + 1], kg2[l:l + 1], bias_p)
        xp = _outproj(xp, oa.reshape(bp * seq, A_WIDTH), ob.reshape(bp * seq, B_WIDTH), w_out_b, l, 512)
        xp = _ffn(xp, norm_ffn[l], w_gate_b, w_up_b, w_down_b, l, tm_p, 256)
        pk.append(kn)
        pv.append(vn)
        ps.append(st)
        u = _norm_matmul(xs, norm_mix[l], w_in_b, l, tm_s, 512).reshape(bd, SAMPLE_ROWS, in_width)
        oa, st = _hgrn_sample(u, lbf_l, oml_l, hgrn_norm[l], state, l, ld)
        ob, kn, vn = _swa_sample(u, ck, cv, l, sinks[l], qg2[l:l + 1], kg2[l:l + 1], bias_s, ld)
        xs = _outproj(xs, oa.reshape(tm_s, A_WIDTH), ob.reshape(tm_s, B_WIDTH), w_out_b, l, tm_s)
        xs = _ffn(xs, norm_ffn[l], w_gate_b, w_up_b, w_down_b, l, tm_s, 512)
        sk.append(kn)
        sv.append(vn)
        ss.append(st)

    kv_shape = lambda n: (depth, n, WINDOW, B_KV_HEADS, B_HD)
    return (xp.reshape(bp, seq, d),
            xs.reshape(bd, SAMPLE_ROWS, d)[:, :ld],
            jnp.stack(pk).reshape(kv_shape(bp)), jnp.stack(pv).reshape(kv_shape(bp)), jnp.stack(ps),
            jnp.stack(sk).reshape(kv_shape(bd)), jnp.stack(sv).reshape(kv_shape(bd)), jnp.stack(ss))
```

```python
import functools
import math

import jax
import jax.numpy as jnp
import numpy as np
from jax import lax
from jax.experimental import pallas as pl
from jax.experimental.pallas import tpu as pltpu

F32 = jnp.float32
BF16 = jnp.bfloat16

LANES = 128
SUBLANES = 8
VMEM_BYTES_V7X = 64 * 1024 * 1024
VMEM_CAP = VMEM_BYTES_V7X - 8 * 1024 * 1024

A_HEADS = 8
A_DK = 128
A_DV = 128
A_KEY_WIDTH = A_HEADS * A_DK
A_WIDTH = A_HEADS * A_DV
B_HEADS = 16
B_KV_HEADS = 4
B_HD = 64
B_WIDTH = B_HEADS * B_HD
B_KV_WIDTH = B_KV_HEADS * B_HD
WINDOW = 128
ATTN_SCALE = 1.0 / math.sqrt(B_HD)
LOG2E = math.log2(math.e)
NEG_LOGIT = -1e30
LB_FLOOR = 1e-20
N_BUCKETS = 32
MAX_DISTANCE = 128
RMS_EPS = 1e-6

Q_A, F_A, I_A, G_A = 0, A_KEY_WIDTH, 2 * A_KEY_WIDTH, 2 * A_KEY_WIDTH + A_WIDTH
Q_B = 2 * A_KEY_WIDTH + 2 * A_WIDTH
K_B = Q_B + B_WIDTH
V_B = K_B + B_KV_WIDTH

HGRN_CHUNK = 128
HGRN_FINE = 4
HGRN_HEADS_PER_STEP = 4
SAMPLE_ROWS = SUBLANES
SWA_SAMPLE_SEQS = 4
KEY_SLOTS = 2 * WINDOW


def _vmem_limit(nbytes):
    return int(min(VMEM_CAP, nbytes * 5 // 4 + (4 << 20)))


def _params(sem, nbytes):
    return pltpu.CompilerParams(dimension_semantics=sem, vmem_limit_bytes=_vmem_limit(nbytes))


def _sig_pair(z):
    t = jnp.exp(-jnp.abs(z))
    r = 1.0 / (1.0 + t)
    tr = t * r
    pos = z >= 0
    return jnp.where(pos, r, tr), jnp.where(pos, tr, r)


def _silu(x):
    return x * _sig_pair(x)[0]


def _rms_rows(x, g):
    ms = jnp.mean(x * x, axis=-1, keepdims=True)
    return x * lax.rsqrt(ms + RMS_EPS) * g


def _split_bf16(x):
    hi = x.astype(BF16)
    lo = (x - hi.astype(F32)).astype(BF16)
    return hi, lo


def _dot(a, b):
    return jnp.dot(a, b, preferred_element_type=F32)


def _dot_nt(a, b):
    return lax.dot_general(a, b, (((1,), (1,)), ((), ())), preferred_element_type=F32)


def _lb_kernel(lb_ref, lbf_ref, oml_ref):
    depth = lb_ref.shape[0]
    rows = [lb_ref[i:i + 1, :] for i in range(depth)]
    m = functools.reduce(jnp.maximum, rows)
    e = [jnp.exp(r - m) for r in rows]
    s = functools.reduce(lambda a, b: a + b, e)
    p = [ei / s for ei in e]
    cum = p[0]
    for i in range(depth):
        if i > 0:
            cum = cum + p[i]
        lb = cum - p[0]
        lbf_ref[i:i + 1, :] = jnp.maximum(lb, LB_FLOOR)
        oml_ref[i:i + 1, :] = 1.0 - lb


def _lower_bounds(lower_bounds):
    shp = jax.ShapeDtypeStruct(lower_bounds.shape, F32)
    return pl.pallas_call(_lb_kernel, out_shape=(shp, shp), name="lb_prep")(lower_bounds.astype(F32))


def _norm_matmul_kernel(x_ref, g_ref, w_ref, o_ref, h_ref):
    @pl.when(pl.program_id(1) == 0)
    def _():
        h_ref[...] = _rms_rows(x_ref[...], g_ref[...]).astype(BF16)

    o_ref[...] = _dot(h_ref[...], w_ref[...])


def _norm_matmul(x, g, w, layer, tm, tn):
    m, d = x.shape
    n = w.shape[2]
    nbytes = 2 * tm * d * 4 + tm * d * 2 + 2 * d * tn * 2 + 2 * tm * tn * 4
    return pl.pallas_call(
        _norm_matmul_kernel,
        grid=(m // tm, n // tn),
        in_specs=[pl.BlockSpec((tm, d), lambda i, j: (i, 0)),
                  pl.BlockSpec((1, d), lambda i, j: (0, 0)),
                  pl.BlockSpec((None, d, tn), lambda i, j: (layer, 0, j))],
        out_specs=pl.BlockSpec((tm, tn), lambda i, j: (i, j)),
        out_shape=jax.ShapeDtypeStruct((m, n), F32),
        scratch_shapes=[pltpu.VMEM((tm, d), BF16)],
        compiler_params=_params(("parallel", "arbitrary"), nbytes),
        name="norm_matmul",
    )(x, g.reshape(1, d), w)


def _outproj_kernel(x_ref, oa_ref, ob_ref, w_ref, o_ref):
    ka = oa_ref.shape[1]
    acc = _dot(oa_ref[...], w_ref[0:ka, :])
    acc = acc + _dot(ob_ref[...], w_ref[ka:, :])
    o_ref[...] = x_ref[...] + acc


def _outproj(x, oa, ob, w, layer, tm):
    m, d = x.shape
    ka, kb = oa.shape[1], ob.shape[1]
    nbytes = 4 * tm * d * 4 + 2 * tm * (ka + kb) * 2 + 2 * (ka + kb) * d * 2
    return pl.pallas_call(
        _outproj_kernel,
        grid=(m // tm,),
        in_specs=[pl.BlockSpec((tm, d), lambda i: (i, 0)),
                  pl.BlockSpec((tm, ka), lambda i: (i, 0)),
                  pl.BlockSpec((tm, kb), lambda i: (i, 0)),
                  pl.BlockSpec((None, ka + kb, d), lambda i: (layer, 0, 0))],
        out_specs=pl.BlockSpec((tm, d), lambda i: (i, 0)),
        out_shape=jax.ShapeDtypeStruct((m, d), F32),
        compiler_params=_params(("parallel",), nbytes),
        name="outproj",
    )(x, oa, ob, w)


def _ffn_kernel(x_ref, g_ref, wg_ref, wu_ref, wd_ref, o_ref, h_ref):
    @pl.when(pl.program_id(1) == 0)
    def _():
        x = x_ref[...]
        h_ref[...] = _rms_rows(x, g_ref[...]).astype(BF16)
        o_ref[...] = x

    h = h_ref[...]
    act = (_silu(_dot(h, wg_ref[...])) * _dot(h, wu_ref[...])).astype(BF16)
    o_ref[...] += _dot(act, wd_ref[...])


def _ffn(x, g, wg, wu, wd, layer, tm, tf):
    m, d = x.shape
    f = wg.shape[2]
    nbytes = 4 * tm * d * 4 + tm * d * 2 + 2 * 3 * d * tf * 2 + 3 * tm * tf * 4
    return pl.pallas_call(
        _ffn_kernel,
        grid=(m // tm, f // tf),
        in_specs=[pl.BlockSpec((tm, d), lambda i, j: (i, 0)),
                  pl.BlockSpec((1, d), lambda i, j: (0, 0)),
                  pl.BlockSpec((None, d, tf), lambda i, j: (layer, 0, j)),
                  pl.BlockSpec((None, d, tf), lambda i, j: (layer, 0, j)),
                  pl.BlockSpec((None, tf, d), lambda i, j: (layer, j, 0))],
        out_specs=pl.BlockSpec((tm, d), lambda i, j: (i, 0)),
        out_shape=jax.ShapeDtypeStruct((m, d), F32),
        scratch_shapes=[pltpu.VMEM((tm, d), BF16)],
        compiler_params=_params(("parallel", "arbitrary"), nbytes),
        name="ffn",
    )(x, g.reshape(1, d), wg, wu, wd)


def _hgrn_static(c):
    t = np.arange(c)
    blocks = [(t[None, :] <= t[:, None])]
    masks = []
    h = c // 2
    while h >= 1:
        grp = t // (2 * h)
        mid = grp * 2 * h + h
        upper = (t % (2 * h)) >= h
        if h < HGRN_FINE:
            col = t[None, :]
            p_up = (col >= mid[:, None]) & (col <= t[:, None])
            p_lo = (col > t[:, None]) & (col < mid[:, None])
            blocks.append(np.where(upper[:, None], p_up, p_lo))
        masks.append((grp[:, None] == grp[None, :]) & upper[:, None] & ~upper[None, :])
        h //= 2
    masks.append(np.eye(c, dtype=bool))
    p_all = np.concatenate(blocks, axis=0).astype(np.float32)
    p2 = np.concatenate([p_all, p_all], axis=1)
    masks = np.stack(masks).astype(np.float32)
    return p2, np.concatenate([masks, masks], axis=2)


def _hgrn_prompt_kernel(q_ref, f_ref, i_ref, g_ref, lbf_ref, oml_ref, gn_ref, p2_ref, m_ref,
                        oa_ref, st_ref, s_scr, *, chunk, heads):
    c_idx = pl.program_id(2)

    @pl.when(c_idx == 0)
    def _():
        s_scr[...] = jnp.zeros_like(s_scr)

    width = heads * A_DK
    pw = 2 * A_DK
    sz, snz = _sig_pair(f_ref[0])
    oml = oml_ref[...]
    lf = jnp.log(lbf_ref[...] + oml * sz)
    ka = oml * snz
    hi, lo = _split_bf16(lf)
    expo = _dot(p2_ref[...], jnp.concatenate([hi, lo], axis=0))
    b = expo[0:chunk]
    q = _silu(q_ref[0])
    v = i_ref[0]
    gate = _silu(g_ref[0])
    gn = gn_ref[...]

    row = lax.broadcasted_iota(jnp.int32, (chunk, width), 0)
    qs, ks = [], []
    h, fine = chunk // 2, 0
    while h >= 1:
        if h >= HGRN_FINE:
            groups = chunk // (2 * h)
            mid = jnp.broadcast_to(b.reshape(groups, 2 * h, width)[:, h - 1:h, :], (groups, 2 * h, width))
            d = b - mid.reshape(chunk, width)
            e = jnp.where((row & h) != 0, d, -d)
        else:
            fine += 1
            e = expo[fine * chunk:(fine + 1) * chunk]
        e = jnp.exp(e)
        qs.append((q * e).astype(BF16))
        ks.append((ka * e).astype(BF16))
        h //= 2
    qs.append(q.astype(BF16))
    ks.append(ka.astype(BF16))
    b_last = b[chunk - 1:chunk, :]
    q0 = (q * jnp.exp(b)).astype(BF16)
    k_tail = (ka * jnp.exp(b_last - b)).astype(BF16)
    decay = jnp.exp(b_last)
    vb = v.astype(BF16)

    left = lax.broadcasted_iota(jnp.int32, (chunk, pw), 1) < A_DK

    def bdiag(x):
        z = jnp.zeros_like(x)
        return jnp.concatenate([jnp.where(left, x, z), jnp.where(left, z, x)], axis=0)

    for p in range(heads // 2):
        sl = slice(p * pw, (p + 1) * pw)
        st = s_scr[p]
        o = _dot_nt(q0[:, sl], bdiag(st.astype(BF16)))
        a = jnp.zeros((chunk, pw), F32)
        for li in range(len(qs)):
            a = a + _dot_nt(qs[li][:, sl], bdiag(ks[li][:, sl])) * m_ref[li]
        o = o + _dot(a.astype(BF16), bdiag(vb[:, sl]))
        vp = v[:, sl]
        vt = jnp.concatenate([vp[:, :A_DV].T, vp[:, A_DV:].T], axis=1).astype(BF16)
        s_scr[p] = st * decay[:, sl] + _dot(vt, bdiag(k_tail[:, sl]))
        for hh in range(2):
            hs = slice(p * pw + hh * A_DV, p * pw + (hh + 1) * A_DV)
            oa_ref[0, :, hs] = (_rms_rows(o[:, hh * A_DV:(hh + 1) * A_DV], gn) * gate[:, hs]).astype(BF16)

    @pl.when(c_idx == pl.num_programs(2) - 1)
    def _():
        for hh in range(heads):
            st_ref[0, hh] = s_scr[hh // 2][:, (hh % 2) * A_DK:(hh % 2 + 1) * A_DK].T


def _hgrn_prompt(u3, lbf, oml, gn):
    bsz, length, _ = u3.shape
    chunk, heads = HGRN_CHUNK, HGRN_HEADS_PER_STEP
    cols = heads * A_DK
    assert length % chunk == 0 and A_HEADS % heads == 0 and heads % 2 == 0 and chunk == A_DV
    p2, masks = _hgrn_static(chunk)
    p2 = jnp.asarray(p2, BF16)
    masks = jnp.asarray(masks, F32)

    def col_spec(offset):
        base = offset // cols
        return pl.BlockSpec((1, chunk, cols), lambda b, h, c: (b, c, base + h))

    vec_spec = pl.BlockSpec((1, cols), lambda b, h, c: (0, h))
    nbytes = (2 * 5 * chunk * cols * 4 + 2 * (p2.size * 2 + masks.size * 4)
              + 6 * p2.shape[0] * cols * 4)
    return pl.pallas_call(
        functools.partial(_hgrn_prompt_kernel, chunk=chunk, heads=heads),
        grid=(bsz, A_HEADS // heads, length // chunk),
        in_specs=[col_spec(Q_A), col_spec(F_A), col_spec(I_A), col_spec(G_A),
                  vec_spec, vec_spec,
                  pl.BlockSpec((1, A_DV), lambda b, h, c: (0, 0)),
                  pl.BlockSpec(p2.shape, lambda b, h, c: (0, 0)),
                  pl.BlockSpec(masks.shape, lambda b, h, c: (0, 0, 0))],
        out_specs=(pl.BlockSpec((1, chunk, cols), lambda b, h, c: (b, c, h)),
                   pl.BlockSpec((1, heads, A_DK, A_DV), lambda b, h, c: (b, h, 0, 0))),
        out_shape=(jax.ShapeDtypeStruct((bsz, length, A_WIDTH), BF16),
                   jax.ShapeDtypeStruct((bsz, A_HEADS, A_DK, A_DV), F32)),
        scratch_shapes=[pltpu.VMEM((heads // 2, A_DV, 2 * A_DK), F32)],
        compiler_params=_params(("parallel", "parallel", "arbitrary"), nbytes),
        name="hgrn_prompt",
    )(u3, u3, u3, u3, lbf, oml, gn.reshape(1, A_DV), p2, masks)


def _hgrn_sample_kernel(u_ref, lbf_ref, oml_ref, gn_ref, s_ref, oa_ref, so_ref, *, steps):
    u = u_ref[0]
    q = _silu(u[:, Q_A:Q_A + A_KEY_WIDTH])
    sz, snz = _sig_pair(u[:, F_A:F_A + A_KEY_WIDTH])
    oml = oml_ref[...]
    f = lbf_ref[...] + oml * sz
    ka = oml * snz
    v = u[:, I_A:I_A + A_WIDTH]
    gate = _silu(u[:, G_A:G_A + A_WIDTH])
    gn = gn_ref[...]
    rows = u.shape[0]
    row = lax.broadcasted_iota(jnp.int32, f.shape, 0)
    live = row < steps
    f = jnp.where(live, f, 1.0)
    ka = jnp.where(live, ka, 0.0)

    def down(x, d, fill):
        return x if d == 0 else jnp.where(row >= d, pltpu.roll(x, d, axis=0), fill)

    def up(x, d, fill):
        return jnp.where(row < rows - d, pltpu.roll(x, rows - d, axis=0), fill)

    decay = [jnp.ones_like(f)]
    for d in range(1, steps + 1):
        decay.append(decay[-1] * down(f, d - 1, 1.0))
    head = decay[steps]
    tail = jnp.ones_like(f)
    for d in range(1, steps):
        tail = tail * up(f, d, 1.0)
    q_in = (q * head).astype(BF16)
    k_tail = ka * tail
    total = head[steps - 1:steps, :]
    t_hi = total.astype(BF16).astype(F32)
    t_mid = (total - t_hi).astype(BF16).astype(F32)
    t_lo = (total - t_hi - t_mid).astype(BF16).astype(F32)
    t_rows = jnp.where(row == 0, t_hi, jnp.where(row == 1, t_mid, jnp.where(row == 2, t_lo, 0.0)))
    prods = [q * down(ka, d, 0.0) * decay[d] for d in range(steps)]
    v_sh = [down(v, d, 0.0) for d in range(steps)]

    zsq = jnp.zeros((A_DK, A_DV), BF16)
    zrows = jnp.zeros((rows, A_DV), F32)
    r2 = lax.broadcasted_iota(jnp.int32, (rows, 2 * A_DV), 0)
    l2 = lax.broadcasted_iota(jnp.int32, (rows, 2 * A_DV), 1)
    ones_right = jnp.where((r2 < 3) & (l2 >= A_DV), 1.0, 0.0)

    for p in range(A_HEADS // 2):
        ps = slice(2 * p * A_DK, (2 * p + 2) * A_DK)
        s0 = [s_ref[0, 2 * p + hh] for hh in range(2)]
        s_bd = jnp.concatenate([jnp.concatenate([s0[0].astype(BF16), zsq], axis=1),
                                jnp.concatenate([zsq, s0[1].astype(BF16)], axis=1)], axis=0)
        o_pair = _dot(q_in[:, ps], s_bd)
        for hh in range(2):
            head_i = 2 * p + hh
            sl = slice(head_i * A_DK, (head_i + 1) * A_DK)
            o = o_pair[:, hh * A_DV:(hh + 1) * A_DV]
            for d in range(steps):
                o = o + jnp.sum(prods[d][:, sl], axis=-1, keepdims=True) * v_sh[d][:, sl]
            lhs = jnp.concatenate([k_tail[:, sl], t_rows[:, sl]], axis=0).astype(BF16)
            rhs = jnp.concatenate([jnp.concatenate([v[:, sl], zrows], axis=1), ones_right],
                                  axis=0).astype(BF16)
            upd = lax.dot_general(lhs, rhs, (((0,), (0,)), ((), ())), preferred_element_type=F32)
            so_ref[0, head_i] = s0[hh] * upd[:, A_DV:] + upd[:, :A_DV]
            oa_ref[0, :, sl] = (_rms_rows(o, gn) * gate[:, sl]).astype(BF16)


def _hgrn_sample(u3, lbf, oml, gn, state, layer, steps):
    bsz, rows, _ = u3.shape
    width = G_A + A_WIDTH
    nbytes = 2 * rows * width * 4 + 4 * A_HEADS * A_DK * A_DV * 4
    return pl.pallas_call(
        functools.partial(_hgrn_sample_kernel, steps=steps),
        grid=(bsz,),
        in_specs=[pl.BlockSpec((1, rows, width), lambda b: (b, 0, 0)),
                  pl.BlockSpec((1, A_KEY_WIDTH), lambda b: (0, 0)),
                  pl.BlockSpec((1, A_KEY_WIDTH), lambda b: (0, 0)),
                  pl.BlockSpec((1, A_DV), lambda b: (0, 0)),
                  pl.BlockSpec((None, 1, A_HEADS, A_DK, A_DV), lambda b: (layer, b, 0, 0, 0))],
        out_specs=(pl.BlockSpec((1, rows, A_WIDTH), lambda b: (b, 0, 0)),
                   pl.BlockSpec((1, A_HEADS, A_DK, A_DV), lambda b: (b, 0, 0, 0))),
        out_shape=(jax.ShapeDtypeStruct((bsz, rows, A_WIDTH), BF16),
                   jax.ShapeDtypeStruct(state.shape[1:], F32)),
        compiler_params=_params(("parallel",), nbytes),
        name="hgrn_sample",
    )(u3, lbf, oml, gn.reshape(1, A_DV), state)


def _t5_bucket(dist):
    n = np.maximum(dist, 0)
    max_exact = N_BUCKETS // 2
    nf = np.maximum(n, 1).astype(np.float32)
    large = max_exact + (np.log(nf / max_exact) / math.log(MAX_DISTANCE / max_exact)
                         * (N_BUCKETS - max_exact)).astype(np.int32)
    large = np.minimum(large, N_BUCKETS - 1)
    return np.where(n < max_exact, n, large).astype(np.int32)


def _bias_kernel(tab_ref, bucket_ref, valid_ref, o_ref):
    rows = bucket_ref.shape[0]
    bucket = bucket_ref[...]
    hits = [bucket == b for b in range(N_BUCKETS)]
    for h in range(B_HEADS):
        acc = jnp.zeros(bucket.shape, F32)
        for b in range(N_BUCKETS):
            acc = jnp.where(hits[b], tab_ref[b, h], acc)
        acc = acc * LOG2E
        j, r, c = h // 4, (h % 4) // 2, h % 2
        for m in range(valid_ref.shape[0]):
            o_ref[m, j, r * rows:(r + 1) * rows, c * KEY_SLOTS:(c + 1) * KEY_SLOTS] = jnp.where(
                valid_ref[m] != 0, acc, NEG_LOGIT * LOG2E)


def _bias_layout(table, dist, valids):
    rows = dist.shape[0]
    return pl.pallas_call(
        _bias_kernel,
        in_specs=[pl.BlockSpec(memory_space=pltpu.SMEM),
                  pl.BlockSpec(memory_space=pltpu.VMEM),
                  pl.BlockSpec(memory_space=pltpu.VMEM)],
        out_shape=jax.ShapeDtypeStruct((valids.shape[0], B_KV_HEADS, 2 * rows, 2 * KEY_SLOTS), F32),
        name="bias_prep",
    )(table.astype(F32), jnp.asarray(_t5_bucket(dist), jnp.int32), jnp.asarray(valids, jnp.int32))


def _block_diag_ones():
    r = lax.broadcasted_iota(jnp.int32, (LANES, LANES), 0) // B_HD
    c = lax.broadcasted_iota(jnp.int32, (LANES, LANES), 1) // B_HD
    return (r == c).astype(BF16)


def _head_norm(x, g2, bd):
    hi, lo = _split_bf16(x * x)
    ss = _dot(hi, bd) + _dot(lo, bd)
    return x * lax.rsqrt(ss * (1.0 / B_HD) + RMS_EPS) * g2


def _place(tile, half, lo_mask):
    rolled = pltpu.roll(tile, B_HD, axis=1)
    zero = jnp.zeros_like(tile)
    if half == 0:
        return jnp.where(lo_mask, tile, zero), jnp.where(lo_mask, zero, rolled)
    return jnp.where(lo_mask, rolled, zero), jnp.where(lo_mask, zero, tile)


def _swa_core(qn_tiles, k_tiles, v_tiles, bias_at, sink_ref, store):
    rows = qn_tiles[0].shape[0]
    lo_mask = lax.broadcasted_iota(jnp.int32, k_tiles[0].shape, 1) < B_HD
    out_lo = lax.broadcasted_iota(jnp.int32, (rows, LANES), 1) < B_HD
    scores, values = [], []
    for j in range(B_KV_HEADS):
        t, half = j // 2, j % 2
        k_lo, k_hi = _place(k_tiles[t], half, lo_mask)
        v_lo, v_hi = _place(v_tiles[t], half, lo_mask)
        kk = jnp.concatenate([k_lo, k_hi], axis=0).astype(BF16)
        values.append(jnp.concatenate([v_lo, v_hi], axis=0).astype(BF16))
        qq = jnp.concatenate([qn_tiles[2 * j], qn_tiles[2 * j + 1]], axis=0).astype(BF16)
        scores.append(_dot_nt(qq, kk) + bias_at(j))
    weights, scales = [], []
    for j in range(B_KV_HEADS):
        e_rows, inv = [], []
        for r in range(2):
            e_cols, inv_r = [], []
            for c in range(2):
                sk = sink_ref[4 * j + 2 * r + c] * LOG2E
                sb = scores[j][r * rows:(r + 1) * rows, c * KEY_SLOTS:(c + 1) * KEY_SLOTS]
                m = jnp.maximum(jnp.max(sb, axis=-1, keepdims=True), sk)
                e = jnp.exp2(sb - m)
                inv_r.append(1.0 / (jnp.sum(e, axis=-1, keepdims=True) + jnp.exp2(sk - m)))
                e_cols.append(e.astype(BF16))
            e_rows.append(jnp.concatenate(e_cols, axis=1))
            inv.append(jnp.where(out_lo, inv_r[0], inv_r[1]))
        weights.append(jnp.concatenate(e_rows, axis=0))
        scales.append(inv)
    for j in range(B_KV_HEADS):
        o = _dot(weights[j], values[j])
        for r in range(2):
            store(2 * j + r, (o[r * rows:(r + 1) * rows] * scales[j][r]).astype(BF16))


def _swa_prompt_kernel(sink_ref, q_ref, kc_ref, kp_ref, vc_ref, vp_ref, qg_ref, kg_ref, bias_ref,
                       ob_ref, ko_ref, vo_ref):
    n = pl.program_id(1)
    bd = _block_diag_ones()
    k_all = jnp.concatenate([kp_ref[0], kc_ref[0]], axis=0)
    v_all = jnp.concatenate([vp_ref[0], vc_ref[0]], axis=0)
    kg = kg_ref[...]
    k_tiles = [_head_norm(k_all[:, t * LANES:(t + 1) * LANES], kg, bd) for t in range(2)]
    v_tiles = [v_all[:, t * LANES:(t + 1) * LANES] for t in range(2)]
    q = q_ref[0]
    qg = qg_ref[...]
    qn_tiles = [_head_norm(q[:, t * LANES:(t + 1) * LANES], qg, bd) for t in range(B_WIDTH // LANES)]

    def store(tile, val):
        ob_ref[0, :, tile * LANES:(tile + 1) * LANES] = val

    _swa_core(qn_tiles, k_tiles, v_tiles, lambda j: bias_ref[j], sink_ref, store)

    @pl.when(n == pl.num_programs(1) - 1)
    def _():
        ko_ref[0] = jnp.concatenate([kt[WINDOW:] for kt in k_tiles], axis=1)
        vo_ref[0] = vc_ref[0]


def _prompt_dist():
    return WINDOW + np.arange(WINDOW)[:, None] - np.arange(KEY_SLOTS)[None, :]


def _prompt_valid():
    j = np.arange(KEY_SLOTS)[None, :]
    dist = _prompt_dist()
    cur = (j >= WINDOW) & (dist >= 0)
    prev = (j < WINDOW) & (dist < WINDOW)
    return np.stack([cur, cur | prev]).astype(np.int32)


def _swa_prompt(u3, sinks, qg, kg, bias):
    bsz, length, _ = u3.shape
    nb = length // WINDOW
    qb, kb, vb = Q_B // B_WIDTH, K_B // B_KV_WIDTH, V_B // B_KV_WIDTH
    assert Q_B % B_WIDTH == 0 and K_B % B_KV_WIDTH == 0 and V_B % B_KV_WIDTH == 0
    kv_spec = lambda col, prev: pl.BlockSpec(
        (1, WINDOW, B_KV_WIDTH),
        (lambda b, n: (b, jnp.maximum(n - 1, 0), col)) if prev else (lambda b, n: (b, n, col)))
    g_spec = pl.BlockSpec((1, LANES), lambda b, n: (0, 0))
    nbytes = (2 * WINDOW * (B_WIDTH + 4 * B_KV_WIDTH) * 4 + bias.size * 4
              + 2 * WINDOW * B_WIDTH * 2 + 24 * WINDOW * 2 * KEY_SLOTS * 4)
    return pl.pallas_call(
        _swa_prompt_kernel,
        grid=(bsz, nb),
        in_specs=[pl.BlockSpec(memory_space=pltpu.SMEM),
                  pl.BlockSpec((1, WINDOW, B_WIDTH), lambda b, n: (b, n, qb)),
                  kv_spec(kb, False), kv_spec(kb, True), kv_spec(vb, False), kv_spec(vb, True),
                  g_spec, g_spec,
                  pl.BlockSpec((None,) + bias.shape[1:], lambda b, n: (jnp.minimum(n, 1), 0, 0, 0))],
        out_specs=(pl.BlockSpec((1, WINDOW, B_WIDTH), lambda b, n: (b, n, 0)),
                   pl.BlockSpec((1, WINDOW, B_KV_WIDTH), lambda b, n: (b, 0, 0)),
                   pl.BlockSpec((1, WINDOW, B_KV_WIDTH), lambda b, n: (b, 0, 0))),
        out_shape=(jax.ShapeDtypeStruct((bsz, length, B_WIDTH), BF16),
                   jax.ShapeDtypeStruct((bsz, WINDOW, B_KV_WIDTH), F32),
                   jax.ShapeDtypeStruct((bsz, WINDOW, B_KV_WIDTH), F32)),
        compiler_params=_params(("parallel", "arbitrary"), nbytes),
        name="swa_prompt",
    )(sinks, u3, u3, u3, u3, u3, qg, kg, bias)


def _swa_sample_kernel(sink_ref, q_ref, kv_ref, ck_ref, cv_ref, qg_ref, kg_ref, bias_ref,
                       ob_ref, ko_ref, vo_ref, *, steps):
    bd = _block_diag_ones()
    rows = q_ref.shape[1]
    kg = kg_ref[...]
    qg = qg_ref[...]
    pad = jnp.zeros((KEY_SLOTS - WINDOW - rows, LANES), F32)
    for i in range(q_ref.shape[0]):
        kv = kv_ref[i]
        k_new = [_head_norm(kv[:, t * LANES:(t + 1) * LANES], kg, bd) for t in range(2)]
        v_new = [kv[:, B_KV_WIDTH + t * LANES:B_KV_WIDTH + (t + 1) * LANES] for t in range(2)]
        ck, cv = ck_ref[i], cv_ref[i]
        k_tiles = [jnp.concatenate([ck[:, t * LANES:(t + 1) * LANES], k_new[t], pad], axis=0)
                   for t in range(2)]
        v_tiles = [jnp.concatenate([cv[:, t * LANES:(t + 1) * LANES], v_new[t], pad], axis=0)
                   for t in range(2)]
        q = q_ref[i]
        qn_tiles = [_head_norm(q[:, t * LANES:(t + 1) * LANES], qg, bd) for t in range(B_WIDTH // LANES)]

        def store(tile, val, i=i):
            ob_ref[i, :, tile * LANES:(tile + 1) * LANES] = val

        _swa_core(qn_tiles, k_tiles, v_tiles, lambda j: bias_ref[0, j], sink_ref, store)
        ko_ref[i, 0:WINDOW - steps, :] = ck_ref[i, steps:WINDOW, :]
        vo_ref[i, 0:WINDOW - steps, :] = cv_ref[i, steps:WINDOW, :]
        ko_ref[i, WINDOW - steps:WINDOW, :] = jnp.concatenate([kt[0:steps] for kt in k_new], axis=1)
        vo_ref[i, WINDOW - steps:WINDOW, :] = kv[0:steps, B_KV_WIDTH:]


def _sample_dist(rows):
    return WINDOW + np.arange(rows)[:, None] - np.arange(KEY_SLOTS)[None, :]


def _sample_valid(rows, steps):
    dist = _sample_dist(rows)
    j = np.arange(KEY_SLOTS)[None, :]
    return ((dist >= 0) & (dist < WINDOW) & (j < WINDOW + steps)).astype(np.int32)[None]


def _swa_sample(u3, cache_k, cache_v, layer, sinks, qg, kg, bias, steps):
    bsz, rows, _ = u3.shape
    w = cache_k.shape[2]
    bt = SWA_SAMPLE_SEQS
    assert w == WINDOW and bsz % bt == 0
    qb, kvb = Q_B // B_WIDTH, K_B // (2 * B_KV_WIDTH)
    assert K_B % (2 * B_KV_WIDTH) == 0
    c_in_spec = pl.BlockSpec((None, bt, w, B_KV_WIDTH), lambda b: (layer, b, 0, 0))
    c_spec = pl.BlockSpec((bt, w, B_KV_WIDTH), lambda b: (b, 0, 0))
    g_spec = pl.BlockSpec((1, LANES), lambda b: (0, 0))
    nbytes = bt * 8 * w * B_KV_WIDTH * 4 + 2 * bias.size * 4 + bt * 16 * KEY_SLOTS * LANES * 4
    return pl.pallas_call(
        functools.partial(_swa_sample_kernel, steps=steps),
        grid=(bsz // bt,),
        in_specs=[pl.BlockSpec(memory_space=pltpu.SMEM),
                  pl.BlockSpec((bt, rows, B_WIDTH), lambda b: (b, 0, qb)),
                  pl.BlockSpec((bt, rows, 2 * B_KV_WIDTH), lambda b: (b, 0, kvb)),
                  c_in_spec, c_in_spec, g_spec, g_spec,
                  pl.BlockSpec(bias.shape, lambda b: (0, 0, 0, 0))],
        out_specs=(pl.BlockSpec((bt, rows, B_WIDTH), lambda b: (b, 0, 0)), c_spec, c_spec),
        out_shape=(jax.ShapeDtypeStruct((bsz, rows, B_WIDTH), BF16),
                   jax.ShapeDtypeStruct((bsz, w, B_KV_WIDTH), F32),
                   jax.ShapeDtypeStruct((bsz, w, B_KV_WIDTH), F32)),
        compiler_params=_params(("parallel",), nbytes),
        name="swa_sample",
    )(sinks, u3, u3, cache_k, cache_v, qg, kg, bias)


def kernel(x_prompt, x_sample, cache_k, cache_v, state_hgrn, norm_mix, w_in, lower_bounds, hgrn_norm,
           q_norm, k_norm, attn_sinks, rel_bias_table, w_out, norm_ffn, w_gate, w_up, w_down):
    depth = w_in.shape[0]
    bp, seq, d = x_prompt.shape
    bd, ld, _ = x_sample.shape
    w = cache_k.shape[2]
    in_width = w_in.shape[2]

    w_in_b, w_out_b = w_in.astype(BF16), w_out.astype(BF16)
    w_gate_b, w_up_b, w_down_b = w_gate.astype(BF16), w_up.astype(BF16), w_down.astype(BF16)
    lbf, oml = _lower_bounds(lower_bounds)
    sinks = attn_sinks.astype(F32)
    qg2 = jnp.tile(q_norm.astype(F32), (1, LANES // B_HD)) * (ATTN_SCALE * LOG2E)
    kg2 = jnp.tile(k_norm.astype(F32), (1, LANES // B_HD))
    ck = cache_k.astype(F32).reshape(depth, bd, w, B_KV_WIDTH)
    cv = cache_v.astype(F32).reshape(depth, bd, w, B_KV_WIDTH)

    xp = x_prompt.reshape(bp * seq, d)
    xs = jnp.pad(x_sample, ((0, 0), (0, SAMPLE_ROWS - ld), (0, 0))).reshape(bd * SAMPLE_ROWS, d)
    tm_p, tm_s = 1024, bd * SAMPLE_ROWS
    bias_p = _bias_layout(rel_bias_table, _prompt_dist(), _prompt_valid())
    bias_s = _bias_layout(rel_bias_table, _sample_dist(SAMPLE_ROWS), _sample_valid(SAMPLE_ROWS, ld))
    state = state_hgrn.astype(F32)

    pk, pv, ps, sk, sv, ss = [], [], [], [], [], []
    for l in range(depth):
        lbf_l, oml_l = lbf[l:l + 1], oml[l:l + 1]
        u = _norm_matmul(xp, norm_mix[l], w_in_b, l, tm_p, 1408).reshape(bp, seq, in_width)
        oa, st = _hgrn_prompt(u, lbf_l, oml_l, hgrn_norm[l])
        ob, kn, vn = _swa_prompt(u, sinks[l], qg2[l:l + 1], kg2[l:l + 1], bias_p)
        xp = _outproj(xp, oa.reshape(bp * seq, A_WIDTH), ob.reshape(bp * seq, B_WIDTH), w_out_b, l, 512)
        xp = _ffn(xp, norm_ffn[l], w_gate_b, w_up_b, w_down_b, l, tm_p, 512)
        pk.append(kn)
        pv.append(vn)
        ps.append(st)
        u = _norm_matmul(xs, norm_mix[l], w_in_b, l, tm_s, 512).reshape(bd, SAMPLE_ROWS, in_width)
        oa, st = _hgrn_sample(u, lbf_l, oml_l, hgrn_norm[l], state, l, ld)
        ob, kn, vn = _swa_sample(u, ck, cv, l, sinks[l], qg2[l:l + 1], kg2[l:l + 1], bias_s, ld)
        xs = _outproj(xs, oa.reshape(tm_s, A_WIDTH), ob.reshape(tm_s, B_WIDTH), w_out_b, l, tm_s)
        xs = _ffn(xs, norm_ffn[l], w_gate_b, w_up_b, w_down_b, l, tm_s, 512)
        sk.append(kn)
        sv.append(vn)
        ss.append(st)

    kv_shape = lambda n: (depth, n, WINDOW, B_KV_HEADS, B_HD)
    return (xp.reshape(bp, seq, d),
            xs.reshape(bd, SAMPLE_ROWS, d)[:, :ld],
            jnp.stack(pk).reshape(kv_shape(bp)), jnp.stack(pv).reshape(kv_shape(bp)), jnp.stack(ps),
            jnp.stack(sk).reshape(kv_shape(bd)), jnp.stack(sv).reshape(kv_shape(bd)), jnp.stack(ss))
```

```python
import functools
import math

import jax
import jax.numpy as jnp
import numpy as np
from jax import lax
from jax.experimental import pallas as pl
from jax.experimental.pallas import tpu as pltpu

F32 = jnp.float32
BF16 = jnp.bfloat16

LANES = 128
SUBLANES = 8
VMEM_BYTES_V7X = 64 * 1024 * 1024
VMEM_CAP = VMEM_BYTES_V7X - 8 * 1024 * 1024

A_HEADS = 8
A_DK = 128
A_DV = 128
A_KEY_WIDTH = A_HEADS * A_DK
A_WIDTH = A_HEADS * A_DV
B_HEADS = 16
B_KV_HEADS = 4
B_HD = 64
B_WIDTH = B_HEADS * B_HD
B_KV_WIDTH = B_KV_HEADS * B_HD
WINDOW = 128
ATTN_SCALE = 1.0 / math.sqrt(B_HD)
LOG2E = math.log2(math.e)
NEG_LOGIT = -1e30
LB_FLOOR = 1e-20
N_BUCKETS = 32
MAX_DISTANCE = 128
RMS_EPS = 1e-6

Q_A, F_A, I_A, G_A = 0, A_KEY_WIDTH, 2 * A_KEY_WIDTH, 2 * A_KEY_WIDTH + A_WIDTH
Q_B = 2 * A_KEY_WIDTH + 2 * A_WIDTH
K_B = Q_B + B_WIDTH
V_B = K_B + B_KV_WIDTH

HGRN_CHUNK = 128
HGRN_FINE = 4
HGRN_HEADS_PER_STEP = 4
SAMPLE_ROWS = SUBLANES
SWA_SAMPLE_SEQS = 4
KEY_SLOTS = 2 * WINDOW


def _vmem_limit(nbytes):
    return int(min(VMEM_CAP, nbytes * 5 // 4 + (4 << 20)))


def _params(sem, nbytes):
    return pltpu.CompilerParams(dimension_semantics=sem, vmem_limit_bytes=_vmem_limit(nbytes))


def _sig_pair(z):
    t = jnp.exp(-jnp.abs(z))
    r = 1.0 / (1.0 + t)
    tr = t * r
    pos = z >= 0
    return jnp.where(pos, r, tr), jnp.where(pos, tr, r)


def _silu(x):
    return x * _sig_pair(x)[0]


def _rms_rows(x, g):
    ms = jnp.mean(x * x, axis=-1, keepdims=True)
    return x * lax.rsqrt(ms + RMS_EPS) * g


def _split_bf16(x):
    hi = x.astype(BF16)
    lo = (x - hi.astype(F32)).astype(BF16)
    return hi, lo


def _dot(a, b):
    return jnp.dot(a, b, preferred_element_type=F32)


def _dot_nt(a, b):
    return lax.dot_general(a, b, (((1,), (1,)), ((), ())), preferred_element_type=F32)


def _lb_kernel(lb_ref, lbf_ref, oml_ref):
    depth = lb_ref.shape[0]
    rows = [lb_ref[i:i + 1, :] for i in range(depth)]
    m = functools.reduce(jnp.maximum, rows)
    e = [jnp.exp(r - m) for r in rows]
    s = functools.reduce(lambda a, b: a + b, e)
    p = [ei / s for ei in e]
    cum = p[0]
    for i in range(depth):
        if i > 0:
            cum = cum + p[i]
        lb = cum - p[0]
        lbf_ref[i:i + 1, :] = jnp.maximum(lb, LB_FLOOR)
        oml_ref[i:i + 1, :] = 1.0 - lb


def _lower_bounds(lower_bounds):
    shp = jax.ShapeDtypeStruct(lower_bounds.shape, F32)
    return pl.pallas_call(_lb_kernel, out_shape=(shp, shp), name="lb_prep")(lower_bounds.astype(F32))


def _weight_spec(w, layer, block, index):
    if w.ndim == 2:
        return pl.BlockSpec(block, index)
    return pl.BlockSpec((None,) + block, lambda *g: (layer,) + index(*g))


def _bf16_weight(w_ref, copy_refs):
    w = w_ref[...]
    if copy_refs:
        w = w.astype(BF16)
        copy_refs[0][...] = w
    return w


def _norm_matmul_kernel(x_ref, g_ref, w_ref, o_ref, *rest):
    h_ref = rest[-1]

    @pl.when(pl.program_id(1) == 0)
    def _():
        h_ref[...] = _rms_rows(x_ref[...], g_ref[...]).astype(BF16)

    o_ref[...] = _dot(h_ref[...], _bf16_weight(w_ref, rest[:-1]))


def _norm_matmul(x, g, w, layer, tm, tn):
    m, d = x.shape
    n = w.shape[-1]
    cast = w.ndim == 3
    assert not cast or m == tm
    wbytes = 4 + 2 if cast else 2
    nbytes = 2 * tm * d * 4 + tm * d * 2 + 2 * d * tn * wbytes + 2 * tm * tn * 4
    u_spec = pl.BlockSpec((tm, tn), lambda i, j: (i, j))
    u_shape = jax.ShapeDtypeStruct((m, n), F32)
    w_block, w_index = (d, tn), lambda i, j: (0, j)
    return pl.pallas_call(
        _norm_matmul_kernel,
        grid=(m // tm, n // tn),
        in_specs=[pl.BlockSpec((tm, d), lambda i, j: (i, 0)),
                  pl.BlockSpec((1, d), lambda i, j: (0, 0)),
                  _weight_spec(w, layer, w_block, w_index)],
        out_specs=(u_spec, pl.BlockSpec(w_block, w_index)) if cast else u_spec,
        out_shape=(u_shape, jax.ShapeDtypeStruct((d, n), BF16)) if cast else u_shape,
        scratch_shapes=[pltpu.VMEM((tm, d), BF16)],
        compiler_params=_params(("parallel", "arbitrary"), nbytes),
        name="norm_matmul",
    )(x, g.reshape(1, d), w)


def _outproj_kernel(x_ref, oa_ref, ob_ref, w_ref, o_ref, *rest):
    ka = oa_ref.shape[1]
    w = _bf16_weight(w_ref, rest)
    o_ref[...] = x_ref[...] + _dot(oa_ref[...], w[0:ka]) + _dot(ob_ref[...], w[ka:])


def _outproj(x, oa, ob, w, layer, tm, tn):
    m, d = x.shape
    ka, kb = oa.shape[1], ob.shape[1]
    cast = w.ndim == 3
    assert not cast or m == tm
    wbytes = 4 + 2 if cast else 2
    nbytes = 4 * tm * tn * 4 + 2 * tm * (ka + kb) * 2 + 2 * (ka + kb) * tn * wbytes
    x_spec = pl.BlockSpec((tm, tn), lambda i, j: (i, j))
    x_shape = jax.ShapeDtypeStruct((m, d), F32)
    w_block, w_index = (ka + kb, tn), lambda i, j: (0, j)
    return pl.pallas_call(
        _outproj_kernel,
        grid=(m // tm, d // tn),
        in_specs=[x_spec,
                  pl.BlockSpec((tm, ka), lambda i, j: (i, 0)),
                  pl.BlockSpec((tm, kb), lambda i, j: (i, 0)),
                  _weight_spec(w, layer, w_block, w_index)],
        out_specs=(x_spec, pl.BlockSpec(w_block, w_index)) if cast else x_spec,
        out_shape=(x_shape, jax.ShapeDtypeStruct((ka + kb, d), BF16)) if cast else x_shape,
        compiler_params=_params(("parallel", "arbitrary"), nbytes),
        name="outproj",
    )(x, oa, ob, w)


def _ffn_kernel(x_ref, g_ref, wg_ref, wu_ref, wd_ref, o_ref, *rest):
    h_ref = rest[-1]
    copies = rest[:-1]

    @pl.when(pl.program_id(1) == 0)
    def _():
        x = x_ref[...]
        h_ref[...] = _rms_rows(x, g_ref[...]).astype(BF16)
        o_ref[...] = x

    h = h_ref[...]
    gate = _dot(h, _bf16_weight(wg_ref, copies[0:1]))
    up = _dot(h, _bf16_weight(wu_ref, copies[1:2]))
    act = (_silu(gate) * up).astype(BF16)
    o_ref[...] += _dot(act, _bf16_weight(wd_ref, copies[2:3]))


def _ffn(x, g, wg, wu, wd, layer, tm, tf):
    m, d = x.shape
    f = wg.shape[-1]
    cast = wg.ndim == 3
    assert not cast or m == tm
    wbytes = 4 + 2 if cast else 2
    nbytes = 4 * tm * d * 4 + tm * d * 2 + 2 * 3 * d * tf * wbytes + 3 * tm * tf * 4
    x_spec = pl.BlockSpec((tm, d), lambda i, j: (i, 0))
    x_shape = jax.ShapeDtypeStruct((m, d), F32)
    col_block, col_index = (d, tf), lambda i, j: (0, j)
    row_block, row_index = (tf, d), lambda i, j: (j, 0)
    copy_specs = (pl.BlockSpec(col_block, col_index), pl.BlockSpec(col_block, col_index),
                  pl.BlockSpec(row_block, row_index))
    copy_shapes = (jax.ShapeDtypeStruct((d, f), BF16), jax.ShapeDtypeStruct((d, f), BF16),
                   jax.ShapeDtypeStruct((f, d), BF16))
    return pl.pallas_call(
        _ffn_kernel,
        grid=(m // tm, f // tf),
        in_specs=[x_spec,
                  pl.BlockSpec((1, d), lambda i, j: (0, 0)),
                  _weight_spec(wg, layer, col_block, col_index),
                  _weight_spec(wu, layer, col_block, col_index),
                  _weight_spec(wd, layer, row_block, row_index)],
        out_specs=(x_spec,) + copy_specs if cast else x_spec,
        out_shape=(x_shape,) + copy_shapes if cast else x_shape,
        scratch_shapes=[pltpu.VMEM((tm, d), BF16)],
        compiler_params=_params(("parallel", "arbitrary"), nbytes),
        name="ffn",
    )(x, g.reshape(1, d), wg, wu, wd)


def _hgrn_static(c):
    t = np.arange(c)
    blocks = [(t[None, :] <= t[:, None])]
    masks = []
    h = c // 2
    while h >= 1:
        grp = t // (2 * h)
        mid = grp * 2 * h + h
        upper = (t % (2 * h)) >= h
        if h < HGRN_FINE:
            col = t[None, :]
            p_up = (col >= mid[:, None]) & (col <= t[:, None])
            p_lo = (col > t[:, None]) & (col < mid[:, None])
            blocks.append(np.where(upper[:, None], p_up, p_lo))
        masks.append((grp[:, None] == grp[None, :]) & upper[:, None] & ~upper[None, :])
        h //= 2
    masks.append(np.eye(c, dtype=bool))
    p_all = np.concatenate(blocks, axis=0).astype(np.float32)
    p2 = np.concatenate([p_all, p_all], axis=1)
    masks = np.stack(masks).astype(np.float32)
    return p2, np.concatenate([masks, masks], axis=2)


def _hgrn_prompt_kernel(q_ref, f_ref, i_ref, g_ref, lbf_ref, oml_ref, gn_ref, p2_ref, m_ref,
                        oa_ref, st_ref, s_scr, *, chunk, heads):
    c_idx = pl.program_id(2)

    @pl.when(c_idx == 0)
    def _():
        s_scr[...] = jnp.zeros_like(s_scr)

    width = heads * A_DK
    pw = 2 * A_DK
    sz, snz = _sig_pair(f_ref[0])
    oml = oml_ref[...]
    lf = jnp.log(lbf_ref[...] + oml * sz)
    ka = oml * snz
    hi, lo = _split_bf16(lf)
    expo = _dot(p2_ref[...], jnp.concatenate([hi, lo], axis=0))
    b = expo[0:chunk]
    q = _silu(q_ref[0])
    v = i_ref[0]
    gate = _silu(g_ref[0])
    gn = gn_ref[...]

    row = lax.broadcasted_iota(jnp.int32, (chunk, width), 0)
    qs, ks = [], []
    h, fine = chunk // 2, 0
    while h >= 1:
        if h >= HGRN_FINE:
            groups = chunk // (2 * h)
            mid = jnp.broadcast_to(b.reshape(groups, 2 * h, width)[:, h - 1:h, :], (groups, 2 * h, width))
            d = b - mid.reshape(chunk, width)
            e = jnp.where((row & h) != 0, d, -d)
        else:
            fine += 1
            e = expo[fine * chunk:(fine + 1) * chunk]
        e = jnp.exp(e)
        qs.append((q * e).astype(BF16))
        ks.append((ka * e).astype(BF16))
        h //= 2
    qs.append(q.astype(BF16))
    ks.append(ka.astype(BF16))
    b_last = b[chunk - 1:chunk, :]
    q0 = (q * jnp.exp(b)).astype(BF16)
    k_tail = (ka * jnp.exp(b_last - b)).astype(BF16)
    decay = jnp.exp(b_last)
    vb = v.astype(BF16)

    left = lax.broadcasted_iota(jnp.int32, (chunk, pw), 1) < A_DK

    def bdiag(x):
        z = jnp.zeros_like(x)
        return jnp.concatenate([jnp.where(left, x, z), jnp.where(left, z, x)], axis=0)

    for p in range(heads // 2):
        sl = slice(p * pw, (p + 1) * pw)
        st = s_scr[p]
        o = _dot_nt(q0[:, sl], bdiag(st.astype(BF16)))
        a = jnp.zeros((chunk, pw), F32)
        for li in range(len(qs)):
            a = a + _dot_nt(qs[li][:, sl], bdiag(ks[li][:, sl])) * m_ref[li]
        o = o + _dot(a.astype(BF16), bdiag(vb[:, sl]))
        vp = v[:, sl]
        vt = jnp.concatenate([vp[:, :A_DV].T, vp[:, A_DV:].T], axis=1).astype(BF16)
        s_scr[p] = st * decay[:, sl] + _dot(vt, bdiag(k_tail[:, sl]))
        for hh in range(2):
            hs = slice(p * pw + hh * A_DV, p * pw + (hh + 1) * A_DV)
            oa_ref[0, :, hs] = (_rms_rows(o[:, hh * A_DV:(hh + 1) * A_DV], gn) * gate[:, hs]).astype(BF16)

    @pl.when(c_idx == pl.num_programs(2) - 1)
    def _():
        for hh in range(heads):
            st_ref[0, hh] = s_scr[hh // 2][:, (hh % 2) * A_DK:(hh % 2 + 1) * A_DK].T


def _hgrn_prompt(u3, lbf, oml, gn):
    bsz, length, _ = u3.shape
    chunk, heads = HGRN_CHUNK, HGRN_HEADS_PER_STEP
    cols = heads * A_DK
    assert length % chunk == 0 and A_HEADS % heads == 0 and heads % 2 == 0 and chunk == A_DV
    p2, masks = _hgrn_static(chunk)
    p2 = jnp.asarray(p2, BF16)
    masks = jnp.asarray(masks, F32)

    def col_spec(offset):
        base = offset // cols
        return pl.BlockSpec((1, chunk, cols), lambda b, h, c: (b, c, base + h))

    vec_spec = pl.BlockSpec((1, cols), lambda b, h, c: (0, h))
    nbytes = (2 * 5 * chunk * cols * 4 + 2 * (p2.size * 2 + masks.size * 4)
              + 6 * p2.shape[0] * cols * 4)
    return pl.pallas_call(
        functools.partial(_hgrn_prompt_kernel, chunk=chunk, heads=heads),
        grid=(bsz, A_HEADS // heads, length // chunk),
        in_specs=[col_spec(Q_A), col_spec(F_A), col_spec(I_A), col_spec(G_A),
                  vec_spec, vec_spec,
                  pl.BlockSpec((1, A_DV), lambda b, h, c: (0, 0)),
                  pl.BlockSpec(p2.shape, lambda b, h, c: (0, 0)),
                  pl.BlockSpec(masks.shape, lambda b, h, c: (0, 0, 0))],
        out_specs=(pl.BlockSpec((1, chunk, cols), lambda b, h, c: (b, c, h)),
                   pl.BlockSpec((1, heads, A_DK, A_DV), lambda b, h, c: (b, h, 0, 0))),
        out_shape=(jax.ShapeDtypeStruct((bsz, length, A_WIDTH), BF16),
                   jax.ShapeDtypeStruct((bsz, A_HEADS, A_DK, A_DV), F32)),
        scratch_shapes=[pltpu.VMEM((heads // 2, A_DV, 2 * A_DK), F32)],
        compiler_params=_params(("parallel", "parallel", "arbitrary"), nbytes),
        name="hgrn_prompt",
    )(u3, u3, u3, u3, lbf, oml, gn.reshape(1, A_DV), p2, masks)


def _hgrn_sample_kernel(u_ref, lbf_ref, oml_ref, gn_ref, s_ref, oa_ref, so_ref, *, steps):
    u = u_ref[0]
    q = _silu(u[:, Q_A:Q_A + A_KEY_WIDTH])
    sz, snz = _sig_pair(u[:, F_A:F_A + A_KEY_WIDTH])
    oml = oml_ref[...]
    f = lbf_ref[...] + oml * sz
    ka = oml * snz
    v = u[:, I_A:I_A + A_WIDTH]
    gate = _silu(u[:, G_A:G_A + A_WIDTH])
    gn = gn_ref[...]
    rows = u.shape[0]
    row = lax.broadcasted_iota(jnp.int32, f.shape, 0)
    live = row < steps
    f = jnp.where(live, f, 1.0)
    ka = jnp.where(live, ka, 0.0)

    def down(x, d, fill):
        return x if d == 0 else jnp.where(row >= d, pltpu.roll(x, d, axis=0), fill)

    def up(x, d, fill):
        return jnp.where(row < rows - d, pltpu.roll(x, rows - d, axis=0), fill)

    decay = [jnp.ones_like(f)]
    for d in range(1, steps + 1):
        decay.append(decay[-1] * down(f, d - 1, 1.0))
    head = decay[steps]
    tail = jnp.ones_like(f)
    for d in range(1, steps):
        tail = tail * up(f, d, 1.0)
    q_in = (q * head).astype(BF16)
    k_tail = ka * tail
    total = head[steps - 1:steps, :]
    t_hi = total.astype(BF16).astype(F32)
    t_mid = (total - t_hi).astype(BF16).astype(F32)
    t_lo = (total - t_hi - t_mid).astype(BF16).astype(F32)
    t_rows = jnp.where(row == 0, t_hi, jnp.where(row == 1, t_mid, jnp.where(row == 2, t_lo, 0.0)))
    prods = [q * down(ka, d, 0.0) * decay[d] for d in range(steps)]
    v_sh = [down(v, d, 0.0) for d in range(steps)]

    zsq = jnp.zeros((A_DK, A_DV), BF16)
    zrows = jnp.zeros((rows, A_DV), F32)
    r2 = lax.broadcasted_iota(jnp.int32, (rows, 2 * A_DV), 0)
    l2 = lax.broadcasted_iota(jnp.int32, (rows, 2 * A_DV), 1)
    ones_right = jnp.where((r2 < 3) & (l2 >= A_DV), 1.0, 0.0)

    for p in range(A_HEADS // 2):
        ps = slice(2 * p * A_DK, (2 * p + 2) * A_DK)
        s0 = [s_ref[0, 2 * p + hh] for hh in range(2)]
        s_bd = jnp.concatenate([jnp.concatenate([s0[0].astype(BF16), zsq], axis=1),
                                jnp.concatenate([zsq, s0[1].astype(BF16)], axis=1)], axis=0)
        o_pair = _dot(q_in[:, ps], s_bd)
        for hh in range(2):
            head_i = 2 * p + hh
            sl = slice(head_i * A_DK, (head_i + 1) * A_DK)
            o = o_pair[:, hh * A_DV:(hh + 1) * A_DV]
            for d in range(steps):
                o = o + jnp.sum(prods[d][:, sl], axis=-1, keepdims=True) * v_sh[d][:, sl]
            lhs = jnp.concatenate([k_tail[:, sl], t_rows[:, sl]], axis=0).astype(BF16)
            rhs = jnp.concatenate([jnp.concatenate([v[:, sl], zrows], axis=1), ones_right],
                                  axis=0).astype(BF16)
            upd = lax.dot_general(lhs, rhs, (((0,), (0,)), ((), ())), preferred_element_type=F32)
            so_ref[0, head_i] = s0[hh] * upd[:, A_DV:] + upd[:, :A_DV]
            oa_ref[0, :, sl] = (_rms_rows(o, gn) * gate[:, sl]).astype(BF16)


def _hgrn_sample(u3, lbf, oml, gn, state, layer, steps):
    bsz, rows, _ = u3.shape
    width = G_A + A_WIDTH
    nbytes = 2 * rows * width * 4 + 4 * A_HEADS * A_DK * A_DV * 4
    return pl.pallas_call(
        functools.partial(_hgrn_sample_kernel, steps=steps),
        grid=(bsz,),
        in_specs=[pl.BlockSpec((1, rows, width), lambda b: (b, 0, 0)),
                  pl.BlockSpec((1, A_KEY_WIDTH), lambda b: (0, 0)),
                  pl.BlockSpec((1, A_KEY_WIDTH), lambda b: (0, 0)),
                  pl.BlockSpec((1, A_DV), lambda b: (0, 0)),
                  pl.BlockSpec((None, 1, A_HEADS, A_DK, A_DV), lambda b: (layer, b, 0, 0, 0))],
        out_specs=(pl.BlockSpec((1, rows, A_WIDTH), lambda b: (b, 0, 0)),
                   pl.BlockSpec((1, A_HEADS, A_DK, A_DV), lambda b: (b, 0, 0, 0))),
        out_shape=(jax.ShapeDtypeStruct((bsz, rows, A_WIDTH), BF16),
                   jax.ShapeDtypeStruct(state.shape[1:], F32)),
        compiler_params=_params(("parallel",), nbytes),
        name="hgrn_sample",
    )(u3, lbf, oml, gn.reshape(1, A_DV), state)


def _t5_bucket(dist):
    n = np.maximum(dist, 0)
    max_exact = N_BUCKETS // 2
    nf = np.maximum(n, 1).astype(np.float32)
    large = max_exact + (np.log(nf / max_exact) / math.log(MAX_DISTANCE / max_exact)
                         * (N_BUCKETS - max_exact)).astype(np.int32)
    large = np.minimum(large, N_BUCKETS - 1)
    return np.where(n < max_exact, n, large).astype(np.int32)


def _bias_kernel(tab_ref, bucket_ref, valid_ref, o_ref):
    rows = bucket_ref.shape[0]
    bucket = bucket_ref[...]
    hits = [bucket == b for b in range(N_BUCKETS)]
    for h in range(B_HEADS):
        acc = jnp.zeros(bucket.shape, F32)
        for b in range(N_BUCKETS):
            acc = jnp.where(hits[b], tab_ref[b, h], acc)
        acc = acc * LOG2E
        j, r, c = h // 4, (h % 4) // 2, h % 2
        for m in range(valid_ref.shape[0]):
            o_ref[m, j, r * rows:(r + 1) * rows, c * KEY_SLOTS:(c + 1) * KEY_SLOTS] = jnp.where(
                valid_ref[m] != 0, acc, NEG_LOGIT * LOG2E)


def _bias_layout(table, dist, valids):
    rows = dist.shape[0]
    return pl.pallas_call(
        _bias_kernel,
        in_specs=[pl.BlockSpec(memory_space=pltpu.SMEM),
                  pl.BlockSpec(memory_space=pltpu.VMEM),
                  pl.BlockSpec(memory_space=pltpu.VMEM)],
        out_shape=jax.ShapeDtypeStruct((valids.shape[0], B_KV_HEADS, 2 * rows, 2 * KEY_SLOTS), F32),
        name="bias_prep",
    )(table.astype(F32), jnp.asarray(_t5_bucket(dist), jnp.int32), jnp.asarray(valids, jnp.int32))


def _block_diag_ones():
    r = lax.broadcasted_iota(jnp.int32, (LANES, LANES), 0) // B_HD
    c = lax.broadcasted_iota(jnp.int32, (LANES, LANES), 1) // B_HD
    return (r == c).astype(BF16)


def _head_norm(x, g2, bd):
    hi, lo = _split_bf16(x * x)
    ss = _dot(hi, bd) + _dot(lo, bd)
    return x * lax.rsqrt(ss * (1.0 / B_HD) + RMS_EPS) * g2


def _place(tile, half, lo_mask):
    rolled = pltpu.roll(tile, B_HD, axis=1)
    zero = jnp.zeros_like(tile)
    if half == 0:
        return jnp.where(lo_mask, tile, zero), jnp.where(lo_mask, zero, rolled)
    return jnp.where(lo_mask, rolled, zero), jnp.where(lo_mask, zero, tile)


def _swa_core(qn_tiles, k_tiles, v_tiles, bias_at, sink_ref, store):
    rows = qn_tiles[0].shape[0]
    lo_mask = lax.broadcasted_iota(jnp.int32, k_tiles[0].shape, 1) < B_HD
    out_lo = lax.broadcasted_iota(jnp.int32, (rows, LANES), 1) < B_HD
    scores, values = [], []
    for j in range(B_KV_HEADS):
        t, half = j // 2, j % 2
        k_lo, k_hi = _place(k_tiles[t], half, lo_mask)
        v_lo, v_hi = _place(v_tiles[t], half, lo_mask)
        kk = jnp.concatenate([k_lo, k_hi], axis=0).astype(BF16)
        values.append(jnp.concatenate([v_lo, v_hi], axis=0).astype(BF16))
        qq = jnp.concatenate([qn_tiles[2 * j], qn_tiles[2 * j + 1]], axis=0).astype(BF16)
        scores.append(_dot_nt(qq, kk) + bias_at(j))
    weights, scales = [], []
    for j in range(B_KV_HEADS):
        e_rows, inv = [], []
        for r in range(2):
            e_cols, inv_r = [], []
            for c in range(2):
                sk = sink_ref[4 * j + 2 * r + c] * LOG2E
                sb = scores[j][r * rows:(r + 1) * rows, c * KEY_SLOTS:(c + 1) * KEY_SLOTS]
                m = jnp.maximum(jnp.max(sb, axis=-1, keepdims=True), sk)
                e = jnp.exp2(sb - m)
                inv_r.append(1.0 / (jnp.sum(e, axis=-1, keepdims=True) + jnp.exp2(sk - m)))
                e_cols.append(e.astype(BF16))
            e_rows.append(jnp.concatenate(e_cols, axis=1))
            inv.append(jnp.where(out_lo, inv_r[0], inv_r[1]))
        weights.append(jnp.concatenate(e_rows, axis=0))
        scales.append(inv)
    for j in range(B_KV_HEADS):
        o = _dot(weights[j], values[j])
        for r in range(2):
            store(2 * j + r, (o[r * rows:(r + 1) * rows] * scales[j][r]).astype(BF16))


def _swa_prompt_kernel(sink_ref, q_ref, kc_ref, kp_ref, vc_ref, vp_ref, qg_ref, kg_ref, bias_ref,
                       ob_ref, ko_ref, vo_ref):
    n = pl.program_id(1)
    bd = _block_diag_ones()
    k_all = jnp.concatenate([kp_ref[0], kc_ref[0]], axis=0)
    v_all = jnp.concatenate([vp_ref[0], vc_ref[0]], axis=0)
    kg = kg_ref[...]
    k_tiles = [_head_norm(k_all[:, t * LANES:(t + 1) * LANES], kg, bd) for t in range(2)]
    v_tiles = [v_all[:, t * LANES:(t + 1) * LANES] for t in range(2)]
    q = q_ref[0]
    qg = qg_ref[...]
    qn_tiles = [_head_norm(q[:, t * LANES:(t + 1) * LANES], qg, bd) for t in range(B_WIDTH // LANES)]

    def store(tile, val):
        ob_ref[0, :, tile * LANES:(tile + 1) * LANES] = val

    _swa_core(qn_tiles, k_tiles, v_tiles, lambda j: bias_ref[j], sink_ref, store)

    @pl.when(n == pl.num_programs(1) - 1)
    def _():
        ko_ref[0] = jnp.concatenate([kt[WINDOW:] for kt in k_tiles], axis=1)
        vo_ref[0] = vc_ref[0]


def _prompt_dist():
    return WINDOW + np.arange(WINDOW)[:, None] - np.arange(KEY_SLOTS)[None, :]


def _prompt_valid():
    j = np.arange(KEY_SLOTS)[None, :]
    dist = _prompt_dist()
    cur = (j >= WINDOW) & (dist >= 0)
    prev = (j < WINDOW) & (dist < WINDOW)
    return np.stack([cur, cur | prev]).astype(np.int32)


def _swa_prompt(u3, sinks, qg, kg, bias):
    bsz, length, _ = u3.shape
    nb = length // WINDOW
    qb, kb, vb = Q_B // B_WIDTH, K_B // B_KV_WIDTH, V_B // B_KV_WIDTH
    assert Q_B % B_WIDTH == 0 and K_B % B_KV_WIDTH == 0 and V_B % B_KV_WIDTH == 0
    kv_spec = lambda col, prev: pl.BlockSpec(
        (1, WINDOW, B_KV_WIDTH),
        (lambda b, n: (b, jnp.maximum(n - 1, 0), col)) if prev else (lambda b, n: (b, n, col)))
    g_spec = pl.BlockSpec((1, LANES), lambda b, n: (0, 0))
    nbytes = (2 * WINDOW * (B_WIDTH + 4 * B_KV_WIDTH) * 4 + bias.size * 4
              + 2 * WINDOW * B_WIDTH * 2 + 24 * WINDOW * 2 * KEY_SLOTS * 4)
    return pl.pallas_call(
        _swa_prompt_kernel,
        grid=(bsz, nb),
        in_specs=[pl.BlockSpec(memory_space=pltpu.SMEM),
                  pl.BlockSpec((1, WINDOW, B_WIDTH), lambda b, n: (b, n, qb)),
                  kv_spec(kb, False), kv_spec(kb, True), kv_spec(vb, False), kv_spec(vb, True),
                  g_spec, g_spec,
                  pl.BlockSpec((None,) + bias.shape[1:], lambda b, n: (jnp.minimum(n, 1), 0, 0, 0))],
        out_specs=(pl.BlockSpec((1, WINDOW, B_WIDTH), lambda b, n: (b, n, 0)),
                   pl.BlockSpec((1, WINDOW, B_KV_WIDTH), lambda b, n: (b, 0, 0)),
                   pl.BlockSpec((1, WINDOW, B_KV_WIDTH), lambda b, n: (b, 0, 0))),
        out_shape=(jax.ShapeDtypeStruct((bsz, length, B_WIDTH), BF16),
                   jax.ShapeDtypeStruct((bsz, WINDOW, B_KV_WIDTH), F32),
                   jax.ShapeDtypeStruct((bsz, WINDOW, B_KV_WIDTH), F32)),
        compiler_params=_params(("parallel", "arbitrary"), nbytes),
        name="swa_prompt",
    )(sinks, u3, u3, u3, u3, u3, qg, kg, bias)


def _swa_sample_kernel(sink_ref, q_ref, kv_ref, ck_ref, cv_ref, qg_ref, kg_ref, bias_ref,
                       ob_ref, ko_ref, vo_ref, *, steps):
    bd = _block_diag_ones()
    rows = q_ref.shape[1]
    kg = kg_ref[...]
    qg = qg_ref[...]
    pad = jnp.zeros((KEY_SLOTS - WINDOW - rows, LANES), F32)
    for i in range(q_ref.shape[0]):
        kv = kv_ref[i]
        k_new = [_head_norm(kv[:, t * LANES:(t + 1) * LANES], kg, bd) for t in range(2)]
        v_new = [kv[:, B_KV_WIDTH + t * LANES:B_KV_WIDTH + (t + 1) * LANES] for t in range(2)]
        ck, cv = ck_ref[i], cv_ref[i]
        k_tiles = [jnp.concatenate([ck[:, t * LANES:(t + 1) * LANES], k_new[t], pad], axis=0)
                   for t in range(2)]
        v_tiles = [jnp.concatenate([cv[:, t * LANES:(t + 1) * LANES], v_new[t], pad], axis=0)
                   for t in range(2)]
        q = q_ref[i]
        qn_tiles = [_head_norm(q[:, t * LANES:(t + 1) * LANES], qg, bd) for t in range(B_WIDTH // LANES)]

        def store(tile, val, i=i):
            ob_ref[i, :, tile * LANES:(tile + 1) * LANES] = val

        _swa_core(qn_tiles, k_tiles, v_tiles, lambda j: bias_ref[0, j], sink_ref, store)
        ko_ref[i, 0:WINDOW - steps, :] = ck_ref[i, steps:WINDOW, :]
        vo_ref[i, 0:WINDOW - steps, :] = cv_ref[i, steps:WINDOW, :]
        ko_ref[i, WINDOW - steps:WINDOW, :] = jnp.concatenate([kt[0:steps] for kt in k_new], axis=1)
        vo_ref[i, WINDOW - steps:WINDOW, :] = kv[0:steps, B_KV_WIDTH:]


def _sample_dist(rows):
    return WINDOW + np.arange(rows)[:, None] - np.arange(KEY_SLOTS)[None, :]


def _sample_valid(rows, steps):
    dist = _sample_dist(rows)
    j = np.arange(KEY_SLOTS)[None, :]
    return ((dist >= 0) & (dist < WINDOW) & (j < WINDOW + steps)).astype(np.int32)[None]


def _swa_sample(u3, cache_k, cache_v, layer, sinks, qg, kg, bias, steps):
    bsz, rows, _ = u3.shape
    w = cache_k.shape[2]
    bt = SWA_SAMPLE_SEQS
    assert w == WINDOW and bsz % bt == 0
    qb, kvb = Q_B // B_WIDTH, K_B // (2 * B_KV_WIDTH)
    assert K_B % (2 * B_KV_WIDTH) == 0
    c_in_spec = pl.BlockSpec((None, bt, w, B_KV_WIDTH), lambda b: (layer, b, 0, 0))
    c_spec = pl.BlockSpec((bt, w, B_KV_WIDTH), lambda b: (b, 0, 0))
    g_spec = pl.BlockSpec((1, LANES), lambda b: (0, 0))
    nbytes = bt * 8 * w * B_KV_WIDTH * 4 + 2 * bias.size * 4 + bt * 16 * KEY_SLOTS * LANES * 4
    return pl.pallas_call(
        functools.partial(_swa_sample_kernel, steps=steps),
        grid=(bsz // bt,),
        in_specs=[pl.BlockSpec(memory_space=pltpu.SMEM),
                  pl.BlockSpec((bt, rows, B_WIDTH), lambda b: (b, 0, qb)),
                  pl.BlockSpec((bt, rows, 2 * B_KV_WIDTH), lambda b: (b, 0, kvb)),
                  c_in_spec, c_in_spec, g_spec, g_spec,
                  pl.BlockSpec(bias.shape, lambda b: (0, 0, 0, 0))],
        out_specs=(pl.BlockSpec((bt, rows, B_WIDTH), lambda b: (b, 0, 0)), c_spec, c_spec),
        out_shape=(jax.ShapeDtypeStruct((bsz, rows, B_WIDTH), BF16),
                   jax.ShapeDtypeStruct((bsz, w, B_KV_WIDTH), F32),
                   jax.ShapeDtypeStruct((bsz, w, B_KV_WIDTH), F32)),
        compiler_params=_params(("parallel",), nbytes),
        name="swa_sample",
    )(sinks, u3, u3, cache_k, cache_v, qg, kg, bias)


def kernel(x_prompt, x_sample, cache_k, cache_v, state_hgrn, norm_mix, w_in, lower_bounds, hgrn_norm,
           q_norm, k_norm, attn_sinks, rel_bias_table, w_out, norm_ffn, w_gate, w_up, w_down):
    depth = w_in.shape[0]
    bp, seq, d = x_prompt.shape
    bd, ld, _ = x_sample.shape
    w = cache_k.shape[2]
    in_width = w_in.shape[2]

    lbf, oml = _lower_bounds(lower_bounds)
    sinks = attn_sinks.astype(F32)
    qg2 = jnp.tile(q_norm.astype(F32), (1, LANES // B_HD)) * (ATTN_SCALE * LOG2E)
    kg2 = jnp.tile(k_norm.astype(F32), (1, LANES // B_HD))
    ck = cache_k.astype(F32).reshape(depth, bd, w, B_KV_WIDTH)
    cv = cache_v.astype(F32).reshape(depth, bd, w, B_KV_WIDTH)

    xp = x_prompt.reshape(bp * seq, d)
    xs = jnp.pad(x_sample, ((0, 0), (0, SAMPLE_ROWS - ld), (0, 0))).reshape(bd * SAMPLE_ROWS, d)
    tm_p, tm_s = 1024, bd * SAMPLE_ROWS
    bias_p = _bias_layout(rel_bias_table, _prompt_dist(), _prompt_valid())
    bias_s = _bias_layout(rel_bias_table, _sample_dist(SAMPLE_ROWS), _sample_valid(SAMPLE_ROWS, ld))
    state = state_hgrn.astype(F32)

    pk, pv, ps, sk, sv, ss = [], [], [], [], [], []
    for l in range(depth):
        lbf_l, oml_l = lbf[l:l + 1], oml[l:l + 1]
        u, w_in_b = _norm_matmul(xs, norm_mix[l], w_in, l, tm_s, 512)
        u = u.reshape(bd, SAMPLE_ROWS, in_width)
        oa, st = _hgrn_sample(u, lbf_l, oml_l, hgrn_norm[l], state, l, ld)
        ob, kn, vn = _swa_sample(u, ck, cv, l, sinks[l], qg2[l:l + 1], kg2[l:l + 1], bias_s, ld)
        xs, w_out_b = _outproj(xs, oa.reshape(tm_s, A_WIDTH), ob.reshape(tm_s, B_WIDTH), w_out, l, tm_s, 512)
        xs, w_gate_b, w_up_b, w_down_b = _ffn(xs, norm_ffn[l], w_gate, w_up, w_down, l, tm_s, 512)
        sk.append(kn)
        sv.append(vn)
        ss.append(st)
        u = _norm_matmul(xp, norm_mix[l], w_in_b, None, tm_p, 1408).reshape(bp, seq, in_width)
        oa, st = _hgrn_prompt(u, lbf_l, oml_l, hgrn_norm[l])
        ob, kn, vn = _swa_prompt(u, sinks[l], qg2[l:l + 1], kg2[l:l + 1], bias_p)
        xp = _outproj(xp, oa.reshape(bp * seq, A_WIDTH), ob.reshape(bp * seq, B_WIDTH), w_out_b, None, 512, d)
        xp = _ffn(xp, norm_ffn[l], w_gate_b, w_up_b, w_down_b, None, tm_p, 512)
        pk.append(kn)
        pv.append(vn)
        ps.append(st)

    kv_shape = lambda n: (depth, n, WINDOW, B_KV_HEADS, B_HD)
    return (xp.reshape(bp, seq, d),
            xs.reshape(bd, SAMPLE_ROWS, d)[:, :ld],
            jnp.stack(pk).reshape(kv_shape(bp)), jnp.stack(pv).reshape(kv_shape(bp)), jnp.stack(ps),
            jnp.stack(sk).reshape(kv_shape(bd)), jnp.stack(sv).reshape(kv_shape(bd)), jnp.stack(ss))
```

```python
import functools
import math

import jax
import jax.numpy as jnp
import numpy as np
from jax import lax
from jax.experimental import pallas as pl
from jax.experimental.pallas import tpu as pltpu

F32 = jnp.float32
BF16 = jnp.bfloat16

LANES = 128
SUBLANES = 8
VMEM_BYTES_V7X = 64 * 1024 * 1024
VMEM_CAP = VMEM_BYTES_V7X - 8 * 1024 * 1024

A_HEADS = 8
A_DK = 128
A_DV = 128
A_KEY_WIDTH = A_HEADS * A_DK
A_WIDTH = A_HEADS * A_DV
B_HEADS = 16
B_KV_HEADS = 4
B_HD = 64
B_WIDTH = B_HEADS * B_HD
B_KV_WIDTH = B_KV_HEADS * B_HD
WINDOW = 128
ATTN_SCALE = 1.0 / math.sqrt(B_HD)
LOG2E = math.log2(math.e)
NEG_LOGIT = -1e30
LB_FLOOR = 1e-20
N_BUCKETS = 32
MAX_DISTANCE = 128
RMS_EPS = 1e-6

Q_A, F_A, I_A, G_A = 0, A_KEY_WIDTH, 2 * A_KEY_WIDTH, 2 * A_KEY_WIDTH + A_WIDTH
Q_B = 2 * A_KEY_WIDTH + 2 * A_WIDTH
K_B = Q_B + B_WIDTH
V_B = K_B + B_KV_WIDTH
IN_WIDTH = V_B + B_KV_WIDTH

HGRN_CHUNK = 128
HGRN_BASE = 4
HGRN_HEADS_PER_STEP = 8
SAMPLE_ROWS = SUBLANES
SWA_SAMPLE_SEQS = 4
KEY_SLOTS = 2 * WINDOW


def _params(sem, nbytes):
    assert nbytes <= VMEM_CAP, nbytes
    return pltpu.CompilerParams(dimension_semantics=sem, vmem_limit_bytes=VMEM_CAP)


def _sig_pair(z):
    t = jnp.exp(-jnp.abs(z))
    r = 1.0 / (1.0 + t)
    tr = t * r
    pos = z >= 0
    return jnp.where(pos, r, tr), jnp.where(pos, tr, r)


def _silu(x):
    return x * _sig_pair(x)[0]


def _rms_rows(x, g):
    ms = jnp.mean(x * x, axis=-1, keepdims=True)
    return x * lax.rsqrt(ms + RMS_EPS) * g


def _split_bf16(x):
    hi = x.astype(BF16)
    lo = (x - hi.astype(F32)).astype(BF16)
    return hi, lo


def _dot(a, b):
    return jnp.dot(a, b, preferred_element_type=F32)


def _dot_nt(a, b):
    return lax.dot_general(a, b, (((1,), (1,)), ((), ())), preferred_element_type=F32)


def _lb_kernel(lb_ref, lbf_ref, oml_ref):
    depth = lb_ref.shape[0]
    rows = [lb_ref[i:i + 1, :] for i in range(depth)]
    m = functools.reduce(jnp.maximum, rows)
    e = [jnp.exp(r - m) for r in rows]
    s = functools.reduce(lambda a, b: a + b, e)
    p = [ei / s for ei in e]
    cum = p[0]
    for i in range(depth):
        if i > 0:
            cum = cum + p[i]
        lb = cum - p[0]
        lbf_ref[i:i + 1, :] = jnp.maximum(lb, LB_FLOOR)
        oml_ref[i:i + 1, :] = 1.0 - lb


def _lower_bounds(lower_bounds):
    shp = jax.ShapeDtypeStruct(lower_bounds.shape, F32)
    return pl.pallas_call(_lb_kernel, out_shape=(shp, shp), name="lb_prep")(lower_bounds.astype(F32))


def _weight_spec(w, layer, block, index):
    if w.ndim == 2:
        return pl.BlockSpec(block, index)
    return pl.BlockSpec((None,) + block, lambda *g: (layer,) + index(*g))


def _bf16_weight(w_ref, copy_refs):
    w = w_ref[...]
    if copy_refs:
        w = w.astype(BF16)
        copy_refs[0][...] = w
    return w


def _norm_matmul_kernel(x_ref, g_ref, w_ref, o_ref, *rest):
    h_ref = rest[-1]

    @pl.when(pl.program_id(1) == 0)
    def _():
        h_ref[...] = _rms_rows(x_ref[...], g_ref[...]).astype(BF16)

    o_ref[...] = _dot(h_ref[...], _bf16_weight(w_ref, rest[:-1]))


def _norm_matmul(x, g, w, layer, tm, tn):
    m, d = x.shape
    n = w.shape[-1]
    cast = w.ndim == 3
    assert not cast or m == tm
    wbytes = 4 + 2 if cast else 2
    nbytes = 2 * tm * d * 4 + tm * d * 2 + 2 * d * tn * wbytes + 3 * tm * tn * 4
    u_spec = pl.BlockSpec((tm, tn), lambda i, j: (i, j))
    u_shape = jax.ShapeDtypeStruct((m, n), F32)
    w_block, w_index = (d, tn), lambda i, j: (0, j)
    return pl.pallas_call(
        _norm_matmul_kernel,
        grid=(m // tm, n // tn),
        in_specs=[pl.BlockSpec((tm, d), lambda i, j: (i, 0)),
                  pl.BlockSpec((1, d), lambda i, j: (0, 0)),
                  _weight_spec(w, layer, w_block, w_index)],
        out_specs=(u_spec, pl.BlockSpec(w_block, w_index)) if cast else u_spec,
        out_shape=(u_shape, jax.ShapeDtypeStruct((d, n), BF16)) if cast else u_shape,
        scratch_shapes=[pltpu.VMEM((tm, d), BF16)],
        compiler_params=_params(("parallel", "arbitrary"), nbytes),
        name="norm_matmul",
    )(x, g.reshape(1, d), w)


def _outproj_kernel(x_ref, oa_ref, ob_ref, w_ref, o_ref, *rest):
    ka = oa_ref.shape[1]
    w = _bf16_weight(w_ref, rest)
    o_ref[...] = x_ref[...] + _dot(oa_ref[...], w[0:ka]) + _dot(ob_ref[...], w[ka:])


def _outproj(x, oa, ob, w, layer, tm, tn):
    m, d = x.shape
    ka, kb = oa.shape[1], ob.shape[1]
    cast = w.ndim == 3
    assert not cast or m == tm
    wbytes = 4 + 2 if cast else 2
    nbytes = 4 * tm * tn * 4 + 2 * tm * (ka + kb) * 2 + 2 * (ka + kb) * tn * wbytes
    x_spec = pl.BlockSpec((tm, tn), lambda i, j: (i, j))
    x_shape = jax.ShapeDtypeStruct((m, d), F32)
    w_block, w_index = (ka + kb, tn), lambda i, j: (0, j)
    return pl.pallas_call(
        _outproj_kernel,
        grid=(m // tm, d // tn),
        in_specs=[x_spec,
                  pl.BlockSpec((tm, ka), lambda i, j: (i, 0)),
                  pl.BlockSpec((tm, kb), lambda i, j: (i, 0)),
                  _weight_spec(w, layer, w_block, w_index)],
        out_specs=(x_spec, pl.BlockSpec(w_block, w_index)) if cast else x_spec,
        out_shape=(x_shape, jax.ShapeDtypeStruct((ka + kb, d), BF16)) if cast else x_shape,
        compiler_params=_params(("parallel", "arbitrary"), nbytes),
        name="outproj",
    )(x, oa, ob, w)


def _ffn_kernel(x_ref, g_ref, wg_ref, wu_ref, wd_ref, o_ref, *rest):
    h_ref = rest[-1]
    copies = rest[:-1]

    @pl.when(pl.program_id(1) == 0)
    def _():
        x = x_ref[...]
        h_ref[...] = _rms_rows(x, g_ref[...]).astype(BF16)
        o_ref[...] = x

    h = h_ref[...]
    gate = _dot(h, _bf16_weight(wg_ref, copies[0:1]))
    up = _dot(h, _bf16_weight(wu_ref, copies[1:2]))
    act = (_silu(gate) * up).astype(BF16)
    o_ref[...] += _dot(act, _bf16_weight(wd_ref, copies[2:3]))


def _ffn(x, g, wg, wu, wd, layer, tm, tf):
    m, d = x.shape
    f = wg.shape[-1]
    cast = wg.ndim == 3
    assert not cast or m == tm
    wbytes = 4 + 2 if cast else 2
    nbytes = 4 * tm * d * 4 + tm * d * 2 + 2 * 3 * d * tf * wbytes + 3 * tm * tf * 4
    x_spec = pl.BlockSpec((tm, d), lambda i, j: (i, 0))
    x_shape = jax.ShapeDtypeStruct((m, d), F32)
    col_block, col_index = (d, tf), lambda i, j: (0, j)
    row_block, row_index = (tf, d), lambda i, j: (j, 0)
    copy_specs = (pl.BlockSpec(col_block, col_index), pl.BlockSpec(col_block, col_index),
                  pl.BlockSpec(row_block, row_index))
    copy_shapes = (jax.ShapeDtypeStruct((d, f), BF16), jax.ShapeDtypeStruct((d, f), BF16),
                   jax.ShapeDtypeStruct((f, d), BF16))
    return pl.pallas_call(
        _ffn_kernel,
        grid=(m // tm, f // tf),
        in_specs=[x_spec,
                  pl.BlockSpec((1, d), lambda i, j: (0, 0)),
                  _weight_spec(wg, layer, col_block, col_index),
                  _weight_spec(wu, layer, col_block, col_index),
                  _weight_spec(wd, layer, row_block, row_index)],
        out_specs=(x_spec,) + copy_specs if cast else x_spec,
        out_shape=(x_shape,) + copy_shapes if cast else x_shape,
        scratch_shapes=[pltpu.VMEM((tm, d), BF16)],
        compiler_params=_params(("parallel", "arbitrary"), nbytes),
        name="ffn",
    )(x, g.reshape(1, d), wg, wu, wd)


def _hgrn_static(c):
    assert (HGRN_BASE - 1) * -math.log(LB_FLOOR) / 2 < 80.0
    t = np.arange(c)
    tri = (t[None, :] <= t[:, None]).astype(np.float32)
    first = HGRN_BASE * (t // HGRN_BASE)
    base = tri - 0.5 * (tri[first] + tri[first + HGRN_BASE - 1])
    masks = []
    h = c // 2
    while h >= HGRN_BASE:
        grp = t // (2 * h)
        upper = (t % (2 * h)) >= h
        masks.append((grp[:, None] == grp[None, :]) & upper[:, None] & ~upper[None, :])
        h //= 2
    masks.append((first[:, None] == first[None, :]) & (t[None, :] <= t[:, None]))
    p_all = np.concatenate([tri, base], axis=0)
    p2 = np.concatenate([p_all, p_all], axis=1)
    masks = np.stack(masks).astype(np.float32)
    return p2, np.concatenate([masks, masks], axis=2)


def _hgrn_prompt_kernel(q_ref, f_ref, i_ref, g_ref, lbf_ref, oml_ref, gn_ref, p2_ref, m_ref,
                        oa_ref, st_ref, s_scr, *, chunk, heads):
    c_idx = pl.program_id(2)

    @pl.when(c_idx == 0)
    def _():
        s_scr[...] = jnp.zeros_like(s_scr)

    width = heads * A_DK
    pw = 2 * A_DK
    sz, snz = _sig_pair(f_ref[0])
    oml = oml_ref[...]
    lf = jnp.log(lbf_ref[...] + oml * sz)
    ka = oml * snz
    hi, lo = _split_bf16(lf)
    expo = _dot(p2_ref[...], jnp.concatenate([hi, lo], axis=0))
    b = expo[0:chunk]
    q = _silu(q_ref[0])
    v = i_ref[0]
    gate = _silu(g_ref[0])
    gn = gn_ref[...]

    row = lax.broadcasted_iota(jnp.int32, (chunk, width), 0)
    qs, ks = [], []
    h = chunk // 2
    while h >= HGRN_BASE:
        groups = chunk // (2 * h)
        mid = jnp.broadcast_to(b.reshape(groups, 2 * h, width)[:, h - 1:h, :], (groups, 2 * h, width))
        d = b - mid.reshape(chunk, width)
        e = jnp.exp(jnp.where((row & h) != 0, d, -d))
        qs.append((q * e).astype(BF16))
        ks.append((ka * e).astype(BF16))
        h //= 2
    e_base = expo[chunk:2 * chunk]
    qs.append((q * jnp.exp(e_base)).astype(BF16))
    ks.append((ka * jnp.exp(-e_base)).astype(BF16))
    b_last = b[chunk - 1:chunk, :]
    q0 = (q * jnp.exp(b)).astype(BF16)
    k_tail = (ka * jnp.exp(b_last - b)).astype(BF16)
    decay = jnp.exp(b_last)
    vb = v.astype(BF16)

    left = lax.broadcasted_iota(jnp.int32, (chunk, pw), 1) < A_DK

    def bdiag(x):
        z = jnp.zeros_like(x)
        return jnp.concatenate([jnp.where(left, x, z), jnp.where(left, z, x)], axis=0)

    for p in range(heads // 2):
        sl = slice(p * pw, (p + 1) * pw)
        st = s_scr[p]
        o = _dot_nt(q0[:, sl], bdiag(st.astype(BF16)))
        a = jnp.zeros((chunk, pw), F32)
        for li in range(len(qs)):
            a = a + _dot_nt(qs[li][:, sl], bdiag(ks[li][:, sl])) * m_ref[li]
        o = o + _dot(a.astype(BF16), bdiag(vb[:, sl]))
        vp = v[:, sl]
        vt = jnp.concatenate([vp[:, :A_DV].T, vp[:, A_DV:].T], axis=1).astype(BF16)
        s_scr[p] = st * decay[:, sl] + _dot(vt, bdiag(k_tail[:, sl]))
        for hh in range(2):
            hs = slice(p * pw + hh * A_DV, p * pw + (hh + 1) * A_DV)
            oa_ref[0, :, hs] = (_rms_rows(o[:, hh * A_DV:(hh + 1) * A_DV], gn) * gate[:, hs]).astype(BF16)

    @pl.when(c_idx == pl.num_programs(2) - 1)
    def _():
        for hh in range(heads):
            st_ref[0, hh] = s_scr[hh // 2][:, (hh % 2) * A_DK:(hh % 2 + 1) * A_DK].T


def _hgrn_prompt(u3, lbf, oml, gn):
    bsz, length, _ = u3.shape
    chunk, heads = HGRN_CHUNK, HGRN_HEADS_PER_STEP
    cols = heads * A_DK
    assert length % chunk == 0 and A_HEADS % heads == 0 and heads % 2 == 0 and chunk == A_DV
    p2, masks = _hgrn_static(chunk)
    p2 = jnp.asarray(p2, BF16)
    masks = jnp.asarray(masks, F32)

    def col_spec(offset):
        base = offset // cols
        return pl.BlockSpec((1, chunk, cols), lambda b, h, c: (b, c, base + h))

    nbytes = (2 * 5 * chunk * cols * 4 + 2 * (p2.size * 2 + masks.size * 4)
              + 6 * p2.shape[0] * cols * 4)
    return pl.pallas_call(
        functools.partial(_hgrn_prompt_kernel, chunk=chunk, heads=heads),
        grid=(bsz, A_HEADS // heads, length // chunk),
        in_specs=[col_spec(Q_A), col_spec(F_A), col_spec(I_A), col_spec(G_A),
                  pl.BlockSpec((1, cols), lambda b, h, c: (0, h)),
                  pl.BlockSpec((1, cols), lambda b, h, c: (0, h)),
                  pl.BlockSpec((1, A_DV), lambda b, h, c: (0, 0)),
                  pl.BlockSpec(p2.shape, lambda b, h, c: (0, 0)),
                  pl.BlockSpec(masks.shape, lambda b, h, c: (0, 0, 0))],
        out_specs=(pl.BlockSpec((1, chunk, cols), lambda b, h, c: (b, c, h)),
                   pl.BlockSpec((1, heads, A_DK, A_DV), lambda b, h, c: (b, h, 0, 0))),
        out_shape=(jax.ShapeDtypeStruct((bsz, length, A_WIDTH), BF16),
                   jax.ShapeDtypeStruct((bsz, A_HEADS, A_DK, A_DV), F32)),
        scratch_shapes=[pltpu.VMEM((heads // 2, A_DV, 2 * A_DK), F32)],
        compiler_params=_params(("parallel", "parallel", "arbitrary"), nbytes),
        name="hgrn_prompt",
    )(u3, u3, u3, u3, lbf, oml, gn.reshape(1, A_DV), p2, masks)


def _hgrn_sample_kernel(u_ref, lbf_ref, oml_ref, gn_ref, s_ref, oa_ref, so_ref, *, steps):
    u = u_ref[0]
    q = _silu(u[:, Q_A:Q_A + A_KEY_WIDTH])
    sz, snz = _sig_pair(u[:, F_A:F_A + A_KEY_WIDTH])
    oml = oml_ref[...]
    f = lbf_ref[...] + oml * sz
    ka = oml * snz
    v = u[:, I_A:I_A + A_WIDTH]
    gate = _silu(u[:, G_A:G_A + A_WIDTH])
    gn = gn_ref[...]
    rows = u.shape[0]
    row = lax.broadcasted_iota(jnp.int32, f.shape, 0)
    live = row < steps
    f = jnp.where(live, f, 1.0)
    ka = jnp.where(live, ka, 0.0)

    def down(x, d, fill):
        return x if d == 0 else jnp.where(row >= d, pltpu.roll(x, d, axis=0), fill)

    def up(x, d, fill):
        return jnp.where(row < rows - d, pltpu.roll(x, rows - d, axis=0), fill)

    decay = [jnp.ones_like(f)]
    for d in range(1, steps + 1):
        decay.append(decay[-1] * down(f, d - 1, 1.0))
    head = decay[steps]
    tail = jnp.ones_like(f)
    for d in range(1, steps):
        tail = tail * up(f, d, 1.0)
    q_in = (q * head).astype(BF16)
    k_tail = ka * tail
    total = head[steps - 1:steps, :]
    t_hi = total.astype(BF16).astype(F32)
    t_mid = (total - t_hi).astype(BF16).astype(F32)
    t_lo = (total - t_hi - t_mid).astype(BF16).astype(F32)
    t_rows = jnp.where(row == 0, t_hi, jnp.where(row == 1, t_mid, jnp.where(row == 2, t_lo, 0.0)))
    prods = [q * down(ka, d, 0.0) * decay[d] for d in range(steps)]
    v_sh = [down(v, d, 0.0) for d in range(steps)]

    zsq = jnp.zeros((A_DK, A_DV), BF16)
    zrows = jnp.zeros((rows, A_DV), F32)
    r2 = lax.broadcasted_iota(jnp.int32, (rows, 2 * A_DV), 0)
    l2 = lax.broadcasted_iota(jnp.int32, (rows, 2 * A_DV), 1)
    ones_right = jnp.where((r2 < 3) & (l2 >= A_DV), 1.0, 0.0)

    for p in range(A_HEADS // 2):
        ps = slice(2 * p * A_DK, (2 * p + 2) * A_DK)
        s0 = [s_ref[0, 2 * p + hh] for hh in range(2)]
        s_bd = jnp.concatenate([jnp.concatenate([s0[0].astype(BF16), zsq], axis=1),
                                jnp.concatenate([zsq, s0[1].astype(BF16)], axis=1)], axis=0)
        o_pair = _dot(q_in[:, ps], s_bd)
        for hh in range(2):
            head_i = 2 * p + hh
            sl = slice(head_i * A_DK, (head_i + 1) * A_DK)
            o = o_pair[:, hh * A_DV:(hh + 1) * A_DV]
            for d in range(steps):
                o = o + jnp.sum(prods[d][:, sl], axis=-1, keepdims=True) * v_sh[d][:, sl]
            lhs = jnp.concatenate([k_tail[:, sl], t_rows[:, sl]], axis=0).astype(BF16)
            rhs = jnp.concatenate([jnp.concatenate([v[:, sl], zrows], axis=1), ones_right],
                                  axis=0).astype(BF16)
            upd = lax.dot_general(lhs, rhs, (((0,), (0,)), ((), ())), preferred_element_type=F32)
            so_ref[0, head_i] = s0[hh] * upd[:, A_DV:] + upd[:, :A_DV]
            oa_ref[0, :, sl] = (_rms_rows(o, gn) * gate[:, sl]).astype(BF16)


def _hgrn_sample(u3, lbf, oml, gn, state, layer, steps):
    bsz, rows, _ = u3.shape
    width = G_A + A_WIDTH
    nbytes = 2 * rows * width * 4 + 4 * A_HEADS * A_DK * A_DV * 4
    return pl.pallas_call(
        functools.partial(_hgrn_sample_kernel, steps=steps),
        grid=(bsz,),
        in_specs=[pl.BlockSpec((1, rows, width), lambda b: (b, 0, 0)),
                  pl.BlockSpec((1, A_KEY_WIDTH), lambda b: (0, 0)),
                  pl.BlockSpec((1, A_KEY_WIDTH), lambda b: (0, 0)),
                  pl.BlockSpec((1, A_DV), lambda b: (0, 0)),
                  pl.BlockSpec((None, 1, A_HEADS, A_DK, A_DV), lambda b: (layer, b, 0, 0, 0))],
        out_specs=(pl.BlockSpec((1, rows, A_WIDTH), lambda b: (b, 0, 0)),
                   pl.BlockSpec((1, A_HEADS, A_DK, A_DV), lambda b: (b, 0, 0, 0))),
        out_shape=(jax.ShapeDtypeStruct((bsz, rows, A_WIDTH), BF16),
                   jax.ShapeDtypeStruct(state.shape[1:], F32)),
        compiler_params=_params(("parallel",), nbytes),
        name="hgrn_sample",
    )(u3, lbf, oml, gn.reshape(1, A_DV), state)


def _t5_bucket(dist):
    n = np.maximum(dist, 0)
    max_exact = N_BUCKETS // 2
    nf = np.maximum(n, 1).astype(np.float32)
    large = max_exact + (np.log(nf / max_exact) / math.log(MAX_DISTANCE / max_exact)
                         * (N_BUCKETS - max_exact)).astype(np.int32)
    large = np.minimum(large, N_BUCKETS - 1)
    return np.where(n < max_exact, n, large).astype(np.int32)


def _bias_kernel(tab_ref, bucket_ref, valid_ref, o_ref):
    rows = bucket_ref.shape[0]
    bucket = bucket_ref[...]
    hits = [bucket == b for b in range(N_BUCKETS)]
    for h in range(B_HEADS):
        acc = jnp.zeros(bucket.shape, F32)
        for b in range(N_BUCKETS):
            acc = jnp.where(hits[b], tab_ref[b, h], acc)
        acc = acc * LOG2E
        j, r, c = h // 4, (h % 4) // 2, h % 2
        for m in range(valid_ref.shape[0]):
            o_ref[m, j, r * rows:(r + 1) * rows, c * KEY_SLOTS:(c + 1) * KEY_SLOTS] = jnp.where(
                valid_ref[m] != 0, acc, NEG_LOGIT * LOG2E)


def _bias_layout(table, dist, valids):
    rows = dist.shape[0]
    return pl.pallas_call(
        _bias_kernel,
        in_specs=[pl.BlockSpec(memory_space=pltpu.SMEM),
                  pl.BlockSpec(memory_space=pltpu.VMEM),
                  pl.BlockSpec(memory_space=pltpu.VMEM)],
        out_shape=jax.ShapeDtypeStruct((valids.shape[0], B_KV_HEADS, 2 * rows, 2 * KEY_SLOTS), F32),
        name="bias_prep",
    )(table.astype(F32), jnp.asarray(_t5_bucket(dist), jnp.int32), jnp.asarray(valids, jnp.int32))


def _block_diag_ones():
    r = lax.broadcasted_iota(jnp.int32, (LANES, LANES), 0) // B_HD
    c = lax.broadcasted_iota(jnp.int32, (LANES, LANES), 1) // B_HD
    return (r == c).astype(BF16)


def _head_norm(x, g2, bd):
    hi, lo = _split_bf16(x * x)
    ss = _dot(hi, bd) + _dot(lo, bd)
    return x * lax.rsqrt(ss * (1.0 / B_HD) + RMS_EPS) * g2


def _place(tile, half, lo_mask):
    rolled = pltpu.roll(tile, B_HD, axis=1)
    zero = jnp.zeros_like(tile)
    if half == 0:
        return jnp.where(lo_mask, tile, zero), jnp.where(lo_mask, zero, rolled)
    return jnp.where(lo_mask, rolled, zero), jnp.where(lo_mask, zero, tile)


def _swa_core(qn_tiles, k_tiles, v_tiles, bias_at, sink_ref, store):
    rows = qn_tiles[0].shape[0]
    lo_mask = lax.broadcasted_iota(jnp.int32, k_tiles[0].shape, 1) < B_HD
    out_lo = lax.broadcasted_iota(jnp.int32, (rows, LANES), 1) < B_HD
    scores, values = [], []
    for j in range(B_KV_HEADS):
        t, half = j // 2, j % 2
        k_lo, k_hi = _place(k_tiles[t], half, lo_mask)
        v_lo, v_hi = _place(v_tiles[t], half, lo_mask)
        kk = jnp.concatenate([k_lo, k_hi], axis=0).astype(BF16)
        values.append(jnp.concatenate([v_lo, v_hi], axis=0).astype(BF16))
        qq = jnp.concatenate([qn_tiles[2 * j], qn_tiles[2 * j + 1]], axis=0).astype(BF16)
        scores.append(_dot_nt(qq, kk) + bias_at(j))
    weights, scales = [], []
    for j in range(B_KV_HEADS):
        e_rows, inv = [], []
        for r in range(2):
            e_cols, inv_r = [], []
            for c in range(2):
                sk = sink_ref[4 * j + 2 * r + c] * LOG2E
                sb = scores[j][r * rows:(r + 1) * rows, c * KEY_SLOTS:(c + 1) * KEY_SLOTS]
                m = jnp.maximum(jnp.max(sb, axis=-1, keepdims=True), sk)
                e = jnp.exp2(sb - m)
                inv_r.append(1.0 / (jnp.sum(e, axis=-1, keepdims=True) + jnp.exp2(sk - m)))
                e_cols.append(e.astype(BF16))
            e_rows.append(jnp.concatenate(e_cols, axis=1))
            inv.append(jnp.where(out_lo, inv_r[0], inv_r[1]))
        weights.append(jnp.concatenate(e_rows, axis=0))
        scales.append(inv)
    for j in range(B_KV_HEADS):
        o = _dot(weights[j], values[j])
        for r in range(2):
            store(2 * j + r, (o[r * rows:(r + 1) * rows] * scales[j][r]).astype(BF16))


def _swa_prompt_kernel(sink_ref, q_ref, kc_ref, kp_ref, vc_ref, vp_ref, qg_ref, kg_ref, bias_ref,
                       ob_ref, ko_ref, vo_ref):
    n = pl.program_id(1)
    bd = _block_diag_ones()
    k_all = jnp.concatenate([kp_ref[0], kc_ref[0]], axis=0)
    v_all = jnp.concatenate([vp_ref[0], vc_ref[0]], axis=0)
    kg = kg_ref[...]
    k_tiles = [_head_norm(k_all[:, t * LANES:(t + 1) * LANES], kg, bd) for t in range(2)]
    v_tiles = [v_all[:, t * LANES:(t + 1) * LANES] for t in range(2)]
    q = q_ref[0]
    qg = qg_ref[...]
    qn_tiles = [_head_norm(q[:, t * LANES:(t + 1) * LANES], qg, bd) for t in range(B_WIDTH // LANES)]

    def store(tile, val):
        ob_ref[0, :, tile * LANES:(tile + 1) * LANES] = val

    _swa_core(qn_tiles, k_tiles, v_tiles, lambda j: bias_ref[j], sink_ref, store)

    @pl.when(n == pl.num_programs(1) - 1)
    def _():
        ko_ref[0] = jnp.concatenate([kt[WINDOW:] for kt in k_tiles], axis=1)
        vo_ref[0] = vc_ref[0]


def _prompt_dist():
    return WINDOW + np.arange(WINDOW)[:, None] - np.arange(KEY_SLOTS)[None, :]


def _prompt_valid():
    j = np.arange(KEY_SLOTS)[None, :]
    dist = _prompt_dist()
    cur = (j >= WINDOW) & (dist >= 0)
    prev = (j < WINDOW) & (dist < WINDOW)
    return np.stack([cur, cur | prev]).astype(np.int32)


def _swa_prompt(u3, sinks, qg, kg, bias):
    bsz, length, _ = u3.shape
    nb = length // WINDOW
    qb, kb, vb = Q_B // B_WIDTH, K_B // B_KV_WIDTH, V_B // B_KV_WIDTH
    assert Q_B % B_WIDTH == 0 and K_B % B_KV_WIDTH == 0 and V_B % B_KV_WIDTH == 0
    kv_spec = lambda col, prev: pl.BlockSpec(
        (1, WINDOW, B_KV_WIDTH),
        (lambda b, n: (b, jnp.maximum(n - 1, 0), col)) if prev else (lambda b, n: (b, n, col)))
    g_spec = pl.BlockSpec((1, LANES), lambda b, n: (0, 0))
    nbytes = (2 * WINDOW * (B_WIDTH + 4 * B_KV_WIDTH) * 4 + bias.size * 4
              + 2 * WINDOW * B_WIDTH * 2 + 24 * WINDOW * 2 * KEY_SLOTS * 4)
    return pl.pallas_call(
        _swa_prompt_kernel,
        grid=(bsz, nb),
        in_specs=[pl.BlockSpec(memory_space=pltpu.SMEM),
                  pl.BlockSpec((1, WINDOW, B_WIDTH), lambda b, n: (b, n, qb)),
                  kv_spec(kb, False), kv_spec(kb, True), kv_spec(vb, False), kv_spec(vb, True),
                  g_spec, g_spec,
                  pl.BlockSpec((None,) + bias.shape[1:], lambda b, n: (jnp.minimum(n, 1), 0, 0, 0))],
        out_specs=(pl.BlockSpec((1, WINDOW, B_WIDTH), lambda b, n: (b, n, 0)),
                   pl.BlockSpec((1, WINDOW, B_KV_WIDTH), lambda b, n: (b, 0, 0)),
                   pl.BlockSpec((1, WINDOW, B_KV_WIDTH), lambda b, n: (b, 0, 0))),
        out_shape=(jax.ShapeDtypeStruct((bsz, length, B_WIDTH), BF16),
                   jax.ShapeDtypeStruct((bsz, WINDOW, B_KV_WIDTH), F32),
                   jax.ShapeDtypeStruct((bsz, WINDOW, B_KV_WIDTH), F32)),
        compiler_params=_params(("parallel", "arbitrary"), nbytes),
        name="swa_prompt",
    )(sinks, u3, u3, u3, u3, u3, qg, kg, bias)


def _swa_sample_kernel(sink_ref, q_ref, kv_ref, ck_ref, cv_ref, qg_ref, kg_ref, bias_ref,
                       ob_ref, ko_ref, vo_ref, *, steps):
    bd = _block_diag_ones()
    rows = q_ref.shape[1]
    kg = kg_ref[...]
    qg = qg_ref[...]
    pad = jnp.zeros((KEY_SLOTS - WINDOW - rows, LANES), F32)
    for i in range(q_ref.shape[0]):
        kv = kv_ref[i]
        k_new = [_head_norm(kv[:, t * LANES:(t + 1) * LANES], kg, bd) for t in range(2)]
        v_new = [kv[:, B_KV_WIDTH + t * LANES:B_KV_WIDTH + (t + 1) * LANES] for t in range(2)]
        ck, cv = ck_ref[i], cv_ref[i]
        k_tiles = [jnp.concatenate([ck[:, t * LANES:(t + 1) * LANES], k_new[t], pad], axis=0)
                   for t in range(2)]
        v_tiles = [jnp.concatenate([cv[:, t * LANES:(t + 1) * LANES], v_new[t], pad], axis=0)
                   for t in range(2)]
        q = q_ref[i]
        qn_tiles = [_head_norm(q[:, t * LANES:(t + 1) * LANES], qg, bd) for t in range(B_WIDTH // LANES)]

        def store(tile, val, i=i):
            ob_ref[i, :, tile * LANES:(tile + 1) * LANES] = val

        _swa_core(qn_tiles, k_tiles, v_tiles, lambda j: bias_ref[0, j], sink_ref, store)
        ko_ref[i, 0:WINDOW - steps, :] = ck_ref[i, steps:WINDOW, :]
        vo_ref[i, 0:WINDOW - steps, :] = cv_ref[i, steps:WINDOW, :]
        ko_ref[i, WINDOW - steps:WINDOW, :] = jnp.concatenate([kt[0:steps] for kt in k_new], axis=1)
        vo_ref[i, WINDOW - steps:WINDOW, :] = kv[0:steps, B_KV_WIDTH:]


def _sample_dist(rows):
    return WINDOW + np.arange(rows)[:, None] - np.arange(KEY_SLOTS)[None, :]


def _sample_valid(rows, steps):
    dist = _sample_dist(rows)
    j = np.arange(KEY_SLOTS)[None, :]
    return ((dist >= 0) & (dist < WINDOW) & (j < WINDOW + steps)).astype(np.int32)[None]


def _swa_sample(u3, cache_k, cache_v, layer, sinks, qg, kg, bias, steps):
    bsz, rows, _ = u3.shape
    w = cache_k.shape[2]
    bt = SWA_SAMPLE_SEQS
    assert w == WINDOW and bsz % bt == 0
    qb, kvb = Q_B // B_WIDTH, K_B // (2 * B_KV_WIDTH)
    assert K_B % (2 * B_KV_WIDTH) == 0
    c_in_spec = pl.BlockSpec((None, bt, w, B_KV_WIDTH), lambda b: (layer, b, 0, 0))
    c_spec = pl.BlockSpec((bt, w, B_KV_WIDTH), lambda b: (b, 0, 0))
    g_spec = pl.BlockSpec((1, LANES), lambda b: (0, 0))
    nbytes = bt * 8 * w * B_KV_WIDTH * 4 + 2 * bias.size * 4 + bt * 16 * KEY_SLOTS * LANES * 4
    return pl.pallas_call(
        functools.partial(_swa_sample_kernel, steps=steps),
        grid=(bsz // bt,),
        in_specs=[pl.BlockSpec(memory_space=pltpu.SMEM),
                  pl.BlockSpec((bt, rows, B_WIDTH), lambda b: (b, 0, qb)),
                  pl.BlockSpec((bt, rows, 2 * B_KV_WIDTH), lambda b: (b, 0, kvb)),
                  c_in_spec, c_in_spec, g_spec, g_spec,
                  pl.BlockSpec(bias.shape, lambda b: (0, 0, 0, 0))],
        out_specs=(pl.BlockSpec((bt, rows, B_WIDTH), lambda b: (b, 0, 0)), c_spec, c_spec),
        out_shape=(jax.ShapeDtypeStruct((bsz, rows, B_WIDTH), BF16),
                   jax.ShapeDtypeStruct((bsz, w, B_KV_WIDTH), F32),
                   jax.ShapeDtypeStruct((bsz, w, B_KV_WIDTH), F32)),
        compiler_params=_params(("parallel",), nbytes),
        name="swa_sample",
    )(sinks, u3, u3, cache_k, cache_v, qg, kg, bias)


def kernel(x_prompt, x_sample, cache_k, cache_v, state_hgrn, norm_mix, w_in, lower_bounds, hgrn_norm,
           q_norm, k_norm, attn_sinks, rel_bias_table, w_out, norm_ffn, w_gate, w_up, w_down):
    depth = w_in.shape[0]
    bp, seq, d = x_prompt.shape
    bd, ld, _ = x_sample.shape
    w = cache_k.shape[2]
    in_width = w_in.shape[2]

    lbf, oml = _lower_bounds(lower_bounds)
    sinks = attn_sinks.astype(F32)
    qg2 = jnp.tile(q_norm.astype(F32), (1, LANES // B_HD)) * (ATTN_SCALE * LOG2E)
    kg2 = jnp.tile(k_norm.astype(F32), (1, LANES // B_HD))
    ck = cache_k.astype(F32).reshape(depth, bd, w, B_KV_WIDTH)
    cv = cache_v.astype(F32).reshape(depth, bd, w, B_KV_WIDTH)

    xp = x_prompt.reshape(bp * seq, d)
    xs = jnp.pad(x_sample, ((0, 0), (0, SAMPLE_ROWS - ld), (0, 0))).reshape(bd * SAMPLE_ROWS, d)
    tm_p, tm_s = 1024, bd * SAMPLE_ROWS
    bias_p = _bias_layout(rel_bias_table, _prompt_dist(), _prompt_valid())
    bias_s = _bias_layout(rel_bias_table, _sample_dist(SAMPLE_ROWS), _sample_valid(SAMPLE_ROWS, ld))
    state = state_hgrn.astype(F32)

    pk, pv, ps, sk, sv, ss = [], [], [], [], [], []
    for l in range(depth):
        lbf_l, oml_l = lbf[l:l + 1], oml[l:l + 1]
        u, w_in_b = _norm_matmul(xs, norm_mix[l], w_in, l, tm_s, 512)
        u = u.reshape(bd, SAMPLE_ROWS, in_width)
        oa, st = _hgrn_sample(u, lbf_l, oml_l, hgrn_norm[l], state, l, ld)
        ob, kn, vn = _swa_sample(u, ck, cv, l, sinks[l], qg2[l:l + 1], kg2[l:l + 1], bias_s, ld)
        xs, w_out_b = _outproj(xs, oa.reshape(tm_s, A_WIDTH), ob.reshape(tm_s, B_WIDTH), w_out, l, tm_s, 512)
        xs, w_gate_b, w_up_b, w_down_b = _ffn(xs, norm_ffn[l], w_gate, w_up, w_down, l, tm_s, 512)
        sk.append(kn)
        sv.append(vn)
        ss.append(st)
        u = _norm_matmul(xp, norm_mix[l], w_in_b, None, tm_p, 1408).reshape(bp, seq, in_width)
        oa, st = _hgrn_prompt(u, lbf_l, oml_l, hgrn_norm[l])
        ob, kn, vn = _swa_prompt(u, sinks[l], qg2[l:l + 1], kg2[l:l + 1], bias_p)
        xp = _outproj(xp, oa.reshape(bp * seq, A_WIDTH), ob.reshape(bp * seq, B_WIDTH), w_out_b, None, 512, d)
        xp = _ffn(xp, norm_ffn[l], w_gate_b, w_up_b, w_down_b, None, tm_p, 512)
        pk.append(kn)
        pv.append(vn)
        ps.append(st)

    kv_shape = lambda n: (depth, n, WINDOW, B_KV_HEADS, B_HD)
    return (xp.reshape(bp, seq, d),
            xs.reshape(bd, SAMPLE_ROWS, d)[:, :ld],
            jnp.stack(pk).reshape(kv_shape(bp)), jnp.stack(pv).reshape(kv_shape(bp)), jnp.stack(ps),
            jnp.stack(sk).reshape(kv_shape(bd)), jnp.stack(sv).reshape(kv_shape(bd)), jnp.stack(ss))
```

```python
import functools
import math

import jax
import jax.numpy as jnp
import numpy as np
from jax import lax
from jax.experimental import pallas as pl
from jax.experimental.pallas import tpu as pltpu

F32 = jnp.float32
BF16 = jnp.bfloat16

LANES = 128
SUBLANES = 8
VMEM_BYTES_V7X = 64 * 1024 * 1024
VMEM_CAP = VMEM_BYTES_V7X - 8 * 1024 * 1024

A_HEADS = 8
A_DK = 128
A_DV = 128
A_KEY_WIDTH = A_HEADS * A_DK
A_WIDTH = A_HEADS * A_DV
B_HEADS = 16
B_KV_HEADS = 4
B_HD = 64
B_WIDTH = B_HEADS * B_HD
B_KV_WIDTH = B_KV_HEADS * B_HD
WINDOW = 128
ATTN_SCALE = 1.0 / math.sqrt(B_HD)
LOG2E = math.log2(math.e)
NEG_LOGIT = -1e30
LB_FLOOR = 1e-20
N_BUCKETS = 32
MAX_DISTANCE = 128
RMS_EPS = 1e-6

Q_A, F_A, I_A, G_A = 0, A_KEY_WIDTH, 2 * A_KEY_WIDTH, 2 * A_KEY_WIDTH + A_WIDTH
Q_B = 2 * A_KEY_WIDTH + 2 * A_WIDTH
K_B = Q_B + B_WIDTH
V_B = K_B + B_KV_WIDTH
IN_WIDTH = V_B + B_KV_WIDTH

HGRN_CHUNK = 128
HGRN_BASE = 4
HGRN_HEADS_PER_STEP = 8
SAMPLE_ROWS = SUBLANES
SWA_SAMPLE_SEQS = 4
HGRN_SAMPLE_SEQS = 2
KEY_SLOTS = 2 * WINDOW


def _params(sem, nbytes):
    assert nbytes <= VMEM_CAP, nbytes
    return pltpu.CompilerParams(dimension_semantics=sem, vmem_limit_bytes=VMEM_CAP)


def _sig_pair(z):
    t = jnp.exp(-jnp.abs(z))
    r = 1.0 / (1.0 + t)
    tr = t * r
    pos = z >= 0
    return jnp.where(pos, r, tr), jnp.where(pos, tr, r)


def _silu(x):
    return x * _sig_pair(x)[0]


def _rms_rows(x, g):
    ms = jnp.mean(x * x, axis=-1, keepdims=True)
    return x * lax.rsqrt(ms + RMS_EPS) * g


def _split_bf16(x):
    hi = x.astype(BF16)
    lo = (x - hi.astype(F32)).astype(BF16)
    return hi, lo


def _dot(a, b):
    return jnp.dot(a, b, preferred_element_type=F32)


def _dot_nt(a, b):
    return lax.dot_general(a, b, (((1,), (1,)), ((), ())), preferred_element_type=F32)


def _lb_kernel(lb_ref, lbf_ref, oml_ref):
    depth = lb_ref.shape[0]
    rows = [lb_ref[i:i + 1, :] for i in range(depth)]
    m = functools.reduce(jnp.maximum, rows)
    e = [jnp.exp(r - m) for r in rows]
    s = functools.reduce(lambda a, b: a + b, e)
    p = [ei / s for ei in e]
    cum = p[0]
    for i in range(depth):
        if i > 0:
            cum = cum + p[i]
        lb = cum - p[0]
        lbf_ref[i:i + 1, :] = jnp.maximum(lb, LB_FLOOR)
        oml_ref[i:i + 1, :] = 1.0 - lb


def _lower_bounds(lower_bounds):
    shp = jax.ShapeDtypeStruct(lower_bounds.shape, F32)
    return pl.pallas_call(_lb_kernel, out_shape=(shp, shp), name="lb_prep")(lower_bounds.astype(F32))


def _weight_spec(w, layer, block, index):
    if w.ndim == 2:
        return pl.BlockSpec(block, index)
    return pl.BlockSpec((None,) + block, lambda *g: (layer,) + index(*g))


def _bf16_weight(w_ref, copy_refs):
    w = w_ref[...]
    if copy_refs:
        w = w.astype(BF16)
        copy_refs[0][...] = w
    return w


def _norm_matmul_kernel(x_ref, g_ref, w_ref, o_ref, *rest):
    h_ref = rest[-1]

    @pl.when(pl.program_id(1) == 0)
    def _():
        h_ref[...] = _rms_rows(x_ref[...], g_ref[...]).astype(BF16)

    o_ref[...] = _dot(h_ref[...], _bf16_weight(w_ref, rest[:-1]))


def _norm_matmul(x, g, w, layer, tm, tn):
    m, d = x.shape
    n = w.shape[-1]
    cast = w.ndim == 3
    assert not cast or m == tm
    wbytes = 4 + 2 if cast else 2
    nbytes = 2 * tm * d * 4 + tm * d * 2 + 2 * d * tn * wbytes + 3 * tm * tn * 4
    u_spec = pl.BlockSpec((tm, tn), lambda i, j: (i, j))
    u_shape = jax.ShapeDtypeStruct((m, n), F32)
    w_block, w_index = (d, tn), lambda i, j: (0, j)
    return pl.pallas_call(
        _norm_matmul_kernel,
        grid=(m // tm, n // tn),
        in_specs=[pl.BlockSpec((tm, d), lambda i, j: (i, 0)),
                  pl.BlockSpec((1, d), lambda i, j: (0, 0)),
                  _weight_spec(w, layer, w_block, w_index)],
        out_specs=(u_spec, pl.BlockSpec(w_block, w_index)) if cast else u_spec,
        out_shape=(u_shape, jax.ShapeDtypeStruct((d, n), BF16)) if cast else u_shape,
        scratch_shapes=[pltpu.VMEM((tm, d), BF16)],
        compiler_params=_params(("parallel", "arbitrary"), nbytes),
        name="norm_matmul",
    )(x, g.reshape(1, d), w)


def _outproj_kernel(x_ref, oa_ref, ob_ref, w_ref, o_ref, *rest):
    ka = oa_ref.shape[1]
    w = _bf16_weight(w_ref, rest)
    o_ref[...] = x_ref[...] + _dot(oa_ref[...], w[0:ka]) + _dot(ob_ref[...], w[ka:])


def _outproj(x, oa, ob, w, layer, tm, tn):
    m, d = x.shape
    ka, kb = oa.shape[1], ob.shape[1]
    cast = w.ndim == 3
    assert not cast or m == tm
    wbytes = 4 + 2 if cast else 2
    nbytes = 4 * tm * tn * 4 + 2 * tm * (ka + kb) * 2 + 2 * (ka + kb) * tn * wbytes
    x_spec = pl.BlockSpec((tm, tn), lambda i, j: (i, j))
    x_shape = jax.ShapeDtypeStruct((m, d), F32)
    w_block, w_index = (ka + kb, tn), lambda i, j: (0, j)
    return pl.pallas_call(
        _outproj_kernel,
        grid=(m // tm, d // tn),
        in_specs=[x_spec,
                  pl.BlockSpec((tm, ka), lambda i, j: (i, 0)),
                  pl.BlockSpec((tm, kb), lambda i, j: (i, 0)),
                  _weight_spec(w, layer, w_block, w_index)],
        out_specs=(x_spec, pl.BlockSpec(w_block, w_index)) if cast else x_spec,
        out_shape=(x_shape, jax.ShapeDtypeStruct((ka + kb, d), BF16)) if cast else x_shape,
        compiler_params=_params(("parallel", "arbitrary"), nbytes),
        name="outproj",
    )(x, oa, ob, w)


def _ffn_kernel(x_ref, g_ref, wg_ref, wu_ref, wd_ref, o_ref, *rest):
    h_ref = rest[-1]
    copies = rest[:-1]

    @pl.when(pl.program_id(1) == 0)
    def _():
        x = x_ref[...]
        h_ref[...] = _rms_rows(x, g_ref[...]).astype(BF16)
        o_ref[...] = x

    h = h_ref[...]
    gate = _dot(h, _bf16_weight(wg_ref, copies[0:1]))
    up = _dot(h, _bf16_weight(wu_ref, copies[1:2]))
    act = (_silu(gate) * up).astype(BF16)
    o_ref[...] += _dot(act, _bf16_weight(wd_ref, copies[2:3]))


def _ffn(x, g, wg, wu, wd, layer, tm, tf):
    m, d = x.shape
    f = wg.shape[-1]
    cast = wg.ndim == 3
    assert not cast or m == tm
    wbytes = 4 + 2 if cast else 2
    nbytes = 4 * tm * d * 4 + tm * d * 2 + 2 * 3 * d * tf * wbytes + 3 * tm * tf * 4
    x_spec = pl.BlockSpec((tm, d), lambda i, j: (i, 0))
    x_shape = jax.ShapeDtypeStruct((m, d), F32)
    col_block, col_index = (d, tf), lambda i, j: (0, j)
    row_block, row_index = (tf, d), lambda i, j: (j, 0)
    copy_specs = (pl.BlockSpec(col_block, col_index), pl.BlockSpec(col_block, col_index),
                  pl.BlockSpec(row_block, row_index))
    copy_shapes = (jax.ShapeDtypeStruct((d, f), BF16), jax.ShapeDtypeStruct((d, f), BF16),
                   jax.ShapeDtypeStruct((f, d), BF16))
    return pl.pallas_call(
        _ffn_kernel,
        grid=(m // tm, f // tf),
        in_specs=[x_spec,
                  pl.BlockSpec((1, d), lambda i, j: (0, 0)),
                  _weight_spec(wg, layer, col_block, col_index),
                  _weight_spec(wu, layer, col_block, col_index),
                  _weight_spec(wd, layer, row_block, row_index)],
        out_specs=(x_spec,) + copy_specs if cast else x_spec,
        out_shape=(x_shape,) + copy_shapes if cast else x_shape,
        scratch_shapes=[pltpu.VMEM((tm, d), BF16)],
        compiler_params=_params(("parallel", "arbitrary"), nbytes),
        name="ffn",
    )(x, g.reshape(1, d), wg, wu, wd)


def _hgrn_static(c):
    assert (HGRN_BASE - 1) * -math.log(LB_FLOOR) / 2 < 80.0
    t = np.arange(c)
    tri = (t[None, :] <= t[:, None]).astype(np.float32)
    first = HGRN_BASE * (t // HGRN_BASE)
    base = tri - 0.5 * (tri[first] + tri[first + HGRN_BASE - 1])
    masks = []
    h = c // 2
    while h >= HGRN_BASE:
        grp = t // (2 * h)
        upper = (t % (2 * h)) >= h
        masks.append((grp[:, None] == grp[None, :]) & upper[:, None] & ~upper[None, :])
        h //= 2
    masks.append((first[:, None] == first[None, :]) & (t[None, :] <= t[:, None]))
    p_all = np.concatenate([tri, base], axis=0)
    p2 = np.concatenate([p_all, p_all], axis=1)
    masks = np.stack(masks).astype(np.float32)
    return p2, np.concatenate([masks, masks], axis=2)


def _hgrn_prompt_kernel(q_ref, f_ref, i_ref, g_ref, lbf_ref, oml_ref, gn_ref, p2_ref, m_ref,
                        oa_ref, st_ref, s_scr, *, chunk, heads):
    c_idx = pl.program_id(2)

    @pl.when(c_idx == 0)
    def _():
        s_scr[...] = jnp.zeros_like(s_scr)

    width = heads * A_DK
    pw = 2 * A_DK
    sz, snz = _sig_pair(f_ref[0])
    oml = oml_ref[...]
    lf = jnp.log(lbf_ref[...] + oml * sz)
    ka = oml * snz
    hi, lo = _split_bf16(lf)
    expo = _dot(p2_ref[...], jnp.concatenate([hi, lo], axis=0))
    b = expo[0:chunk]
    q = _silu(q_ref[0])
    v = i_ref[0]
    gate = _silu(g_ref[0])
    gn = gn_ref[...]

    row = lax.broadcasted_iota(jnp.int32, (chunk, width), 0)
    qs, ks = [], []
    h = chunk // 2
    while h >= HGRN_BASE:
        groups = chunk // (2 * h)
        mid = jnp.broadcast_to(b.reshape(groups, 2 * h, width)[:, h - 1:h, :], (groups, 2 * h, width))
        d = b - mid.reshape(chunk, width)
        e = jnp.exp(jnp.where((row & h) != 0, d, -d))
        qs.append((q * e).astype(BF16))
        ks.append((ka * e).astype(BF16))
        h //= 2
    e_base = expo[chunk:2 * chunk]
    qs.append((q * jnp.exp(e_base)).astype(BF16))
    ks.append((ka * jnp.exp(-e_base)).astype(BF16))
    b_last = b[chunk - 1:chunk, :]
    q0 = (q * jnp.exp(b)).astype(BF16)
    k_tail = (ka * jnp.exp(b_last - b)).astype(BF16)
    decay = jnp.exp(b_last)
    vb = v.astype(BF16)

    left = lax.broadcasted_iota(jnp.int32, (chunk, pw), 1) < A_DK

    def bdiag(x):
        z = jnp.zeros_like(x)
        return jnp.concatenate([jnp.where(left, x, z), jnp.where(left, z, x)], axis=0)

    for p in range(heads // 2):
        sl = slice(p * pw, (p + 1) * pw)
        st = s_scr[p]
        o = _dot_nt(q0[:, sl], bdiag(st.astype(BF16)))
        a = jnp.zeros((chunk, pw), F32)
        for li in range(len(qs)):
            a = a + _dot_nt(qs[li][:, sl], bdiag(ks[li][:, sl])) * m_ref[li]
        o = o + _dot(a.astype(BF16), bdiag(vb[:, sl]))
        vp = v[:, sl]
        vt = jnp.concatenate([vp[:, :A_DV].T, vp[:, A_DV:].T], axis=1).astype(BF16)
        s_scr[p] = st * decay[:, sl] + _dot(vt, bdiag(k_tail[:, sl]))
        for hh in range(2):
            hs = slice(p * pw + hh * A_DV, p * pw + (hh + 1) * A_DV)
            oa_ref[0, :, hs] = (_rms_rows(o[:, hh * A_DV:(hh + 1) * A_DV], gn) * gate[:, hs]).astype(BF16)

    @pl.when(c_idx == pl.num_programs(2) - 1)
    def _():
        for hh in range(heads):
            st_ref[0, hh] = s_scr[hh // 2][:, (hh % 2) * A_DK:(hh % 2 + 1) * A_DK].T


def _hgrn_prompt(u3, lbf, oml, gn):
    bsz, length, _ = u3.shape
    chunk, heads = HGRN_CHUNK, HGRN_HEADS_PER_STEP
    cols = heads * A_DK
    assert length % chunk == 0 and A_HEADS % heads == 0 and heads % 2 == 0 and chunk == A_DV
    p2, masks = _hgrn_static(chunk)
    p2 = jnp.asarray(p2, BF16)
    masks = jnp.asarray(masks, F32)

    def col_spec(offset):
        base = offset // cols
        return pl.BlockSpec((1, chunk, cols), lambda b, h, c: (b, c, base + h))

    nbytes = (2 * 5 * chunk * cols * 4 + 2 * (p2.size * 2 + masks.size * 4)
              + 6 * p2.shape[0] * cols * 4)
    return pl.pallas_call(
        functools.partial(_hgrn_prompt_kernel, chunk=chunk, heads=heads),
        grid=(bsz, A_HEADS // heads, length // chunk),
        in_specs=[col_spec(Q_A), col_spec(F_A), col_spec(I_A), col_spec(G_A),
                  pl.BlockSpec((1, cols), lambda b, h, c: (0, h)),
                  pl.BlockSpec((1, cols), lambda b, h, c: (0, h)),
                  pl.BlockSpec((1, A_DV), lambda b, h, c: (0, 0)),
                  pl.BlockSpec(p2.shape, lambda b, h, c: (0, 0)),
                  pl.BlockSpec(masks.shape, lambda b, h, c: (0, 0, 0))],
        out_specs=(pl.BlockSpec((1, chunk, cols), lambda b, h, c: (b, c, h)),
                   pl.BlockSpec((1, heads, A_DK, A_DV), lambda b, h, c: (b, h, 0, 0))),
        out_shape=(jax.ShapeDtypeStruct((bsz, length, A_WIDTH), BF16),
                   jax.ShapeDtypeStruct((bsz, A_HEADS, A_DK, A_DV), F32)),
        scratch_shapes=[pltpu.VMEM((heads // 2, A_DV, 2 * A_DK), F32)],
        compiler_params=_params(("parallel", "parallel", "arbitrary"), nbytes),
        name="hgrn_prompt",
    )(u3, u3, u3, u3, lbf, oml, gn.reshape(1, A_DV), p2, masks)


def _hgrn_sample_kernel(u_ref, lbf_ref, oml_ref, gn_ref, s_ref, oa_ref, so_ref, *, steps):
    for i in range(u_ref.shape[0]):
        _hgrn_sample_one(i, u_ref, lbf_ref, oml_ref, gn_ref, s_ref, oa_ref, so_ref, steps)


def _hgrn_sample_one(i, u_ref, lbf_ref, oml_ref, gn_ref, s_ref, oa_ref, so_ref, steps):
    u = u_ref[i]
    q = _silu(u[:, Q_A:Q_A + A_KEY_WIDTH])
    sz, snz = _sig_pair(u[:, F_A:F_A + A_KEY_WIDTH])
    oml = oml_ref[...]
    f = lbf_ref[...] + oml * sz
    ka = oml * snz
    v = u[:, I_A:I_A + A_WIDTH]
    gate = _silu(u[:, G_A:G_A + A_WIDTH])
    gn = gn_ref[...]
    rows = u.shape[0]
    row = lax.broadcasted_iota(jnp.int32, f.shape, 0)
    live = row < steps
    f = jnp.where(live, f, 1.0)
    ka = jnp.where(live, ka, 0.0)

    def down(x, d, fill):
        return x if d == 0 else jnp.where(row >= d, pltpu.roll(x, d, axis=0), fill)

    def up(x, d, fill):
        return jnp.where(row < rows - d, pltpu.roll(x, rows - d, axis=0), fill)

    decay = [jnp.ones_like(f)]
    for d in range(1, steps + 1):
        decay.append(decay[-1] * down(f, d - 1, 1.0))
    head = decay[steps]
    tail = jnp.ones_like(f)
    for d in range(1, steps):
        tail = tail * up(f, d, 1.0)
    q_in = (q * head).astype(BF16)
    k_tail = ka * tail
    total = head[steps - 1:steps, :]
    t_hi = total.astype(BF16).astype(F32)
    t_mid = (total - t_hi).astype(BF16).astype(F32)
    t_lo = (total - t_hi - t_mid).astype(BF16).astype(F32)
    t_rows = jnp.where(row == 0, t_hi, jnp.where(row == 1, t_mid, jnp.where(row == 2, t_lo, 0.0)))
    prods = [q * down(ka, d, 0.0) * decay[d] for d in range(steps)]
    v_sh = [down(v, d, 0.0) for d in range(steps)]

    zsq = jnp.zeros((A_DK, A_DV), BF16)
    zrows = jnp.zeros((rows, A_DV), F32)
    r2 = lax.broadcasted_iota(jnp.int32, (rows, 2 * A_DV), 0)
    l2 = lax.broadcasted_iota(jnp.int32, (rows, 2 * A_DV), 1)
    ones_right = jnp.where((r2 < 3) & (l2 >= A_DV), 1.0, 0.0)

    for p in range(A_HEADS // 2):
        ps = slice(2 * p * A_DK, (2 * p + 2) * A_DK)
        s0 = [s_ref[i, 2 * p + hh] for hh in range(2)]
        s_bd = jnp.concatenate([jnp.concatenate([s0[0].astype(BF16), zsq], axis=1),
                                jnp.concatenate([zsq, s0[1].astype(BF16)], axis=1)], axis=0)
        o_pair = _dot(q_in[:, ps], s_bd)
        for hh in range(2):
            head_i = 2 * p + hh
            sl = slice(head_i * A_DK, (head_i + 1) * A_DK)
            o = o_pair[:, hh * A_DV:(hh + 1) * A_DV]
            for d in range(steps):
                o = o + jnp.sum(prods[d][:, sl], axis=-1, keepdims=True) * v_sh[d][:, sl]
            lhs = jnp.concatenate([k_tail[:, sl], t_rows[:, sl]], axis=0).astype(BF16)
            rhs = jnp.concatenate([jnp.concatenate([v[:, sl], zrows], axis=1), ones_right],
                                  axis=0).astype(BF16)
            upd = lax.dot_general(lhs, rhs, (((0,), (0,)), ((), ())), preferred_element_type=F32)
            so_ref[i, head_i] = s0[hh] * upd[:, A_DV:] + upd[:, :A_DV]
            oa_ref[i, :, sl] = (_rms_rows(o, gn) * gate[:, sl]).astype(BF16)


def _hgrn_sample(u3, lbf, oml, gn, state, layer, steps):
    bsz, rows, _ = u3.shape
    width = G_A + A_WIDTH
    bt = HGRN_SAMPLE_SEQS
    assert bsz % bt == 0
    nbytes = bt * (2 * rows * width * 4 + 4 * A_HEADS * A_DK * A_DV * 4)
    return pl.pallas_call(
        functools.partial(_hgrn_sample_kernel, steps=steps),
        grid=(bsz // bt,),
        in_specs=[pl.BlockSpec((bt, rows, width), lambda b: (b, 0, 0)),
                  pl.BlockSpec((1, A_KEY_WIDTH), lambda b: (0, 0)),
                  pl.BlockSpec((1, A_KEY_WIDTH), lambda b: (0, 0)),
                  pl.BlockSpec((1, A_DV), lambda b: (0, 0)),
                  pl.BlockSpec((None, bt, A_HEADS, A_DK, A_DV), lambda b: (layer, b, 0, 0, 0))],
        out_specs=(pl.BlockSpec((bt, rows, A_WIDTH), lambda b: (b, 0, 0)),
                   pl.BlockSpec((bt, A_HEADS, A_DK, A_DV), lambda b: (b, 0, 0, 0))),
        out_shape=(jax.ShapeDtypeStruct((bsz, rows, A_WIDTH), BF16),
                   jax.ShapeDtypeStruct(state.shape[1:], F32)),
        compiler_params=_params(("parallel",), nbytes),
        name="hgrn_sample",
    )(u3, lbf, oml, gn.reshape(1, A_DV), state)


def _t5_bucket(dist):
    n = np.maximum(dist, 0)
    max_exact = N_BUCKETS // 2
    nf = np.maximum(n, 1).astype(np.float32)
    large = max_exact + (np.log(nf / max_exact) / math.log(MAX_DISTANCE / max_exact)
                         * (N_BUCKETS - max_exact)).astype(np.int32)
    large = np.minimum(large, N_BUCKETS - 1)
    return np.where(n < max_exact, n, large).astype(np.int32)


def _bias_kernel(tab_ref, bucket_ref, valid_ref, o_ref):
    rows = bucket_ref.shape[0]
    bucket = bucket_ref[...]
    hits = [bucket == b for b in range(N_BUCKETS)]
    for h in range(B_HEADS):
        acc = jnp.zeros(bucket.shape, F32)
        for b in range(N_BUCKETS):
            acc = jnp.where(hits[b], tab_ref[b, h], acc)
        acc = acc * LOG2E
        j, r, c = h // 4, (h % 4) // 2, h % 2
        for m in range(valid_ref.shape[0]):
            o_ref[m, j, r * rows:(r + 1) * rows, c * KEY_SLOTS:(c + 1) * KEY_SLOTS] = jnp.where(
                valid_ref[m] != 0, acc, NEG_LOGIT * LOG2E)


def _bias_layout(table, dist, valids):
    rows = dist.shape[0]
    return pl.pallas_call(
        _bias_kernel,
        in_specs=[pl.BlockSpec(memory_space=pltpu.SMEM),
                  pl.BlockSpec(memory_space=pltpu.VMEM),
                  pl.BlockSpec(memory_space=pltpu.VMEM)],
        out_shape=jax.ShapeDtypeStruct((valids.shape[0], B_KV_HEADS, 2 * rows, 2 * KEY_SLOTS), F32),
        name="bias_prep",
    )(table.astype(F32), jnp.asarray(_t5_bucket(dist), jnp.int32), jnp.asarray(valids, jnp.int32))


def _block_diag_ones():
    r = lax.broadcasted_iota(jnp.int32, (LANES, LANES), 0) // B_HD
    c = lax.broadcasted_iota(jnp.int32, (LANES, LANES), 1) // B_HD
    return (r == c).astype(BF16)


def _head_norm(x, g2, bd):
    hi, lo = _split_bf16(x * x)
    ss = _dot(hi, bd) + _dot(lo, bd)
    return x * lax.rsqrt(ss * (1.0 / B_HD) + RMS_EPS) * g2


def _place(tile, half, lo_mask):
    rolled = pltpu.roll(tile, B_HD, axis=1)
    zero = jnp.zeros_like(tile)
    if half == 0:
        return jnp.where(lo_mask, tile, zero), jnp.where(lo_mask, zero, rolled)
    return jnp.where(lo_mask, rolled, zero), jnp.where(lo_mask, zero, tile)


def _swa_core(problems, bias_at, sink_ref):
    rows = problems[0][0][0].shape[0]
    lo_mask = lax.broadcasted_iota(jnp.int32, problems[0][1][0].shape, 1) < B_HD
    out_lo = lax.broadcasted_iota(jnp.int32, (rows, LANES), 1) < B_HD
    units = [(p, j) for p in range(len(problems)) for j in range(B_KV_HEADS)]
    scores, values = {}, {}
    for p, j in units:
        qn_tiles, k_tiles, v_tiles, _ = problems[p]
        t, half = j // 2, j % 2
        k_lo, k_hi = _place(k_tiles[t], half, lo_mask)
        v_lo, v_hi = _place(v_tiles[t], half, lo_mask)
        kk = jnp.concatenate([k_lo, k_hi], axis=0).astype(BF16)
        values[p, j] = jnp.concatenate([v_lo, v_hi], axis=0).astype(BF16)
        qq = jnp.concatenate([qn_tiles[2 * j], qn_tiles[2 * j + 1]], axis=0).astype(BF16)
        scores[p, j] = _dot_nt(qq, kk) + bias_at(j)
    weights, scales = {}, {}
    for p, j in units:
        e_rows, inv = [], []
        for r in range(2):
            e_cols, inv_r = [], []
            for c in range(2):
                sk = sink_ref[4 * j + 2 * r + c] * LOG2E
                sb = scores[p, j][r * rows:(r + 1) * rows, c * KEY_SLOTS:(c + 1) * KEY_SLOTS]
                m = jnp.maximum(jnp.max(sb, axis=-1, keepdims=True), sk)
                e = jnp.exp2(sb - m)
                inv_r.append(1.0 / (jnp.sum(e, axis=-1, keepdims=True) + jnp.exp2(sk - m)))
                e_cols.append(e.astype(BF16))
            e_rows.append(jnp.concatenate(e_cols, axis=1))
            inv.append(jnp.where(out_lo, inv_r[0], inv_r[1]))
        weights[p, j] = jnp.concatenate(e_rows, axis=0)
        scales[p, j] = inv
    for p, j in units:
        o = _dot(weights[p, j], values[p, j])
        for r in range(2):
            problems[p][3](2 * j + r, (o[r * rows:(r + 1) * rows] * scales[p, j][r]).astype(BF16))


def _swa_prompt_kernel(sink_ref, q_ref, kc_ref, kp_ref, vc_ref, vp_ref, qg_ref, kg_ref, bias_ref,
                       ob_ref, ko_ref, vo_ref):
    n = pl.program_id(1)
    bd = _block_diag_ones()
    k_all = jnp.concatenate([kp_ref[0], kc_ref[0]], axis=0)
    v_all = jnp.concatenate([vp_ref[0], vc_ref[0]], axis=0)
    kg = kg_ref[...]
    k_tiles = [_head_norm(k_all[:, t * LANES:(t + 1) * LANES], kg, bd) for t in range(2)]
    v_tiles = [v_all[:, t * LANES:(t + 1) * LANES] for t in range(2)]
    q = q_ref[0]
    qg = qg_ref[...]
    qn_tiles = [_head_norm(q[:, t * LANES:(t + 1) * LANES], qg, bd) for t in range(B_WIDTH // LANES)]

    def store(tile, val):
        ob_ref[0, :, tile * LANES:(tile + 1) * LANES] = val

    _swa_core([(qn_tiles, k_tiles, v_tiles, store)], lambda j: bias_ref[j], sink_ref)

    @pl.when(n == pl.num_programs(1) - 1)
    def _():
        ko_ref[0] = jnp.concatenate([kt[WINDOW:] for kt in k_tiles], axis=1)
        vo_ref[0] = vc_ref[0]


def _prompt_dist():
    return WINDOW + np.arange(WINDOW)[:, None] - np.arange(KEY_SLOTS)[None, :]


def _prompt_valid():
    j = np.arange(KEY_SLOTS)[None, :]
    dist = _prompt_dist()
    cur = (j >= WINDOW) & (dist >= 0)
    prev = (j < WINDOW) & (dist < WINDOW)
    return np.stack([cur, cur | prev]).astype(np.int32)


def _swa_prompt(u3, sinks, qg, kg, bias):
    bsz, length, _ = u3.shape
    nb = length // WINDOW
    qb, kb, vb = Q_B // B_WIDTH, K_B // B_KV_WIDTH, V_B // B_KV_WIDTH
    assert Q_B % B_WIDTH == 0 and K_B % B_KV_WIDTH == 0 and V_B % B_KV_WIDTH == 0
    kv_spec = lambda col, prev: pl.BlockSpec(
        (1, WINDOW, B_KV_WIDTH),
        (lambda b, n: (b, jnp.maximum(n - 1, 0), col)) if prev else (lambda b, n: (b, n, col)))
    g_spec = pl.BlockSpec((1, LANES), lambda b, n: (0, 0))
    nbytes = (2 * WINDOW * (B_WIDTH + 4 * B_KV_WIDTH) * 4 + bias.size * 4
              + 2 * WINDOW * B_WIDTH * 2 + 24 * WINDOW * 2 * KEY_SLOTS * 4)
    return pl.pallas_call(
        _swa_prompt_kernel,
        grid=(bsz, nb),
        in_specs=[pl.BlockSpec(memory_space=pltpu.SMEM),
                  pl.BlockSpec((1, WINDOW, B_WIDTH), lambda b, n: (b, n, qb)),
                  kv_spec(kb, False), kv_spec(kb, True), kv_spec(vb, False), kv_spec(vb, True),
                  g_spec, g_spec,
                  pl.BlockSpec((None,) + bias.shape[1:], lambda b, n: (jnp.minimum(n, 1), 0, 0, 0))],
        out_specs=(pl.BlockSpec((1, WINDOW, B_WIDTH), lambda b, n: (b, n, 0)),
                   pl.BlockSpec((1, WINDOW, B_KV_WIDTH), lambda b, n: (b, 0, 0)),
                   pl.BlockSpec((1, WINDOW, B_KV_WIDTH), lambda b, n: (b, 0, 0))),
        out_shape=(jax.ShapeDtypeStruct((bsz, length, B_WIDTH), BF16),
                   jax.ShapeDtypeStruct((bsz, WINDOW, B_KV_WIDTH), F32),
                   jax.ShapeDtypeStruct((bsz, WINDOW, B_KV_WIDTH), F32)),
        compiler_params=_params(("parallel", "arbitrary"), nbytes),
        name="swa_prompt",
    )(sinks, u3, u3, u3, u3, u3, qg, kg, bias)


def _swa_sample_kernel(sink_ref, q_ref, kv_ref, ck_ref, cv_ref, qg_ref, kg_ref, bias_ref,
                       ob_ref, ko_ref, vo_ref, *, steps):
    bd = _block_diag_ones()
    rows = q_ref.shape[1]
    kg = kg_ref[...]
    qg = qg_ref[...]
    pad = jnp.zeros((KEY_SLOTS - WINDOW - rows, LANES), F32)
    problems = []
    for i in range(q_ref.shape[0]):
        kv = kv_ref[i]
        k_new = [_head_norm(kv[:, t * LANES:(t + 1) * LANES], kg, bd) for t in range(2)]
        v_new = [kv[:, B_KV_WIDTH + t * LANES:B_KV_WIDTH + (t + 1) * LANES] for t in range(2)]
        ck, cv = ck_ref[i], cv_ref[i]
        k_tiles = [jnp.concatenate([ck[:, t * LANES:(t + 1) * LANES], k_new[t], pad], axis=0)
                   for t in range(2)]
        v_tiles = [jnp.concatenate([cv[:, t * LANES:(t + 1) * LANES], v_new[t], pad], axis=0)
                   for t in range(2)]
        q = q_ref[i]
        qn_tiles = [_head_norm(q[:, t * LANES:(t + 1) * LANES], qg, bd) for t in range(B_WIDTH // LANES)]

        def store(tile, val, i=i):
            ob_ref[i, :, tile * LANES:(tile + 1) * LANES] = val

        problems.append((qn_tiles, k_tiles, v_tiles, store))
        ko_ref[i, 0:WINDOW - steps, :] = ck_ref[i, steps:WINDOW, :]
        vo_ref[i, 0:WINDOW - steps, :] = cv_ref[i, steps:WINDOW, :]
        ko_ref[i, WINDOW - steps:WINDOW, :] = jnp.concatenate([kt[0:steps] for kt in k_new], axis=1)
        vo_ref[i, WINDOW - steps:WINDOW, :] = kv[0:steps, B_KV_WIDTH:]
    _swa_core(problems, lambda j: bias_ref[0, j], sink_ref)


def _sample_dist(rows):
    return WINDOW + np.arange(rows)[:, None] - np.arange(KEY_SLOTS)[None, :]


def _sample_valid(rows, steps):
    dist = _sample_dist(rows)
    j = np.arange(KEY_SLOTS)[None, :]
    return ((dist >= 0) & (dist < WINDOW) & (j < WINDOW + steps)).astype(np.int32)[None]


def _swa_sample(u3, cache_k, cache_v, layer, sinks, qg, kg, bias, steps):
    bsz, rows, _ = u3.shape
    w = cache_k.shape[2]
    bt = SWA_SAMPLE_SEQS
    assert w == WINDOW and bsz % bt == 0
    qb, kvb = Q_B // B_WIDTH, K_B // (2 * B_KV_WIDTH)
    assert K_B % (2 * B_KV_WIDTH) == 0
    c_in_spec = pl.BlockSpec((None, bt, w, B_KV_WIDTH), lambda b: (layer, b, 0, 0))
    c_spec = pl.BlockSpec((bt, w, B_KV_WIDTH), lambda b: (b, 0, 0))
    g_spec = pl.BlockSpec((1, LANES), lambda b: (0, 0))
    nbytes = bt * 8 * w * B_KV_WIDTH * 4 + 2 * bias.size * 4 + bt * 16 * KEY_SLOTS * LANES * 4
    return pl.pallas_call(
        functools.partial(_swa_sample_kernel, steps=steps),
        grid=(bsz // bt,),
        in_specs=[pl.BlockSpec(memory_space=pltpu.SMEM),
                  pl.BlockSpec((bt, rows, B_WIDTH), lambda b: (b, 0, qb)),
                  pl.BlockSpec((bt, rows, 2 * B_KV_WIDTH), lambda b: (b, 0, kvb)),
                  c_in_spec, c_in_spec, g_spec, g_spec,
                  pl.BlockSpec(bias.shape, lambda b: (0, 0, 0, 0))],
        out_specs=(pl.BlockSpec((bt, rows, B_WIDTH), lambda b: (b, 0, 0)), c_spec, c_spec),
        out_shape=(jax.ShapeDtypeStruct((bsz, rows, B_WIDTH), BF16),
                   jax.ShapeDtypeStruct((bsz, w, B_KV_WIDTH), F32),
                   jax.ShapeDtypeStruct((bsz, w, B_KV_WIDTH), F32)),
        compiler_params=_params(("parallel",), nbytes),
        name="swa_sample",
    )(sinks, u3, u3, cache_k, cache_v, qg, kg, bias)


def kernel(x_prompt, x_sample, cache_k, cache_v, state_hgrn, norm_mix, w_in, lower_bounds, hgrn_norm,
           q_norm, k_norm, attn_sinks, rel_bias_table, w_out, norm_ffn, w_gate, w_up, w_down):
    depth = w_in.shape[0]
    bp, seq, d = x_prompt.shape
    bd, ld, _ = x_sample.shape
    w = cache_k.shape[2]
    in_width = w_in.shape[2]

    lbf, oml = _lower_bounds(lower_bounds)
    sinks = attn_sinks.astype(F32)
    qg2 = jnp.tile(q_norm.astype(F32), (1, LANES // B_HD)) * (ATTN_SCALE * LOG2E)
    kg2 = jnp.tile(k_norm.astype(F32), (1, LANES // B_HD))
    ck = cache_k.astype(F32).reshape(depth, bd, w, B_KV_WIDTH)
    cv = cache_v.astype(F32).reshape(depth, bd, w, B_KV_WIDTH)

    xp = x_prompt.reshape(bp * seq, d)
    xs = jnp.pad(x_sample, ((0, 0), (0, SAMPLE_ROWS - ld), (0, 0))).reshape(bd * SAMPLE_ROWS, d)
    tm_p, tm_s = 1024, bd * SAMPLE_ROWS
    bias_p = _bias_layout(rel_bias_table, _prompt_dist(), _prompt_valid())
    bias_s = _bias_layout(rel_bias_table, _sample_dist(SAMPLE_ROWS), _sample_valid(SAMPLE_ROWS, ld))
    state = state_hgrn.astype(F32)

    pk, pv, ps, sk, sv, ss = [], [], [], [], [], []
    for l in range(depth):
        lbf_l, oml_l = lbf[l:l + 1], oml[l:l + 1]
        u, w_in_b = _norm_matmul(xs, norm_mix[l], w_in, l, tm_s, 512)
        u = u.reshape(bd, SAMPLE_ROWS, in_width)
        oa, st = _hgrn_sample(u, lbf_l, oml_l, hgrn_norm[l], state, l, ld)
        ob, kn, vn = _swa_sample(u, ck, cv, l, sinks[l], qg2[l:l + 1], kg2[l:l + 1], bias_s, ld)
        xs, w_out_b = _outproj(xs, oa.reshape(tm_s, A_WIDTH), ob.reshape(tm_s, B_WIDTH), w_out, l, tm_s, 512)
        xs, w_gate_b, w_up_b, w_down_b = _ffn(xs, norm_ffn[l], w_gate, w_up, w_down, l, tm_s, 512)
        sk.append(kn)
        sv.append(vn)
        ss.append(st)
        u = _norm_matmul(xp, norm_mix[l], w_in_b, None, 512, 2816).reshape(bp, seq, in_width)
        oa, st = _hgrn_prompt(u, lbf_l, oml_l, hgrn_norm[l])
        ob, kn, vn = _swa_prompt(u, sinks[l], qg2[l:l + 1], kg2[l:l + 1], bias_p)
        xp = _outproj(xp, oa.reshape(bp * seq, A_WIDTH), ob.reshape(bp * seq, B_WIDTH), w_out_b, None, 512, d)
        xp = _ffn(xp, norm_ffn[l], w_gate_b, w_up_b, w_down_b, None, tm_p, 512)
        pk.append(kn)
        pv.append(vn)
        ps.append(st)

    kv_shape = lambda n: (depth, n, WINDOW, B_KV_HEADS, B_HD)
    return (xp.reshape(bp, seq, d),
            xs.reshape(bd, SAMPLE_ROWS, d)[:, :ld],
            jnp.stack(pk).reshape(kv_shape(bp)), jnp.stack(pv).reshape(kv_shape(bp)), jnp.stack(ps),
            jnp.stack(sk).reshape(kv_shape(bd)), jnp.stack(sv).reshape(kv_shape(bd)), jnp.stack(ss))
```

```python
import functools
import math

import jax
import jax.numpy as jnp
import numpy as np
from jax import lax
from jax.experimental import pallas as pl
from jax.experimental.pallas import tpu as pltpu

F32 = jnp.float32
BF16 = jnp.bfloat16

LANES = 128
SUBLANES = 8
VMEM_BYTES_V7X = 64 * 1024 * 1024
VMEM_CAP = VMEM_BYTES_V7X - 8 * 1024 * 1024

A_HEADS = 8
A_DK = 128
A_DV = 128
A_KEY_WIDTH = A_HEADS * A_DK
A_WIDTH = A_HEADS * A_DV
B_HEADS = 16
B_KV_HEADS = 4
B_HD = 64
B_WIDTH = B_HEADS * B_HD
B_KV_WIDTH = B_KV_HEADS * B_HD
WINDOW = 128
ATTN_SCALE = 1.0 / math.sqrt(B_HD)
LOG2E = math.log2(math.e)
NEG_LOGIT = -1e30
LB_FLOOR = 1e-20
N_BUCKETS = 32
MAX_DISTANCE = 128
RMS_EPS = 1e-6

Q_A, F_A, I_A, G_A = 0, A_KEY_WIDTH, 2 * A_KEY_WIDTH, 2 * A_KEY_WIDTH + A_WIDTH
Q_B = 2 * A_KEY_WIDTH + 2 * A_WIDTH
K_B = Q_B + B_WIDTH
V_B = K_B + B_KV_WIDTH
IN_WIDTH = V_B + B_KV_WIDTH

HGRN_CHUNK = 128
HGRN_BASE = 4
HGRN_HEADS_PER_STEP = 8
SAMPLE_ROWS = SUBLANES
SWA_SAMPLE_SEQS = 4
HGRN_SAMPLE_SEQS = 2
KEY_SLOTS = 2 * WINDOW


def _params(sem, nbytes):
    assert nbytes <= VMEM_CAP, nbytes
    return pltpu.CompilerParams(dimension_semantics=sem, vmem_limit_bytes=VMEM_CAP)


def _sig_pair(z):
    th = 0.5 * jnp.tanh(0.5 * z)
    return 0.5 + th, 0.5 - th


def _silu(x):
    return x * _sig_pair(x)[0]


def _rms_rows(x, g):
    ms = jnp.mean(x * x, axis=-1, keepdims=True)
    return x * lax.rsqrt(ms + RMS_EPS) * g


def _split_bf16(x):
    hi = x.astype(BF16)
    lo = (x - hi.astype(F32)).astype(BF16)
    return hi, lo


def _dot(a, b):
    return jnp.dot(a, b, preferred_element_type=F32)


def _dot_nt(a, b):
    return lax.dot_general(a, b, (((1,), (1,)), ((), ())), preferred_element_type=F32)


def _lb_kernel(lb_ref, lbf_ref, oml_ref):
    depth = lb_ref.shape[0]
    rows = [lb_ref[i:i + 1, :] for i in range(depth)]
    m = functools.reduce(jnp.maximum, rows)
    e = [jnp.exp(r - m) for r in rows]
    s = functools.reduce(lambda a, b: a + b, e)
    p = [ei / s for ei in e]
    cum = p[0]
    for i in range(depth):
        if i > 0:
            cum = cum + p[i]
        lb = cum - p[0]
        lbf_ref[i:i + 1, :] = jnp.maximum(lb, LB_FLOOR)
        oml_ref[i:i + 1, :] = 1.0 - lb


def _lower_bounds(lower_bounds):
    shp = jax.ShapeDtypeStruct(lower_bounds.shape, F32)
    return pl.pallas_call(_lb_kernel, out_shape=(shp, shp), name="lb_prep")(lower_bounds.astype(F32))


def _weight_spec(w, layer, block, index):
    if w.ndim == 2:
        return pl.BlockSpec(block, index)
    return pl.BlockSpec((None,) + block, lambda *g: (layer,) + index(*g))


def _bf16_weight(w_ref, copy_refs):
    w = w_ref[...]
    if copy_refs:
        w = w.astype(BF16)
        copy_refs[0][...] = w
    return w


def _norm_matmul_kernel(x_ref, g_ref, w_ref, o_ref, *rest):
    h_ref = rest[-1]

    @pl.when(pl.program_id(1) == 0)
    def _():
        h_ref[...] = _rms_rows(x_ref[...], g_ref[...]).astype(BF16)

    o_ref[...] = _dot(h_ref[...], _bf16_weight(w_ref, rest[:-1]))


def _norm_matmul(x, g, w, layer, tm, tn):
    m, d = x.shape
    n = w.shape[-1]
    cast = w.ndim == 3
    assert not cast or m == tm
    wbytes = 4 + 2 if cast else 2
    nbytes = 2 * tm * d * 4 + tm * d * 2 + 2 * d * tn * wbytes + 3 * tm * tn * 4
    u_spec = pl.BlockSpec((tm, tn), lambda i, j: (i, j))
    u_shape = jax.ShapeDtypeStruct((m, n), F32)
    w_block, w_index = (d, tn), lambda i, j: (0, j)
    return pl.pallas_call(
        _norm_matmul_kernel,
        grid=(m // tm, n // tn),
        in_specs=[pl.BlockSpec((tm, d), lambda i, j: (i, 0)),
                  pl.BlockSpec((1, d), lambda i, j: (0, 0)),
                  _weight_spec(w, layer, w_block, w_index)],
        out_specs=(u_spec, pl.BlockSpec(w_block, w_index)) if cast else u_spec,
        out_shape=(u_shape, jax.ShapeDtypeStruct((d, n), BF16)) if cast else u_shape,
        scratch_shapes=[pltpu.VMEM((tm, d), BF16)],
        compiler_params=_params(("parallel", "arbitrary"), nbytes),
        name="norm_matmul",
    )(x, g.reshape(1, d), w)


def _outproj_kernel(x_ref, oa_ref, ob_ref, w_ref, o_ref, *rest):
    ka = oa_ref.shape[1]
    w = _bf16_weight(w_ref, rest)
    o_ref[...] = x_ref[...] + _dot(oa_ref[...], w[0:ka]) + _dot(ob_ref[...], w[ka:])


def _outproj(x, oa, ob, w, layer, tm, tn):
    m, d = x.shape
    ka, kb = oa.shape[1], ob.shape[1]
    cast = w.ndim == 3
    assert not cast or m == tm
    wbytes = 4 + 2 if cast else 2
    nbytes = 4 * tm * tn * 4 + 2 * tm * (ka + kb) * 2 + 2 * (ka + kb) * tn * wbytes
    x_spec = pl.BlockSpec((tm, tn), lambda i, j: (i, j))
    x_shape = jax.ShapeDtypeStruct((m, d), F32)
    w_block, w_index = (ka + kb, tn), lambda i, j: (0, j)
    return pl.pallas_call(
        _outproj_kernel,
        grid=(m // tm, d // tn),
        in_specs=[x_spec,
                  pl.BlockSpec((tm, ka), lambda i, j: (i, 0)),
                  pl.BlockSpec((tm, kb), lambda i, j: (i, 0)),
                  _weight_spec(w, layer, w_block, w_index)],
        out_specs=(x_spec, pl.BlockSpec(w_block, w_index)) if cast else x_spec,
        out_shape=(x_shape, jax.ShapeDtypeStruct((ka + kb, d), BF16)) if cast else x_shape,
        compiler_params=_params(("parallel", "arbitrary"), nbytes),
        name="outproj",
    )(x, oa, ob, w)


def _ffn_kernel(x_ref, g_ref, wg_ref, wu_ref, wd_ref, o_ref, *rest):
    h_ref = rest[-1]
    copies = rest[:-1]

    @pl.when(pl.program_id(1) == 0)
    def _():
        x = x_ref[...]
        h_ref[...] = _rms_rows(x, g_ref[...]).astype(BF16)
        o_ref[...] = x

    h = h_ref[...]
    gate = _dot(h, _bf16_weight(wg_ref, copies[0:1]))
    up = _dot(h, _bf16_weight(wu_ref, copies[1:2]))
    act = (_silu(gate) * up).astype(BF16)
    o_ref[...] += _dot(act, _bf16_weight(wd_ref, copies[2:3]))


def _ffn(x, g, wg, wu, wd, layer, tm, tf):
    m, d = x.shape
    f = wg.shape[-1]
    cast = wg.ndim == 3
    assert not cast or m == tm
    wbytes = 4 + 2 if cast else 2
    nbytes = 4 * tm * d * 4 + tm * d * 2 + 2 * 3 * d * tf * wbytes + 3 * tm * tf * 4
    x_spec = pl.BlockSpec((tm, d), lambda i, j: (i, 0))
    x_shape = jax.ShapeDtypeStruct((m, d), F32)
    col_block, col_index = (d, tf), lambda i, j: (0, j)
    row_block, row_index = (tf, d), lambda i, j: (j, 0)
    copy_specs = (pl.BlockSpec(col_block, col_index), pl.BlockSpec(col_block, col_index),
                  pl.BlockSpec(row_block, row_index))
    copy_shapes = (jax.ShapeDtypeStruct((d, f), BF16), jax.ShapeDtypeStruct((d, f), BF16),
                   jax.ShapeDtypeStruct((f, d), BF16))
    return pl.pallas_call(
        _ffn_kernel,
        grid=(m // tm, f // tf),
        in_specs=[x_spec,
                  pl.BlockSpec((1, d), lambda i, j: (0, 0)),
                  _weight_spec(wg, layer, col_block, col_index),
                  _weight_spec(wu, layer, col_block, col_index),
                  _weight_spec(wd, layer, row_block, row_index)],
        out_specs=(x_spec,) + copy_specs if cast else x_spec,
        out_shape=(x_shape,) + copy_shapes if cast else x_shape,
        scratch_shapes=[pltpu.VMEM((tm, d), BF16)],
        compiler_params=_params(("parallel", "arbitrary"), nbytes),
        name="ffn",
    )(x, g.reshape(1, d), wg, wu, wd)


def _hgrn_static(c):
    assert (HGRN_BASE - 1) * -math.log(LB_FLOOR) / 2 < 80.0
    t = np.arange(c)
    tri = (t[None, :] <= t[:, None]).astype(np.float32)
    first = HGRN_BASE * (t // HGRN_BASE)
    base = tri - 0.5 * (tri[first] + tri[first + HGRN_BASE - 1])
    masks = []
    h = c // 2
    while h >= HGRN_BASE:
        grp = t // (2 * h)
        upper = (t % (2 * h)) >= h
        masks.append((grp[:, None] == grp[None, :]) & upper[:, None] & ~upper[None, :])
        h //= 2
    masks.append((first[:, None] == first[None, :]) & (t[None, :] <= t[:, None]))
    p_all = np.concatenate([tri, base], axis=0)
    p2 = np.concatenate([p_all, p_all], axis=1)
    masks = np.stack(masks).astype(np.float32)
    return p2, np.concatenate([masks, masks], axis=2)


def _hgrn_prompt_kernel(q_ref, f_ref, i_ref, g_ref, lbf_ref, oml_ref, gn_ref, p2_ref, m_ref,
                        oa_ref, st_ref, s_scr, *, chunk, heads):
    c_idx = pl.program_id(2)

    @pl.when(c_idx == 0)
    def _():
        s_scr[...] = jnp.zeros_like(s_scr)

    width = heads * A_DK
    pw = 2 * A_DK
    sz, snz = _sig_pair(f_ref[0])
    oml = oml_ref[...]
    lf = jnp.log(lbf_ref[...] + oml * sz)
    ka = oml * snz
    hi, lo = _split_bf16(lf)
    expo = _dot(p2_ref[...], jnp.concatenate([hi, lo], axis=0)) * LOG2E
    b = expo[0:chunk]
    q = _silu(q_ref[0])
    v = i_ref[0]
    gate = _silu(g_ref[0])
    gn = gn_ref[...]

    row = lax.broadcasted_iota(jnp.int32, (chunk, width), 0)
    qs, ks = [], []
    h = chunk // 2
    while h >= HGRN_BASE:
        groups = chunk // (2 * h)
        mid = jnp.broadcast_to(b.reshape(groups, 2 * h, width)[:, h - 1:h, :], (groups, 2 * h, width))
        d = b - mid.reshape(chunk, width)
        e = jnp.exp2(jnp.where((row & h) != 0, d, -d))
        qs.append((q * e).astype(BF16))
        ks.append((ka * e).astype(BF16))
        h //= 2
    e_base = expo[chunk:2 * chunk]
    qs.append((q * jnp.exp2(e_base)).astype(BF16))
    ks.append((ka * jnp.exp2(-e_base)).astype(BF16))
    b_last = b[chunk - 1:chunk, :]
    q0 = (q * jnp.exp2(b)).astype(BF16)
    k_tail = (ka * jnp.exp2(b_last - b)).astype(BF16)
    decay = jnp.exp2(b_last)
    vb = v.astype(BF16)

    left = lax.broadcasted_iota(jnp.int32, (chunk, pw), 1) < A_DK

    def bdiag(x):
        z = jnp.zeros_like(x)
        return jnp.concatenate([jnp.where(left, x, z), jnp.where(left, z, x)], axis=0)

    for p in range(heads // 2):
        sl = slice(p * pw, (p + 1) * pw)
        st = s_scr[p]
        o = _dot_nt(q0[:, sl], bdiag(st.astype(BF16)))
        a = jnp.zeros((chunk, pw), BF16)
        for li in range(len(qs)):
            a = a + _dot_nt(qs[li][:, sl], bdiag(ks[li][:, sl])).astype(BF16) * m_ref[li]
        o = o + _dot(a, bdiag(vb[:, sl]))
        vp = v[:, sl]
        vt = jnp.concatenate([vp[:, :A_DV].T, vp[:, A_DV:].T], axis=1).astype(BF16)
        s_scr[p] = st * decay[:, sl] + _dot(vt, bdiag(k_tail[:, sl]))
        for hh in range(2):
            hs = slice(p * pw + hh * A_DV, p * pw + (hh + 1) * A_DV)
            oa_ref[0, :, hs] = (_rms_rows(o[:, hh * A_DV:(hh + 1) * A_DV], gn) * gate[:, hs]).astype(BF16)

    @pl.when(c_idx == pl.num_programs(2) - 1)
    def _():
        for hh in range(heads):
            st_ref[0, hh] = s_scr[hh // 2][:, (hh % 2) * A_DK:(hh % 2 + 1) * A_DK].T


def _hgrn_prompt(u3, lbf, oml, gn):
    bsz, length, _ = u3.shape
    chunk, heads = HGRN_CHUNK, HGRN_HEADS_PER_STEP
    cols = heads * A_DK
    assert length % chunk == 0 and A_HEADS % heads == 0 and heads % 2 == 0 and chunk == A_DV
    p2, masks = _hgrn_static(chunk)
    p2 = jnp.asarray(p2, BF16)
    masks = jnp.asarray(masks, BF16)

    def col_spec(offset):
        base = offset // cols
        return pl.BlockSpec((1, chunk, cols), lambda b, h, c: (b, c, base + h))

    nbytes = (2 * 5 * chunk * cols * 4 + 2 * (p2.size * 2 + masks.size * 4)
              + 6 * p2.shape[0] * cols * 4)
    return pl.pallas_call(
        functools.partial(_hgrn_prompt_kernel, chunk=chunk, heads=heads),
        grid=(bsz, A_HEADS // heads, length // chunk),
        in_specs=[col_spec(Q_A), col_spec(F_A), col_spec(I_A), col_spec(G_A),
                  pl.BlockSpec((1, cols), lambda b, h, c: (0, h)),
                  pl.BlockSpec((1, cols), lambda b, h, c: (0, h)),
                  pl.BlockSpec((1, A_DV), lambda b, h, c: (0, 0)),
                  pl.BlockSpec(p2.shape, lambda b, h, c: (0, 0)),
                  pl.BlockSpec(masks.shape, lambda b, h, c: (0, 0, 0))],
        out_specs=(pl.BlockSpec((1, chunk, cols), lambda b, h, c: (b, c, h)),
                   pl.BlockSpec((1, heads, A_DK, A_DV), lambda b, h, c: (b, h, 0, 0))),
        out_shape=(jax.ShapeDtypeStruct((bsz, length, A_WIDTH), BF16),
                   jax.ShapeDtypeStruct((bsz, A_HEADS, A_DK, A_DV), F32)),
        scratch_shapes=[pltpu.VMEM((heads // 2, A_DV, 2 * A_DK), F32)],
        compiler_params=_params(("parallel", "parallel", "arbitrary"), nbytes),
        name="hgrn_prompt",
    )(u3, u3, u3, u3, lbf, oml, gn.reshape(1, A_DV), p2, masks)


def _hgrn_sample_kernel(u_ref, lbf_ref, oml_ref, gn_ref, s_ref, oa_ref, so_ref, *, steps):
    for i in range(u_ref.shape[0]):
        _hgrn_sample_one(i, u_ref, lbf_ref, oml_ref, gn_ref, s_ref, oa_ref, so_ref, steps)


def _hgrn_sample_one(i, u_ref, lbf_ref, oml_ref, gn_ref, s_ref, oa_ref, so_ref, steps):
    u = u_ref[i]
    q = _silu(u[:, Q_A:Q_A + A_KEY_WIDTH])
    sz, snz = _sig_pair(u[:, F_A:F_A + A_KEY_WIDTH])
    oml = oml_ref[...]
    f = lbf_ref[...] + oml * sz
    ka = oml * snz
    v = u[:, I_A:I_A + A_WIDTH]
    gate = _silu(u[:, G_A:G_A + A_WIDTH])
    gn = gn_ref[...]
    rows = u.shape[0]
    row = lax.broadcasted_iota(jnp.int32, f.shape, 0)
    live = row < steps
    f = jnp.where(live, f, 1.0)
    ka = jnp.where(live, ka, 0.0)

    def down(x, d, fill):
        return x if d == 0 else jnp.where(row >= d, pltpu.roll(x, d, axis=0), fill)

    def up(x, d, fill):
        return jnp.where(row < rows - d, pltpu.roll(x, rows - d, axis=0), fill)

    decay = [jnp.ones_like(f)]
    for d in range(1, steps + 1):
        decay.append(decay[-1] * down(f, d - 1, 1.0))
    head = decay[steps]
    tail = jnp.ones_like(f)
    for d in range(1, steps):
        tail = tail * up(f, d, 1.0)
    q_in = (q * head).astype(BF16)
    k_tail = ka * tail
    total = head[steps - 1:steps, :]
    t_hi = total.astype(BF16).astype(F32)
    t_mid = (total - t_hi).astype(BF16).astype(F32)
    t_lo = (total - t_hi - t_mid).astype(BF16).astype(F32)
    t_rows = jnp.where(row == 0, t_hi, jnp.where(row == 1, t_mid, jnp.where(row == 2, t_lo, 0.0)))
    prods = [q * down(ka, d, 0.0) * decay[d] for d in range(steps)]
    v_sh = [down(v, d, 0.0) for d in range(steps)]

    zsq = jnp.zeros((A_DK, A_DV), BF16)
    zrows = jnp.zeros((rows, A_DV), F32)
    r2 = lax.broadcasted_iota(jnp.int32, (rows, 2 * A_DV), 0)
    l2 = lax.broadcasted_iota(jnp.int32, (rows, 2 * A_DV), 1)
    ones_right = jnp.where((r2 < 3) & (l2 >= A_DV), 1.0, 0.0)

    for p in range(A_HEADS // 2):
        ps = slice(2 * p * A_DK, (2 * p + 2) * A_DK)
        s0 = [s_ref[i, 2 * p + hh] for hh in range(2)]
        s_bd = jnp.concatenate([jnp.concatenate([s0[0].astype(BF16), zsq], axis=1),
                                jnp.concatenate([zsq, s0[1].astype(BF16)], axis=1)], axis=0)
        o_pair = _dot(q_in[:, ps], s_bd)
        for hh in range(2):
            head_i = 2 * p + hh
            sl = slice(head_i * A_DK, (head_i + 1) * A_DK)
            o = o_pair[:, hh * A_DV:(hh + 1) * A_DV]
            for d in range(steps):
                o = o + jnp.sum(prods[d][:, sl], axis=-1, keepdims=True) * v_sh[d][:, sl]
            lhs = jnp.concatenate([k_tail[:, sl], t_rows[:, sl]], axis=0).astype(BF16)
            rhs = jnp.concatenate([jnp.concatenate([v[:, sl], zrows], axis=1), ones_right],
                                  axis=0).astype(BF16)
            upd = lax.dot_general(lhs, rhs, (((0,), (0,)), ((), ())), preferred_element_type=F32)
            so_ref[i, head_i] = s0[hh] * upd[:, A_DV:] + upd[:, :A_DV]
            oa_ref[i, :, sl] = (_rms_rows(o, gn) * gate[:, sl]).astype(BF16)


def _hgrn_sample(u3, lbf, oml, gn, state, layer, steps):
    bsz, rows, _ = u3.shape
    width = G_A + A_WIDTH
    bt = HGRN_SAMPLE_SEQS
    assert bsz % bt == 0
    nbytes = bt * (2 * rows * width * 4 + 4 * A_HEADS * A_DK * A_DV * 4)
    return pl.pallas_call(
        functools.partial(_hgrn_sample_kernel, steps=steps),
        grid=(bsz // bt,),
        in_specs=[pl.BlockSpec((bt, rows, width), lambda b: (b, 0, 0)),
                  pl.BlockSpec((1, A_KEY_WIDTH), lambda b: (0, 0)),
                  pl.BlockSpec((1, A_KEY_WIDTH), lambda b: (0, 0)),
                  pl.BlockSpec((1, A_DV), lambda b: (0, 0)),
                  pl.BlockSpec((None, bt, A_HEADS, A_DK, A_DV), lambda b: (layer, b, 0, 0, 0))],
        out_specs=(pl.BlockSpec((bt, rows, A_WIDTH), lambda b: (b, 0, 0)),
                   pl.BlockSpec((bt, A_HEADS, A_DK, A_DV), lambda b: (b, 0, 0, 0))),
        out_shape=(jax.ShapeDtypeStruct((bsz, rows, A_WIDTH), BF16),
                   jax.ShapeDtypeStruct(state.shape[1:], F32)),
        compiler_params=_params(("parallel",), nbytes),
        name="hgrn_sample",
    )(u3, lbf, oml, gn.reshape(1, A_DV), state)


def _t5_bucket(dist):
    n = np.maximum(dist, 0)
    max_exact = N_BUCKETS // 2
    nf = np.maximum(n, 1).astype(np.float32)
    large = max_exact + (np.log(nf / max_exact) / math.log(MAX_DISTANCE / max_exact)
                         * (N_BUCKETS - max_exact)).astype(np.int32)
    large = np.minimum(large, N_BUCKETS - 1)
    return np.where(n < max_exact, n, large).astype(np.int32)


def _bias_kernel(tab_ref, bucket_ref, valid_ref, o_ref):
    rows = bucket_ref.shape[0]
    bucket = bucket_ref[...]
    hits = [bucket == b for b in range(N_BUCKETS)]
    for h in range(B_HEADS):
        acc = jnp.zeros(bucket.shape, F32)
        for b in range(N_BUCKETS):
            acc = jnp.where(hits[b], tab_ref[b, h], acc)
        acc = acc * LOG2E
        j, r, c = h // 4, (h % 4) // 2, h % 2
        for m in range(valid_ref.shape[0]):
            o_ref[m, j, r * rows:(r + 1) * rows, c * KEY_SLOTS:(c + 1) * KEY_SLOTS] = jnp.where(
                valid_ref[m] != 0, acc, NEG_LOGIT * LOG2E)


def _bias_layout(table, dist, valids):
    rows = dist.shape[0]
    return pl.pallas_call(
        _bias_kernel,
        in_specs=[pl.BlockSpec(memory_space=pltpu.SMEM),
                  pl.BlockSpec(memory_space=pltpu.VMEM),
                  pl.BlockSpec(memory_space=pltpu.VMEM)],
        out_shape=jax.ShapeDtypeStruct((valids.shape[0], B_KV_HEADS, 2 * rows, 2 * KEY_SLOTS), F32),
        name="bias_prep",
    )(table.astype(F32), jnp.asarray(_t5_bucket(dist), jnp.int32), jnp.asarray(valids, jnp.int32))


def _block_diag_ones():
    r = lax.broadcasted_iota(jnp.int32, (LANES, LANES), 0) // B_HD
    c = lax.broadcasted_iota(jnp.int32, (LANES, LANES), 1) // B_HD
    return (r == c).astype(BF16)


def _head_norm(x, g2, bd):
    hi, lo = _split_bf16(x * x)
    ss = _dot(hi, bd) + _dot(lo, bd)
    return x * lax.rsqrt(ss * (1.0 / B_HD) + RMS_EPS) * g2


def _place(tile, half, lo_mask):
    rolled = pltpu.roll(tile, B_HD, axis=1)
    zero = jnp.zeros_like(tile)
    if half == 0:
        return jnp.where(lo_mask, tile, zero), jnp.where(lo_mask, zero, rolled)
    return jnp.where(lo_mask, rolled, zero), jnp.where(lo_mask, zero, tile)


def _swa_core(problems, bias_at, sink_ref):
    rows = problems[0][0][0].shape[0]
    lo_mask = lax.broadcasted_iota(jnp.int32, problems[0][1][0].shape, 1) < B_HD
    out_lo = lax.broadcasted_iota(jnp.int32, (rows, LANES), 1) < B_HD
    units = [(p, j) for p in range(len(problems)) for j in range(B_KV_HEADS)]
    scores, values = {}, {}
    for p, j in units:
        qn_tiles, k_tiles, v_tiles, _ = problems[p]
        t, half = j // 2, j % 2
        k_lo, k_hi = _place(k_tiles[t], half, lo_mask)
        v_lo, v_hi = _place(v_tiles[t], half, lo_mask)
        kk = jnp.concatenate([k_lo, k_hi], axis=0).astype(BF16)
        values[p, j] = jnp.concatenate([v_lo, v_hi], axis=0).astype(BF16)
        qq = jnp.concatenate([qn_tiles[2 * j], qn_tiles[2 * j + 1]], axis=0).astype(BF16)
        scores[p, j] = _dot_nt(qq, kk) + bias_at(j)
    weights, scales = {}, {}
    for p, j in units:
        e_rows, inv = [], []
        for r in range(2):
            e_cols, inv_r = [], []
            for c in range(2):
                sk = sink_ref[4 * j + 2 * r + c] * LOG2E
                sb = scores[p, j][r * rows:(r + 1) * rows, c * KEY_SLOTS:(c + 1) * KEY_SLOTS]
                m = jnp.maximum(jnp.max(sb, axis=-1, keepdims=True), sk)
                e = jnp.exp2(sb - m)
                inv_r.append(1.0 / (jnp.sum(e, axis=-1, keepdims=True) + jnp.exp2(sk - m)))
                e_cols.append(e.astype(BF16))
            e_rows.append(jnp.concatenate(e_cols, axis=1))
            inv.append(jnp.where(out_lo, inv_r[0], inv_r[1]))
        weights[p, j] = jnp.concatenate(e_rows, axis=0)
        scales[p, j] = inv
    for p, j in units:
        o = _dot(weights[p, j], values[p, j])
        for r in range(2):
            problems[p][3](2 * j + r, (o[r * rows:(r + 1) * rows] * scales[p, j][r]).astype(BF16))


def _swa_prompt_kernel(sink_ref, q_ref, kc_ref, kp_ref, vc_ref, vp_ref, qg_ref, kg_ref, bias_ref,
                       ob_ref, ko_ref, vo_ref):
    n = pl.program_id(1)
    bd = _block_diag_ones()
    k_all = jnp.concatenate([kp_ref[0], kc_ref[0]], axis=0)
    v_all = jnp.concatenate([vp_ref[0], vc_ref[0]], axis=0)
    kg = kg_ref[...]
    k_tiles = [_head_norm(k_all[:, t * LANES:(t + 1) * LANES], kg, bd) for t in range(2)]
    v_tiles = [v_all[:, t * LANES:(t + 1) * LANES] for t in range(2)]
    q = q_ref[0]
    qg = qg_ref[...]
    qn_tiles = [_head_norm(q[:, t * LANES:(t + 1) * LANES], qg, bd) for t in range(B_WIDTH // LANES)]

    def store(tile, val):
        ob_ref[0, :, tile * LANES:(tile + 1) * LANES] = val

    _swa_core([(qn_tiles, k_tiles, v_tiles, store)], lambda j: bias_ref[j], sink_ref)

    @pl.when(n == pl.num_programs(1) - 1)
    def _():
        ko_ref[0] = jnp.concatenate([kt[WINDOW:] for kt in k_tiles], axis=1)
        vo_ref[0] = vc_ref[0]


def _prompt_dist():
    return WINDOW + np.arange(WINDOW)[:, None] - np.arange(KEY_SLOTS)[None, :]


def _prompt_valid():
    j = np.arange(KEY_SLOTS)[None, :]
    dist = _prompt_dist()
    cur = (j >= WINDOW) & (dist >= 0)
    prev = (j < WINDOW) & (dist < WINDOW)
    return np.stack([cur, cur | prev]).astype(np.int32)


def _swa_prompt(u3, sinks, qg, kg, bias):
    bsz, length, _ = u3.shape
    nb = length // WINDOW
    qb, kb, vb = Q_B // B_WIDTH, K_B // B_KV_WIDTH, V_B // B_KV_WIDTH
    assert Q_B % B_WIDTH == 0 and K_B % B_KV_WIDTH == 0 and V_B % B_KV_WIDTH == 0
    kv_spec = lambda col, prev: pl.BlockSpec(
        (1, WINDOW, B_KV_WIDTH),
        (lambda b, n: (b, jnp.maximum(n - 1, 0), col)) if prev else (lambda b, n: (b, n, col)))
    g_spec = pl.BlockSpec((1, LANES), lambda b, n: (0, 0))
    nbytes = (2 * WINDOW * (B_WIDTH + 4 * B_KV_WIDTH) * 4 + bias.size * 4
              + 2 * WINDOW * B_WIDTH * 2 + 24 * WINDOW * 2 * KEY_SLOTS * 4)
    return pl.pallas_call(
        _swa_prompt_kernel,
        grid=(bsz, nb),
        in_specs=[pl.BlockSpec(memory_space=pltpu.SMEM),
                  pl.BlockSpec((1, WINDOW, B_WIDTH), lambda b, n: (b, n, qb)),
                  kv_spec(kb, False), kv_spec(kb, True), kv_spec(vb, False), kv_spec(vb, True),
                  g_spec, g_spec,
                  pl.BlockSpec((None,) + bias.shape[1:], lambda b, n: (jnp.minimum(n, 1), 0, 0, 0))],
        out_specs=(pl.BlockSpec((1, WINDOW, B_WIDTH), lambda b, n: (b, n, 0)),
                   pl.BlockSpec((1, WINDOW, B_KV_WIDTH), lambda b, n: (b, 0, 0)),
                   pl.BlockSpec((1, WINDOW, B_KV_WIDTH), lambda b, n: (b, 0, 0))),
        out_shape=(jax.ShapeDtypeStruct((bsz, length, B_WIDTH), BF16),
                   jax.ShapeDtypeStruct((bsz, WINDOW, B_KV_WIDTH), F32),
                   jax.ShapeDtypeStruct((bsz, WINDOW, B_KV_WIDTH), F32)),
        compiler_params=_params(("parallel", "arbitrary"), nbytes),
        name="swa_prompt",
    )(sinks, u3, u3, u3, u3, u3, qg, kg, bias)


def _swa_sample_kernel(sink_ref, q_ref, kv_ref, ck_ref, cv_ref, qg_ref, kg_ref, bias_ref,
                       ob_ref, ko_ref, vo_ref, *, steps):
    bd = _block_diag_ones()
    rows = q_ref.shape[1]
    kg = kg_ref[...]
    qg = qg_ref[...]
    pad = jnp.zeros((KEY_SLOTS - WINDOW - rows, LANES), F32)
    problems = []
    for i in range(q_ref.shape[0]):
        kv = kv_ref[i]
        k_new = [_head_norm(kv[:, t * LANES:(t + 1) * LANES], kg, bd) for t in range(2)]
        v_new = [kv[:, B_KV_WIDTH + t * LANES:B_KV_WIDTH + (t + 1) * LANES] for t in range(2)]
        ck, cv = ck_ref[i], cv_ref[i]
        k_tiles = [jnp.concatenate([ck[:, t * LANES:(t + 1) * LANES], k_new[t], pad], axis=0)
                   for t in range(2)]
        v_tiles = [jnp.concatenate([cv[:, t * LANES:(t + 1) * LANES], v_new[t], pad], axis=0)
                   for t in range(2)]
        q = q_ref[i]
        qn_tiles = [_head_norm(q[:, t * LANES:(t + 1) * LANES], qg, bd) for t in range(B_WIDTH // LANES)]

        def store(tile, val, i=i):
            ob_ref[i, :, tile * LANES:(tile + 1) * LANES] = val

        problems.append((qn_tiles, k_tiles, v_tiles, store))
        ko_ref[i, 0:WINDOW - steps, :] = ck_ref[i, steps:WINDOW, :]
        vo_ref[i, 0:WINDOW - steps, :] = cv_ref[i, steps:WINDOW, :]
        ko_ref[i, WINDOW - steps:WINDOW, :] = jnp.concatenate([kt[0:steps] for kt in k_new], axis=1)
        vo_ref[i, WINDOW - steps:WINDOW, :] = kv[0:steps, B_KV_WIDTH:]
    _swa_core(problems, lambda j: bias_ref[0, j], sink_ref)


def _sample_dist(rows):
    return WINDOW + np.arange(rows)[:, None] - np.arange(KEY_SLOTS)[None, :]


def _sample_valid(rows, steps):
    dist = _sample_dist(rows)
    j = np.arange(KEY_SLOTS)[None, :]
    return ((dist >= 0) & (dist < WINDOW) & (j < WINDOW + steps)).astype(np.int32)[None]


def _swa_sample(u3, cache_k, cache_v, layer, sinks, qg, kg, bias, steps):
    bsz, rows, _ = u3.shape
    w = cache_k.shape[2]
    bt = SWA_SAMPLE_SEQS
    assert w == WINDOW and bsz % bt == 0
    qb, kvb = Q_B // B_WIDTH, K_B // (2 * B_KV_WIDTH)
    assert K_B % (2 * B_KV_WIDTH) == 0
    c_in_spec = pl.BlockSpec((None, bt, w, B_KV_WIDTH), lambda b: (layer, b, 0, 0))
    c_spec = pl.BlockSpec((bt, w, B_KV_WIDTH), lambda b: (b, 0, 0))
    g_spec = pl.BlockSpec((1, LANES), lambda b: (0, 0))
    nbytes = bt * 8 * w * B_KV_WIDTH * 4 + 2 * bias.size * 4 + bt * 16 * KEY_SLOTS * LANES * 4
    return pl.pallas_call(
        functools.partial(_swa_sample_kernel, steps=steps),
        grid=(bsz // bt,),
        in_specs=[pl.BlockSpec(memory_space=pltpu.SMEM),
                  pl.BlockSpec((bt, rows, B_WIDTH), lambda b: (b, 0, qb)),
                  pl.BlockSpec((bt, rows, 2 * B_KV_WIDTH), lambda b: (b, 0, kvb)),
                  c_in_spec, c_in_spec, g_spec, g_spec,
                  pl.BlockSpec(bias.shape, lambda b: (0, 0, 0, 0))],
        out_specs=(pl.BlockSpec((bt, rows, B_WIDTH), lambda b: (b, 0, 0)), c_spec, c_spec),
        out_shape=(jax.ShapeDtypeStruct((bsz, rows, B_WIDTH), BF16),
                   jax.ShapeDtypeStruct((bsz, w, B_KV_WIDTH), F32),
                   jax.ShapeDtypeStruct((bsz, w, B_KV_WIDTH), F32)),
        compiler_params=_params(("parallel",), nbytes),
        name="swa_sample",
    )(sinks, u3, u3, cache_k, cache_v, qg, kg, bias)


def kernel(x_prompt, x_sample, cache_k, cache_v, state_hgrn, norm_mix, w_in, lower_bounds, hgrn_norm,
           q_norm, k_norm, attn_sinks, rel_bias_table, w_out, norm_ffn, w_gate, w_up, w_down):
    depth = w_in.shape[0]
    bp, seq, d = x_prompt.shape
    bd, ld, _ = x_sample.shape
    w = cache_k.shape[2]
    in_width = w_in.shape[2]

    lbf, oml = _lower_bounds(lower_bounds)
    sinks = attn_sinks.astype(F32)
    qg2 = jnp.tile(q_norm.astype(F32), (1, LANES // B_HD)) * (ATTN_SCALE * LOG2E)
    kg2 = jnp.tile(k_norm.astype(F32), (1, LANES // B_HD))
    ck = cache_k.astype(F32).reshape(depth, bd, w, B_KV_WIDTH)
    cv = cache_v.astype(F32).reshape(depth, bd, w, B_KV_WIDTH)

    xp = x_prompt.reshape(bp * seq, d)
    xs = jnp.pad(x_sample, ((0, 0), (0, SAMPLE_ROWS - ld), (0, 0))).reshape(bd * SAMPLE_ROWS, d)
    tm_p, tm_s = 1024, bd * SAMPLE_ROWS
    bias_p = _bias_layout(rel_bias_table, _prompt_dist(), _prompt_valid())
    bias_s = _bias_layout(rel_bias_table, _sample_dist(SAMPLE_ROWS), _sample_valid(SAMPLE_ROWS, ld))
    state = state_hgrn.astype(F32)

    pk, pv, ps, sk, sv, ss = [], [], [], [], [], []
    for l in range(depth):
        lbf_l, oml_l = lbf[l:l + 1], oml[l:l + 1]
        u, w_in_b = _norm_matmul(xs, norm_mix[l], w_in, l, tm_s, 512)
        u = u.reshape(bd, SAMPLE_ROWS, in_width)
        oa, st = _hgrn_sample(u, lbf_l, oml_l, hgrn_norm[l], state, l, ld)
        ob, kn, vn = _swa_sample(u, ck, cv, l, sinks[l], qg2[l:l + 1], kg2[l:l + 1], bias_s, ld)
        xs, w_out_b = _outproj(xs, oa.reshape(tm_s, A_WIDTH), ob.reshape(tm_s, B_WIDTH), w_out, l, tm_s, 512)
        xs, w_gate_b, w_up_b, w_down_b = _ffn(xs, norm_ffn[l], w_gate, w_up, w_down, l, tm_s, 512)
        sk.append(kn)
        sv.append(vn)
        ss.append(st)
        u = _norm_matmul(xp, norm_mix[l], w_in_b, None, 512, 2816).reshape(bp, seq, in_width)
        oa, st = _hgrn_prompt(u, lbf_l, oml_l, hgrn_norm[l])
        ob, kn, vn = _swa_prompt(u, sinks[l], qg2[l:l + 1], kg2[l:l + 1], bias_p)
        xp = _outproj(xp, oa.reshape(bp * seq, A_WIDTH), ob.reshape(bp * seq, B_WIDTH), w_out_b, None, 512, d)
        xp = _ffn(xp, norm_ffn[l], w_gate_b, w_up_b, w_down_b, None, tm_p, 512)
        pk.append(kn)
        pv.append(vn)
        ps.append(st)

    kv_shape = lambda n: (depth, n, WINDOW, B_KV_HEADS, B_HD)
    return (xp.reshape(bp, seq, d),
            xs.reshape(bd, SAMPLE_ROWS, d)[:, :ld],
            jnp.stack(pk).reshape(kv_shape(bp)), jnp.stack(pv).reshape(kv_shape(bp)), jnp.stack(ps),
            jnp.stack(sk).reshape(kv_shape(bd)), jnp.stack(sv).reshape(kv_shape(bd)), jnp.stack(ss))
```

```python
import functools
import math

import jax
import jax.numpy as jnp
import numpy as np
from jax import lax
from jax.experimental import pallas as pl
from jax.experimental.pallas import tpu as pltpu

F32 = jnp.float32
BF16 = jnp.bfloat16

LANES = 128
SUBLANES = 8
VMEM_BYTES_V7X = 64 * 1024 * 1024
VMEM_CAP = VMEM_BYTES_V7X - 8 * 1024 * 1024

A_HEADS = 8
A_DK = 128
A_DV = 128
A_KEY_WIDTH = A_HEADS * A_DK
A_WIDTH = A_HEADS * A_DV
B_HEADS = 16
B_KV_HEADS = 4
B_HD = 64
B_WIDTH = B_HEADS * B_HD
B_KV_WIDTH = B_KV_HEADS * B_HD
WINDOW = 128
ATTN_SCALE = 1.0 / math.sqrt(B_HD)
LOG2E = math.log2(math.e)
NEG_LOGIT = -1e30
LB_FLOOR = 1e-20
N_BUCKETS = 32
MAX_DISTANCE = 128
RMS_EPS = 1e-6

Q_A, F_A, I_A, G_A = 0, A_KEY_WIDTH, 2 * A_KEY_WIDTH, 2 * A_KEY_WIDTH + A_WIDTH
Q_B = 2 * A_KEY_WIDTH + 2 * A_WIDTH
K_B = Q_B + B_WIDTH
V_B = K_B + B_KV_WIDTH
IN_WIDTH = V_B + B_KV_WIDTH

HGRN_CHUNK = 128
HGRN_BASE = 4
HGRN_HEADS_PER_STEP = 8
SAMPLE_ROWS = SUBLANES
SWA_SAMPLE_SEQS = 4
HGRN_SAMPLE_SEQS = 2
KEY_SLOTS = 2 * WINDOW


def _params(sem, nbytes):
    assert nbytes <= VMEM_CAP, nbytes
    return pltpu.CompilerParams(dimension_semantics=sem, vmem_limit_bytes=VMEM_CAP)


def _sig_pair(z):
    th = 0.5 * jnp.tanh(0.5 * z)
    return 0.5 + th, 0.5 - th


def _silu(x):
    return x * _sig_pair(x)[0]


def _rms_rows(x, g):
    ms = jnp.mean(x * x, axis=-1, keepdims=True)
    return x * lax.rsqrt(ms + RMS_EPS) * g


def _split_bf16(x):
    hi = x.astype(BF16)
    lo = (x - hi.astype(F32)).astype(BF16)
    return hi, lo


def _dot(a, b):
    return jnp.dot(a, b, preferred_element_type=F32)


def _dot_nt(a, b):
    return lax.dot_general(a, b, (((1,), (1,)), ((), ())), preferred_element_type=F32)


def _lb_kernel(lb_ref, lbf_ref, oml_ref):
    depth = lb_ref.shape[0]
    rows = [lb_ref[i:i + 1, :] for i in range(depth)]
    m = functools.reduce(jnp.maximum, rows)
    e = [jnp.exp(r - m) for r in rows]
    s = functools.reduce(lambda a, b: a + b, e)
    p = [ei / s for ei in e]
    cum = p[0]
    for i in range(depth):
        if i > 0:
            cum = cum + p[i]
        lb = cum - p[0]
        lbf_ref[i:i + 1, :] = jnp.maximum(lb, LB_FLOOR)
        oml_ref[i:i + 1, :] = 1.0 - lb


def _lower_bounds(lower_bounds):
    shp = jax.ShapeDtypeStruct(lower_bounds.shape, F32)
    return pl.pallas_call(_lb_kernel, out_shape=(shp, shp), name="lb_prep")(lower_bounds.astype(F32))


def _weight_spec(w, layer, block, index):
    if w.ndim == 2:
        return pl.BlockSpec(block, index)
    return pl.BlockSpec((None,) + block, lambda *g: (layer,) + index(*g))


def _bf16_weight(w_ref, copy_refs):
    w = w_ref[...]
    if copy_refs:
        w = w.astype(BF16)
        copy_refs[0][...] = w
    return w


def _norm_matmul_kernel(x_ref, g_ref, w_ref, o_ref, *copy_refs):
    h = _rms_rows(x_ref[...], g_ref[...]).astype(BF16)
    o_ref[...] = _dot(h, _bf16_weight(w_ref, copy_refs))


def _norm_matmul(x, g, w, layer, tm, tn):
    m, d = x.shape
    n = w.shape[-1]
    cast = w.ndim == 3
    assert not cast or m == tm
    wbytes = 4 + 2 if cast else 2
    nbytes = 2 * tm * d * 4 + tm * d * 2 + 2 * d * tn * wbytes + 3 * tm * tn * 4
    u_spec = pl.BlockSpec((tm, tn), lambda j, i: (i, j))
    u_shape = jax.ShapeDtypeStruct((m, n), F32)
    w_block, w_index = (d, tn), lambda j, i: (0, j)
    return pl.pallas_call(
        _norm_matmul_kernel,
        grid=(n // tn, m // tm),
        in_specs=[pl.BlockSpec((tm, d), lambda j, i: (i, 0)),
                  pl.BlockSpec((1, d), lambda j, i: (0, 0)),
                  _weight_spec(w, layer, w_block, w_index)],
        out_specs=(u_spec, pl.BlockSpec(w_block, w_index)) if cast else u_spec,
        out_shape=(u_shape, jax.ShapeDtypeStruct((d, n), BF16)) if cast else u_shape,
        compiler_params=_params(("parallel", "parallel"), nbytes),
        name="norm_matmul",
    )(x, g.reshape(1, d), w)


def _outproj_kernel(x_ref, oa_ref, ob_ref, w_ref, o_ref, *rest):
    ka = oa_ref.shape[1]
    w = _bf16_weight(w_ref, rest)
    o_ref[...] = x_ref[...] + _dot(oa_ref[...], w[0:ka]) + _dot(ob_ref[...], w[ka:])


def _outproj(x, oa, ob, w, layer, tm, tn):
    m, d = x.shape
    ka, kb = oa.shape[1], ob.shape[1]
    cast = w.ndim == 3
    assert not cast or m == tm
    wbytes = 4 + 2 if cast else 2
    nbytes = 4 * tm * tn * 4 + 2 * tm * (ka + kb) * 2 + 2 * (ka + kb) * tn * wbytes
    x_spec = pl.BlockSpec((tm, tn), lambda i, j: (i, j))
    x_shape = jax.ShapeDtypeStruct((m, d), F32)
    w_block, w_index = (ka + kb, tn), lambda i, j: (0, j)
    return pl.pallas_call(
        _outproj_kernel,
        grid=(m // tm, d // tn),
        in_specs=[x_spec,
                  pl.BlockSpec((tm, ka), lambda i, j: (i, 0)),
                  pl.BlockSpec((tm, kb), lambda i, j: (i, 0)),
                  _weight_spec(w, layer, w_block, w_index)],
        out_specs=(x_spec, pl.BlockSpec(w_block, w_index)) if cast else x_spec,
        out_shape=(x_shape, jax.ShapeDtypeStruct((ka + kb, d), BF16)) if cast else x_shape,
        compiler_params=_params(("parallel", "arbitrary"), nbytes),
        name="outproj",
    )(x, oa, ob, w)


def _ffn_kernel(x_ref, g_ref, wg_ref, wu_ref, wd_ref, o_ref, *rest):
    h_ref = rest[-1]
    copies = rest[:-1]

    @pl.when(pl.program_id(1) == 0)
    def _():
        x = x_ref[...]
        h_ref[...] = _rms_rows(x, g_ref[...]).astype(BF16)
        o_ref[...] = x

    h = h_ref[...]
    gate = _dot(h, _bf16_weight(wg_ref, copies[0:1]))
    up = _dot(h, _bf16_weight(wu_ref, copies[1:2]))
    act = (_silu(gate) * up).astype(BF16)
    o_ref[...] += _dot(act, _bf16_weight(wd_ref, copies[2:3]))


def _ffn(x, g, wg, wu, wd, layer, tm, tf):
    m, d = x.shape
    f = wg.shape[-1]
    cast = wg.ndim == 3
    assert not cast or m == tm
    wbytes = 4 + 2 if cast else 2
    nbytes = 4 * tm * d * 4 + tm * d * 2 + 2 * 3 * d * tf * wbytes + 3 * tm * tf * 4
    x_spec = pl.BlockSpec((tm, d), lambda i, j: (i, 0))
    x_shape = jax.ShapeDtypeStruct((m, d), F32)
    col_block, col_index = (d, tf), lambda i, j: (0, j)
    row_block, row_index = (tf, d), lambda i, j: (j, 0)
    copy_specs = (pl.BlockSpec(col_block, col_index), pl.BlockSpec(col_block, col_index),
                  pl.BlockSpec(row_block, row_index))
    copy_shapes = (jax.ShapeDtypeStruct((d, f), BF16), jax.ShapeDtypeStruct((d, f), BF16),
                   jax.ShapeDtypeStruct((f, d), BF16))
    return pl.pallas_call(
        _ffn_kernel,
        grid=(m // tm, f // tf),
        in_specs=[x_spec,
                  pl.BlockSpec((1, d), lambda i, j: (0, 0)),
                  _weight_spec(wg, layer, col_block, col_index),
                  _weight_spec(wu, layer, col_block, col_index),
                  _weight_spec(wd, layer, row_block, row_index)],
        out_specs=(x_spec,) + copy_specs if cast else x_spec,
        out_shape=(x_shape,) + copy_shapes if cast else x_shape,
        scratch_shapes=[pltpu.VMEM((tm, d), BF16)],
        compiler_params=_params(("parallel", "arbitrary"), nbytes),
        name="ffn",
    )(x, g.reshape(1, d), wg, wu, wd)


def _hgrn_static(c):
    assert (HGRN_BASE - 1) * -math.log(LB_FLOOR) / 2 < 80.0
    t = np.arange(c)
    tri = (t[None, :] <= t[:, None]).astype(np.float32)
    first = HGRN_BASE * (t // HGRN_BASE)
    base = tri - 0.5 * (tri[first] + tri[first + HGRN_BASE - 1])
    masks = []
    h = c // 2
    while h >= HGRN_BASE:
        grp = t // (2 * h)
        upper = (t % (2 * h)) >= h
        masks.append((grp[:, None] == grp[None, :]) & upper[:, None] & ~upper[None, :])
        h //= 2
    masks.append((first[:, None] == first[None, :]) & (t[None, :] <= t[:, None]))
    p_all = np.concatenate([tri, base], axis=0)
    p2 = np.concatenate([p_all, p_all], axis=1)
    masks = np.stack(masks).astype(np.float32)
    return p2, np.concatenate([masks, masks], axis=2)


def _hgrn_prompt_kernel(q_ref, f_ref, i_ref, g_ref, lbf_ref, oml_ref, gn_ref, p2_ref, m_ref,
                        oa_ref, st_ref, s_scr, *, chunk, heads):
    c_idx = pl.program_id(2)

    @pl.when(c_idx == 0)
    def _():
        s_scr[...] = jnp.zeros_like(s_scr)

    width = heads * A_DK
    pw = 2 * A_DK
    sz, snz = _sig_pair(f_ref[0])
    oml = oml_ref[...]
    lf = jnp.log(lbf_ref[...] + oml * sz)
    ka = oml * snz
    hi, lo = _split_bf16(lf)
    expo = _dot(p2_ref[...], jnp.concatenate([hi, lo], axis=0)) * LOG2E
    b = expo[0:chunk]
    q = _silu(q_ref[0])
    v = i_ref[0]
    gate = _silu(g_ref[0])
    gn = gn_ref[...]

    row = lax.broadcasted_iota(jnp.int32, (chunk, pw), 0)
    left = lax.broadcasted_iota(jnp.int32, (chunk, pw), 1) < A_DK

    def bdiag(x):
        z = jnp.zeros_like(x)
        return jnp.concatenate([jnp.where(left, x, z), jnp.where(left, z, x)], axis=0)

    for p in range(heads // 2):
        sl = slice(p * pw, (p + 1) * pw)
        qp, kp, vp, bp = q[:, sl], ka[:, sl], v[:, sl], b[:, sl]
        qs, ks = [], []
        h = chunk // 2
        while h >= HGRN_BASE:
            groups = chunk // (2 * h)
            mid = jnp.broadcast_to(bp.reshape(groups, 2 * h, pw)[:, h - 1:h, :], (groups, 2 * h, pw))
            d = bp - mid.reshape(chunk, pw)
            e = jnp.exp2(jnp.where((row & h) != 0, d, -d))
            qs.append((qp * e).astype(BF16))
            ks.append((kp * e).astype(BF16))
            h //= 2
        e_base = expo[chunk:2 * chunk, sl]
        qs.append((qp * jnp.exp2(e_base)).astype(BF16))
        ks.append((kp * jnp.exp2(-e_base)).astype(BF16))
        b_last = bp[chunk - 1:chunk, :]
        st = s_scr[p]
        o = _dot_nt((qp * jnp.exp2(bp)).astype(BF16), bdiag(st.astype(BF16)))
        a = jnp.zeros((chunk, pw), BF16)
        for li in range(len(qs)):
            a = a + _dot_nt(qs[li], bdiag(ks[li])).astype(BF16) * m_ref[li]
        o = o + _dot(a, bdiag(vp.astype(BF16)))
        vt = jnp.concatenate([vp[:, :A_DV].T, vp[:, A_DV:].T], axis=1).astype(BF16)
        k_tail = (kp * jnp.exp2(b_last - bp)).astype(BF16)
        s_scr[p] = st * jnp.exp2(b_last) + _dot(vt, bdiag(k_tail))
        for hh in range(2):
            hs = slice(p * pw + hh * A_DV, p * pw + (hh + 1) * A_DV)
            oa_ref[0, :, hs] = (_rms_rows(o[:, hh * A_DV:(hh + 1) * A_DV], gn) * gate[:, hs]).astype(BF16)

    @pl.when(c_idx == pl.num_programs(2) - 1)
    def _():
        for hh in range(heads):
            st_ref[0, hh] = s_scr[hh // 2][:, (hh % 2) * A_DK:(hh % 2 + 1) * A_DK].T


def _hgrn_prompt(u3, lbf, oml, gn):
    bsz, length, _ = u3.shape
    chunk, heads = HGRN_CHUNK, HGRN_HEADS_PER_STEP
    cols = heads * A_DK
    assert length % chunk == 0 and A_HEADS % heads == 0 and heads % 2 == 0 and chunk == A_DV
    p2, masks = _hgrn_static(chunk)
    p2 = jnp.asarray(p2, BF16)
    masks = jnp.asarray(masks, BF16)

    def col_spec(offset):
        base = offset // cols
        return pl.BlockSpec((1, chunk, cols), lambda b, h, c: (b, c, base + h))

    nbytes = (2 * 5 * chunk * cols * 4 + 2 * (p2.size * 2 + masks.size * 4)
              + 6 * p2.shape[0] * cols * 4)
    return pl.pallas_call(
        functools.partial(_hgrn_prompt_kernel, chunk=chunk, heads=heads),
        grid=(bsz, A_HEADS // heads, length // chunk),
        in_specs=[col_spec(Q_A), col_spec(F_A), col_spec(I_A), col_spec(G_A),
                  pl.BlockSpec((1, cols), lambda b, h, c: (0, h)),
                  pl.BlockSpec((1, cols), lambda b, h, c: (0, h)),
                  pl.BlockSpec((1, A_DV), lambda b, h, c: (0, 0)),
                  pl.BlockSpec(p2.shape, lambda b, h, c: (0, 0)),
                  pl.BlockSpec(masks.shape, lambda b, h, c: (0, 0, 0))],
        out_specs=(pl.BlockSpec((1, chunk, cols), lambda b, h, c: (b, c, h)),
                   pl.BlockSpec((1, heads, A_DK, A_DV), lambda b, h, c: (b, h, 0, 0))),
        out_shape=(jax.ShapeDtypeStruct((bsz, length, A_WIDTH), BF16),
                   jax.ShapeDtypeStruct((bsz, A_HEADS, A_DK, A_DV), F32)),
        scratch_shapes=[pltpu.VMEM((heads // 2, A_DV, 2 * A_DK), F32)],
        compiler_params=_params(("parallel", "parallel", "arbitrary"), nbytes),
        name="hgrn_prompt",
    )(u3, u3, u3, u3, lbf, oml, gn.reshape(1, A_DV), p2, masks)


def _hgrn_sample_kernel(u_ref, lbf_ref, oml_ref, gn_ref, s_ref, oa_ref, so_ref, *, steps):
    for i in range(u_ref.shape[0]):
        _hgrn_sample_one(i, u_ref, lbf_ref, oml_ref, gn_ref, s_ref, oa_ref, so_ref, steps)


def _hgrn_sample_one(i, u_ref, lbf_ref, oml_ref, gn_ref, s_ref, oa_ref, so_ref, steps):
    u = u_ref[i]
    q = _silu(u[:, Q_A:Q_A + A_KEY_WIDTH])
    sz, snz = _sig_pair(u[:, F_A:F_A + A_KEY_WIDTH])
    oml = oml_ref[...]
    f = lbf_ref[...] + oml * sz
    ka = oml * snz
    v = u[:, I_A:I_A + A_WIDTH]
    gate = _silu(u[:, G_A:G_A + A_WIDTH])
    gn = gn_ref[...]
    rows = u.shape[0]
    row = lax.broadcasted_iota(jnp.int32, f.shape, 0)
    live = row < steps
    f = jnp.where(live, f, 1.0)
    ka = jnp.where(live, ka, 0.0)

    def down(x, d, fill):
        return x if d == 0 else jnp.where(row >= d, pltpu.roll(x, d, axis=0), fill)

    def up(x, d, fill):
        return jnp.where(row < rows - d, pltpu.roll(x, rows - d, axis=0), fill)

    decay = [jnp.ones_like(f)]
    for d in range(1, steps + 1):
        decay.append(decay[-1] * down(f, d - 1, 1.0))
    head = decay[steps]
    tail = jnp.ones_like(f)
    for d in range(1, steps):
        tail = tail * up(f, d, 1.0)
    q_in = (q * head).astype(BF16)
    k_tail = ka * tail
    total = head[steps - 1:steps, :]
    t_hi = total.astype(BF16).astype(F32)
    t_mid = (total - t_hi).astype(BF16).astype(F32)
    t_lo = (total - t_hi - t_mid).astype(BF16).astype(F32)
    t_rows = jnp.where(row == 0, t_hi, jnp.where(row == 1, t_mid, jnp.where(row == 2, t_lo, 0.0)))
    prods = [q * down(ka, d, 0.0) * decay[d] for d in range(steps)]
    v_sh = [down(v, d, 0.0) for d in range(steps)]

    zsq = jnp.zeros((A_DK, A_DV), BF16)
    zrows = jnp.zeros((rows, A_DV), F32)
    r2 = lax.broadcasted_iota(jnp.int32, (rows, 2 * A_DV), 0)
    l2 = lax.broadcasted_iota(jnp.int32, (rows, 2 * A_DV), 1)
    ones_right = jnp.where((r2 < 3) & (l2 >= A_DV), 1.0, 0.0)

    for p in range(A_HEADS // 2):
        ps = slice(2 * p * A_DK, (2 * p + 2) * A_DK)
        s0 = [s_ref[i, 2 * p + hh] for hh in range(2)]
        s_bd = jnp.concatenate([jnp.concatenate([s0[0].astype(BF16), zsq], axis=1),
                                jnp.concatenate([zsq, s0[1].astype(BF16)], axis=1)], axis=0)
        o_pair = _dot(q_in[:, ps], s_bd)
        for hh in range(2):
            head_i = 2 * p + hh
            sl = slice(head_i * A_DK, (head_i + 1) * A_DK)
            o = o_pair[:, hh * A_DV:(hh + 1) * A_DV]
            for d in range(steps):
                o = o + jnp.sum(prods[d][:, sl], axis=-1, keepdims=True) * v_sh[d][:, sl]
            lhs = jnp.concatenate([k_tail[:, sl], t_rows[:, sl]], axis=0).astype(BF16)
            rhs = jnp.concatenate([jnp.concatenate([v[:, sl], zrows], axis=1), ones_right],
                                  axis=0).astype(BF16)
            upd = lax.dot_general(lhs, rhs, (((0,), (0,)), ((), ())), preferred_element_type=F32)
            so_ref[i, head_i] = s0[hh] * upd[:, A_DV:] + upd[:, :A_DV]
            oa_ref[i, :, sl] = (_rms_rows(o, gn) * gate[:, sl]).astype(BF16)


def _hgrn_sample(u3, lbf, oml, gn, state, layer, steps):
    bsz, rows, _ = u3.shape
    width = G_A + A_WIDTH
    bt = HGRN_SAMPLE_SEQS
    assert bsz % bt == 0
    nbytes = bt * (2 * rows * width * 4 + 4 * A_HEADS * A_DK * A_DV * 4)
    return pl.pallas_call(
        functools.partial(_hgrn_sample_kernel, steps=steps),
        grid=(bsz // bt,),
        in_specs=[pl.BlockSpec((bt, rows, width), lambda b: (b, 0, 0)),
                  pl.BlockSpec((1, A_KEY_WIDTH), lambda b: (0, 0)),
                  pl.BlockSpec((1, A_KEY_WIDTH), lambda b: (0, 0)),
                  pl.BlockSpec((1, A_DV), lambda b: (0, 0)),
                  pl.BlockSpec((None, bt, A_HEADS, A_DK, A_DV), lambda b: (layer, b, 0, 0, 0))],
        out_specs=(pl.BlockSpec((bt, rows, A_WIDTH), lambda b: (b, 0, 0)),
                   pl.BlockSpec((bt, A_HEADS, A_DK, A_DV), lambda b: (b, 0, 0, 0))),
        out_shape=(jax.ShapeDtypeStruct((bsz, rows, A_WIDTH), BF16),
                   jax.ShapeDtypeStruct(state.shape[1:], F32)),
        compiler_params=_params(("parallel",), nbytes),
        name="hgrn_sample",
    )(u3, lbf, oml, gn.reshape(1, A_DV), state)


def _t5_bucket(dist):
    n = np.maximum(dist, 0)
    max_exact = N_BUCKETS // 2
    nf = np.maximum(n, 1).astype(np.float32)
    large = max_exact + (np.log(nf / max_exact) / math.log(MAX_DISTANCE / max_exact)
                         * (N_BUCKETS - max_exact)).astype(np.int32)
    large = np.minimum(large, N_BUCKETS - 1)
    return np.where(n < max_exact, n, large).astype(np.int32)


def _bias_kernel(tab_ref, bucket_ref, valid_ref, o_ref):
    rows = bucket_ref.shape[0]
    bucket = bucket_ref[...]
    hits = [bucket == b for b in range(N_BUCKETS)]
    for h in range(B_HEADS):
        acc = jnp.zeros(bucket.shape, F32)
        for b in range(N_BUCKETS):
            acc = jnp.where(hits[b], tab_ref[b, h], acc)
        acc = acc * LOG2E
        j, r, c = h // 4, (h % 4) // 2, h % 2
        for m in range(valid_ref.shape[0]):
            o_ref[m, j, r * rows:(r + 1) * rows, c * KEY_SLOTS:(c + 1) * KEY_SLOTS] = jnp.where(
                valid_ref[m] != 0, acc, NEG_LOGIT * LOG2E)


def _bias_layout(table, dist, valids):
    rows = dist.shape[0]
    return pl.pallas_call(
        _bias_kernel,
        in_specs=[pl.BlockSpec(memory_space=pltpu.SMEM),
                  pl.BlockSpec(memory_space=pltpu.VMEM),
                  pl.BlockSpec(memory_space=pltpu.VMEM)],
        out_shape=jax.ShapeDtypeStruct((valids.shape[0], B_KV_HEADS, 2 * rows, 2 * KEY_SLOTS), F32),
        name="bias_prep",
    )(table.astype(F32), jnp.asarray(_t5_bucket(dist), jnp.int32), jnp.asarray(valids, jnp.int32))


def _block_diag_ones():
    r = lax.broadcasted_iota(jnp.int32, (LANES, LANES), 0) // B_HD
    c = lax.broadcasted_iota(jnp.int32, (LANES, LANES), 1) // B_HD
    return (r == c).astype(BF16)


def _head_norm(x, g2, bd):
    hi, lo = _split_bf16(x * x)
    ss = _dot(hi, bd) + _dot(lo, bd)
    return x * lax.rsqrt(ss * (1.0 / B_HD) + RMS_EPS) * g2


def _place(tile, half, lo_mask):
    rolled = pltpu.roll(tile, B_HD, axis=1)
    zero = jnp.zeros_like(tile)
    if half == 0:
        return jnp.where(lo_mask, tile, zero), jnp.where(lo_mask, zero, rolled)
    return jnp.where(lo_mask, rolled, zero), jnp.where(lo_mask, zero, tile)


def _swa_core(problems, bias_at, sink_ref):
    rows = problems[0][0][0].shape[0]
    lo_mask = lax.broadcasted_iota(jnp.int32, problems[0][1][0].shape, 1) < B_HD
    out_lo = lax.broadcasted_iota(jnp.int32, (rows, LANES), 1) < B_HD
    units = [(p, j) for p in range(len(problems)) for j in range(B_KV_HEADS)]
    scores, values = {}, {}
    for p, j in units:
        qn_tiles, k_tiles, v_tiles, _ = problems[p]
        t, half = j // 2, j % 2
        k_lo, k_hi = _place(k_tiles[t], half, lo_mask)
        v_lo, v_hi = _place(v_tiles[t], half, lo_mask)
        kk = jnp.concatenate([k_lo, k_hi], axis=0).astype(BF16)
        values[p, j] = jnp.concatenate([v_lo, v_hi], axis=0).astype(BF16)
        qq = jnp.concatenate([qn_tiles[2 * j], qn_tiles[2 * j + 1]], axis=0).astype(BF16)
        scores[p, j] = _dot_nt(qq, kk) + bias_at(j)
    weights, scales = {}, {}
    for p, j in units:
        e_rows, inv = [], []
        for r in range(2):
            e_cols, inv_r = [], []
            for c in range(2):
                sk = sink_ref[4 * j + 2 * r + c] * LOG2E
                sb = scores[p, j][r * rows:(r + 1) * rows, c * KEY_SLOTS:(c + 1) * KEY_SLOTS]
                m = jnp.maximum(jnp.max(sb, axis=-1, keepdims=True), sk)
                e = jnp.exp2(sb - m)
                inv_r.append(1.0 / (jnp.sum(e, axis=-1, keepdims=True) + jnp.exp2(sk - m)))
                e_cols.append(e.astype(BF16))
            e_rows.append(jnp.concatenate(e_cols, axis=1))
            inv.append(jnp.where(out_lo, inv_r[0], inv_r[1]))
        weights[p, j] = jnp.concatenate(e_rows, axis=0)
        scales[p, j] = inv
    for p, j in units:
        o = _dot(weights[p, j], values[p, j])
        for r in range(2):
            problems[p][3](2 * j + r, (o[r * rows:(r + 1) * rows] * scales[p, j][r]).astype(BF16))


def _swa_prompt_kernel(sink_ref, q_ref, kc_ref, kp_ref, vc_ref, vp_ref, qg_ref, kg_ref, bias_ref,
                       ob_ref, ko_ref, vo_ref):
    n = pl.program_id(1)
    bd = _block_diag_ones()
    k_all = jnp.concatenate([kp_ref[0], kc_ref[0]], axis=0)
    v_all = jnp.concatenate([vp_ref[0], vc_ref[0]], axis=0)
    kg = kg_ref[...]
    k_tiles = [_head_norm(k_all[:, t * LANES:(t + 1) * LANES], kg, bd) for t in range(2)]
    v_tiles = [v_all[:, t * LANES:(t + 1) * LANES] for t in range(2)]
    q = q_ref[0]
    qg = qg_ref[...]
    qn_tiles = [_head_norm(q[:, t * LANES:(t + 1) * LANES], qg, bd) for t in range(B_WIDTH // LANES)]

    def store(tile, val):
        ob_ref[0, :, tile * LANES:(tile + 1) * LANES] = val

    _swa_core([(qn_tiles, k_tiles, v_tiles, store)], lambda j: bias_ref[j], sink_ref)

    @pl.when(n == pl.num_programs(1) - 1)
    def _():
        ko_ref[0] = jnp.concatenate([kt[WINDOW:] for kt in k_tiles], axis=1)
        vo_ref[0] = vc_ref[0]


def _prompt_dist():
    return WINDOW + np.arange(WINDOW)[:, None] - np.arange(KEY_SLOTS)[None, :]


def _prompt_valid():
    j = np.arange(KEY_SLOTS)[None, :]
    dist = _prompt_dist()
    cur = (j >= WINDOW) & (dist >= 0)
    prev = (j < WINDOW) & (dist < WINDOW)
    return np.stack([cur, cur | prev]).astype(np.int32)


def _swa_prompt(u3, sinks, qg, kg, bias):
    bsz, length, _ = u3.shape
    nb = length // WINDOW
    qb, kb, vb = Q_B // B_WIDTH, K_B // B_KV_WIDTH, V_B // B_KV_WIDTH
    assert Q_B % B_WIDTH == 0 and K_B % B_KV_WIDTH == 0 and V_B % B_KV_WIDTH == 0
    kv_spec = lambda col, prev: pl.BlockSpec(
        (1, WINDOW, B_KV_WIDTH),
        (lambda b, n: (b, jnp.maximum(n - 1, 0), col)) if prev else (lambda b, n: (b, n, col)))
    g_spec = pl.BlockSpec((1, LANES), lambda b, n: (0, 0))
    nbytes = (2 * WINDOW * (B_WIDTH + 4 * B_KV_WIDTH) * 4 + bias.size * 4
              + 2 * WINDOW * B_WIDTH * 2 + 24 * WINDOW * 2 * KEY_SLOTS * 4)
    return pl.pallas_call(
        _swa_prompt_kernel,
        grid=(bsz, nb),
        in_specs=[pl.BlockSpec(memory_space=pltpu.SMEM),
                  pl.BlockSpec((1, WINDOW, B_WIDTH), lambda b, n: (b, n, qb)),
                  kv_spec(kb, False), kv_spec(kb, True), kv_spec(vb, False), kv_spec(vb, True),
                  g_spec, g_spec,
                  pl.BlockSpec((None,) + bias.shape[1:], lambda b, n: (jnp.minimum(n, 1), 0, 0, 0))],
        out_specs=(pl.BlockSpec((1, WINDOW, B_WIDTH), lambda b, n: (b, n, 0)),
                   pl.BlockSpec((1, WINDOW, B_KV_WIDTH), lambda b, n: (b, 0, 0)),
                   pl.BlockSpec((1, WINDOW, B_KV_WIDTH), lambda b, n: (b, 0, 0))),
        out_shape=(jax.ShapeDtypeStruct((bsz, length, B_WIDTH), BF16),
                   jax.ShapeDtypeStruct((bsz, WINDOW, B_KV_WIDTH), F32),
                   jax.ShapeDtypeStruct((bsz, WINDOW, B_KV_WIDTH), F32)),
        compiler_params=_params(("parallel", "arbitrary"), nbytes),
        name="swa_prompt",
    )(sinks, u3, u3, u3, u3, u3, qg, kg, bias)


def _swa_sample_kernel(sink_ref, q_ref, kv_ref, ck_ref, cv_ref, qg_ref, kg_ref, bias_ref,
                       ob_ref, ko_ref, vo_ref, *, steps):
    bd = _block_diag_ones()
    rows = q_ref.shape[1]
    kg = kg_ref[...]
    qg = qg_ref[...]
    pad = jnp.zeros((KEY_SLOTS - WINDOW - rows, LANES), F32)
    problems = []
    for i in range(q_ref.shape[0]):
        kv = kv_ref[i]
        k_new = [_head_norm(kv[:, t * LANES:(t + 1) * LANES], kg, bd) for t in range(2)]
        v_new = [kv[:, B_KV_WIDTH + t * LANES:B_KV_WIDTH + (t + 1) * LANES] for t in range(2)]
        ck, cv = ck_ref[i], cv_ref[i]
        k_tiles = [jnp.concatenate([ck[:, t * LANES:(t + 1) * LANES], k_new[t], pad], axis=0)
                   for t in range(2)]
        v_tiles = [jnp.concatenate([cv[:, t * LANES:(t + 1) * LANES], v_new[t], pad], axis=0)
                   for t in range(2)]
        q = q_ref[i]
        qn_tiles = [_head_norm(q[:, t * LANES:(t + 1) * LANES], qg, bd) for t in range(B_WIDTH // LANES)]

        def store(tile, val, i=i):
            ob_ref[i, :, tile * LANES:(tile + 1) * LANES] = val

        problems.append((qn_tiles, k_tiles, v_tiles, store))
        ko_ref[i, 0:WINDOW - steps, :] = ck_ref[i, steps:WINDOW, :]
        vo_ref[i, 0:WINDOW - steps, :] = cv_ref[i, steps:WINDOW, :]
        ko_ref[i, WINDOW - steps:WINDOW, :] = jnp.concatenate([kt[0:steps] for kt in k_new], axis=1)
        vo_ref[i, WINDOW - steps:WINDOW, :] = kv[0:steps, B_KV_WIDTH:]
    _swa_core(problems, lambda j: bias_ref[0, j], sink_ref)


def _sample_dist(rows):
    return WINDOW + np.arange(rows)[:, None] - np.arange(KEY_SLOTS)[None, :]


def _sample_valid(rows, steps):
    dist = _sample_dist(rows)
    j = np.arange(KEY_SLOTS)[None, :]
    return ((dist >= 0) & (dist < WINDOW) & (j < WINDOW + steps)).astype(np.int32)[None]


def _swa_sample(u3, cache_k, cache_v, layer, sinks, qg, kg, bias, steps):
    bsz, rows, _ = u3.shape
    w = cache_k.shape[2]
    bt = SWA_SAMPLE_SEQS
    assert w == WINDOW and bsz % bt == 0
    qb, kvb = Q_B // B_WIDTH, K_B // (2 * B_KV_WIDTH)
    assert K_B % (2 * B_KV_WIDTH) == 0
    c_in_spec = pl.BlockSpec((None, bt, w, B_KV_WIDTH), lambda b: (layer, b, 0, 0))
    c_spec = pl.BlockSpec((bt, w, B_KV_WIDTH), lambda b: (b, 0, 0))
    g_spec = pl.BlockSpec((1, LANES), lambda b: (0, 0))
    nbytes = bt * 8 * w * B_KV_WIDTH * 4 + 2 * bias.size * 4 + bt * 16 * KEY_SLOTS * LANES * 4
    return pl.pallas_call(
        functools.partial(_swa_sample_kernel, steps=steps),
        grid=(bsz // bt,),
        in_specs=[pl.BlockSpec(memory_space=pltpu.SMEM),
                  pl.BlockSpec((bt, rows, B_WIDTH), lambda b: (b, 0, qb)),
                  pl.BlockSpec((bt, rows, 2 * B_KV_WIDTH), lambda b: (b, 0, kvb)),
                  c_in_spec, c_in_spec, g_spec, g_spec,
                  pl.BlockSpec(bias.shape, lambda b: (0, 0, 0, 0))],
        out_specs=(pl.BlockSpec((bt, rows, B_WIDTH), lambda b: (b, 0, 0)), c_spec, c_spec),
        out_shape=(jax.ShapeDtypeStruct((bsz, rows, B_WIDTH), BF16),
                   jax.ShapeDtypeStruct((bsz, w, B_KV_WIDTH), F32),
                   jax.ShapeDtypeStruct((bsz, w, B_KV_WIDTH), F32)),
        compiler_params=_params(("parallel",), nbytes),
        name="swa_sample",
    )(sinks, u3, u3, cache_k, cache_v, qg, kg, bias)


def kernel(x_prompt, x_sample, cache_k, cache_v, state_hgrn, norm_mix, w_in, lower_bounds, hgrn_norm,
           q_norm, k_norm, attn_sinks, rel_bias_table, w_out, norm_ffn, w_gate, w_up, w_down):
    depth = w_in.shape[0]
    bp, seq, d = x_prompt.shape
    bd, ld, _ = x_sample.shape
    w = cache_k.shape[2]
    in_width = w_in.shape[2]

    lbf, oml = _lower_bounds(lower_bounds)
    sinks = attn_sinks.astype(F32)
    qg2 = jnp.tile(q_norm.astype(F32), (1, LANES // B_HD)) * (ATTN_SCALE * LOG2E)
    kg2 = jnp.tile(k_norm.astype(F32), (1, LANES // B_HD))
    ck = cache_k.astype(F32).reshape(depth, bd, w, B_KV_WIDTH)
    cv = cache_v.astype(F32).reshape(depth, bd, w, B_KV_WIDTH)

    xp = x_prompt.reshape(bp * seq, d)
    xs = jnp.pad(x_sample, ((0, 0), (0, SAMPLE_ROWS - ld), (0, 0))).reshape(bd * SAMPLE_ROWS, d)
    tm_p, tm_s = 1024, bd * SAMPLE_ROWS
    bias_p = _bias_layout(rel_bias_table, _prompt_dist(), _prompt_valid())
    bias_s = _bias_layout(rel_bias_table, _sample_dist(SAMPLE_ROWS), _sample_valid(SAMPLE_ROWS, ld))
    state = state_hgrn.astype(F32)

    pk, pv, ps, sk, sv, ss = [], [], [], [], [], []
    for l in range(depth):
        lbf_l, oml_l = lbf[l:l + 1], oml[l:l + 1]
        u, w_in_b = _norm_matmul(xs, norm_mix[l], w_in, l, tm_s, 512)
        u = u.reshape(bd, SAMPLE_ROWS, in_width)
        oa, st = _hgrn_sample(u, lbf_l, oml_l, hgrn_norm[l], state, l, ld)
        ob, kn, vn = _swa_sample(u, ck, cv, l, sinks[l], qg2[l:l + 1], kg2[l:l + 1], bias_s, ld)
        xs, w_out_b = _outproj(xs, oa.reshape(tm_s, A_WIDTH), ob.reshape(tm_s, B_WIDTH), w_out, l, tm_s, 512)
        xs, w_gate_b, w_up_b, w_down_b = _ffn(xs, norm_ffn[l], w_gate, w_up, w_down, l, tm_s, 512)
        sk.append(kn)
        sv.append(vn)
        ss.append(st)
        u = _norm_matmul(xp, norm_mix[l], w_in_b, None, 512, 2816).reshape(bp, seq, in_width)
        oa, st = _hgrn_prompt(u, lbf_l, oml_l, hgrn_norm[l])
        ob, kn, vn = _swa_prompt(u, sinks[l], qg2[l:l + 1], kg2[l:l + 1], bias_p)
        xp = _outproj(xp, oa.reshape(bp * seq, A_WIDTH), ob.reshape(bp * seq, B_WIDTH), w_out_b, None, 512, d)
        xp = _ffn(xp, norm_ffn[l], w_gate_b, w_up_b, w_down_b, None, tm_p, 512)
        pk.append(kn)
        pv.append(vn)
        ps.append(st)

    kv_shape = lambda n: (depth, n, WINDOW, B_KV_HEADS, B_HD)
    return (xp.reshape(bp, seq, d),
            xs.reshape(bd, SAMPLE_ROWS, d)[:, :ld],
            jnp.stack(pk).reshape(kv_shape(bp)), jnp.stack(pv).reshape(kv_shape(bp)), jnp.stack(ps),
            jnp.stack(sk).reshape(kv_shape(bd)), jnp.stack(sv).reshape(kv_shape(bd)), jnp.stack(ss))
```

```python
import functools
import math

import jax
import jax.numpy as jnp
import numpy as np
from jax import lax
from jax.experimental import pallas as pl
from jax.experimental.pallas import tpu as pltpu

F32 = jnp.float32
BF16 = jnp.bfloat16

LANES = 128
SUBLANES = 8
VMEM_BYTES_V7X = 64 * 1024 * 1024
VMEM_CAP = VMEM_BYTES_V7X - 8 * 1024 * 1024

A_HEADS = 8
A_DK = 128
A_DV = 128
A_KEY_WIDTH = A_HEADS * A_DK
A_WIDTH = A_HEADS * A_DV
B_HEADS = 16
B_KV_HEADS = 4
B_HD = 64
B_WIDTH = B_HEADS * B_HD
B_KV_WIDTH = B_KV_HEADS * B_HD
WINDOW = 128
ATTN_SCALE = 1.0 / math.sqrt(B_HD)
LOG2E = math.log2(math.e)
NEG_LOGIT = -1e30
LB_FLOOR = 1e-20
N_BUCKETS = 32
MAX_DISTANCE = 128
RMS_EPS = 1e-6

Q_A, F_A, I_A, G_A = 0, A_KEY_WIDTH, 2 * A_KEY_WIDTH, 2 * A_KEY_WIDTH + A_WIDTH
Q_B = 2 * A_KEY_WIDTH + 2 * A_WIDTH
K_B = Q_B + B_WIDTH
V_B = K_B + B_KV_WIDTH
IN_WIDTH = V_B + B_KV_WIDTH

W_IN_TILE = (512, 2816)
W_OUT_ROWS = 512
FFN_TILE = (1024, 512)
CAST_COLS = 512

HGRN_CHUNK = 128
HGRN_BASE = 4
HGRN_HEADS_PER_STEP = 8
SAMPLE_ROWS = SUBLANES
SWA_PROMPT_BLOCKS = 2
SWA_SAMPLE_SEQS = 4
HGRN_SAMPLE_SEQS = 2
KEY_SLOTS = 2 * WINDOW


def _params(sem, nbytes):
    assert nbytes <= VMEM_CAP, nbytes
    return pltpu.CompilerParams(dimension_semantics=sem, vmem_limit_bytes=VMEM_CAP)


def _sig_pair(z):
    th = 0.5 * jnp.tanh(0.5 * z)
    return 0.5 + th, 0.5 - th


def _silu(x):
    return x * _sig_pair(x)[0]


def _rms_rows(x, g):
    ms = jnp.mean(x * x, axis=-1, keepdims=True)
    return x * lax.rsqrt(ms + RMS_EPS) * g


def _split_bf16(x):
    hi = x.astype(BF16)
    lo = (x - hi.astype(F32)).astype(BF16)
    return hi, lo


def _dot(a, b):
    return jnp.dot(a, b, preferred_element_type=F32)


def _dot_nt(a, b):
    return lax.dot_general(a, b, (((1,), (1,)), ((), ())), preferred_element_type=F32)


def _lb_kernel(lb_ref, lbf_ref, oml_ref):
    depth = lb_ref.shape[0]
    rows = [lb_ref[i:i + 1, :] for i in range(depth)]
    m = functools.reduce(jnp.maximum, rows)
    e = [jnp.exp(r - m) for r in rows]
    s = functools.reduce(lambda a, b: a + b, e)
    p = [ei / s for ei in e]
    cum = p[0]
    for i in range(depth):
        if i > 0:
            cum = cum + p[i]
        lb = cum - p[0]
        lbf_ref[i:i + 1, :] = jnp.maximum(lb, LB_FLOOR)
        oml_ref[i:i + 1, :] = 1.0 - lb


def _lower_bounds(lower_bounds):
    shp = jax.ShapeDtypeStruct(lower_bounds.shape, F32)
    return pl.pallas_call(_lb_kernel, out_shape=(shp, shp), name="lb_prep")(lower_bounds.astype(F32))


def _weight_spec(w, layer, block, index):
    if w.ndim == 2:
        return pl.BlockSpec(block, index)
    return pl.BlockSpec((None,) + block, lambda *g: (layer,) + index(*g))


def _bf16_weight(w_ref, copy_refs):
    w = w_ref[...]
    if copy_refs:
        w = w.astype(BF16)
        copy_refs[0][...] = w
    return w


def _norm_matmul_kernel(x_ref, g_ref, w_ref, o_ref, *copy_refs):
    h = _rms_rows(x_ref[...], g_ref[...]).astype(BF16)
    o_ref[...] = _dot(h, _bf16_weight(w_ref, copy_refs))


def _norm_matmul(x, g, w, layer, tm, tn):
    m, d = x.shape
    n = w.shape[-1]
    cast = w.ndim == 3
    assert not cast or m == tm
    wbytes = 4 + 2 if cast else 2
    nbytes = 2 * tm * d * 4 + tm * d * 2 + 2 * d * tn * wbytes + 3 * tm * tn * 4
    u_spec = pl.BlockSpec((tm, tn), lambda j, i: (i, j))
    u_shape = jax.ShapeDtypeStruct((m, n), F32)
    w_block, w_index = (d, tn), lambda j, i: (0, j)
    return pl.pallas_call(
        _norm_matmul_kernel,
        grid=(n // tn, m // tm),
        in_specs=[pl.BlockSpec((tm, d), lambda j, i: (i, 0)),
                  pl.BlockSpec((1, d), lambda j, i: (0, 0)),
                  _weight_spec(w, layer, w_block, w_index)],
        out_specs=(u_spec, pl.BlockSpec(w_block, w_index)) if cast else u_spec,
        out_shape=(u_shape, jax.ShapeDtypeStruct((d, n), BF16)) if cast else u_shape,
        compiler_params=_params(("parallel", "parallel"), nbytes),
        name="norm_matmul",
    )(x, g.reshape(1, d), w)


def _outproj_kernel(x_ref, oa_ref, ob_ref, w_ref, o_ref, *rest):
    ka = oa_ref.shape[1]
    w = _bf16_weight(w_ref, rest)
    o_ref[...] = x_ref[...] + _dot(oa_ref[...], w[0:ka]) + _dot(ob_ref[...], w[ka:])


def _outproj(x, oa, ob, w, layer, tm, tn):
    m, d = x.shape
    ka, kb = oa.shape[1], ob.shape[1]
    cast = w.ndim == 3
    assert not cast or m == tm
    wbytes = 4 + 2 if cast else 2
    nbytes = 4 * tm * tn * 4 + 2 * tm * (ka + kb) * 2 + 2 * (ka + kb) * tn * wbytes
    x_spec = pl.BlockSpec((tm, tn), lambda i, j: (i, j))
    x_shape = jax.ShapeDtypeStruct((m, d), F32)
    w_block, w_index = (ka + kb, tn), lambda i, j: (0, j)
    return pl.pallas_call(
        _outproj_kernel,
        grid=(m // tm, d // tn),
        in_specs=[x_spec,
                  pl.BlockSpec((tm, ka), lambda i, j: (i, 0)),
                  pl.BlockSpec((tm, kb), lambda i, j: (i, 0)),
                  _weight_spec(w, layer, w_block, w_index)],
        out_specs=(x_spec, pl.BlockSpec(w_block, w_index)) if cast else x_spec,
        out_shape=(x_shape, jax.ShapeDtypeStruct((ka + kb, d), BF16)) if cast else x_shape,
        compiler_params=_params(("parallel", "arbitrary"), nbytes),
        name="outproj",
    )(x, oa, ob, w)


def _ffn_kernel(x_ref, g_ref, wg_ref, wu_ref, wd_ref, o_ref, *rest):
    h_ref = rest[-1]
    copies = rest[:-1]

    @pl.when(pl.program_id(1) == 0)
    def _():
        x = x_ref[...]
        h_ref[...] = _rms_rows(x, g_ref[...]).astype(BF16)
        o_ref[...] = x

    h = h_ref[...]
    gate = _dot(h, _bf16_weight(wg_ref, copies[0:1]))
    up = _dot(h, _bf16_weight(wu_ref, copies[1:2]))
    act = (_silu(gate) * up).astype(BF16)
    o_ref[...] += _dot(act, _bf16_weight(wd_ref, copies[2:3]))


def _ffn(x, g, wg, wu, wd, layer, tm, tf):
    m, d = x.shape
    f = wg.shape[-1]
    cast = wg.ndim == 3
    assert not cast or m == tm
    wbytes = 4 + 2 if cast else 2
    nbytes = 4 * tm * d * 4 + tm * d * 2 + 2 * 3 * d * tf * wbytes + 3 * tm * tf * 4
    x_spec = pl.BlockSpec((tm, d), lambda i, j: (i, 0))
    x_shape = jax.ShapeDtypeStruct((m, d), F32)
    col_block, col_index = (d, tf), lambda i, j: (0, j)
    row_block, row_index = (tf, d), lambda i, j: (j, 0)
    copy_specs = (pl.BlockSpec(col_block, col_index), pl.BlockSpec(col_block, col_index),
                  pl.BlockSpec(row_block, row_index))
    copy_shapes = (jax.ShapeDtypeStruct((d, f), BF16), jax.ShapeDtypeStruct((d, f), BF16),
                   jax.ShapeDtypeStruct((f, d), BF16))
    return pl.pallas_call(
        _ffn_kernel,
        grid=(m // tm, f // tf),
        in_specs=[x_spec,
                  pl.BlockSpec((1, d), lambda i, j: (0, 0)),
                  _weight_spec(wg, layer, col_block, col_index),
                  _weight_spec(wu, layer, col_block, col_index),
                  _weight_spec(wd, layer, row_block, row_index)],
        out_specs=(x_spec,) + copy_specs if cast else x_spec,
        out_shape=(x_shape,) + copy_shapes if cast else x_shape,
        scratch_shapes=[pltpu.VMEM((tm, d), BF16)],
        compiler_params=_params(("parallel", "arbitrary"), nbytes),
        name="ffn",
    )(x, g.reshape(1, d), wg, wu, wd)


def _hgrn_static(c):
    assert (HGRN_BASE - 1) * -math.log(LB_FLOOR) / 2 < 80.0
    t = np.arange(c)
    tri = (t[None, :] <= t[:, None]).astype(np.float32)
    first = HGRN_BASE * (t // HGRN_BASE)
    base = tri - 0.5 * (tri[first] + tri[first + HGRN_BASE - 1])
    masks = []
    h = c // 2
    while h >= HGRN_BASE:
        grp = t // (2 * h)
        upper = (t % (2 * h)) >= h
        masks.append((grp[:, None] == grp[None, :]) & upper[:, None] & ~upper[None, :])
        h //= 2
    masks.append((first[:, None] == first[None, :]) & (t[None, :] <= t[:, None]))
    p_all = np.concatenate([tri, base], axis=0)
    p2 = np.concatenate([p_all, p_all], axis=1)
    masks = np.stack(masks).astype(np.float32)
    return p2, np.concatenate([masks, masks], axis=2)


def _hgrn_prompt_kernel(q_ref, f_ref, i_ref, g_ref, lbf_ref, oml_ref, gn_ref, p2_ref, m_ref,
                        oa_ref, st_ref, s_scr, *, chunk, heads):
    c_idx = pl.program_id(2)

    @pl.when(c_idx == 0)
    def _():
        s_scr[...] = jnp.zeros_like(s_scr)

    width = heads * A_DK
    pw = 2 * A_DK
    sz, snz = _sig_pair(f_ref[0])
    oml = oml_ref[...]
    lf = jnp.log(lbf_ref[...] + oml * sz)
    ka = oml * snz
    hi, lo = _split_bf16(lf)
    expo = _dot(p2_ref[...], jnp.concatenate([hi, lo], axis=0)) * LOG2E
    b = expo[0:chunk]
    q = _silu(q_ref[0])
    v = i_ref[0]
    gate = _silu(g_ref[0])
    gn = gn_ref[...]

    row = lax.broadcasted_iota(jnp.int32, (chunk, pw), 0)
    left = lax.broadcasted_iota(jnp.int32, (chunk, pw), 1) < A_DK

    def bdiag(x):
        z = jnp.zeros_like(x)
        return jnp.concatenate([jnp.where(left, x, z), jnp.where(left, z, x)], axis=0)

    for p in range(heads // 2):
        sl = slice(p * pw, (p + 1) * pw)
        qp, kp, vp, bp = q[:, sl], ka[:, sl], v[:, sl], b[:, sl]
        qs, ks = [], []
        h = chunk // 2
        while h >= HGRN_BASE:
            groups = chunk // (2 * h)
            mid = jnp.broadcast_to(bp.reshape(groups, 2 * h, pw)[:, h - 1:h, :], (groups, 2 * h, pw))
            d = bp - mid.reshape(chunk, pw)
            e = jnp.exp2(jnp.where((row & h) != 0, d, -d))
            qs.append((qp * e).astype(BF16))
            ks.append((kp * e).astype(BF16))
            h //= 2
        e_base = expo[chunk:2 * chunk, sl]
        qs.append((qp * jnp.exp2(e_base)).astype(BF16))
        ks.append((kp * jnp.exp2(-e_base)).astype(BF16))
        b_last = bp[chunk - 1:chunk, :]
        st = s_scr[p]
        o = _dot_nt((qp * jnp.exp2(bp)).astype(BF16), bdiag(st.astype(BF16)))
        a = jnp.zeros((chunk, pw), BF16)
        for li in range(len(qs)):
            a = a + _dot_nt(qs[li], bdiag(ks[li])).astype(BF16) * m_ref[li]
        o = o + _dot(a, bdiag(vp.astype(BF16)))
        vt = jnp.concatenate([vp[:, :A_DV].T, vp[:, A_DV:].T], axis=1).astype(BF16)
        k_tail = (kp * jnp.exp2(b_last - bp)).astype(BF16)
        s_scr[p] = st * jnp.exp2(b_last) + _dot(vt, bdiag(k_tail))
        for hh in range(2):
            hs = slice(p * pw + hh * A_DV, p * pw + (hh + 1) * A_DV)
            oa_ref[0, :, hs] = (_rms_rows(o[:, hh * A_DV:(hh + 1) * A_DV], gn) * gate[:, hs]).astype(BF16)

    @pl.when(c_idx == pl.num_programs(2) - 1)
    def _():
        for hh in range(heads):
            st_ref[0, hh] = s_scr[hh // 2][:, (hh % 2) * A_DK:(hh % 2 + 1) * A_DK].T


def _hgrn_prompt(u3, lbf, oml, gn):
    bsz, length, _ = u3.shape
    chunk, heads = HGRN_CHUNK, HGRN_HEADS_PER_STEP
    cols = heads * A_DK
    assert length % chunk == 0 and A_HEADS % heads == 0 and heads % 2 == 0 and chunk == A_DV
    p2, masks = _hgrn_static(chunk)
    p2 = jnp.asarray(p2, BF16)
    masks = jnp.asarray(masks, BF16)

    def col_spec(offset):
        base = offset // cols
        return pl.BlockSpec((1, chunk, cols), lambda b, h, c: (b, c, base + h))

    nbytes = (2 * 5 * chunk * cols * 4 + 2 * (p2.size * 2 + masks.size * 4)
              + 6 * p2.shape[0] * cols * 4)
    return pl.pallas_call(
        functools.partial(_hgrn_prompt_kernel, chunk=chunk, heads=heads),
        grid=(bsz, A_HEADS // heads, length // chunk),
        in_specs=[col_spec(Q_A), col_spec(F_A), col_spec(I_A), col_spec(G_A),
                  pl.BlockSpec((1, cols), lambda b, h, c: (0, h)),
                  pl.BlockSpec((1, cols), lambda b, h, c: (0, h)),
                  pl.BlockSpec((1, A_DV), lambda b, h, c: (0, 0)),
                  pl.BlockSpec(p2.shape, lambda b, h, c: (0, 0)),
                  pl.BlockSpec(masks.shape, lambda b, h, c: (0, 0, 0))],
        out_specs=(pl.BlockSpec((1, chunk, cols), lambda b, h, c: (b, c, h)),
                   pl.BlockSpec((1, heads, A_DK, A_DV), lambda b, h, c: (b, h, 0, 0))),
        out_shape=(jax.ShapeDtypeStruct((bsz, length, A_WIDTH), BF16),
                   jax.ShapeDtypeStruct((bsz, A_HEADS, A_DK, A_DV), F32)),
        scratch_shapes=[pltpu.VMEM((heads // 2, A_DV, 2 * A_DK), F32)],
        compiler_params=_params(("parallel", "parallel", "arbitrary"), nbytes),
        name="hgrn_prompt",
    )(u3, u3, u3, u3, lbf, oml, gn.reshape(1, A_DV), p2, masks)


def _hgrn_sample_kernel(u_ref, lbf_ref, oml_ref, gn_ref, s_ref, oa_ref, so_ref, *, steps):
    for i in range(u_ref.shape[0]):
        _hgrn_sample_one(i, u_ref, lbf_ref, oml_ref, gn_ref, s_ref, oa_ref, so_ref, steps)


def _hgrn_sample_one(i, u_ref, lbf_ref, oml_ref, gn_ref, s_ref, oa_ref, so_ref, steps):
    u = u_ref[i]
    q = _silu(u[:, Q_A:Q_A + A_KEY_WIDTH])
    sz, snz = _sig_pair(u[:, F_A:F_A + A_KEY_WIDTH])
    oml = oml_ref[...]
    f = lbf_ref[...] + oml * sz
    ka = oml * snz
    v = u[:, I_A:I_A + A_WIDTH]
    gate = _silu(u[:, G_A:G_A + A_WIDTH])
    gn = gn_ref[...]
    rows = u.shape[0]
    row = lax.broadcasted_iota(jnp.int32, f.shape, 0)
    live = row < steps
    f = jnp.where(live, f, 1.0)
    ka = jnp.where(live, ka, 0.0)

    def down(x, d, fill):
        return x if d == 0 else jnp.where(row >= d, pltpu.roll(x, d, axis=0), fill)

    def up(x, d, fill):
        return jnp.where(row < rows - d, pltpu.roll(x, rows - d, axis=0), fill)

    decay = [jnp.ones_like(f)]
    for d in range(1, steps + 1):
        decay.append(decay[-1] * down(f, d - 1, 1.0))
    head = decay[steps]
    tail = jnp.ones_like(f)
    for d in range(1, steps):
        tail = tail * up(f, d, 1.0)
    q_in = (q * head).astype(BF16)
    k_tail = ka * tail
    total = head[steps - 1:steps, :]
    t_hi = total.astype(BF16).astype(F32)
    t_mid = (total - t_hi).astype(BF16).astype(F32)
    t_lo = (total - t_hi - t_mid).astype(BF16).astype(F32)
    t_rows = jnp.where(row == 0, t_hi, jnp.where(row == 1, t_mid, jnp.where(row == 2, t_lo, 0.0)))
    prods = [q * down(ka, d, 0.0) * decay[d] for d in range(steps)]
    v_sh = [down(v, d, 0.0) for d in range(steps)]

    zsq = jnp.zeros((A_DK, A_DV), BF16)
    zrows = jnp.zeros((rows, A_DV), F32)
    r2 = lax.broadcasted_iota(jnp.int32, (rows, 2 * A_DV), 0)
    l2 = lax.broadcasted_iota(jnp.int32, (rows, 2 * A_DV), 1)
    ones_right = jnp.where((r2 < 3) & (l2 >= A_DV), 1.0, 0.0)

    for p in range(A_HEADS // 2):
        ps = slice(2 * p * A_DK, (2 * p + 2) * A_DK)
        s0 = [s_ref[i, 2 * p + hh] for hh in range(2)]
        s_bd = jnp.concatenate([jnp.concatenate([s0[0].astype(BF16), zsq], axis=1),
                                jnp.concatenate([zsq, s0[1].astype(BF16)], axis=1)], axis=0)
        o_pair = _dot(q_in[:, ps], s_bd)
        for hh in range(2):
            head_i = 2 * p + hh
            sl = slice(head_i * A_DK, (head_i + 1) * A_DK)
            o = o_pair[:, hh * A_DV:(hh + 1) * A_DV]
            for d in range(steps):
                o = o + jnp.sum(prods[d][:, sl], axis=-1, keepdims=True) * v_sh[d][:, sl]
            lhs = jnp.concatenate([k_tail[:, sl], t_rows[:, sl]], axis=0).astype(BF16)
            rhs = jnp.concatenate([jnp.concatenate([v[:, sl], zrows], axis=1), ones_right],
                                  axis=0).astype(BF16)
            upd = lax.dot_general(lhs, rhs, (((0,), (0,)), ((), ())), preferred_element_type=F32)
            so_ref[i, head_i] = s0[hh] * upd[:, A_DV:] + upd[:, :A_DV]
            oa_ref[i, :, sl] = (_rms_rows(o, gn) * gate[:, sl]).astype(BF16)


def _hgrn_sample(u3, lbf, oml, gn, state, layer, steps):
    bsz, rows, _ = u3.shape
    width = G_A + A_WIDTH
    bt = HGRN_SAMPLE_SEQS
    assert bsz % bt == 0
    nbytes = bt * (2 * rows * width * 4 + 4 * A_HEADS * A_DK * A_DV * 4)
    return pl.pallas_call(
        functools.partial(_hgrn_sample_kernel, steps=steps),
        grid=(bsz // bt,),
        in_specs=[pl.BlockSpec((bt, rows, width), lambda b: (b, 0, 0)),
                  pl.BlockSpec((1, A_KEY_WIDTH), lambda b: (0, 0)),
                  pl.BlockSpec((1, A_KEY_WIDTH), lambda b: (0, 0)),
                  pl.BlockSpec((1, A_DV), lambda b: (0, 0)),
                  pl.BlockSpec((None, bt, A_HEADS, A_DK, A_DV), lambda b: (layer, b, 0, 0, 0))],
        out_specs=(pl.BlockSpec((bt, rows, A_WIDTH), lambda b: (b, 0, 0)),
                   pl.BlockSpec((bt, A_HEADS, A_DK, A_DV), lambda b: (b, 0, 0, 0))),
        out_shape=(jax.ShapeDtypeStruct((bsz, rows, A_WIDTH), BF16),
                   jax.ShapeDtypeStruct(state.shape[1:], F32)),
        compiler_params=_params(("parallel",), nbytes),
        name="hgrn_sample",
    )(u3, lbf, oml, gn.reshape(1, A_DV), state)


def _t5_bucket(dist):
    n = np.maximum(dist, 0)
    max_exact = N_BUCKETS // 2
    nf = np.maximum(n, 1).astype(np.float32)
    large = max_exact + (np.log(nf / max_exact) / math.log(MAX_DISTANCE / max_exact)
                         * (N_BUCKETS - max_exact)).astype(np.int32)
    large = np.minimum(large, N_BUCKETS - 1)
    return np.where(n < max_exact, n, large).astype(np.int32)


def _bias_kernel(tab_ref, bucket_ref, valid_ref, o_ref):
    rows = bucket_ref.shape[0]
    bucket = bucket_ref[...]
    hits = [bucket == b for b in range(N_BUCKETS)]
    for h in range(B_HEADS):
        acc = jnp.zeros(bucket.shape, F32)
        for b in range(N_BUCKETS):
            acc = jnp.where(hits[b], tab_ref[b, h], acc)
        acc = acc * LOG2E
        j, r, c = h // 4, (h % 4) // 2, h % 2
        for m in range(valid_ref.shape[0]):
            o_ref[m, j, r * rows:(r + 1) * rows, c * KEY_SLOTS:(c + 1) * KEY_SLOTS] = jnp.where(
                valid_ref[m] != 0, acc, NEG_LOGIT * LOG2E)


def _bias_layout(table, dist, valids):
    rows = dist.shape[0]
    return pl.pallas_call(
        _bias_kernel,
        in_specs=[pl.BlockSpec(memory_space=pltpu.SMEM),
                  pl.BlockSpec(memory_space=pltpu.VMEM),
                  pl.BlockSpec(memory_space=pltpu.VMEM)],
        out_shape=jax.ShapeDtypeStruct((valids.shape[0], B_KV_HEADS, 2 * rows, 2 * KEY_SLOTS), F32),
        name="bias_prep",
    )(table.astype(F32), jnp.asarray(_t5_bucket(dist), jnp.int32), jnp.asarray(valids, jnp.int32))


def _block_diag_ones():
    r = lax.broadcasted_iota(jnp.int32, (LANES, LANES), 0) // B_HD
    c = lax.broadcasted_iota(jnp.int32, (LANES, LANES), 1) // B_HD
    return (r == c).astype(BF16)


def _head_norm(x, g2, bd):
    hi, lo = _split_bf16(x * x)
    ss = _dot(hi, bd) + _dot(lo, bd)
    return x * lax.rsqrt(ss * (1.0 / B_HD) + RMS_EPS) * g2


def _place(tile, half, lo_mask):
    rolled = pltpu.roll(tile, B_HD, axis=1)
    zero = jnp.zeros_like(tile)
    if half == 0:
        return jnp.where(lo_mask, tile, zero), jnp.where(lo_mask, zero, rolled)
    return jnp.where(lo_mask, rolled, zero), jnp.where(lo_mask, zero, tile)


def _swa_core(problems, bias_at, sink_ref):
    rows = problems[0][0][0].shape[0]
    lo_mask = lax.broadcasted_iota(jnp.int32, problems[0][1][0].shape, 1) < B_HD
    out_lo = lax.broadcasted_iota(jnp.int32, (rows, LANES), 1) < B_HD
    units = [(p, j) for p in range(len(problems)) for j in range(B_KV_HEADS)]
    scores, values = {}, {}
    for p, j in units:
        qn_tiles, k_tiles, v_tiles, _ = problems[p]
        t, half = j // 2, j % 2
        k_lo, k_hi = _place(k_tiles[t], half, lo_mask)
        v_lo, v_hi = _place(v_tiles[t], half, lo_mask)
        kk = jnp.concatenate([k_lo, k_hi], axis=0).astype(BF16)
        values[p, j] = jnp.concatenate([v_lo, v_hi], axis=0).astype(BF16)
        qq = jnp.concatenate([qn_tiles[2 * j], qn_tiles[2 * j + 1]], axis=0).astype(BF16)
        scores[p, j] = _dot_nt(qq, kk) + bias_at(p, j)
    weights, scales = {}, {}
    for p, j in units:
        e_rows, inv = [], []
        for r in range(2):
            e_cols, inv_r = [], []
            for c in range(2):
                sk = sink_ref[4 * j + 2 * r + c] * LOG2E
                sb = scores[p, j][r * rows:(r + 1) * rows, c * KEY_SLOTS:(c + 1) * KEY_SLOTS]
                m = jnp.maximum(jnp.max(sb, axis=-1, keepdims=True), sk)
                e = jnp.exp2(sb - m)
                inv_r.append(1.0 / (jnp.sum(e, axis=-1, keepdims=True) + jnp.exp2(sk - m)))
                e_cols.append(e.astype(BF16))
            e_rows.append(jnp.concatenate(e_cols, axis=1))
            inv.append(jnp.where(out_lo, inv_r[0], inv_r[1]))
        weights[p, j] = jnp.concatenate(e_rows, axis=0)
        scales[p, j] = inv
    for p, j in units:
        o = _dot(weights[p, j], values[p, j])
        for r in range(2):
            problems[p][3](2 * j + r, (o[r * rows:(r + 1) * rows] * scales[p, j][r]).astype(BF16))


def _swa_prompt_kernel(sink_ref, q_ref, kc_ref, kp_ref, vc_ref, vp_ref, qg_ref, kg_ref, bias_ref,
                       ob_ref, ko_ref, vo_ref):
    m = pl.program_id(1)
    blocks = q_ref.shape[1] // WINDOW
    bd = _block_diag_ones()
    k_all = jnp.concatenate([kp_ref[0], kc_ref[0]], axis=0)
    v_all = jnp.concatenate([vp_ref[0], vc_ref[0]], axis=0)
    kg = kg_ref[...]
    k_norm = [_head_norm(k_all[:, t * LANES:(t + 1) * LANES], kg, bd) for t in range(2)]
    q = q_ref[0]
    qg = qg_ref[...]
    qn = [_head_norm(q[:, t * LANES:(t + 1) * LANES], qg, bd) for t in range(B_WIDTH // LANES)]
    first = jnp.minimum(m, 1)

    problems, biases = [], []
    for i in range(blocks):
        rows = slice(i * WINDOW, (i + 1) * WINDOW)
        keys = slice(i * WINDOW, i * WINDOW + KEY_SLOTS)

        def store(tile, val, rows=rows):
            ob_ref[0, rows, tile * LANES:(tile + 1) * LANES] = val

        problems.append(([qt[rows] for qt in qn], [kt[keys] for kt in k_norm],
                         [v_all[keys, t * LANES:(t + 1) * LANES] for t in range(2)], store))
        biases.append(first if i == 0 else 1)
    _swa_core(problems, lambda p, j: bias_ref[biases[p], j], sink_ref)

    @pl.when(m == pl.num_programs(1) - 1)
    def _():
        ko_ref[0] = jnp.concatenate([kt[blocks * WINDOW:] for kt in k_norm], axis=1)
        vo_ref[0] = vc_ref[0, (blocks - 1) * WINDOW:, :]


def _prompt_dist():
    return WINDOW + np.arange(WINDOW)[:, None] - np.arange(KEY_SLOTS)[None, :]


def _prompt_valid():
    j = np.arange(KEY_SLOTS)[None, :]
    dist = _prompt_dist()
    cur = (j >= WINDOW) & (dist >= 0)
    prev = (j < WINDOW) & (dist < WINDOW)
    return np.stack([cur, cur | prev]).astype(np.int32)


def _swa_prompt(u3, sinks, qg, kg, bias):
    bsz, length, _ = u3.shape
    blocks = SWA_PROMPT_BLOCKS
    rows = blocks * WINDOW
    assert length % rows == 0
    qb, kb, vb = Q_B // B_WIDTH, K_B // B_KV_WIDTH, V_B // B_KV_WIDTH
    assert Q_B % B_WIDTH == 0 and K_B % B_KV_WIDTH == 0 and V_B % B_KV_WIDTH == 0

    def kv_spec(col, prev):
        if prev:
            return pl.BlockSpec((1, WINDOW, B_KV_WIDTH), lambda b, m: (b, jnp.maximum(m * blocks - 1, 0), col))
        return pl.BlockSpec((1, rows, B_KV_WIDTH), lambda b, m: (b, m, col))

    g_spec = pl.BlockSpec((1, LANES), lambda b, m: (0, 0))
    nbytes = (2 * rows * (B_WIDTH + 4 * B_KV_WIDTH) * 4 + 2 * bias.size * 4
              + 2 * rows * B_WIDTH * 2 + blocks * 24 * WINDOW * 2 * KEY_SLOTS * 4)
    return pl.pallas_call(
        _swa_prompt_kernel,
        grid=(bsz, length // rows),
        in_specs=[pl.BlockSpec(memory_space=pltpu.SMEM),
                  pl.BlockSpec((1, rows, B_WIDTH), lambda b, m: (b, m, qb)),
                  kv_spec(kb, False), kv_spec(kb, True), kv_spec(vb, False), kv_spec(vb, True),
                  g_spec, g_spec,
                  pl.BlockSpec(bias.shape, lambda b, m: (0, 0, 0, 0))],
        out_specs=(pl.BlockSpec((1, rows, B_WIDTH), lambda b, m: (b, m, 0)),
                   pl.BlockSpec((1, WINDOW, B_KV_WIDTH), lambda b, n: (b, 0, 0)),
                   pl.BlockSpec((1, WINDOW, B_KV_WIDTH), lambda b, n: (b, 0, 0))),
        out_shape=(jax.ShapeDtypeStruct((bsz, length, B_WIDTH), BF16),
                   jax.ShapeDtypeStruct((bsz, WINDOW, B_KV_WIDTH), F32),
                   jax.ShapeDtypeStruct((bsz, WINDOW, B_KV_WIDTH), F32)),
        compiler_params=_params(("parallel", "arbitrary"), nbytes),
        name="swa_prompt",
    )(sinks, u3, u3, u3, u3, u3, qg, kg, bias)


def _swa_sample_kernel(sink_ref, q_ref, kv_ref, ck_ref, cv_ref, qg_ref, kg_ref, bias_ref,
                       ob_ref, ko_ref, vo_ref, *, steps):
    bd = _block_diag_ones()
    rows = q_ref.shape[1]
    kg = kg_ref[...]
    qg = qg_ref[...]
    pad = jnp.zeros((KEY_SLOTS - WINDOW - rows, LANES), F32)
    problems = []
    for i in range(q_ref.shape[0]):
        kv = kv_ref[i]
        k_new = [_head_norm(kv[:, t * LANES:(t + 1) * LANES], kg, bd) for t in range(2)]
        v_new = [kv[:, B_KV_WIDTH + t * LANES:B_KV_WIDTH + (t + 1) * LANES] for t in range(2)]
        ck, cv = ck_ref[i], cv_ref[i]
        k_tiles = [jnp.concatenate([ck[:, t * LANES:(t + 1) * LANES], k_new[t], pad], axis=0)
                   for t in range(2)]
        v_tiles = [jnp.concatenate([cv[:, t * LANES:(t + 1) * LANES], v_new[t], pad], axis=0)
                   for t in range(2)]
        q = q_ref[i]
        qn_tiles = [_head_norm(q[:, t * LANES:(t + 1) * LANES], qg, bd) for t in range(B_WIDTH // LANES)]

        def store(tile, val, i=i):
            ob_ref[i, :, tile * LANES:(tile + 1) * LANES] = val

        problems.append((qn_tiles, k_tiles, v_tiles, store))
        ko_ref[i, 0:WINDOW - steps, :] = ck_ref[i, steps:WINDOW, :]
        vo_ref[i, 0:WINDOW - steps, :] = cv_ref[i, steps:WINDOW, :]
        ko_ref[i, WINDOW - steps:WINDOW, :] = jnp.concatenate([kt[0:steps] for kt in k_new], axis=1)
        vo_ref[i, WINDOW - steps:WINDOW, :] = kv[0:steps, B_KV_WIDTH:]
    _swa_core(problems, lambda p, j: bias_ref[0, j], sink_ref)


def _sample_dist(rows):
    return WINDOW + np.arange(rows)[:, None] - np.arange(KEY_SLOTS)[None, :]


def _sample_valid(rows, steps):
    dist = _sample_dist(rows)
    j = np.arange(KEY_SLOTS)[None, :]
    return ((dist >= 0) & (dist < WINDOW) & (j < WINDOW + steps)).astype(np.int32)[None]


def _swa_sample(u3, cache_k, cache_v, layer, sinks, qg, kg, bias, steps):
    bsz, rows, _ = u3.shape
    w = cache_k.shape[2]
    bt = SWA_SAMPLE_SEQS
    assert w == WINDOW and bsz % bt == 0
    qb, kvb = Q_B // B_WIDTH, K_B // (2 * B_KV_WIDTH)
    assert K_B % (2 * B_KV_WIDTH) == 0
    c_in_spec = pl.BlockSpec((None, bt, w, B_KV_WIDTH), lambda b: (layer, b, 0, 0))
    c_spec = pl.BlockSpec((bt, w, B_KV_WIDTH), lambda b: (b, 0, 0))
    g_spec = pl.BlockSpec((1, LANES), lambda b: (0, 0))
    nbytes = bt * 8 * w * B_KV_WIDTH * 4 + 2 * bias.size * 4 + bt * 16 * KEY_SLOTS * LANES * 4
    return pl.pallas_call(
        functools.partial(_swa_sample_kernel, steps=steps),
        grid=(bsz // bt,),
        in_specs=[pl.BlockSpec(memory_space=pltpu.SMEM),
                  pl.BlockSpec((bt, rows, B_WIDTH), lambda b: (b, 0, qb)),
                  pl.BlockSpec((bt, rows, 2 * B_KV_WIDTH), lambda b: (b, 0, kvb)),
                  c_in_spec, c_in_spec, g_spec, g_spec,
                  pl.BlockSpec(bias.shape, lambda b: (0, 0, 0, 0))],
        out_specs=(pl.BlockSpec((bt, rows, B_WIDTH), lambda b: (b, 0, 0)), c_spec, c_spec),
        out_shape=(jax.ShapeDtypeStruct((bsz, rows, B_WIDTH), BF16),
                   jax.ShapeDtypeStruct((bsz, w, B_KV_WIDTH), F32),
                   jax.ShapeDtypeStruct((bsz, w, B_KV_WIDTH), F32)),
        compiler_params=_params(("parallel",), nbytes),
        name="swa_sample",
    )(sinks, u3, u3, cache_k, cache_v, qg, kg, bias)


def kernel(x_prompt, x_sample, cache_k, cache_v, state_hgrn, norm_mix, w_in, lower_bounds, hgrn_norm,
           q_norm, k_norm, attn_sinks, rel_bias_table, w_out, norm_ffn, w_gate, w_up, w_down):
    depth = w_in.shape[0]
    bp, seq, d = x_prompt.shape
    bd, ld, _ = x_sample.shape
    w = cache_k.shape[2]
    in_width = w_in.shape[2]

    lbf, oml = _lower_bounds(lower_bounds)
    sinks = attn_sinks.astype(F32)
    qg2 = jnp.tile(q_norm.astype(F32), (1, LANES // B_HD)) * (ATTN_SCALE * LOG2E)
    kg2 = jnp.tile(k_norm.astype(F32), (1, LANES // B_HD))
    ck = cache_k.astype(F32).reshape(depth, bd, w, B_KV_WIDTH)
    cv = cache_v.astype(F32).reshape(depth, bd, w, B_KV_WIDTH)

    xp = x_prompt.reshape(bp * seq, d)
    xs = jnp.pad(x_sample, ((0, 0), (0, SAMPLE_ROWS - ld), (0, 0))).reshape(bd * SAMPLE_ROWS, d)
    tm_s = bd * SAMPLE_ROWS
    bias_p = _bias_layout(rel_bias_table, _prompt_dist(), _prompt_valid())
    bias_s = _bias_layout(rel_bias_table, _sample_dist(SAMPLE_ROWS), _sample_valid(SAMPLE_ROWS, ld))
    state = state_hgrn.astype(F32)

    pk, pv, ps, sk, sv, ss = [], [], [], [], [], []
    for l in range(depth):
        lbf_l, oml_l = lbf[l:l + 1], oml[l:l + 1]
        u, w_in_b = _norm_matmul(xs, norm_mix[l], w_in, l, tm_s, CAST_COLS)
        u = u.reshape(bd, SAMPLE_ROWS, in_width)
        oa, st = _hgrn_sample(u, lbf_l, oml_l, hgrn_norm[l], state, l, ld)
        ob, kn, vn = _swa_sample(u, ck, cv, l, sinks[l], qg2[l:l + 1], kg2[l:l + 1], bias_s, ld)
        xs, w_out_b = _outproj(xs, oa.reshape(tm_s, A_WIDTH), ob.reshape(tm_s, B_WIDTH), w_out, l,
                               tm_s, CAST_COLS)
        xs, w_gate_b, w_up_b, w_down_b = _ffn(xs, norm_ffn[l], w_gate, w_up, w_down, l, tm_s, CAST_COLS)
        sk.append(kn)
        sv.append(vn)
        ss.append(st)
        u = _norm_matmul(xp, norm_mix[l], w_in_b, None, *W_IN_TILE).reshape(bp, seq, in_width)
        oa, st = _hgrn_prompt(u, lbf_l, oml_l, hgrn_norm[l])
        ob, kn, vn = _swa_prompt(u, sinks[l], qg2[l:l + 1], kg2[l:l + 1], bias_p)
        xp = _outproj(xp, oa.reshape(bp * seq, A_WIDTH), ob.reshape(bp * seq, B_WIDTH), w_out_b, None,
                      W_OUT_ROWS, d)
        xp = _ffn(xp, norm_ffn[l], w_gate_b, w_up_b, w_down_b, None, *FFN_TILE)
        pk.append(kn)
        pv.append(vn)
        ps.append(st)

    kv_shape = lambda n: (depth, n, WINDOW, B_KV_HEADS, B_HD)
    return (xp.reshape(bp, seq, d),
            xs.reshape(bd, SAMPLE_ROWS, d)[:, :ld],
            jnp.stack(pk).reshape(kv_shape(bp)), jnp.stack(pv).reshape(kv_shape(bp)), jnp.stack(ps),
            jnp.stack(sk).reshape(kv_shape(bd)), jnp.stack(sv).reshape(kv_shape(bd)), jnp.stack(ss))
```

```python
import functools
import math

import jax
import jax.numpy as jnp
import numpy as np
from jax import lax
from jax.experimental import pallas as pl
from jax.experimental.pallas import tpu as pltpu

F32 = jnp.float32
BF16 = jnp.bfloat16

LANES = 128
SUBLANES = 8
VMEM_BYTES_V7X = 64 * 1024 * 1024
VMEM_CAP = VMEM_BYTES_V7X - 8 * 1024 * 1024

A_HEADS = 8
A_DK = 128
A_DV = 128
A_KEY_WIDTH = A_HEADS * A_DK
A_WIDTH = A_HEADS * A_DV
B_HEADS = 16
B_KV_HEADS = 4
B_HD = 64
B_WIDTH = B_HEADS * B_HD
B_KV_WIDTH = B_KV_HEADS * B_HD
WINDOW = 128
ATTN_SCALE = 1.0 / math.sqrt(B_HD)
LOG2E = math.log2(math.e)
NEG_LOGIT = -1e30
LB_FLOOR = 1e-20
N_BUCKETS = 32
MAX_DISTANCE = 128
RMS_EPS = 1e-6

Q_A, F_A, I_A, G_A = 0, A_KEY_WIDTH, 2 * A_KEY_WIDTH, 2 * A_KEY_WIDTH + A_WIDTH
Q_B = 2 * A_KEY_WIDTH + 2 * A_WIDTH
K_B = Q_B + B_WIDTH
V_B = K_B + B_KV_WIDTH
IN_WIDTH = V_B + B_KV_WIDTH

W_IN_TILE = (512, 2816)
W_OUT_ROWS = 512
FFN_TILE = (1024, 512)
CAST_COLS = 512

HGRN_CHUNK = 128
HGRN_BASE = 4
HGRN_HEADS_PER_STEP = 8
HGRN_STEPS_PER_CAST = 2
SAMPLE_ROWS = SUBLANES
SWA_PROMPT_BLOCKS = 2
SWA_SAMPLE_SEQS = 4
HGRN_SAMPLE_SEQS = 2
KEY_SLOTS = 2 * WINDOW


def _params(sem, nbytes):
    assert nbytes <= VMEM_CAP, nbytes
    return pltpu.CompilerParams(dimension_semantics=sem, vmem_limit_bytes=VMEM_CAP)


def _sig_pair(z):
    th = 0.5 * jnp.tanh(0.5 * z)
    return 0.5 + th, 0.5 - th


def _silu(x):
    return x * _sig_pair(x)[0]


def _rms_rows(x, g):
    ms = jnp.mean(x * x, axis=-1, keepdims=True)
    return x * lax.rsqrt(ms + RMS_EPS) * g


def _split_bf16(x):
    hi = x.astype(BF16)
    lo = (x - hi.astype(F32)).astype(BF16)
    return hi, lo


def _dot(a, b):
    return jnp.dot(a, b, preferred_element_type=F32)


def _dot_nt(a, b):
    return lax.dot_general(a, b, (((1,), (1,)), ((), ())), preferred_element_type=F32)


def _lb_kernel(lb_ref, lbf_ref, oml_ref):
    depth = lb_ref.shape[0]
    rows = [lb_ref[i:i + 1, :] for i in range(depth)]
    m = functools.reduce(jnp.maximum, rows)
    e = [jnp.exp(r - m) for r in rows]
    s = functools.reduce(lambda a, b: a + b, e)
    p = [ei / s for ei in e]
    cum = p[0]
    for i in range(depth):
        if i > 0:
            cum = cum + p[i]
        lb = cum - p[0]
        lbf_ref[i:i + 1, :] = jnp.maximum(lb, LB_FLOOR)
        oml_ref[i:i + 1, :] = 1.0 - lb


def _lower_bounds(lower_bounds):
    shp = jax.ShapeDtypeStruct(lower_bounds.shape, F32)
    return pl.pallas_call(_lb_kernel, out_shape=(shp, shp), name="lb_prep")(lower_bounds.astype(F32))


def _weight_spec(w, layer, block, index):
    if w.ndim == 2:
        return pl.BlockSpec(block, index)
    return pl.BlockSpec((None,) + block, lambda *g: (layer,) + index(*g))


def _bf16_weight(w_ref, copy_refs):
    w = w_ref[...]
    if copy_refs:
        w = w.astype(BF16)
        copy_refs[0][...] = w
    return w


def _norm_matmul_kernel(x_ref, g_ref, w_ref, o_ref, *copy_refs):
    h = _rms_rows(x_ref[...], g_ref[...]).astype(BF16)
    o_ref[...] = _dot(h, _bf16_weight(w_ref, copy_refs))


def _norm_matmul(x, g, w, layer, tm, tn):
    m, d = x.shape
    n = w.shape[-1]
    cast = w.ndim == 3
    assert not cast or m == tm
    wbytes = 4 + 2 if cast else 2
    nbytes = 2 * tm * d * 4 + tm * d * 2 + 2 * d * tn * wbytes + 3 * tm * tn * 4
    u_spec = pl.BlockSpec((tm, tn), lambda j, i: (i, j))
    u_shape = jax.ShapeDtypeStruct((m, n), F32)
    w_block, w_index = (d, tn), lambda j, i: (0, j)
    return pl.pallas_call(
        _norm_matmul_kernel,
        grid=(n // tn, m // tm),
        in_specs=[pl.BlockSpec((tm, d), lambda j, i: (i, 0)),
                  pl.BlockSpec((1, d), lambda j, i: (0, 0)),
                  _weight_spec(w, layer, w_block, w_index)],
        out_specs=(u_spec, pl.BlockSpec(w_block, w_index)) if cast else u_spec,
        out_shape=(u_shape, jax.ShapeDtypeStruct((d, n), BF16)) if cast else u_shape,
        compiler_params=_params(("parallel", "parallel"), nbytes),
        name="norm_matmul",
    )(x, g.reshape(1, d), w)


def _outproj_kernel(x_ref, oa_ref, ob_ref, w_ref, o_ref, *rest):
    ka = oa_ref.shape[1]
    w = _bf16_weight(w_ref, rest)
    o_ref[...] = x_ref[...] + _dot(oa_ref[...], w[0:ka]) + _dot(ob_ref[...], w[ka:])


def _outproj(x, oa, ob, w, layer, tm, tn):
    m, d = x.shape
    ka, kb = oa.shape[1], ob.shape[1]
    cast = w.ndim == 3
    assert not cast or m == tm
    wbytes = 4 + 2 if cast else 2
    nbytes = 4 * tm * tn * 4 + 2 * tm * (ka + kb) * 2 + 2 * (ka + kb) * tn * wbytes
    x_spec = pl.BlockSpec((tm, tn), lambda i, j: (i, j))
    x_shape = jax.ShapeDtypeStruct((m, d), F32)
    w_block, w_index = (ka + kb, tn), lambda i, j: (0, j)
    return pl.pallas_call(
        _outproj_kernel,
        grid=(m // tm, d // tn),
        in_specs=[x_spec,
                  pl.BlockSpec((tm, ka), lambda i, j: (i, 0)),
                  pl.BlockSpec((tm, kb), lambda i, j: (i, 0)),
                  _weight_spec(w, layer, w_block, w_index)],
        out_specs=(x_spec, pl.BlockSpec(w_block, w_index)) if cast else x_spec,
        out_shape=(x_shape, jax.ShapeDtypeStruct((ka + kb, d), BF16)) if cast else x_shape,
        compiler_params=_params(("parallel", "arbitrary"), nbytes),
        name="outproj",
    )(x, oa, ob, w)


def _ffn_kernel(x_ref, g_ref, wg_ref, wu_ref, wd_ref, o_ref, *rest):
    h_ref = rest[-1]
    copies = rest[:-1]

    @pl.when(pl.program_id(1) == 0)
    def _():
        x = x_ref[...]
        h_ref[...] = _rms_rows(x, g_ref[...]).astype(BF16)
        o_ref[...] = x

    h = h_ref[...]
    gate = _dot(h, _bf16_weight(wg_ref, copies[0:1]))
    up = _dot(h, _bf16_weight(wu_ref, copies[1:2]))
    act = (_silu(gate) * up).astype(BF16)
    o_ref[...] += _dot(act, _bf16_weight(wd_ref, copies[2:3]))


def _ffn(x, g, wg, wu, wd, layer, tm, tf):
    m, d = x.shape
    f = wg.shape[-1]
    cast = wg.ndim == 3
    assert not cast or m == tm
    wbytes = 4 + 2 if cast else 2
    nbytes = 4 * tm * d * 4 + tm * d * 2 + 2 * 3 * d * tf * wbytes + 3 * tm * tf * 4
    x_spec = pl.BlockSpec((tm, d), lambda i, j: (i, 0))
    x_shape = jax.ShapeDtypeStruct((m, d), F32)
    col_block, col_index = (d, tf), lambda i, j: (0, j)
    row_block, row_index = (tf, d), lambda i, j: (j, 0)
    copy_specs = (pl.BlockSpec(col_block, col_index), pl.BlockSpec(col_block, col_index),
                  pl.BlockSpec(row_block, row_index))
    copy_shapes = (jax.ShapeDtypeStruct((d, f), BF16), jax.ShapeDtypeStruct((d, f), BF16),
                   jax.ShapeDtypeStruct((f, d), BF16))
    return pl.pallas_call(
        _ffn_kernel,
        grid=(m // tm, f // tf),
        in_specs=[x_spec,
                  pl.BlockSpec((1, d), lambda i, j: (0, 0)),
                  _weight_spec(wg, layer, col_block, col_index),
                  _weight_spec(wu, layer, col_block, col_index),
                  _weight_spec(wd, layer, row_block, row_index)],
        out_specs=(x_spec,) + copy_specs if cast else x_spec,
        out_shape=(x_shape,) + copy_shapes if cast else x_shape,
        scratch_shapes=[pltpu.VMEM((tm, d), BF16)],
        compiler_params=_params(("parallel", "arbitrary"), nbytes),
        name="ffn",
    )(x, g.reshape(1, d), wg, wu, wd)


def _hgrn_static(c):
    assert (HGRN_BASE - 1) * -math.log(LB_FLOOR) / 2 < 80.0
    t = np.arange(c)
    tri = (t[None, :] <= t[:, None]).astype(np.float32)
    first = HGRN_BASE * (t // HGRN_BASE)
    base = tri - 0.5 * (tri[first] + tri[first + HGRN_BASE - 1])
    masks = []
    h = c // 2
    while h >= HGRN_BASE:
        grp = t // (2 * h)
        upper = (t % (2 * h)) >= h
        masks.append((grp[:, None] == grp[None, :]) & upper[:, None] & ~upper[None, :])
        h //= 2
    masks.append((first[:, None] == first[None, :]) & (t[None, :] <= t[:, None]))
    p_all = np.concatenate([tri, base], axis=0)
    p2 = np.concatenate([p_all, p_all], axis=1)
    masks = np.stack(masks).astype(np.float32)
    return p2, np.concatenate([masks, masks], axis=2)


def _hgrn_prompt_kernel(q_ref, f_ref, i_ref, g_ref, lbf_ref, oml_ref, gn_ref, p2_ref, m_ref, *rest,
                        chunk, heads, n_cast):
    w_refs, (oa_ref, st_ref), wb_refs, s_scr = (rest[:n_cast], rest[n_cast:n_cast + 2],
                                                rest[n_cast + 2:2 * n_cast + 2], rest[-1])
    c_idx = pl.program_id(2)

    @pl.when(c_idx == 0)
    def _():
        s_scr[...] = jnp.zeros_like(s_scr)

    if n_cast:
        @pl.when(c_idx % HGRN_STEPS_PER_CAST == 0)
        def _():
            for w_ref, wb_ref in zip(w_refs, wb_refs):
                wb_ref[...] = w_ref[...].astype(BF16)

    width = heads * A_DK
    pw = 2 * A_DK
    sz, snz = _sig_pair(f_ref[0])
    oml = oml_ref[...]
    lf = jnp.log(lbf_ref[...] + oml * sz)
    ka = oml * snz
    hi, lo = _split_bf16(lf)
    expo = _dot(p2_ref[...], jnp.concatenate([hi, lo], axis=0)) * LOG2E
    b = expo[0:chunk]
    q = _silu(q_ref[0])
    v = i_ref[0]
    gate = _silu(g_ref[0])
    gn = gn_ref[...]

    row = lax.broadcasted_iota(jnp.int32, (chunk, pw), 0)
    left = lax.broadcasted_iota(jnp.int32, (chunk, pw), 1) < A_DK

    def bdiag(x):
        z = jnp.zeros_like(x)
        return jnp.concatenate([jnp.where(left, x, z), jnp.where(left, z, x)], axis=0)

    for p in range(heads // 2):
        sl = slice(p * pw, (p + 1) * pw)
        qp, kp, vp, bp = q[:, sl], ka[:, sl], v[:, sl], b[:, sl]
        qs, ks = [], []
        h = chunk // 2
        while h >= HGRN_BASE:
            groups = chunk // (2 * h)
            mid = jnp.broadcast_to(bp.reshape(groups, 2 * h, pw)[:, h - 1:h, :], (groups, 2 * h, pw))
            d = bp - mid.reshape(chunk, pw)
            e = jnp.exp2(jnp.where((row & h) != 0, d, -d))
            qs.append((qp * e).astype(BF16))
            ks.append((kp * e).astype(BF16))
            h //= 2
        e_base = expo[chunk:2 * chunk, sl]
        qs.append((qp * jnp.exp2(e_base)).astype(BF16))
        ks.append((kp * jnp.exp2(-e_base)).astype(BF16))
        b_last = bp[chunk - 1:chunk, :]
        st = s_scr[p]
        o = _dot_nt((qp * jnp.exp2(bp)).astype(BF16), bdiag(st.astype(BF16)))
        a = jnp.zeros((chunk, pw), BF16)
        for li in range(len(qs)):
            a = a + _dot_nt(qs[li], bdiag(ks[li])).astype(BF16) * m_ref[li]
        o = o + _dot(a, bdiag(vp.astype(BF16)))
        vt = jnp.concatenate([vp[:, :A_DV].T, vp[:, A_DV:].T], axis=1).astype(BF16)
        k_tail = (kp * jnp.exp2(b_last - bp)).astype(BF16)
        s_scr[p] = st * jnp.exp2(b_last) + _dot(vt, bdiag(k_tail))
        for hh in range(2):
            hs = slice(p * pw + hh * A_DV, p * pw + (hh + 1) * A_DV)
            oa_ref[0, :, hs] = (_rms_rows(o[:, hh * A_DV:(hh + 1) * A_DV], gn) * gate[:, hs]).astype(BF16)

    @pl.when(c_idx == pl.num_programs(2) - 1)
    def _():
        for hh in range(heads):
            st_ref[0, hh] = s_scr[hh // 2][:, (hh % 2) * A_DK:(hh % 2 + 1) * A_DK].T


def _hgrn_prompt(u3, lbf, oml, gn, cast_weights=(), cast_layer=None):
    bsz, length, _ = u3.shape
    chunk, heads = HGRN_CHUNK, HGRN_HEADS_PER_STEP
    cols = heads * A_DK
    assert length % chunk == 0 and A_HEADS % heads == 0 and heads % 2 == 0 and chunk == A_DV
    p2, masks = _hgrn_static(chunk)
    p2 = jnp.asarray(p2, BF16)
    masks = jnp.asarray(masks, BF16)
    n_chunks = length // chunk
    assert not cast_weights or (A_HEADS == heads and n_chunks % HGRN_STEPS_PER_CAST == 0)
    n_blocks = bsz * n_chunks // HGRN_STEPS_PER_CAST

    def col_spec(offset):
        base = offset // cols
        return pl.BlockSpec((1, chunk, cols), lambda b, h, c: (b, c, base + h))

    def cast_block(b, h, c):
        return (b * n_chunks + c) // HGRN_STEPS_PER_CAST

    cast_in, cast_out, cast_shape, cast_bytes = [], [], [], 0
    for w in cast_weights:
        rows, width = w.shape[1] // n_blocks, w.shape[2]
        assert w.shape[1] % n_blocks == 0 and rows % (2 * SUBLANES) == 0
        cast_in.append(pl.BlockSpec((None, rows, width), lambda b, h, c: (cast_layer, cast_block(b, h, c), 0)))
        cast_out.append(pl.BlockSpec((rows, width), lambda b, h, c: (cast_block(b, h, c), 0)))
        cast_shape.append(jax.ShapeDtypeStruct(w.shape[1:], BF16))
        cast_bytes += 2 * rows * width * (4 + 2)

    nbytes = (2 * 5 * chunk * cols * 4 + 2 * (p2.size * 2 + masks.size * 4)
              + 6 * p2.shape[0] * cols * 4 + cast_bytes)
    outs = pl.pallas_call(
        functools.partial(_hgrn_prompt_kernel, chunk=chunk, heads=heads, n_cast=len(cast_weights)),
        grid=(bsz, A_HEADS // heads, n_chunks),
        in_specs=[col_spec(Q_A), col_spec(F_A), col_spec(I_A), col_spec(G_A),
                  pl.BlockSpec((1, cols), lambda b, h, c: (0, h)),
                  pl.BlockSpec((1, cols), lambda b, h, c: (0, h)),
                  pl.BlockSpec((1, A_DV), lambda b, h, c: (0, 0)),
                  pl.BlockSpec(p2.shape, lambda b, h, c: (0, 0)),
                  pl.BlockSpec(masks.shape, lambda b, h, c: (0, 0, 0))] + cast_in,
        out_specs=[pl.BlockSpec((1, chunk, cols), lambda b, h, c: (b, c, h)),
                   pl.BlockSpec((1, heads, A_DK, A_DV), lambda b, h, c: (b, h, 0, 0))] + cast_out,
        out_shape=[jax.ShapeDtypeStruct((bsz, length, A_WIDTH), BF16),
                   jax.ShapeDtypeStruct((bsz, A_HEADS, A_DK, A_DV), F32)] + cast_shape,
        scratch_shapes=[pltpu.VMEM((heads // 2, A_DV, 2 * A_DK), F32)],
        compiler_params=_params(("parallel", "parallel", "arbitrary"), nbytes),
        name="hgrn_prompt",
    )(u3, u3, u3, u3, lbf, oml, gn.reshape(1, A_DV), p2, masks, *cast_weights)
    return outs[0], outs[1], outs[2:]


def _hgrn_sample_kernel(u_ref, lbf_ref, oml_ref, gn_ref, s_ref, oa_ref, so_ref, *, steps):
    for i in range(u_ref.shape[0]):
        _hgrn_sample_one(i, u_ref, lbf_ref, oml_ref, gn_ref, s_ref, oa_ref, so_ref, steps)


def _hgrn_sample_one(i, u_ref, lbf_ref, oml_ref, gn_ref, s_ref, oa_ref, so_ref, steps):
    u = u_ref[i]
    q = _silu(u[:, Q_A:Q_A + A_KEY_WIDTH])
    sz, snz = _sig_pair(u[:, F_A:F_A + A_KEY_WIDTH])
    oml = oml_ref[...]
    f = lbf_ref[...] + oml * sz
    ka = oml * snz
    v = u[:, I_A:I_A + A_WIDTH]
    gate = _silu(u[:, G_A:G_A + A_WIDTH])
    gn = gn_ref[...]
    rows = u.shape[0]
    row = lax.broadcasted_iota(jnp.int32, f.shape, 0)
    live = row < steps
    f = jnp.where(live, f, 1.0)
    ka = jnp.where(live, ka, 0.0)

    def down(x, d, fill):
        return x if d == 0 else jnp.where(row >= d, pltpu.roll(x, d, axis=0), fill)

    def up(x, d, fill):
        return jnp.where(row < rows - d, pltpu.roll(x, rows - d, axis=0), fill)

    decay = [jnp.ones_like(f)]
    for d in range(1, steps + 1):
        decay.append(decay[-1] * down(f, d - 1, 1.0))
    head = decay[steps]
    tail = jnp.ones_like(f)
    for d in range(1, steps):
        tail = tail * up(f, d, 1.0)
    q_in = (q * head).astype(BF16)
    k_tail = ka * tail
    total = head[steps - 1:steps, :]
    t_hi = total.astype(BF16).astype(F32)
    t_mid = (total - t_hi).astype(BF16).astype(F32)
    t_lo = (total - t_hi - t_mid).astype(BF16).astype(F32)
    t_rows = jnp.where(row == 0, t_hi, jnp.where(row == 1, t_mid, jnp.where(row == 2, t_lo, 0.0)))
    prods = [q * down(ka, d, 0.0) * decay[d] for d in range(steps)]
    v_sh = [down(v, d, 0.0) for d in range(steps)]

    zsq = jnp.zeros((A_DK, A_DV), BF16)
    zrows = jnp.zeros((rows, A_DV), F32)
    r2 = lax.broadcasted_iota(jnp.int32, (rows, 2 * A_DV), 0)
    l2 = lax.broadcasted_iota(jnp.int32, (rows, 2 * A_DV), 1)
    ones_right = jnp.where((r2 < 3) & (l2 >= A_DV), 1.0, 0.0)

    for p in range(A_HEADS // 2):
        ps = slice(2 * p * A_DK, (2 * p + 2) * A_DK)
        s0 = [s_ref[i, 2 * p + hh] for hh in range(2)]
        s_bd = jnp.concatenate([jnp.concatenate([s0[0].astype(BF16), zsq], axis=1),
                                jnp.concatenate([zsq, s0[1].astype(BF16)], axis=1)], axis=0)
        o_pair = _dot(q_in[:, ps], s_bd)
        for hh in range(2):
            head_i = 2 * p + hh
            sl = slice(head_i * A_DK, (head_i + 1) * A_DK)
            o = o_pair[:, hh * A_DV:(hh + 1) * A_DV]
            for d in range(steps):
                o = o + jnp.sum(prods[d][:, sl], axis=-1, keepdims=True) * v_sh[d][:, sl]
            lhs = jnp.concatenate([k_tail[:, sl], t_rows[:, sl]], axis=0).astype(BF16)
            rhs = jnp.concatenate([jnp.concatenate([v[:, sl], zrows], axis=1), ones_right],
                                  axis=0).astype(BF16)
            upd = lax.dot_general(lhs, rhs, (((0,), (0,)), ((), ())), preferred_element_type=F32)
            so_ref[i, head_i] = s0[hh] * upd[:, A_DV:] + upd[:, :A_DV]
            oa_ref[i, :, sl] = (_rms_rows(o, gn) * gate[:, sl]).astype(BF16)


def _hgrn_sample(u3, lbf, oml, gn, state, layer, steps):
    bsz, rows, _ = u3.shape
    width = G_A + A_WIDTH
    bt = HGRN_SAMPLE_SEQS
    assert bsz % bt == 0
    nbytes = bt * (2 * rows * width * 4 + 4 * A_HEADS * A_DK * A_DV * 4)
    return pl.pallas_call(
        functools.partial(_hgrn_sample_kernel, steps=steps),
        grid=(bsz // bt,),
        in_specs=[pl.BlockSpec((bt, rows, width), lambda b: (b, 0, 0)),
                  pl.BlockSpec((1, A_KEY_WIDTH), lambda b: (0, 0)),
                  pl.BlockSpec((1, A_KEY_WIDTH), lambda b: (0, 0)),
                  pl.BlockSpec((1, A_DV), lambda b: (0, 0)),
                  pl.BlockSpec((None, bt, A_HEADS, A_DK, A_DV), lambda b: (layer, b, 0, 0, 0))],
        out_specs=(pl.BlockSpec((bt, rows, A_WIDTH), lambda b: (b, 0, 0)),
                   pl.BlockSpec((bt, A_HEADS, A_DK, A_DV), lambda b: (b, 0, 0, 0))),
        out_shape=(jax.ShapeDtypeStruct((bsz, rows, A_WIDTH), BF16),
                   jax.ShapeDtypeStruct(state.shape[1:], F32)),
        compiler_params=_params(("parallel",), nbytes),
        name="hgrn_sample",
    )(u3, lbf, oml, gn.reshape(1, A_DV), state)


def _t5_bucket(dist):
    n = np.maximum(dist, 0)
    max_exact = N_BUCKETS // 2
    nf = np.maximum(n, 1).astype(np.float32)
    large = max_exact + (np.log(nf / max_exact) / math.log(MAX_DISTANCE / max_exact)
                         * (N_BUCKETS - max_exact)).astype(np.int32)
    large = np.minimum(large, N_BUCKETS - 1)
    return np.where(n < max_exact, n, large).astype(np.int32)


def _bias_kernel(tab_ref, bucket_ref, valid_ref, o_ref):
    rows = bucket_ref.shape[0]
    bucket = bucket_ref[...]
    hits = [bucket == b for b in range(N_BUCKETS)]
    for h in range(B_HEADS):
        acc = jnp.zeros(bucket.shape, F32)
        for b in range(N_BUCKETS):
            acc = jnp.where(hits[b], tab_ref[b, h], acc)
        acc = acc * LOG2E
        j, r, c = h // 4, (h % 4) // 2, h % 2
        for m in range(valid_ref.shape[0]):
            o_ref[m, j, r * rows:(r + 1) * rows, c * KEY_SLOTS:(c + 1) * KEY_SLOTS] = jnp.where(
                valid_ref[m] != 0, acc, NEG_LOGIT * LOG2E)


def _bias_layout(table, dist, valids):
    rows = dist.shape[0]
    return pl.pallas_call(
        _bias_kernel,
        in_specs=[pl.BlockSpec(memory_space=pltpu.SMEM),
                  pl.BlockSpec(memory_space=pltpu.VMEM),
                  pl.BlockSpec(memory_space=pltpu.VMEM)],
        out_shape=jax.ShapeDtypeStruct((valids.shape[0], B_KV_HEADS, 2 * rows, 2 * KEY_SLOTS), F32),
        name="bias_prep",
    )(table.astype(F32), jnp.asarray(_t5_bucket(dist), jnp.int32), jnp.asarray(valids, jnp.int32))


def _block_diag_ones():
    r = lax.broadcasted_iota(jnp.int32, (LANES, LANES), 0) // B_HD
    c = lax.broadcasted_iota(jnp.int32, (LANES, LANES), 1) // B_HD
    return (r == c).astype(BF16)


def _head_norm(x, g2, bd):
    hi, lo = _split_bf16(x * x)
    ss = _dot(hi, bd) + _dot(lo, bd)
    return x * lax.rsqrt(ss * (1.0 / B_HD) + RMS_EPS) * g2


def _place(tile, half, lo_mask):
    rolled = pltpu.roll(tile, B_HD, axis=1)
    zero = jnp.zeros_like(tile)
    if half == 0:
        return jnp.where(lo_mask, tile, zero), jnp.where(lo_mask, zero, rolled)
    return jnp.where(lo_mask, rolled, zero), jnp.where(lo_mask, zero, tile)


def _swa_core(problems, bias_at, sink_ref):
    rows = problems[0][0][0].shape[0]
    lo_mask = lax.broadcasted_iota(jnp.int32, problems[0][1][0].shape, 1) < B_HD
    out_lo = lax.broadcasted_iota(jnp.int32, (rows, LANES), 1) < B_HD
    units = [(p, j) for p in range(len(problems)) for j in range(B_KV_HEADS)]
    scores, values = {}, {}
    for p, j in units:
        qn_tiles, k_tiles, v_tiles, _ = problems[p]
        t, half = j // 2, j % 2
        k_lo, k_hi = _place(k_tiles[t], half, lo_mask)
        v_lo, v_hi = _place(v_tiles[t], half, lo_mask)
        kk = jnp.concatenate([k_lo, k_hi], axis=0).astype(BF16)
        values[p, j] = jnp.concatenate([v_lo, v_hi], axis=0).astype(BF16)
        qq = jnp.concatenate([qn_tiles[2 * j], qn_tiles[2 * j + 1]], axis=0).astype(BF16)
        scores[p, j] = _dot_nt(qq, kk) + bias_at(p, j)
    weights, scales = {}, {}
    for p, j in units:
        e_rows, inv = [], []
        for r in range(2):
            e_cols, inv_r = [], []
            for c in range(2):
                sk = sink_ref[4 * j + 2 * r + c] * LOG2E
                sb = scores[p, j][r * rows:(r + 1) * rows, c * KEY_SLOTS:(c + 1) * KEY_SLOTS]
                m = jnp.maximum(jnp.max(sb, axis=-1, keepdims=True), sk)
                e = jnp.exp2(sb - m)
                inv_r.append(1.0 / (jnp.sum(e, axis=-1, keepdims=True) + jnp.exp2(sk - m)))
                e_cols.append(e.astype(BF16))
            e_rows.append(jnp.concatenate(e_cols, axis=1))
            inv.append(jnp.where(out_lo, inv_r[0], inv_r[1]))
        weights[p, j] = jnp.concatenate(e_rows, axis=0)
        scales[p, j] = inv
    for p, j in units:
        o = _dot(weights[p, j], values[p, j])
        for r in range(2):
            problems[p][3](2 * j + r, (o[r * rows:(r + 1) * rows] * scales[p, j][r]).astype(BF16))


def _swa_prompt_kernel(sink_ref, q_ref, kc_ref, kp_ref, vc_ref, vp_ref, qg_ref, kg_ref, bias_ref,
                       ob_ref, ko_ref, vo_ref):
    m = pl.program_id(1)
    blocks = q_ref.shape[1] // WINDOW
    bd = _block_diag_ones()
    k_all = jnp.concatenate([kp_ref[0], kc_ref[0]], axis=0)
    v_all = jnp.concatenate([vp_ref[0], vc_ref[0]], axis=0)
    kg = kg_ref[...]
    k_norm = [_head_norm(k_all[:, t * LANES:(t + 1) * LANES], kg, bd) for t in range(2)]
    q = q_ref[0]
    qg = qg_ref[...]
    qn = [_head_norm(q[:, t * LANES:(t + 1) * LANES], qg, bd) for t in range(B_WIDTH // LANES)]
    first = jnp.minimum(m, 1)

    problems, biases = [], []
    for i in range(blocks):
        rows = slice(i * WINDOW, (i + 1) * WINDOW)
        keys = slice(i * WINDOW, i * WINDOW + KEY_SLOTS)

        def store(tile, val, rows=rows):
            ob_ref[0, rows, tile * LANES:(tile + 1) * LANES] = val

        problems.append(([qt[rows] for qt in qn], [kt[keys] for kt in k_norm],
                         [v_all[keys, t * LANES:(t + 1) * LANES] for t in range(2)], store))
        biases.append(first if i == 0 else 1)
    _swa_core(problems, lambda p, j: bias_ref[biases[p], j], sink_ref)

    @pl.when(m == pl.num_programs(1) - 1)
    def _():
        ko_ref[0] = jnp.concatenate([kt[blocks * WINDOW:] for kt in k_norm], axis=1)
        vo_ref[0] = vc_ref[0, (blocks - 1) * WINDOW:, :]


def _prompt_dist():
    return WINDOW + np.arange(WINDOW)[:, None] - np.arange(KEY_SLOTS)[None, :]


def _prompt_valid():
    j = np.arange(KEY_SLOTS)[None, :]
    dist = _prompt_dist()
    cur = (j >= WINDOW) & (dist >= 0)
    prev = (j < WINDOW) & (dist < WINDOW)
    return np.stack([cur, cur | prev]).astype(np.int32)


def _swa_prompt(u3, sinks, qg, kg, bias):
    bsz, length, _ = u3.shape
    blocks = SWA_PROMPT_BLOCKS
    rows = blocks * WINDOW
    assert length % rows == 0
    qb, kb, vb = Q_B // B_WIDTH, K_B // B_KV_WIDTH, V_B // B_KV_WIDTH
    assert Q_B % B_WIDTH == 0 and K_B % B_KV_WIDTH == 0 and V_B % B_KV_WIDTH == 0

    def kv_spec(col, prev):
        if prev:
            return pl.BlockSpec((1, WINDOW, B_KV_WIDTH), lambda b, m: (b, jnp.maximum(m * blocks - 1, 0), col))
        return pl.BlockSpec((1, rows, B_KV_WIDTH), lambda b, m: (b, m, col))

    g_spec = pl.BlockSpec((1, LANES), lambda b, m: (0, 0))
    nbytes = (2 * rows * (B_WIDTH + 4 * B_KV_WIDTH) * 4 + 2 * bias.size * 4
              + 2 * rows * B_WIDTH * 2 + blocks * 24 * WINDOW * 2 * KEY_SLOTS * 4)
    return pl.pallas_call(
        _swa_prompt_kernel,
        grid=(bsz, length // rows),
        in_specs=[pl.BlockSpec(memory_space=pltpu.SMEM),
                  pl.BlockSpec((1, rows, B_WIDTH), lambda b, m: (b, m, qb)),
                  kv_spec(kb, False), kv_spec(kb, True), kv_spec(vb, False), kv_spec(vb, True),
                  g_spec, g_spec,
                  pl.BlockSpec(bias.shape, lambda b, m: (0, 0, 0, 0))],
        out_specs=(pl.BlockSpec((1, rows, B_WIDTH), lambda b, m: (b, m, 0)),
                   pl.BlockSpec((1, WINDOW, B_KV_WIDTH), lambda b, n: (b, 0, 0)),
                   pl.BlockSpec((1, WINDOW, B_KV_WIDTH), lambda b, n: (b, 0, 0))),
        out_shape=(jax.ShapeDtypeStruct((bsz, length, B_WIDTH), BF16),
                   jax.ShapeDtypeStruct((bsz, WINDOW, B_KV_WIDTH), F32),
                   jax.ShapeDtypeStruct((bsz, WINDOW, B_KV_WIDTH), F32)),
        compiler_params=_params(("parallel", "arbitrary"), nbytes),
        name="swa_prompt",
    )(sinks, u3, u3, u3, u3, u3, qg, kg, bias)


def _swa_sample_kernel(sink_ref, q_ref, kv_ref, ck_ref, cv_ref, qg_ref, kg_ref, bias_ref,
                       ob_ref, ko_ref, vo_ref, *, steps):
    bd = _block_diag_ones()
    rows = q_ref.shape[1]
    kg = kg_ref[...]
    qg = qg_ref[...]
    pad = jnp.zeros((KEY_SLOTS - WINDOW - rows, LANES), F32)
    problems = []
    for i in range(q_ref.shape[0]):
        kv = kv_ref[i]
        k_new = [_head_norm(kv[:, t * LANES:(t + 1) * LANES], kg, bd) for t in range(2)]
        v_new = [kv[:, B_KV_WIDTH + t * LANES:B_KV_WIDTH + (t + 1) * LANES] for t in range(2)]
        ck, cv = ck_ref[i], cv_ref[i]
        k_tiles = [jnp.concatenate([ck[:, t * LANES:(t + 1) * LANES], k_new[t], pad], axis=0)
                   for t in range(2)]
        v_tiles = [jnp.concatenate([cv[:, t * LANES:(t + 1) * LANES], v_new[t], pad], axis=0)
                   for t in range(2)]
        q = q_ref[i]
        qn_tiles = [_head_norm(q[:, t * LANES:(t + 1) * LANES], qg, bd) for t in range(B_WIDTH // LANES)]

        def store(tile, val, i=i):
            ob_ref[i, :, tile * LANES:(tile + 1) * LANES] = val

        problems.append((qn_tiles, k_tiles, v_tiles, store))
        ko_ref[i, 0:WINDOW - steps, :] = ck_ref[i, steps:WINDOW, :]
        vo_ref[i, 0:WINDOW - steps, :] = cv_ref[i, steps:WINDOW, :]
        ko_ref[i, WINDOW - steps:WINDOW, :] = jnp.concatenate([kt[0:steps] for kt in k_new], axis=1)
        vo_ref[i, WINDOW - steps:WINDOW, :] = kv[0:steps, B_KV_WIDTH:]
    _swa_core(problems, lambda p, j: bias_ref[0, j], sink_ref)


def _sample_dist(rows):
    return WINDOW + np.arange(rows)[:, None] - np.arange(KEY_SLOTS)[None, :]


def _sample_valid(rows, steps):
    dist = _sample_dist(rows)
    j = np.arange(KEY_SLOTS)[None, :]
    return ((dist >= 0) & (dist < WINDOW) & (j < WINDOW + steps)).astype(np.int32)[None]


def _swa_sample(u3, cache_k, cache_v, layer, sinks, qg, kg, bias, steps):
    bsz, rows, _ = u3.shape
    w = cache_k.shape[2]
    bt = SWA_SAMPLE_SEQS
    assert w == WINDOW and bsz % bt == 0
    qb, kvb = Q_B // B_WIDTH, K_B // (2 * B_KV_WIDTH)
    assert K_B % (2 * B_KV_WIDTH) == 0
    c_in_spec = pl.BlockSpec((None, bt, w, B_KV_WIDTH), lambda b: (layer, b, 0, 0))
    c_spec = pl.BlockSpec((bt, w, B_KV_WIDTH), lambda b: (b, 0, 0))
    g_spec = pl.BlockSpec((1, LANES), lambda b: (0, 0))
    nbytes = bt * 8 * w * B_KV_WIDTH * 4 + 2 * bias.size * 4 + bt * 16 * KEY_SLOTS * LANES * 4
    return pl.pallas_call(
        functools.partial(_swa_sample_kernel, steps=steps),
        grid=(bsz // bt,),
        in_specs=[pl.BlockSpec(memory_space=pltpu.SMEM),
                  pl.BlockSpec((bt, rows, B_WIDTH), lambda b: (b, 0, qb)),
                  pl.BlockSpec((bt, rows, 2 * B_KV_WIDTH), lambda b: (b, 0, kvb)),
                  c_in_spec, c_in_spec, g_spec, g_spec,
                  pl.BlockSpec(bias.shape, lambda b: (0, 0, 0, 0))],
        out_specs=(pl.BlockSpec((bt, rows, B_WIDTH), lambda b: (b, 0, 0)), c_spec, c_spec),
        out_shape=(jax.ShapeDtypeStruct((bsz, rows, B_WIDTH), BF16),
                   jax.ShapeDtypeStruct((bsz, w, B_KV_WIDTH), F32),
                   jax.ShapeDtypeStruct((bsz, w, B_KV_WIDTH), F32)),
        compiler_params=_params(("parallel",), nbytes),
        name="swa_sample",
    )(sinks, u3, u3, cache_k, cache_v, qg, kg, bias)


def kernel(x_prompt, x_sample, cache_k, cache_v, state_hgrn, norm_mix, w_in, lower_bounds, hgrn_norm,
           q_norm, k_norm, attn_sinks, rel_bias_table, w_out, norm_ffn, w_gate, w_up, w_down):
    depth = w_in.shape[0]
    bp, seq, d = x_prompt.shape
    bd, ld, _ = x_sample.shape
    w = cache_k.shape[2]
    in_width = w_in.shape[2]

    lbf, oml = _lower_bounds(lower_bounds)
    sinks = attn_sinks.astype(F32)
    qg2 = jnp.tile(q_norm.astype(F32), (1, LANES // B_HD)) * (ATTN_SCALE * LOG2E)
    kg2 = jnp.tile(k_norm.astype(F32), (1, LANES // B_HD))
    ck = cache_k.astype(F32).reshape(depth, bd, w, B_KV_WIDTH)
    cv = cache_v.astype(F32).reshape(depth, bd, w, B_KV_WIDTH)

    xp = x_prompt.reshape(bp * seq, d)
    xs = jnp.pad(x_sample, ((0, 0), (0, SAMPLE_ROWS - ld), (0, 0))).reshape(bd * SAMPLE_ROWS, d)
    tm_s = bd * SAMPLE_ROWS
    bias_p = _bias_layout(rel_bias_table, _prompt_dist(), _prompt_valid())
    bias_s = _bias_layout(rel_bias_table, _sample_dist(SAMPLE_ROWS), _sample_valid(SAMPLE_ROWS, ld))
    state = state_hgrn.astype(F32)

    stacked = (w_in, w_out, w_gate, w_up, w_down)
    w_in_b = w_out_b = w_gate_b = w_up_b = w_down_b = None
    pk, pv, ps, sk, sv, ss = [], [], [], [], [], []
    for l in range(depth):
        lbf_l, oml_l = lbf[l:l + 1], oml[l:l + 1]
        if w_in_b is None:
            u, w_in_b = _norm_matmul(xs, norm_mix[l], w_in, l, tm_s, CAST_COLS)
        else:
            u = _norm_matmul(xs, norm_mix[l], w_in_b, None, tm_s, W_IN_TILE[1])
        u = u.reshape(bd, SAMPLE_ROWS, in_width)
        oa, st = _hgrn_sample(u, lbf_l, oml_l, hgrn_norm[l], state, l, ld)
        ob, kn, vn = _swa_sample(u, ck, cv, l, sinks[l], qg2[l:l + 1], kg2[l:l + 1], bias_s, ld)
        oa, ob = oa.reshape(tm_s, A_WIDTH), ob.reshape(tm_s, B_WIDTH)
        if w_out_b is None:
            xs, w_out_b = _outproj(xs, oa, ob, w_out, l, tm_s, CAST_COLS)
            xs, w_gate_b, w_up_b, w_down_b = _ffn(xs, norm_ffn[l], w_gate, w_up, w_down, l, tm_s, CAST_COLS)
        else:
            xs = _outproj(xs, oa, ob, w_out_b, None, tm_s, d)
            xs = _ffn(xs, norm_ffn[l], w_gate_b, w_up_b, w_down_b, None, tm_s, FFN_TILE[1])
        sk.append(kn)
        sv.append(vn)
        ss.append(st)
        u = _norm_matmul(xp, norm_mix[l], w_in_b, None, *W_IN_TILE).reshape(bp, seq, in_width)
        oa, st, next_b = _hgrn_prompt(u, lbf_l, oml_l, hgrn_norm[l],
                                      stacked if l + 1 < depth else (), l + 1)
        ob, kn, vn = _swa_prompt(u, sinks[l], qg2[l:l + 1], kg2[l:l + 1], bias_p)
        xp = _outproj(xp, oa.reshape(bp * seq, A_WIDTH), ob.reshape(bp * seq, B_WIDTH), w_out_b, None,
                      W_OUT_ROWS, d)
        xp = _ffn(xp, norm_ffn[l], w_gate_b, w_up_b, w_down_b, None, *FFN_TILE)
        pk.append(kn)
        pv.append(vn)
        ps.append(st)
        if next_b:
            w_in_b, w_out_b, w_gate_b, w_up_b, w_down_b = next_b

    kv_shape = lambda n: (depth, n, WINDOW, B_KV_HEADS, B_HD)
    return (xp.reshape(bp, seq, d),
            xs.reshape(bd, SAMPLE_ROWS, d)[:, :ld],
            jnp.stack(pk).reshape(kv_shape(bp)), jnp.stack(pv).reshape(kv_shape(bp)), jnp.stack(ps),
            jnp.stack(sk).reshape(kv_shape(bd)), jnp.stack(sv).reshape(kv_shape(bd)), jnp.stack(ss))
```

```python
import functools
import math

import jax
import jax.numpy as jnp
import numpy as np
from jax import lax
from jax.experimental import pallas as pl
from jax.experimental.pallas import tpu as pltpu

F32 = jnp.float32
BF16 = jnp.bfloat16

LANES = 128
SUBLANES = 8
VMEM_BYTES_V7X = 64 * 1024 * 1024
VMEM_CAP = VMEM_BYTES_V7X - 8 * 1024 * 1024

A_HEADS = 8
A_DK = 128
A_DV = 128
A_KEY_WIDTH = A_HEADS * A_DK
A_WIDTH = A_HEADS * A_DV
B_HEADS = 16
B_KV_HEADS = 4
B_HD = 64
B_WIDTH = B_HEADS * B_HD
B_KV_WIDTH = B_KV_HEADS * B_HD
WINDOW = 128
ATTN_SCALE = 1.0 / math.sqrt(B_HD)
LOG2E = math.log2(math.e)
NEG_LOGIT = -1e30
LB_FLOOR = 1e-20
N_BUCKETS = 32
MAX_DISTANCE = 128
RMS_EPS = 1e-6

Q_A, F_A, I_A, G_A = 0, A_KEY_WIDTH, 2 * A_KEY_WIDTH, 2 * A_KEY_WIDTH + A_WIDTH
Q_B = 2 * A_KEY_WIDTH + 2 * A_WIDTH
K_B = Q_B + B_WIDTH
V_B = K_B + B_KV_WIDTH
IN_WIDTH = V_B + B_KV_WIDTH

W_IN_TILE = (512, 2816)
W_OUT_ROWS = 512
FFN_TILE = (1024, 512)
CAST_COLS = 512

HGRN_CHUNK = 128
HGRN_BASE = 4
HGRN_HEADS_PER_STEP = 8
SAMPLE_ROWS = SUBLANES
SWA_PROMPT_BLOCKS = 2
SWA_SAMPLE_SEQS = 4
HGRN_SAMPLE_SEQS = 2
KEY_SLOTS = 2 * WINDOW


def _params(sem, nbytes):
    assert nbytes <= VMEM_CAP, nbytes
    return pltpu.CompilerParams(dimension_semantics=sem, vmem_limit_bytes=VMEM_CAP)


def _sig_pair(z):
    th = 0.5 * jnp.tanh(0.5 * z)
    return 0.5 + th, 0.5 - th


def _silu(x):
    return x * _sig_pair(x)[0]


def _rms_rows(x, g):
    ms = jnp.mean(x * x, axis=-1, keepdims=True)
    return x * lax.rsqrt(ms + RMS_EPS) * g


def _split_bf16(x):
    hi = x.astype(BF16)
    lo = (x - hi.astype(F32)).astype(BF16)
    return hi, lo


def _dot(a, b):
    return jnp.dot(a, b, preferred_element_type=F32)


def _dot_nt(a, b):
    return lax.dot_general(a, b, (((1,), (1,)), ((), ())), preferred_element_type=F32)


def _lb_kernel(lb_ref, lbf_ref, oml_ref):
    depth = lb_ref.shape[0]
    rows = [lb_ref[i:i + 1, :] for i in range(depth)]
    m = functools.reduce(jnp.maximum, rows)
    e = [jnp.exp(r - m) for r in rows]
    s = functools.reduce(lambda a, b: a + b, e)
    p = [ei / s for ei in e]
    cum = p[0]
    for i in range(depth):
        if i > 0:
            cum = cum + p[i]
        lb = cum - p[0]
        lbf_ref[i:i + 1, :] = jnp.maximum(lb, LB_FLOOR)
        oml_ref[i:i + 1, :] = 1.0 - lb


def _lower_bounds(lower_bounds):
    shp = jax.ShapeDtypeStruct(lower_bounds.shape, F32)
    return pl.pallas_call(_lb_kernel, out_shape=(shp, shp), name="lb_prep")(lower_bounds.astype(F32))


def _weight_spec(w, layer, block, index):
    if w.ndim == 2:
        return pl.BlockSpec(block, index)
    return pl.BlockSpec((None,) + block, lambda *g: (layer,) + index(*g))


def _bf16_weight(w_ref, copy_refs):
    w = w_ref[...]
    if copy_refs:
        w = w.astype(BF16)
        copy_refs[0][...] = w
    return w


def _norm_matmul_kernel(x_ref, g_ref, w_ref, o_ref, *copy_refs):
    h = _rms_rows(x_ref[...], g_ref[...]).astype(BF16)
    o_ref[...] = _dot(h, _bf16_weight(w_ref, copy_refs))


def _norm_matmul(x, g, w, layer, tm, tn):
    m, d = x.shape
    n = w.shape[-1]
    cast = w.ndim == 3
    assert not cast or m == tm
    wbytes = 4 + 2 if cast else 2
    nbytes = 2 * tm * d * 4 + tm * d * 2 + 2 * d * tn * wbytes + 3 * tm * tn * 4
    u_spec = pl.BlockSpec((tm, tn), lambda j, i: (i, j))
    u_shape = jax.ShapeDtypeStruct((m, n), F32)
    w_block, w_index = (d, tn), lambda j, i: (0, j)
    return pl.pallas_call(
        _norm_matmul_kernel,
        grid=(n // tn, m // tm),
        in_specs=[pl.BlockSpec((tm, d), lambda j, i: (i, 0)),
                  pl.BlockSpec((1, d), lambda j, i: (0, 0)),
                  _weight_spec(w, layer, w_block, w_index)],
        out_specs=(u_spec, pl.BlockSpec(w_block, w_index)) if cast else u_spec,
        out_shape=(u_shape, jax.ShapeDtypeStruct((d, n), BF16)) if cast else u_shape,
        compiler_params=_params(("parallel", "parallel"), nbytes),
        name="norm_matmul",
    )(x, g.reshape(1, d), w)


def _outproj_kernel(x_ref, oa_ref, ob_ref, w_ref, o_ref, *rest):
    ka = oa_ref.shape[1]
    w = _bf16_weight(w_ref, rest)
    o_ref[...] = x_ref[...] + _dot(oa_ref[...], w[0:ka]) + _dot(ob_ref[...], w[ka:])


def _outproj(x, oa, ob, w, layer, tm, tn):
    m, d = x.shape
    ka, kb = oa.shape[1], ob.shape[1]
    cast = w.ndim == 3
    assert not cast or m == tm
    wbytes = 4 + 2 if cast else 2
    nbytes = 4 * tm * tn * 4 + 2 * tm * (ka + kb) * 2 + 2 * (ka + kb) * tn * wbytes
    x_spec = pl.BlockSpec((tm, tn), lambda i, j: (i, j))
    x_shape = jax.ShapeDtypeStruct((m, d), F32)
    w_block, w_index = (ka + kb, tn), lambda i, j: (0, j)
    return pl.pallas_call(
        _outproj_kernel,
        grid=(m // tm, d // tn),
        in_specs=[x_spec,
                  pl.BlockSpec((tm, ka), lambda i, j: (i, 0)),
                  pl.BlockSpec((tm, kb), lambda i, j: (i, 0)),
                  _weight_spec(w, layer, w_block, w_index)],
        out_specs=(x_spec, pl.BlockSpec(w_block, w_index)) if cast else x_spec,
        out_shape=(x_shape, jax.ShapeDtypeStruct((ka + kb, d), BF16)) if cast else x_shape,
        compiler_params=_params(("parallel", "arbitrary"), nbytes),
        name="outproj",
    )(x, oa, ob, w)


def _ffn_kernel(x_ref, g_ref, wg_ref, wu_ref, wd_ref, o_ref, *rest):
    h_ref = rest[-1]
    copies = rest[:-1]

    @pl.when(pl.program_id(1) == 0)
    def _():
        x = x_ref[...]
        h_ref[...] = _rms_rows(x, g_ref[...]).astype(BF16)
        o_ref[...] = x

    h = h_ref[...]
    gate = _dot(h, _bf16_weight(wg_ref, copies[0:1]))
    up = _dot(h, _bf16_weight(wu_ref, copies[1:2]))
    act = (_silu(gate) * up).astype(BF16)
    o_ref[...] += _dot(act, _bf16_weight(wd_ref, copies[2:3]))


def _ffn(x, g, wg, wu, wd, layer, tm, tf):
    m, d = x.shape
    f = wg.shape[-1]
    cast = wg.ndim == 3
    assert not cast or m == tm
    wbytes = 4 + 2 if cast else 2
    nbytes = 4 * tm * d * 4 + tm * d * 2 + 2 * 3 * d * tf * wbytes + 3 * tm * tf * 4
    x_spec = pl.BlockSpec((tm, d), lambda i, j: (i, 0))
    x_shape = jax.ShapeDtypeStruct((m, d), F32)
    col_block, col_index = (d, tf), lambda i, j: (0, j)
    row_block, row_index = (tf, d), lambda i, j: (j, 0)
    copy_specs = (pl.BlockSpec(col_block, col_index), pl.BlockSpec(col_block, col_index),
                  pl.BlockSpec(row_block, row_index))
    copy_shapes = (jax.ShapeDtypeStruct((d, f), BF16), jax.ShapeDtypeStruct((d, f), BF16),
                   jax.ShapeDtypeStruct((f, d), BF16))
    return pl.pallas_call(
        _ffn_kernel,
        grid=(m // tm, f // tf),
        in_specs=[x_spec,
                  pl.BlockSpec((1, d), lambda i, j: (0, 0)),
                  _weight_spec(wg, layer, col_block, col_index),
                  _weight_spec(wu, layer, col_block, col_index),
                  _weight_spec(wd, layer, row_block, row_index)],
        out_specs=(x_spec,) + copy_specs if cast else x_spec,
        out_shape=(x_shape,) + copy_shapes if cast else x_shape,
        scratch_shapes=[pltpu.VMEM((tm, d), BF16)],
        compiler_params=_params(("parallel", "arbitrary"), nbytes),
        name="ffn",
    )(x, g.reshape(1, d), wg, wu, wd)


def _hgrn_static(c):
    assert (HGRN_BASE - 1) * -math.log(LB_FLOOR) / 2 < 80.0
    t = np.arange(c)
    tri = (t[None, :] <= t[:, None]).astype(np.float32)
    first = HGRN_BASE * (t // HGRN_BASE)
    base = tri - 0.5 * (tri[first] + tri[first + HGRN_BASE - 1])
    masks = []
    h = c // 2
    while h >= HGRN_BASE:
        grp = t // (2 * h)
        upper = (t % (2 * h)) >= h
        masks.append((grp[:, None] == grp[None, :]) & upper[:, None] & ~upper[None, :])
        h //= 2
    masks.append((first[:, None] == first[None, :]) & (t[None, :] <= t[:, None]))
    p_all = np.concatenate([tri, base], axis=0)
    p2 = np.concatenate([p_all, p_all], axis=1)
    masks = np.stack(masks).astype(np.float32)
    return p2, np.concatenate([masks, masks], axis=2)


def _hgrn_prompt_kernel(q_ref, f_ref, i_ref, g_ref, lbf_ref, oml_ref, gn_ref, p2_ref, m_ref, *rest,
                        chunk, heads, cast_spans):
    n_cast = len(cast_spans)
    w_refs, (oa_ref, st_ref), wb_refs, s_scr = (rest[:n_cast], rest[n_cast:n_cast + 2],
                                                rest[n_cast + 2:2 * n_cast + 2], rest[-1])
    c_idx = pl.program_id(2)

    @pl.when(c_idx == 0)
    def _():
        s_scr[...] = jnp.zeros_like(s_scr)

    for w_ref, wb_ref, span in zip(w_refs, wb_refs, cast_spans):
        if span == 1:
            wb_ref[...] = w_ref[...].astype(BF16)
        else:
            @pl.when(c_idx % span == 0)
            def _(w_ref=w_ref, wb_ref=wb_ref):
                wb_ref[...] = w_ref[...].astype(BF16)

    width = heads * A_DK
    pw = 2 * A_DK
    sz, snz = _sig_pair(f_ref[0])
    oml = oml_ref[...]
    lf = jnp.log(lbf_ref[...] + oml * sz)
    ka = oml * snz
    hi, lo = _split_bf16(lf)
    expo = _dot(p2_ref[...], jnp.concatenate([hi, lo], axis=0)) * LOG2E
    b = expo[0:chunk]
    q = _silu(q_ref[0])
    v = i_ref[0]
    gate = _silu(g_ref[0])
    gn = gn_ref[...]

    row = lax.broadcasted_iota(jnp.int32, (chunk, pw), 0)
    left = lax.broadcasted_iota(jnp.int32, (chunk, pw), 1) < A_DK

    def bdiag(x):
        z = jnp.zeros_like(x)
        return jnp.concatenate([jnp.where(left, x, z), jnp.where(left, z, x)], axis=0)

    for p in range(heads // 2):
        sl = slice(p * pw, (p + 1) * pw)
        qp, kp, vp, bp = q[:, sl], ka[:, sl], v[:, sl], b[:, sl]
        qs, ks = [], []
        h = chunk // 2
        while h >= HGRN_BASE:
            groups = chunk // (2 * h)
            mid = jnp.broadcast_to(bp.reshape(groups, 2 * h, pw)[:, h - 1:h, :], (groups, 2 * h, pw))
            d = bp - mid.reshape(chunk, pw)
            e = jnp.exp2(jnp.where((row & h) != 0, d, -d))
            qs.append((qp * e).astype(BF16))
            ks.append((kp * e).astype(BF16))
            h //= 2
        e_base = expo[chunk:2 * chunk, sl]
        qs.append((qp * jnp.exp2(e_base)).astype(BF16))
        ks.append((kp * jnp.exp2(-e_base)).astype(BF16))
        b_last = bp[chunk - 1:chunk, :]
        st = s_scr[p]
        o = _dot_nt((qp * jnp.exp2(bp)).astype(BF16), bdiag(st.astype(BF16)))
        a = jnp.zeros((chunk, pw), BF16)
        for li in range(len(qs)):
            a = a + _dot_nt(qs[li], bdiag(ks[li])).astype(BF16) * m_ref[li]
        o = o + _dot(a, bdiag(vp.astype(BF16)))
        vt = jnp.concatenate([vp[:, :A_DV].T, vp[:, A_DV:].T], axis=1).astype(BF16)
        k_tail = (kp * jnp.exp2(b_last - bp)).astype(BF16)
        s_scr[p] = st * jnp.exp2(b_last) + _dot(vt, bdiag(k_tail))
        for hh in range(2):
            hs = slice(p * pw + hh * A_DV, p * pw + (hh + 1) * A_DV)
            oa_ref[0, :, hs] = (_rms_rows(o[:, hh * A_DV:(hh + 1) * A_DV], gn) * gate[:, hs]).astype(BF16)

    @pl.when(c_idx == pl.num_programs(2) - 1)
    def _():
        for hh in range(heads):
            st_ref[0, hh] = s_scr[hh // 2][:, (hh % 2) * A_DK:(hh % 2 + 1) * A_DK].T


def _hgrn_prompt(u3, lbf, oml, gn, cast_weights=(), cast_layer=None):
    bsz, length, _ = u3.shape
    chunk, heads = HGRN_CHUNK, HGRN_HEADS_PER_STEP
    cols = heads * A_DK
    assert length % chunk == 0 and A_HEADS % heads == 0 and heads % 2 == 0 and chunk == A_DV
    p2, masks = _hgrn_static(chunk)
    p2 = jnp.asarray(p2, BF16)
    masks = jnp.asarray(masks, BF16)
    n_chunks = length // chunk
    assert not cast_weights or A_HEADS == heads

    def col_spec(offset):
        base = offset // cols
        return pl.BlockSpec((1, chunk, cols), lambda b, h, c: (b, c, base + h))

    cast_in, cast_out, cast_shape, cast_spans, cast_bytes = [], [], [], [], 0
    for w in cast_weights:
        span = next(s for s in (1, 2, 4) if n_chunks % s == 0 and w.shape[1] % (bsz * n_chunks // s) == 0
                    and (w.shape[1] // (bsz * n_chunks // s)) % (2 * SUBLANES) == 0)
        rows, width = w.shape[1] // (bsz * n_chunks // span), w.shape[2]

        def block(b, h, c, span=span):
            return (b * n_chunks + c) // span

        cast_in.append(pl.BlockSpec((None, rows, width), lambda b, h, c, block=block: (cast_layer, block(b, h, c), 0)))
        cast_out.append(pl.BlockSpec((rows, width), lambda b, h, c, block=block: (block(b, h, c), 0)))
        cast_shape.append(jax.ShapeDtypeStruct(w.shape[1:], BF16))
        cast_spans.append(span)
        cast_bytes += 2 * rows * width * (4 + 2)

    nbytes = (2 * 5 * chunk * cols * 4 + 2 * (p2.size * 2 + masks.size * 4)
              + 6 * p2.shape[0] * cols * 4 + cast_bytes)
    outs = pl.pallas_call(
        functools.partial(_hgrn_prompt_kernel, chunk=chunk, heads=heads, cast_spans=tuple(cast_spans)),
        grid=(bsz, A_HEADS // heads, n_chunks),
        in_specs=[col_spec(Q_A), col_spec(F_A), col_spec(I_A), col_spec(G_A),
                  pl.BlockSpec((1, cols), lambda b, h, c: (0, h)),
                  pl.BlockSpec((1, cols), lambda b, h, c: (0, h)),
                  pl.BlockSpec((1, A_DV), lambda b, h, c: (0, 0)),
                  pl.BlockSpec(p2.shape, lambda b, h, c: (0, 0)),
                  pl.BlockSpec(masks.shape, lambda b, h, c: (0, 0, 0))] + cast_in,
        out_specs=[pl.BlockSpec((1, chunk, cols), lambda b, h, c: (b, c, h)),
                   pl.BlockSpec((1, heads, A_DK, A_DV), lambda b, h, c: (b, h, 0, 0))] + cast_out,
        out_shape=[jax.ShapeDtypeStruct((bsz, length, A_WIDTH), BF16),
                   jax.ShapeDtypeStruct((bsz, A_HEADS, A_DK, A_DV), F32)] + cast_shape,
        scratch_shapes=[pltpu.VMEM((heads // 2, A_DV, 2 * A_DK), F32)],
        compiler_params=_params(("parallel", "parallel", "arbitrary"), nbytes),
        name="hgrn_prompt",
    )(u3, u3, u3, u3, lbf, oml, gn.reshape(1, A_DV), p2, masks, *cast_weights)
    return outs[0], outs[1], outs[2:]


def _hgrn_sample_kernel(u_ref, lbf_ref, oml_ref, gn_ref, s_ref, oa_ref, so_ref, *, steps):
    for i in range(u_ref.shape[0]):
        _hgrn_sample_one(i, u_ref, lbf_ref, oml_ref, gn_ref, s_ref, oa_ref, so_ref, steps)


def _hgrn_sample_one(i, u_ref, lbf_ref, oml_ref, gn_ref, s_ref, oa_ref, so_ref, steps):
    u = u_ref[i]
    q = _silu(u[:, Q_A:Q_A + A_KEY_WIDTH])
    sz, snz = _sig_pair(u[:, F_A:F_A + A_KEY_WIDTH])
    oml = oml_ref[...]
    f = lbf_ref[...] + oml * sz
    ka = oml * snz
    v = u[:, I_A:I_A + A_WIDTH]
    gate = _silu(u[:, G_A:G_A + A_WIDTH])
    gn = gn_ref[...]
    rows = u.shape[0]
    row = lax.broadcasted_iota(jnp.int32, f.shape, 0)
    live = row < steps
    f = jnp.where(live, f, 1.0)
    ka = jnp.where(live, ka, 0.0)

    def down(x, d, fill):
        return x if d == 0 else jnp.where(row >= d, pltpu.roll(x, d, axis=0), fill)

    def up(x, d, fill):
        return jnp.where(row < rows - d, pltpu.roll(x, rows - d, axis=0), fill)

    decay = [jnp.ones_like(f)]
    for d in range(1, steps + 1):
        decay.append(decay[-1] * down(f, d - 1, 1.0))
    head = decay[steps]
    tail = jnp.ones_like(f)
    for d in range(1, steps):
        tail = tail * up(f, d, 1.0)
    q_in = (q * head).astype(BF16)
    k_tail = ka * tail
    total = head[steps - 1:steps, :]
    t_hi = total.astype(BF16).astype(F32)
    t_mid = (total - t_hi).astype(BF16).astype(F32)
    t_lo = (total - t_hi - t_mid).astype(BF16).astype(F32)
    t_rows = jnp.where(row == 0, t_hi, jnp.where(row == 1, t_mid, jnp.where(row == 2, t_lo, 0.0)))
    prods = [q * down(ka, d, 0.0) * decay[d] for d in range(steps)]
    v_sh = [down(v, d, 0.0) for d in range(steps)]

    zsq = jnp.zeros((A_DK, A_DV), BF16)
    zrows = jnp.zeros((rows, A_DV), F32)
    r2 = lax.broadcasted_iota(jnp.int32, (rows, 2 * A_DV), 0)
    l2 = lax.broadcasted_iota(jnp.int32, (rows, 2 * A_DV), 1)
    ones_right = jnp.where((r2 < 3) & (l2 >= A_DV), 1.0, 0.0)

    for p in range(A_HEADS // 2):
        ps = slice(2 * p * A_DK, (2 * p + 2) * A_DK)
        s0 = [s_ref[i, 2 * p + hh] for hh in range(2)]
        s_bd = jnp.concatenate([jnp.concatenate([s0[0].astype(BF16), zsq], axis=1),
                                jnp.concatenate([zsq, s0[1].astype(BF16)], axis=1)], axis=0)
        o_pair = _dot(q_in[:, ps], s_bd)
        for hh in range(2):
            head_i = 2 * p + hh
            sl = slice(head_i * A_DK, (head_i + 1) * A_DK)
            o = o_pair[:, hh * A_DV:(hh + 1) * A_DV]
            for d in range(steps):
                o = o + jnp.sum(prods[d][:, sl], axis=-1, keepdims=True) * v_sh[d][:, sl]
            lhs = jnp.concatenate([k_tail[:, sl], t_rows[:, sl]], axis=0).astype(BF16)
            rhs = jnp.concatenate([jnp.concatenate([v[:, sl], zrows], axis=1), ones_right],
                                  axis=0).astype(BF16)
            upd = lax.dot_general(lhs, rhs, (((0,), (0,)), ((), ())), preferred_element_type=F32)
            so_ref[i, head_i] = s0[hh] * upd[:, A_DV:] + upd[:, :A_DV]
            oa_ref[i, :, sl] = (_rms_rows(o, gn) * gate[:, sl]).astype(BF16)


def _hgrn_sample(u3, lbf, oml, gn, state, layer, steps):
    bsz, rows, _ = u3.shape
    width = G_A + A_WIDTH
    bt = HGRN_SAMPLE_SEQS
    assert bsz % bt == 0
    nbytes = bt * (2 * rows * width * 4 + 4 * A_HEADS * A_DK * A_DV * 4)
    return pl.pallas_call(
        functools.partial(_hgrn_sample_kernel, steps=steps),
        grid=(bsz // bt,),
        in_specs=[pl.BlockSpec((bt, rows, width), lambda b: (b, 0, 0)),
                  pl.BlockSpec((1, A_KEY_WIDTH), lambda b: (0, 0)),
                  pl.BlockSpec((1, A_KEY_WIDTH), lambda b: (0, 0)),
                  pl.BlockSpec((1, A_DV), lambda b: (0, 0)),
                  pl.BlockSpec((None, bt, A_HEADS, A_DK, A_DV), lambda b: (layer, b, 0, 0, 0))],
        out_specs=(pl.BlockSpec((bt, rows, A_WIDTH), lambda b: (b, 0, 0)),
                   pl.BlockSpec((bt, A_HEADS, A_DK, A_DV), lambda b: (b, 0, 0, 0))),
        out_shape=(jax.ShapeDtypeStruct((bsz, rows, A_WIDTH), BF16),
                   jax.ShapeDtypeStruct(state.shape[1:], F32)),
        compiler_params=_params(("parallel",), nbytes),
        name="hgrn_sample",
    )(u3, lbf, oml, gn.reshape(1, A_DV), state)


def _t5_bucket(dist):
    n = np.maximum(dist, 0)
    max_exact = N_BUCKETS // 2
    nf = np.maximum(n, 1).astype(np.float32)
    large = max_exact + (np.log(nf / max_exact) / math.log(MAX_DISTANCE / max_exact)
                         * (N_BUCKETS - max_exact)).astype(np.int32)
    large = np.minimum(large, N_BUCKETS - 1)
    return np.where(n < max_exact, n, large).astype(np.int32)


def _bias_kernel(tab_ref, bucket_ref, valid_ref, o_ref):
    rows = bucket_ref.shape[0]
    bucket = bucket_ref[...]
    hits = [bucket == b for b in range(N_BUCKETS)]
    for h in range(B_HEADS):
        acc = jnp.zeros(bucket.shape, F32)
        for b in range(N_BUCKETS):
            acc = jnp.where(hits[b], tab_ref[b, h], acc)
        acc = acc * LOG2E
        j, r, c = h // 4, (h % 4) // 2, h % 2
        for m in range(valid_ref.shape[0]):
            o_ref[m, j, r * rows:(r + 1) * rows, c * KEY_SLOTS:(c + 1) * KEY_SLOTS] = jnp.where(
                valid_ref[m] != 0, acc, NEG_LOGIT * LOG2E)


def _bias_layout(table, dist, valids):
    rows = dist.shape[0]
    return pl.pallas_call(
        _bias_kernel,
        in_specs=[pl.BlockSpec(memory_space=pltpu.SMEM),
                  pl.BlockSpec(memory_space=pltpu.VMEM),
                  pl.BlockSpec(memory_space=pltpu.VMEM)],
        out_shape=jax.ShapeDtypeStruct((valids.shape[0], B_KV_HEADS, 2 * rows, 2 * KEY_SLOTS), F32),
        name="bias_prep",
    )(table.astype(F32), jnp.asarray(_t5_bucket(dist), jnp.int32), jnp.asarray(valids, jnp.int32))


def _block_diag_ones():
    r = lax.broadcasted_iota(jnp.int32, (LANES, LANES), 0) // B_HD
    c = lax.broadcasted_iota(jnp.int32, (LANES, LANES), 1) // B_HD
    return (r == c).astype(BF16)


def _head_norm(x, g2, bd):
    hi, lo = _split_bf16(x * x)
    ss = _dot(hi, bd) + _dot(lo, bd)
    return x * lax.rsqrt(ss * (1.0 / B_HD) + RMS_EPS) * g2


def _place(tile, half, lo_mask):
    rolled = pltpu.roll(tile, B_HD, axis=1)
    zero = jnp.zeros_like(tile)
    if half == 0:
        return jnp.where(lo_mask, tile, zero), jnp.where(lo_mask, zero, rolled)
    return jnp.where(lo_mask, rolled, zero), jnp.where(lo_mask, zero, tile)


def _swa_core(problems, bias_at, sink_ref):
    rows = problems[0][0][0].shape[0]
    lo_mask = lax.broadcasted_iota(jnp.int32, problems[0][1][0].shape, 1) < B_HD
    out_lo = lax.broadcasted_iota(jnp.int32, (rows, LANES), 1) < B_HD
    units = [(p, j) for p in range(len(problems)) for j in range(B_KV_HEADS)]
    scores, values = {}, {}
    for p, j in units:
        qn_tiles, k_tiles, v_tiles, _ = problems[p]
        t, half = j // 2, j % 2
        k_lo, k_hi = _place(k_tiles[t], half, lo_mask)
        v_lo, v_hi = _place(v_tiles[t], half, lo_mask)
        kk = jnp.concatenate([k_lo, k_hi], axis=0).astype(BF16)
        values[p, j] = jnp.concatenate([v_lo, v_hi], axis=0).astype(BF16)
        qq = jnp.concatenate([qn_tiles[2 * j], qn_tiles[2 * j + 1]], axis=0).astype(BF16)
        scores[p, j] = _dot_nt(qq, kk) + bias_at(p, j)
    weights, scales = {}, {}
    for p, j in units:
        e_rows, inv = [], []
        for r in range(2):
            e_cols, inv_r = [], []
            for c in range(2):
                sk = sink_ref[4 * j + 2 * r + c] * LOG2E
                sb = scores[p, j][r * rows:(r + 1) * rows, c * KEY_SLOTS:(c + 1) * KEY_SLOTS]
                m = jnp.maximum(jnp.max(sb, axis=-1, keepdims=True), sk)
                e = jnp.exp2(sb - m)
                inv_r.append(1.0 / (jnp.sum(e, axis=-1, keepdims=True) + jnp.exp2(sk - m)))
                e_cols.append(e.astype(BF16))
            e_rows.append(jnp.concatenate(e_cols, axis=1))
            inv.append(jnp.where(out_lo, inv_r[0], inv_r[1]))
        weights[p, j] = jnp.concatenate(e_rows, axis=0)
        scales[p, j] = inv
    for p, j in units:
        o = _dot(weights[p, j], values[p, j])
        for r in range(2):
            problems[p][3](2 * j + r, (o[r * rows:(r + 1) * rows] * scales[p, j][r]).astype(BF16))


def _swa_prompt_kernel(sink_ref, q_ref, kc_ref, kp_ref, vc_ref, vp_ref, qg_ref, kg_ref, bias_ref,
                       ob_ref, ko_ref, vo_ref):
    m = pl.program_id(1)
    blocks = q_ref.shape[1] // WINDOW
    bd = _block_diag_ones()
    k_all = jnp.concatenate([kp_ref[0], kc_ref[0]], axis=0)
    v_all = jnp.concatenate([vp_ref[0], vc_ref[0]], axis=0)
    kg = kg_ref[...]
    k_norm = [_head_norm(k_all[:, t * LANES:(t + 1) * LANES], kg, bd) for t in range(2)]
    q = q_ref[0]
    qg = qg_ref[...]
    qn = [_head_norm(q[:, t * LANES:(t + 1) * LANES], qg, bd) for t in range(B_WIDTH // LANES)]
    first = jnp.minimum(m, 1)

    problems, biases = [], []
    for i in range(blocks):
        rows = slice(i * WINDOW, (i + 1) * WINDOW)
        keys = slice(i * WINDOW, i * WINDOW + KEY_SLOTS)

        def store(tile, val, rows=rows):
            ob_ref[0, rows, tile * LANES:(tile + 1) * LANES] = val

        problems.append(([qt[rows] for qt in qn], [kt[keys] for kt in k_norm],
                         [v_all[keys, t * LANES:(t + 1) * LANES] for t in range(2)], store))
        biases.append(first if i == 0 else 1)
    _swa_core(problems, lambda p, j: bias_ref[biases[p], j], sink_ref)

    @pl.when(m == pl.num_programs(1) - 1)
    def _():
        ko_ref[0] = jnp.concatenate([kt[blocks * WINDOW:] for kt in k_norm], axis=1)
        vo_ref[0] = vc_ref[0, (blocks - 1) * WINDOW:, :]


def _prompt_dist():
    return WINDOW + np.arange(WINDOW)[:, None] - np.arange(KEY_SLOTS)[None, :]


def _prompt_valid():
    j = np.arange(KEY_SLOTS)[None, :]
    dist = _prompt_dist()
    cur = (j >= WINDOW) & (dist >= 0)
    prev = (j < WINDOW) & (dist < WINDOW)
    return np.stack([cur, cur | prev]).astype(np.int32)


def _swa_prompt(u3, sinks, qg, kg, bias):
    bsz, length, _ = u3.shape
    blocks = SWA_PROMPT_BLOCKS
    rows = blocks * WINDOW
    assert length % rows == 0
    qb, kb, vb = Q_B // B_WIDTH, K_B // B_KV_WIDTH, V_B // B_KV_WIDTH
    assert Q_B % B_WIDTH == 0 and K_B % B_KV_WIDTH == 0 and V_B % B_KV_WIDTH == 0

    def kv_spec(col, prev):
        if prev:
            return pl.BlockSpec((1, WINDOW, B_KV_WIDTH), lambda b, m: (b, jnp.maximum(m * blocks - 1, 0), col))
        return pl.BlockSpec((1, rows, B_KV_WIDTH), lambda b, m: (b, m, col))

    g_spec = pl.BlockSpec((1, LANES), lambda b, m: (0, 0))
    nbytes = (2 * rows * (B_WIDTH + 4 * B_KV_WIDTH) * 4 + 2 * bias.size * 4
              + 2 * rows * B_WIDTH * 2 + blocks * 24 * WINDOW * 2 * KEY_SLOTS * 4)
    return pl.pallas_call(
        _swa_prompt_kernel,
        grid=(bsz, length // rows),
        in_specs=[pl.BlockSpec(memory_space=pltpu.SMEM),
                  pl.BlockSpec((1, rows, B_WIDTH), lambda b, m: (b, m, qb)),
                  kv_spec(kb, False), kv_spec(kb, True), kv_spec(vb, False), kv_spec(vb, True),
                  g_spec, g_spec,
                  pl.BlockSpec(bias.shape, lambda b, m: (0, 0, 0, 0))],
        out_specs=(pl.BlockSpec((1, rows, B_WIDTH), lambda b, m: (b, m, 0)),
                   pl.BlockSpec((1, WINDOW, B_KV_WIDTH), lambda b, n: (b, 0, 0)),
                   pl.BlockSpec((1, WINDOW, B_KV_WIDTH), lambda b, n: (b, 0, 0))),
        out_shape=(jax.ShapeDtypeStruct((bsz, length, B_WIDTH), BF16),
                   jax.ShapeDtypeStruct((bsz, WINDOW, B_KV_WIDTH), F32),
                   jax.ShapeDtypeStruct((bsz, WINDOW, B_KV_WIDTH), F32)),
        compiler_params=_params(("parallel", "arbitrary"), nbytes),
        name="swa_prompt",
    )(sinks, u3, u3, u3, u3, u3, qg, kg, bias)


def _swa_sample_kernel(sink_ref, q_ref, kv_ref, ck_ref, cv_ref, qg_ref, kg_ref, bias_ref,
                       ob_ref, ko_ref, vo_ref, *, steps):
    bd = _block_diag_ones()
    rows = q_ref.shape[1]
    kg = kg_ref[...]
    qg = qg_ref[...]
    pad = jnp.zeros((KEY_SLOTS - WINDOW - rows, LANES), F32)
    problems = []
    for i in range(q_ref.shape[0]):
        kv = kv_ref[i]
        k_new = [_head_norm(kv[:, t * LANES:(t + 1) * LANES], kg, bd) for t in range(2)]
        v_new = [kv[:, B_KV_WIDTH + t * LANES:B_KV_WIDTH + (t + 1) * LANES] for t in range(2)]
        ck, cv = ck_ref[i], cv_ref[i]
        k_tiles = [jnp.concatenate([ck[:, t * LANES:(t + 1) * LANES], k_new[t], pad], axis=0)
                   for t in range(2)]
        v_tiles = [jnp.concatenate([cv[:, t * LANES:(t + 1) * LANES], v_new[t], pad], axis=0)
                   for t in range(2)]
        q = q_ref[i]
        qn_tiles = [_head_norm(q[:, t * LANES:(t + 1) * LANES], qg, bd) for t in range(B_WIDTH // LANES)]

        def store(tile, val, i=i):
            ob_ref[i, :, tile * LANES:(tile + 1) * LANES] = val

        problems.append((qn_tiles, k_tiles, v_tiles, store))
        ko_ref[i, 0:WINDOW - steps, :] = ck_ref[i, steps:WINDOW, :]
        vo_ref[i, 0:WINDOW - steps, :] = cv_ref[i, steps:WINDOW, :]
        ko_ref[i, WINDOW - steps:WINDOW, :] = jnp.concatenate([kt[0:steps] for kt in k_new], axis=1)
        vo_ref[i, WINDOW - steps:WINDOW, :] = kv[0:steps, B_KV_WIDTH:]
    _swa_core(problems, lambda p, j: bias_ref[0, j], sink_ref)


def _sample_dist(rows):
    return WINDOW + np.arange(rows)[:, None] - np.arange(KEY_SLOTS)[None, :]


def _sample_valid(rows, steps):
    dist = _sample_dist(rows)
    j = np.arange(KEY_SLOTS)[None, :]
    return ((dist >= 0) & (dist < WINDOW) & (j < WINDOW + steps)).astype(np.int32)[None]


def _swa_sample(u3, cache_k, cache_v, layer, sinks, qg, kg, bias, steps):
    bsz, rows, _ = u3.shape
    w = cache_k.shape[2]
    bt = SWA_SAMPLE_SEQS
    assert w == WINDOW and bsz % bt == 0
    qb, kvb = Q_B // B_WIDTH, K_B // (2 * B_KV_WIDTH)
    assert K_B % (2 * B_KV_WIDTH) == 0
    c_in_spec = pl.BlockSpec((None, bt, w, B_KV_WIDTH), lambda b: (layer, b, 0, 0))
    c_spec = pl.BlockSpec((bt, w, B_KV_WIDTH), lambda b: (b, 0, 0))
    g_spec = pl.BlockSpec((1, LANES), lambda b: (0, 0))
    nbytes = bt * 8 * w * B_KV_WIDTH * 4 + 2 * bias.size * 4 + bt * 16 * KEY_SLOTS * LANES * 4
    return pl.pallas_call(
        functools.partial(_swa_sample_kernel, steps=steps),
        grid=(bsz // bt,),
        in_specs=[pl.BlockSpec(memory_space=pltpu.SMEM),
                  pl.BlockSpec((bt, rows, B_WIDTH), lambda b: (b, 0, qb)),
                  pl.BlockSpec((bt, rows, 2 * B_KV_WIDTH), lambda b: (b, 0, kvb)),
                  c_in_spec, c_in_spec, g_spec, g_spec,
                  pl.BlockSpec(bias.shape, lambda b: (0, 0, 0, 0))],
        out_specs=(pl.BlockSpec((bt, rows, B_WIDTH), lambda b: (b, 0, 0)), c_spec, c_spec),
        out_shape=(jax.ShapeDtypeStruct((bsz, rows, B_WIDTH), BF16),
                   jax.ShapeDtypeStruct((bsz, w, B_KV_WIDTH), F32),
                   jax.ShapeDtypeStruct((bsz, w, B_KV_WIDTH), F32)),
        compiler_params=_params(("parallel",), nbytes),
        name="swa_sample",
    )(sinks, u3, u3, cache_k, cache_v, qg, kg, bias)


def kernel(x_prompt, x_sample, cache_k, cache_v, state_hgrn, norm_mix, w_in, lower_bounds, hgrn_norm,
           q_norm, k_norm, attn_sinks, rel_bias_table, w_out, norm_ffn, w_gate, w_up, w_down):
    depth = w_in.shape[0]
    bp, seq, d = x_prompt.shape
    bd, ld, _ = x_sample.shape
    w = cache_k.shape[2]
    in_width = w_in.shape[2]

    lbf, oml = _lower_bounds(lower_bounds)
    sinks = attn_sinks.astype(F32)
    qg2 = jnp.tile(q_norm.astype(F32), (1, LANES // B_HD)) * (ATTN_SCALE * LOG2E)
    kg2 = jnp.tile(k_norm.astype(F32), (1, LANES // B_HD))
    ck = cache_k.astype(F32).reshape(depth, bd, w, B_KV_WIDTH)
    cv = cache_v.astype(F32).reshape(depth, bd, w, B_KV_WIDTH)

    xp = x_prompt.reshape(bp * seq, d)
    xs = jnp.pad(x_sample, ((0, 0), (0, SAMPLE_ROWS - ld), (0, 0))).reshape(bd * SAMPLE_ROWS, d)
    tm_s = bd * SAMPLE_ROWS
    bias_p = _bias_layout(rel_bias_table, _prompt_dist(), _prompt_valid())
    bias_s = _bias_layout(rel_bias_table, _sample_dist(SAMPLE_ROWS), _sample_valid(SAMPLE_ROWS, ld))
    state = state_hgrn.astype(F32)

    stacked = (w_in, w_out, w_gate, w_up, w_down)
    w_in_b = w_out_b = w_gate_b = w_up_b = w_down_b = None
    pk, pv, ps, sk, sv, ss = [], [], [], [], [], []
    for l in range(depth):
        lbf_l, oml_l = lbf[l:l + 1], oml[l:l + 1]
        if w_in_b is None:
            u, w_in_b = _norm_matmul(xs, norm_mix[l], w_in, l, tm_s, CAST_COLS)
        else:
            u = _norm_matmul(xs, norm_mix[l], w_in_b, None, tm_s, W_IN_TILE[1])
        u = u.reshape(bd, SAMPLE_ROWS, in_width)
        oa, st = _hgrn_sample(u, lbf_l, oml_l, hgrn_norm[l], state, l, ld)
        ob, kn, vn = _swa_sample(u, ck, cv, l, sinks[l], qg2[l:l + 1], kg2[l:l + 1], bias_s, ld)
        oa, ob = oa.reshape(tm_s, A_WIDTH), ob.reshape(tm_s, B_WIDTH)
        if w_out_b is None:
            xs, w_out_b = _outproj(xs, oa, ob, w_out, l, tm_s, CAST_COLS)
            xs, w_gate_b, w_up_b, w_down_b = _ffn(xs, norm_ffn[l], w_gate, w_up, w_down, l, tm_s, CAST_COLS)
        else:
            xs = _outproj(xs, oa, ob, w_out_b, None, tm_s, d)
            xs = _ffn(xs, norm_ffn[l], w_gate_b, w_up_b, w_down_b, None, tm_s, FFN_TILE[1])
        sk.append(kn)
        sv.append(vn)
        ss.append(st)
        u = _norm_matmul(xp, norm_mix[l], w_in_b, None, *W_IN_TILE).reshape(bp, seq, in_width)
        oa, st, next_b = _hgrn_prompt(u, lbf_l, oml_l, hgrn_norm[l],
                                      stacked if l + 1 < depth else (), l + 1)
        ob, kn, vn = _swa_prompt(u, sinks[l], qg2[l:l + 1], kg2[l:l + 1], bias_p)
        xp = _outproj(xp, oa.reshape(bp * seq, A_WIDTH), ob.reshape(bp * seq, B_WIDTH), w_out_b, None,
                      W_OUT_ROWS, d)
        xp = _ffn(xp, norm_ffn[l], w_gate_b, w_up_b, w_down_b, None, *FFN_TILE)
        pk.append(kn)
        pv.append(vn)
        ps.append(st)
        if next_b:
            w_in_b, w_out_b, w_gate_b, w_up_b, w_down_b = next_b

    kv_shape = lambda n: (depth, n, WINDOW, B_KV_HEADS, B_HD)
    return (xp.reshape(bp, seq, d),
            xs.reshape(bd, SAMPLE_ROWS, d)[:, :ld],
            jnp.stack(pk).reshape(kv_shape(bp)), jnp.stack(pv).reshape(kv_shape(bp)), jnp.stack(ps),
            jnp.stack(sk).reshape(kv_shape(bd)), jnp.stack(sv).reshape(kv_shape(bd)), jnp.stack(ss))
```

```python
import functools
import math

import jax
import jax.numpy as jnp
import numpy as np
from jax import lax
from jax.experimental import pallas as pl
from jax.experimental.pallas import tpu as pltpu

F32 = jnp.float32
BF16 = jnp.bfloat16

LANES = 128
SUBLANES = 8
VMEM_BYTES_V7X = 64 * 1024 * 1024
VMEM_CAP = VMEM_BYTES_V7X - 8 * 1024 * 1024

A_HEADS = 8
A_DK = 128
A_DV = 128
A_KEY_WIDTH = A_HEADS * A_DK
A_WIDTH = A_HEADS * A_DV
B_HEADS = 16
B_KV_HEADS = 4
B_HD = 64
B_WIDTH = B_HEADS * B_HD
B_KV_WIDTH = B_KV_HEADS * B_HD
WINDOW = 128
ATTN_SCALE = 1.0 / math.sqrt(B_HD)
LOG2E = math.log2(math.e)
NEG_LOGIT = -1e30
LB_FLOOR = 1e-20
N_BUCKETS = 32
MAX_DISTANCE = 128
RMS_EPS = 1e-6

Q_A, F_A, I_A, G_A = 0, A_KEY_WIDTH, 2 * A_KEY_WIDTH, 2 * A_KEY_WIDTH + A_WIDTH
Q_B = 2 * A_KEY_WIDTH + 2 * A_WIDTH
K_B = Q_B + B_WIDTH
V_B = K_B + B_KV_WIDTH

W_IN_TILE = (512, 2816)
W_OUT_ROWS = 512
FFN_TILE = (1024, 512)
CAST_COLS = 512

HGRN_CHUNK = 128
HGRN_BASE = 4
HGRN_HEADS_PER_STEP = 8
SAMPLE_ROWS = SUBLANES
SWA_PROMPT_BLOCKS = 2
SWA_SAMPLE_SEQS = 4
HGRN_SAMPLE_SEQS = 2
KEY_SLOTS = 2 * WINDOW


def _params(sem, nbytes):
    assert nbytes <= VMEM_CAP, nbytes
    return pltpu.CompilerParams(dimension_semantics=sem, vmem_limit_bytes=VMEM_CAP)


def _sig_pair(z):
    th = 0.5 * jnp.tanh(0.5 * z)
    return 0.5 + th, 0.5 - th


def _silu(x):
    return x * _sig_pair(x)[0]


def _rms_rows(x, g):
    ms = jnp.mean(x * x, axis=-1, keepdims=True)
    return x * lax.rsqrt(ms + RMS_EPS) * g


def _split_bf16(x):
    hi = x.astype(BF16)
    lo = (x - hi.astype(F32)).astype(BF16)
    return hi, lo


def _dot(a, b):
    return jnp.dot(a, b, preferred_element_type=F32)


def _dot_nt(a, b):
    return lax.dot_general(a, b, (((1,), (1,)), ((), ())), preferred_element_type=F32)


def _lb_kernel(lb_ref, lbf_ref, oml_ref):
    depth = lb_ref.shape[0]
    rows = [lb_ref[i:i + 1, :] for i in range(depth)]
    m = functools.reduce(jnp.maximum, rows)
    e = [jnp.exp(r - m) for r in rows]
    s = functools.reduce(lambda a, b: a + b, e)
    p = [ei / s for ei in e]
    cum = p[0]
    for i in range(depth):
        if i > 0:
            cum = cum + p[i]
        lb = cum - p[0]
        lbf_ref[i:i + 1, :] = jnp.maximum(lb, LB_FLOOR)
        oml_ref[i:i + 1, :] = 1.0 - lb


def _lower_bounds(lower_bounds):
    shp = jax.ShapeDtypeStruct(lower_bounds.shape, F32)
    return pl.pallas_call(_lb_kernel, out_shape=(shp, shp), name="lb_prep")(lower_bounds.astype(F32))


def _weight_spec(w, layer, block, index):
    if w.ndim == 2:
        return pl.BlockSpec(block, index)
    return pl.BlockSpec((None,) + block, lambda *g: (layer,) + index(*g))


def _bf16_weight(w_ref, copy_refs):
    w = w_ref[...]
    if copy_refs:
        w = w.astype(BF16)
        copy_refs[0][...] = w
    return w


def _norm_matmul_kernel(x_ref, g_ref, w_ref, o_ref, *copy_refs):
    h = _rms_rows(x_ref[...], g_ref[...]).astype(BF16)
    o_ref[...] = _dot(h, _bf16_weight(w_ref, copy_refs))


def _norm_matmul(x, g, w, layer, tm, tn):
    m, d = x.shape
    n = w.shape[-1]
    cast = w.ndim == 3
    assert not cast or m == tm
    wbytes = 4 + 2 if cast else 2
    nbytes = 2 * tm * d * 4 + tm * d * 2 + 2 * d * tn * wbytes + 3 * tm * tn * 4
    u_spec = pl.BlockSpec((tm, tn), lambda j, i: (i, j))
    u_shape = jax.ShapeDtypeStruct((m, n), F32)
    w_block, w_index = (d, tn), lambda j, i: (0, j)
    return pl.pallas_call(
        _norm_matmul_kernel,
        grid=(n // tn, m // tm),
        in_specs=[pl.BlockSpec((tm, d), lambda j, i: (i, 0)),
                  pl.BlockSpec((1, d), lambda j, i: (0, 0)),
                  _weight_spec(w, layer, w_block, w_index)],
        out_specs=(u_spec, pl.BlockSpec(w_block, w_index)) if cast else u_spec,
        out_shape=(u_shape, jax.ShapeDtypeStruct((d, n), BF16)) if cast else u_shape,
        compiler_params=_params(("parallel", "parallel"), nbytes),
        name="norm_matmul",
    )(x, g.reshape(1, d), w)


def _outproj_kernel(x_ref, oa_ref, ob_ref, w_ref, o_ref, *rest):
    ka = oa_ref.shape[1]
    w = _bf16_weight(w_ref, rest)
    o_ref[...] = x_ref[...] + _dot(oa_ref[...], w[0:ka]) + _dot(ob_ref[...], w[ka:])


def _outproj(x, oa, ob, w, layer, tm, tn):
    m, d = x.shape
    ka, kb = oa.shape[1], ob.shape[1]
    cast = w.ndim == 3
    assert not cast or m == tm
    wbytes = 4 + 2 if cast else 2
    nbytes = 4 * tm * tn * 4 + 2 * tm * (ka + kb) * 2 + 2 * (ka + kb) * tn * wbytes
    x_spec = pl.BlockSpec((tm, tn), lambda i, j: (i, j))
    x_shape = jax.ShapeDtypeStruct((m, d), F32)
    w_block, w_index = (ka + kb, tn), lambda i, j: (0, j)
    return pl.pallas_call(
        _outproj_kernel,
        grid=(m // tm, d // tn),
        in_specs=[x_spec,
                  pl.BlockSpec((tm, ka), lambda i, j: (i, 0)),
                  pl.BlockSpec((tm, kb), lambda i, j: (i, 0)),
                  _weight_spec(w, layer, w_block, w_index)],
        out_specs=(x_spec, pl.BlockSpec(w_block, w_index)) if cast else x_spec,
        out_shape=(x_shape, jax.ShapeDtypeStruct((ka + kb, d), BF16)) if cast else x_shape,
        compiler_params=_params(("parallel", "arbitrary"), nbytes),
        name="outproj",
    )(x, oa, ob, w)


def _ffn_kernel(x_ref, g_ref, wg_ref, wu_ref, wd_ref, o_ref, *rest):
    h_ref = rest[-1]
    copies = rest[:-1]

    @pl.when(pl.program_id(1) == 0)
    def _():
        x = x_ref[...]
        h_ref[...] = _rms_rows(x, g_ref[...]).astype(BF16)
        o_ref[...] = x

    h = h_ref[...]
    gate = _dot(h, _bf16_weight(wg_ref, copies[0:1]))
    up = _dot(h, _bf16_weight(wu_ref, copies[1:2]))
    act = (_silu(gate) * up).astype(BF16)
    o_ref[...] += _dot(act, _bf16_weight(wd_ref, copies[2:3]))


def _ffn(x, g, wg, wu, wd, layer, tm, tf):
    m, d = x.shape
    f = wg.shape[-1]
    cast = wg.ndim == 3
    assert not cast or m == tm
    wbytes = 4 + 2 if cast else 2
    nbytes = 4 * tm * d * 4 + tm * d * 2 + 2 * 3 * d * tf * wbytes + 3 * tm * tf * 4
    x_spec = pl.BlockSpec((tm, d), lambda i, j: (i, 0))
    x_shape = jax.ShapeDtypeStruct((m, d), F32)
    col_block, col_index = (d, tf), lambda i, j: (0, j)
    row_block, row_index = (tf, d), lambda i, j: (j, 0)
    copy_specs = (pl.BlockSpec(col_block, col_index), pl.BlockSpec(col_block, col_index),
                  pl.BlockSpec(row_block, row_index))
    copy_shapes = (jax.ShapeDtypeStruct((d, f), BF16), jax.ShapeDtypeStruct((d, f), BF16),
                   jax.ShapeDtypeStruct((f, d), BF16))
    return pl.pallas_call(
        _ffn_kernel,
        grid=(m // tm, f // tf),
        in_specs=[x_spec,
                  pl.BlockSpec((1, d), lambda i, j: (0, 0)),
                  _weight_spec(wg, layer, col_block, col_index),
                  _weight_spec(wu, layer, col_block, col_index),
                  _weight_spec(wd, layer, row_block, row_index)],
        out_specs=(x_spec,) + copy_specs if cast else x_spec,
        out_shape=(x_shape,) + copy_shapes if cast else x_shape,
        scratch_shapes=[pltpu.VMEM((tm, d), BF16)],
        compiler_params=_params(("parallel", "arbitrary"), nbytes),
        name="ffn",
    )(x, g.reshape(1, d), wg, wu, wd)


def _hgrn_static(c):
    assert (HGRN_BASE - 1) * -math.log(LB_FLOOR) / 2 < 80.0
    t = np.arange(c)
    tri = (t[None, :] <= t[:, None]).astype(np.float32)
    first = HGRN_BASE * (t // HGRN_BASE)
    base = tri - 0.5 * (tri[first] + tri[first + HGRN_BASE - 1])
    masks = []
    h = c // 2
    while h >= HGRN_BASE:
        grp = t // (2 * h)
        upper = (t % (2 * h)) >= h
        masks.append((grp[:, None] == grp[None, :]) & upper[:, None] & ~upper[None, :])
        h //= 2
    masks.append((first[:, None] == first[None, :]) & (t[None, :] <= t[:, None]))
    p_all = np.concatenate([tri, base], axis=0)
    p2 = np.concatenate([p_all, p_all], axis=1)
    masks = np.stack(masks).astype(np.float32)
    return p2, np.concatenate([masks, masks], axis=2)


def _hgrn_prompt_kernel(q_ref, f_ref, i_ref, g_ref, lbf_ref, oml_ref, gn_ref, p2_ref, m_ref, *rest,
                        chunk, heads, cast_spans):
    n_cast = len(cast_spans)
    w_refs, (oa_ref, st_ref), wb_refs, s_scr = (rest[:n_cast], rest[n_cast:n_cast + 2],
                                                rest[n_cast + 2:2 * n_cast + 2], rest[-1])
    c_idx = pl.program_id(2)

    @pl.when(c_idx == 0)
    def _():
        s_scr[...] = jnp.zeros_like(s_scr)

    for w_ref, wb_ref, span in zip(w_refs, wb_refs, cast_spans):
        if span == 1:
            wb_ref[...] = w_ref[...].astype(BF16)
        else:
            @pl.when(c_idx % span == 0)
            def _(w_ref=w_ref, wb_ref=wb_ref):
                wb_ref[...] = w_ref[...].astype(BF16)

    width = heads * A_DK
    pw = 2 * A_DK
    sz, snz = _sig_pair(f_ref[0])
    oml = oml_ref[...]
    lf = jnp.log(lbf_ref[...] + oml * sz)
    ka = oml * snz
    hi, lo = _split_bf16(lf)
    expo = _dot(p2_ref[...], jnp.concatenate([hi, lo], axis=0)) * LOG2E
    b = expo[0:chunk]
    q = _silu(q_ref[0])
    v = i_ref[0]
    gate = _silu(g_ref[0])
    gn = gn_ref[...]

    row = lax.broadcasted_iota(jnp.int32, (chunk, pw), 0)
    left = lax.broadcasted_iota(jnp.int32, (chunk, pw), 1) < A_DK

    def bdiag(x):
        z = jnp.zeros_like(x)
        return jnp.concatenate([jnp.where(left, x, z), jnp.where(left, z, x)], axis=0)

    for p in range(heads // 2):
        sl = slice(p * pw, (p + 1) * pw)
        qp, kp, vp, bp = q[:, sl], ka[:, sl], v[:, sl], b[:, sl]
        qs, ks = [], []
        h = chunk // 2
        while h >= HGRN_BASE:
            groups = chunk // (2 * h)
            mid = jnp.broadcast_to(bp.reshape(groups, 2 * h, pw)[:, h - 1:h, :], (groups, 2 * h, pw))
            d = bp - mid.reshape(chunk, pw)
            e = jnp.exp2(jnp.where((row & h) != 0, d, -d))
            qs.append((qp * e).astype(BF16))
            ks.append((kp * e).astype(BF16))
            h //= 2
        e_base = expo[chunk:2 * chunk, sl]
        qs.append((qp * jnp.exp2(e_base)).astype(BF16))
        ks.append((kp * jnp.exp2(-e_base)).astype(BF16))
        b_last = bp[chunk - 1:chunk, :]
        st = s_scr[p]
        o = _dot_nt((qp * jnp.exp2(bp)).astype(BF16), bdiag(st.astype(BF16)))
        a = jnp.zeros((chunk, pw), BF16)
        for li in range(len(qs)):
            a = jnp.where(m_ref[li] != 0, _dot_nt(qs[li], bdiag(ks[li])).astype(BF16), a)
        o = o + _dot(a, bdiag(vp.astype(BF16)))
        vt = jnp.concatenate([vp[:, :A_DV].T, vp[:, A_DV:].T], axis=1).astype(BF16)
        k_tail = (kp * jnp.exp2(b_last - bp)).astype(BF16)
        s_scr[p] = st * jnp.exp2(b_last) + _dot(vt, bdiag(k_tail))
        for hh in range(2):
            hs = slice(p * pw + hh * A_DV, p * pw + (hh + 1) * A_DV)
            oa_ref[0, :, hs] = (_rms_rows(o[:, hh * A_DV:(hh + 1) * A_DV], gn) * gate[:, hs]).astype(BF16)

    @pl.when(c_idx == pl.num_programs(2) - 1)
    def _():
        for hh in range(heads):
            st_ref[0, hh] = s_scr[hh // 2][:, (hh % 2) * A_DK:(hh % 2 + 1) * A_DK].T


def _hgrn_prompt(u3, lbf, oml, gn, cast_weights=(), cast_layer=None):
    bsz, length, _ = u3.shape
    chunk, heads = HGRN_CHUNK, HGRN_HEADS_PER_STEP
    cols = heads * A_DK
    assert length % chunk == 0 and A_HEADS % heads == 0 and heads % 2 == 0 and chunk == A_DV
    p2, masks = _hgrn_static(chunk)
    p2 = jnp.asarray(p2, BF16)
    masks = jnp.asarray(masks, BF16)
    n_chunks = length // chunk
    assert not cast_weights or A_HEADS == heads

    def col_spec(offset):
        base = offset // cols
        return pl.BlockSpec((1, chunk, cols), lambda b, h, c: (b, c, base + h))

    cast_in, cast_out, cast_shape, cast_spans, cast_bytes = [], [], [], [], 0
    for w in cast_weights:
        span = next(s for s in (1, 2, 4) if n_chunks % s == 0 and w.shape[1] % (bsz * n_chunks // s) == 0
                    and (w.shape[1] // (bsz * n_chunks // s)) % (2 * SUBLANES) == 0)
        rows, width = w.shape[1] // (bsz * n_chunks // span), w.shape[2]

        def block(b, h, c, span=span):
            return (b * n_chunks + c) // span

        cast_in.append(pl.BlockSpec((None, rows, width), lambda b, h, c, block=block: (cast_layer, block(b, h, c), 0)))
        cast_out.append(pl.BlockSpec((rows, width), lambda b, h, c, block=block: (block(b, h, c), 0)))
        cast_shape.append(jax.ShapeDtypeStruct(w.shape[1:], BF16))
        cast_spans.append(span)
        cast_bytes += 2 * rows * width * (4 + 2)

    nbytes = (2 * 5 * chunk * cols * 4 + 2 * (p2.size * 2 + masks.size * 4)
              + 6 * p2.shape[0] * cols * 4 + cast_bytes)
    outs = pl.pallas_call(
        functools.partial(_hgrn_prompt_kernel, chunk=chunk, heads=heads, cast_spans=tuple(cast_spans)),
        grid=(bsz, A_HEADS // heads, n_chunks),
        in_specs=[col_spec(Q_A), col_spec(F_A), col_spec(I_A), col_spec(G_A),
                  pl.BlockSpec((1, cols), lambda b, h, c: (0, h)),
                  pl.BlockSpec((1, cols), lambda b, h, c: (0, h)),
                  pl.BlockSpec((1, A_DV), lambda b, h, c: (0, 0)),
                  pl.BlockSpec(p2.shape, lambda b, h, c: (0, 0)),
                  pl.BlockSpec(masks.shape, lambda b, h, c: (0, 0, 0))] + cast_in,
        out_specs=[pl.BlockSpec((1, chunk, cols), lambda b, h, c: (b, c, h)),
                   pl.BlockSpec((1, heads, A_DK, A_DV), lambda b, h, c: (b, h, 0, 0))] + cast_out,
        out_shape=[jax.ShapeDtypeStruct((bsz, length, A_WIDTH), BF16),
                   jax.ShapeDtypeStruct((bsz, A_HEADS, A_DK, A_DV), F32)] + cast_shape,
        scratch_shapes=[pltpu.VMEM((heads // 2, A_DV, 2 * A_DK), F32)],
        compiler_params=_params(("parallel", "parallel", "arbitrary"), nbytes),
        name="hgrn_prompt",
    )(u3, u3, u3, u3, lbf, oml, gn.reshape(1, A_DV), p2, masks, *cast_weights)
    return outs[0], outs[1], outs[2:]


def _hgrn_sample_kernel(u_ref, lbf_ref, oml_ref, gn_ref, s_ref, oa_ref, so_ref, *, steps):
    for i in range(u_ref.shape[0]):
        _hgrn_sample_one(i, u_ref, lbf_ref, oml_ref, gn_ref, s_ref, oa_ref, so_ref, steps)


def _hgrn_sample_one(i, u_ref, lbf_ref, oml_ref, gn_ref, s_ref, oa_ref, so_ref, steps):
    u = u_ref[i]
    q = _silu(u[:, Q_A:Q_A + A_KEY_WIDTH])
    sz, snz = _sig_pair(u[:, F_A:F_A + A_KEY_WIDTH])
    oml = oml_ref[...]
    f = lbf_ref[...] + oml * sz
    ka = oml * snz
    v = u[:, I_A:I_A + A_WIDTH]
    gate = _silu(u[:, G_A:G_A + A_WIDTH])
    gn = gn_ref[...]
    rows = u.shape[0]
    row = lax.broadcasted_iota(jnp.int32, f.shape, 0)
    live = row < steps
    f = jnp.where(live, f, 1.0)
    ka = jnp.where(live, ka, 0.0)

    def down(x, d, fill):
        return x if d == 0 else jnp.where(row >= d, pltpu.roll(x, d, axis=0), fill)

    def up(x, d, fill):
        return jnp.where(row < rows - d, pltpu.roll(x, rows - d, axis=0), fill)

    decay = [jnp.ones_like(f)]
    for d in range(1, steps + 1):
        decay.append(decay[-1] * down(f, d - 1, 1.0))
    head = decay[steps]
    tail = jnp.ones_like(f)
    for d in range(1, steps):
        tail = tail * up(f, d, 1.0)
    q_in = (q * head).astype(BF16)
    k_tail = ka * tail
    total = head[steps - 1:steps, :]
    t_hi = total.astype(BF16).astype(F32)
    t_mid = (total - t_hi).astype(BF16).astype(F32)
    t_lo = (total - t_hi - t_mid).astype(BF16).astype(F32)
    t_rows = jnp.where(row == 0, t_hi, jnp.where(row == 1, t_mid, jnp.where(row == 2, t_lo, 0.0)))
    prods = [q * down(ka, d, 0.0) * decay[d] for d in range(steps)]
    v_sh = [down(v, d, 0.0) for d in range(steps)]

    zsq = jnp.zeros((A_DK, A_DV), BF16)
    zrows = jnp.zeros((rows, A_DV), F32)
    r2 = lax.broadcasted_iota(jnp.int32, (rows, 2 * A_DV), 0)
    l2 = lax.broadcasted_iota(jnp.int32, (rows, 2 * A_DV), 1)
    ones_right = jnp.where((r2 < 3) & (l2 >= A_DV), 1.0, 0.0)

    for p in range(A_HEADS // 2):
        ps = slice(2 * p * A_DK, (2 * p + 2) * A_DK)
        s0 = [s_ref[i, 2 * p + hh] for hh in range(2)]
        s_bd = jnp.concatenate([jnp.concatenate([s0[0].astype(BF16), zsq], axis=1),
                                jnp.concatenate([zsq, s0[1].astype(BF16)], axis=1)], axis=0)
        o_pair = _dot(q_in[:, ps], s_bd)
        for hh in range(2):
            head_i = 2 * p + hh
            sl = slice(head_i * A_DK, (head_i + 1) * A_DK)
            o = o_pair[:, hh * A_DV:(hh + 1) * A_DV]
            for d in range(steps):
                o = o + jnp.sum(prods[d][:, sl], axis=-1, keepdims=True) * v_sh[d][:, sl]
            lhs = jnp.concatenate([k_tail[:, sl], t_rows[:, sl]], axis=0).astype(BF16)
            rhs = jnp.concatenate([jnp.concatenate([v[:, sl], zrows], axis=1), ones_right],
                                  axis=0).astype(BF16)
            upd = lax.dot_general(lhs, rhs, (((0,), (0,)), ((), ())), preferred_element_type=F32)
            so_ref[i, head_i] = s0[hh] * upd[:, A_DV:] + upd[:, :A_DV]
            oa_ref[i, :, sl] = (_rms_rows(o, gn) * gate[:, sl]).astype(BF16)


def _hgrn_sample(u3, lbf, oml, gn, state, layer, steps):
    bsz, rows, _ = u3.shape
    width = G_A + A_WIDTH
    bt = HGRN_SAMPLE_SEQS
    assert bsz % bt == 0
    nbytes = bt * (2 * rows * width * 4 + 4 * A_HEADS * A_DK * A_DV * 4)
    return pl.pallas_call(
        functools.partial(_hgrn_sample_kernel, steps=steps),
        grid=(bsz // bt,),
        in_specs=[pl.BlockSpec((bt, rows, width), lambda b: (b, 0, 0)),
                  pl.BlockSpec((1, A_KEY_WIDTH), lambda b: (0, 0)),
                  pl.BlockSpec((1, A_KEY_WIDTH), lambda b: (0, 0)),
                  pl.BlockSpec((1, A_DV), lambda b: (0, 0)),
                  pl.BlockSpec((None, bt, A_HEADS, A_DK, A_DV), lambda b: (layer, b, 0, 0, 0))],
        out_specs=(pl.BlockSpec((bt, rows, A_WIDTH), lambda b: (b, 0, 0)),
                   pl.BlockSpec((bt, A_HEADS, A_DK, A_DV), lambda b: (b, 0, 0, 0))),
        out_shape=(jax.ShapeDtypeStruct((bsz, rows, A_WIDTH), BF16),
                   jax.ShapeDtypeStruct(state.shape[1:], F32)),
        compiler_params=_params(("parallel",), nbytes),
        name="hgrn_sample",
    )(u3, lbf, oml, gn.reshape(1, A_DV), state)


def _t5_bucket(dist):
    n = np.maximum(dist, 0)
    max_exact = N_BUCKETS // 2
    nf = np.maximum(n, 1).astype(np.float32)
    large = max_exact + (np.log(nf / max_exact) / math.log(MAX_DISTANCE / max_exact)
                         * (N_BUCKETS - max_exact)).astype(np.int32)
    large = np.minimum(large, N_BUCKETS - 1)
    return np.where(n < max_exact, n, large).astype(np.int32)


def _bias_kernel(tab_ref, bucket_ref, valid_ref, o_ref):
    rows = bucket_ref.shape[0]
    bucket = bucket_ref[...]
    hits = [bucket == b for b in range(N_BUCKETS)]
    for h in range(B_HEADS):
        acc = jnp.zeros(bucket.shape, F32)
        for b in range(N_BUCKETS):
            acc = jnp.where(hits[b], tab_ref[b, h], acc)
        acc = acc * LOG2E
        j, r, c = h // 4, (h % 4) // 2, h % 2
        for m in range(valid_ref.shape[0]):
            o_ref[m, j, r * rows:(r + 1) * rows, c * KEY_SLOTS:(c + 1) * KEY_SLOTS] = jnp.where(
                valid_ref[m] != 0, acc, NEG_LOGIT * LOG2E)


def _bias_layout(table, dist, valids):
    rows = dist.shape[0]
    return pl.pallas_call(
        _bias_kernel,
        in_specs=[pl.BlockSpec(memory_space=pltpu.SMEM),
                  pl.BlockSpec(memory_space=pltpu.VMEM),
                  pl.BlockSpec(memory_space=pltpu.VMEM)],
        out_shape=jax.ShapeDtypeStruct((valids.shape[0], B_KV_HEADS, 2 * rows, 2 * KEY_SLOTS), F32),
        name="bias_prep",
    )(table.astype(F32), jnp.asarray(_t5_bucket(dist), jnp.int32), jnp.asarray(valids, jnp.int32))


def _block_diag_ones():
    r = lax.broadcasted_iota(jnp.int32, (LANES, LANES), 0) // B_HD
    c = lax.broadcasted_iota(jnp.int32, (LANES, LANES), 1) // B_HD
    return (r == c).astype(BF16)


def _head_norm(x, g2, bd):
    hi, lo = _split_bf16(x * x)
    ss = _dot(hi, bd) + _dot(lo, bd)
    return x * lax.rsqrt(ss * (1.0 / B_HD) + RMS_EPS) * g2


def _place(tile, half, lo_mask):
    rolled = pltpu.roll(tile, B_HD, axis=1)
    zero = jnp.zeros_like(tile)
    if half == 0:
        return jnp.where(lo_mask, tile, zero), jnp.where(lo_mask, zero, rolled)
    return jnp.where(lo_mask, rolled, zero), jnp.where(lo_mask, zero, tile)


def _swa_core(problems, bias_at, sink_ref):
    rows = problems[0][0][0].shape[0]
    lo_mask = lax.broadcasted_iota(jnp.int32, problems[0][1][0].shape, 1) < B_HD
    out_lo = lax.broadcasted_iota(jnp.int32, (rows, LANES), 1) < B_HD
    units = [(p, j) for p in range(len(problems)) for j in range(B_KV_HEADS)]
    scores, values = {}, {}
    for p, j in units:
        qn_tiles, k_tiles, v_tiles, _ = problems[p]
        t, half = j // 2, j % 2
        k_lo, k_hi = _place(k_tiles[t], half, lo_mask)
        v_lo, v_hi = _place(v_tiles[t], half, lo_mask)
        kk = jnp.concatenate([k_lo, k_hi], axis=0).astype(BF16)
        values[p, j] = jnp.concatenate([v_lo, v_hi], axis=0).astype(BF16)
        qq = jnp.concatenate([qn_tiles[2 * j], qn_tiles[2 * j + 1]], axis=0).astype(BF16)
        scores[p, j] = _dot_nt(qq, kk) + bias_at(p, j)
    weights, scales = {}, {}
    for p, j in units:
        e_rows, inv = [], []
        for r in range(2):
            e_cols, inv_r = [], []
            for c in range(2):
                sk = sink_ref[4 * j + 2 * r + c] * LOG2E
                sb = scores[p, j][r * rows:(r + 1) * rows, c * KEY_SLOTS:(c + 1) * KEY_SLOTS]
                m = jnp.maximum(jnp.max(sb, axis=-1, keepdims=True), sk)
                e = jnp.exp2(sb - m)
                inv_r.append(1.0 / (jnp.sum(e, axis=-1, keepdims=True) + jnp.exp2(sk - m)))
                e_cols.append(e.astype(BF16))
            e_rows.append(jnp.concatenate(e_cols, axis=1))
            inv.append(jnp.where(out_lo, inv_r[0], inv_r[1]))
        weights[p, j] = jnp.concatenate(e_rows, axis=0)
        scales[p, j] = inv
    for p, j in units:
        o = _dot(weights[p, j], values[p, j])
        for r in range(2):
            problems[p][3](2 * j + r, (o[r * rows:(r + 1) * rows] * scales[p, j][r]).astype(BF16))


def _swa_prompt_kernel(sink_ref, q_ref, kc_ref, kp_ref, vc_ref, vp_ref, qg_ref, kg_ref, bias_ref,
                       ob_ref, ko_ref, vo_ref):
    m = pl.program_id(1)
    blocks = q_ref.shape[1] // WINDOW
    bd = _block_diag_ones()
    k_all = jnp.concatenate([kp_ref[0], kc_ref[0]], axis=0)
    v_all = jnp.concatenate([vp_ref[0], vc_ref[0]], axis=0)
    kg = kg_ref[...]
    k_norm = [_head_norm(k_all[:, t * LANES:(t + 1) * LANES], kg, bd) for t in range(2)]
    q = q_ref[0]
    qg = qg_ref[...]
    qn = [_head_norm(q[:, t * LANES:(t + 1) * LANES], qg, bd) for t in range(B_WIDTH // LANES)]
    first = jnp.minimum(m, 1)

    problems, biases = [], []
    for i in range(blocks):
        rows = slice(i * WINDOW, (i + 1) * WINDOW)
        keys = slice(i * WINDOW, i * WINDOW + KEY_SLOTS)

        def store(tile, val, rows=rows):
            ob_ref[0, rows, tile * LANES:(tile + 1) * LANES] = val

        problems.append(([qt[rows] for qt in qn], [kt[keys] for kt in k_norm],
                         [v_all[keys, t * LANES:(t + 1) * LANES] for t in range(2)], store))
        biases.append(first if i == 0 else 1)
    _swa_core(problems, lambda p, j: bias_ref[biases[p], j], sink_ref)

    @pl.when(m == pl.num_programs(1) - 1)
    def _():
        ko_ref[0] = jnp.concatenate([kt[blocks * WINDOW:] for kt in k_norm], axis=1)
        vo_ref[0] = vc_ref[0, (blocks - 1) * WINDOW:, :]


def _prompt_dist():
    return WINDOW + np.arange(WINDOW)[:, None] - np.arange(KEY_SLOTS)[None, :]


def _prompt_valid():
    j = np.arange(KEY_SLOTS)[None, :]
    dist = _prompt_dist()
    cur = (j >= WINDOW) & (dist >= 0)
    prev = (j < WINDOW) & (dist < WINDOW)
    return np.stack([cur, cur | prev]).astype(np.int32)


def _swa_prompt(u3, sinks, qg, kg, bias):
    bsz, length, _ = u3.shape
    blocks = SWA_PROMPT_BLOCKS
    rows = blocks * WINDOW
    assert length % rows == 0
    qb, kb, vb = Q_B // B_WIDTH, K_B // B_KV_WIDTH, V_B // B_KV_WIDTH
    assert Q_B % B_WIDTH == 0 and K_B % B_KV_WIDTH == 0 and V_B % B_KV_WIDTH == 0

    def kv_spec(col, prev):
        if prev:
            return pl.BlockSpec((1, WINDOW, B_KV_WIDTH), lambda b, m: (b, jnp.maximum(m * blocks - 1, 0), col))
        return pl.BlockSpec((1, rows, B_KV_WIDTH), lambda b, m: (b, m, col))

    g_spec = pl.BlockSpec((1, LANES), lambda b, m: (0, 0))
    nbytes = (2 * rows * (B_WIDTH + 4 * B_KV_WIDTH) * 4 + 2 * bias.size * 4
              + 2 * rows * B_WIDTH * 2 + blocks * 24 * WINDOW * 2 * KEY_SLOTS * 4)
    return pl.pallas_call(
        _swa_prompt_kernel,
        grid=(bsz, length // rows),
        in_specs=[pl.BlockSpec(memory_space=pltpu.SMEM),
                  pl.BlockSpec((1, rows, B_WIDTH), lambda b, m: (b, m, qb)),
                  kv_spec(kb, False), kv_spec(kb, True), kv_spec(vb, False), kv_spec(vb, True),
                  g_spec, g_spec,
                  pl.BlockSpec(bias.shape, lambda b, m: (0, 0, 0, 0))],
        out_specs=(pl.BlockSpec((1, rows, B_WIDTH), lambda b, m: (b, m, 0)),
                   pl.BlockSpec((1, WINDOW, B_KV_WIDTH), lambda b, n: (b, 0, 0)),
                   pl.BlockSpec((1, WINDOW, B_KV_WIDTH), lambda b, n: (b, 0, 0))),
        out_shape=(jax.ShapeDtypeStruct((bsz, length, B_WIDTH), BF16),
                   jax.ShapeDtypeStruct((bsz, WINDOW, B_KV_WIDTH), F32),
                   jax.ShapeDtypeStruct((bsz, WINDOW, B_KV_WIDTH), F32)),
        compiler_params=_params(("parallel", "arbitrary"), nbytes),
        name="swa_prompt",
    )(sinks, u3, u3, u3, u3, u3, qg, kg, bias)


def _swa_sample_kernel(sink_ref, q_ref, kv_ref, ck_ref, cv_ref, qg_ref, kg_ref, bias_ref,
                       ob_ref, ko_ref, vo_ref, *, steps):
    bd = _block_diag_ones()
    rows = q_ref.shape[1]
    kg = kg_ref[...]
    qg = qg_ref[...]
    pad = jnp.zeros((KEY_SLOTS - WINDOW - rows, LANES), F32)
    problems = []
    for i in range(q_ref.shape[0]):
        kv = kv_ref[i]
        k_new = [_head_norm(kv[:, t * LANES:(t + 1) * LANES], kg, bd) for t in range(2)]
        v_new = [kv[:, B_KV_WIDTH + t * LANES:B_KV_WIDTH + (t + 1) * LANES] for t in range(2)]
        ck, cv = ck_ref[i], cv_ref[i]
        k_tiles = [jnp.concatenate([ck[:, t * LANES:(t + 1) * LANES], k_new[t], pad], axis=0)
                   for t in range(2)]
        v_tiles = [jnp.concatenate([cv[:, t * LANES:(t + 1) * LANES], v_new[t], pad], axis=0)
                   for t in range(2)]
        q = q_ref[i]
        qn_tiles = [_head_norm(q[:, t * LANES:(t + 1) * LANES], qg, bd) for t in range(B_WIDTH // LANES)]

        def store(tile, val, i=i):
            ob_ref[i, :, tile * LANES:(tile + 1) * LANES] = val

        problems.append((qn_tiles, k_tiles, v_tiles, store))
        ko_ref[i, 0:WINDOW - steps, :] = ck_ref[i, steps:WINDOW, :]
        vo_ref[i, 0:WINDOW - steps, :] = cv_ref[i, steps:WINDOW, :]
        ko_ref[i, WINDOW - steps:WINDOW, :] = jnp.concatenate([kt[0:steps] for kt in k_new], axis=1)
        vo_ref[i, WINDOW - steps:WINDOW, :] = kv[0:steps, B_KV_WIDTH:]
    _swa_core(problems, lambda p, j: bias_ref[0, j], sink_ref)


def _sample_dist(rows):
    return WINDOW + np.arange(rows)[:, None] - np.arange(KEY_SLOTS)[None, :]


def _sample_valid(rows, steps):
    dist = _sample_dist(rows)
    j = np.arange(KEY_SLOTS)[None, :]
    return ((dist >= 0) & (dist < WINDOW) & (j < WINDOW + steps)).astype(np.int32)[None]


def _swa_sample(u3, cache_k, cache_v, layer, sinks, qg, kg, bias, steps):
    bsz, rows, _ = u3.shape
    w = cache_k.shape[2]
    bt = SWA_SAMPLE_SEQS
    assert w == WINDOW and bsz % bt == 0
    qb, kvb = Q_B // B_WIDTH, K_B // (2 * B_KV_WIDTH)
    assert K_B % (2 * B_KV_WIDTH) == 0
    c_in_spec = pl.BlockSpec((None, bt, w, B_KV_WIDTH), lambda b: (layer, b, 0, 0))
    c_spec = pl.BlockSpec((bt, w, B_KV_WIDTH), lambda b: (b, 0, 0))
    g_spec = pl.BlockSpec((1, LANES), lambda b: (0, 0))
    nbytes = bt * 8 * w * B_KV_WIDTH * 4 + 2 * bias.size * 4 + bt * 16 * KEY_SLOTS * LANES * 4
    return pl.pallas_call(
        functools.partial(_swa_sample_kernel, steps=steps),
        grid=(bsz // bt,),
        in_specs=[pl.BlockSpec(memory_space=pltpu.SMEM),
                  pl.BlockSpec((bt, rows, B_WIDTH), lambda b: (b, 0, qb)),
                  pl.BlockSpec((bt, rows, 2 * B_KV_WIDTH), lambda b: (b, 0, kvb)),
                  c_in_spec, c_in_spec, g_spec, g_spec,
                  pl.BlockSpec(bias.shape, lambda b: (0, 0, 0, 0))],
        out_specs=(pl.BlockSpec((bt, rows, B_WIDTH), lambda b: (b, 0, 0)), c_spec, c_spec),
        out_shape=(jax.ShapeDtypeStruct((bsz, rows, B_WIDTH), BF16),
                   jax.ShapeDtypeStruct((bsz, w, B_KV_WIDTH), F32),
                   jax.ShapeDtypeStruct((bsz, w, B_KV_WIDTH), F32)),
        compiler_params=_params(("parallel",), nbytes),
        name="swa_sample",
    )(sinks, u3, u3, cache_k, cache_v, qg, kg, bias)


def kernel(x_prompt, x_sample, cache_k, cache_v, state_hgrn, norm_mix, w_in, lower_bounds, hgrn_norm,
           q_norm, k_norm, attn_sinks, rel_bias_table, w_out, norm_ffn, w_gate, w_up, w_down):
    depth = w_in.shape[0]
    bp, seq, d = x_prompt.shape
    bd, ld, _ = x_sample.shape
    w = cache_k.shape[2]
    in_width = w_in.shape[2]

    lbf, oml = _lower_bounds(lower_bounds)
    sinks = attn_sinks.astype(F32)
    qg2 = jnp.tile(q_norm.astype(F32), (1, LANES // B_HD)) * (ATTN_SCALE * LOG2E)
    kg2 = jnp.tile(k_norm.astype(F32), (1, LANES // B_HD))
    ck = cache_k.astype(F32).reshape(depth, bd, w, B_KV_WIDTH)
    cv = cache_v.astype(F32).reshape(depth, bd, w, B_KV_WIDTH)

    xp = x_prompt.reshape(bp * seq, d)
    xs = jnp.pad(x_sample, ((0, 0), (0, SAMPLE_ROWS - ld), (0, 0))).reshape(bd * SAMPLE_ROWS, d)
    tm_s = bd * SAMPLE_ROWS
    bias_p = _bias_layout(rel_bias_table, _prompt_dist(), _prompt_valid())
    bias_s = _bias_layout(rel_bias_table, _sample_dist(SAMPLE_ROWS), _sample_valid(SAMPLE_ROWS, ld))
    state = state_hgrn.astype(F32)

    stacked = (w_in, w_out, w_gate, w_up, w_down)
    w_in_b = w_out_b = w_gate_b = w_up_b = w_down_b = None
    pk, pv, ps, sk, sv, ss = [], [], [], [], [], []
    for l in range(depth):
        lbf_l, oml_l = lbf[l:l + 1], oml[l:l + 1]
        if w_in_b is None:
            u, w_in_b = _norm_matmul(xs, norm_mix[l], w_in, l, tm_s, CAST_COLS)
        else:
            u = _norm_matmul(xs, norm_mix[l], w_in_b, None, tm_s, W_IN_TILE[1] // 2)
        u = u.reshape(bd, SAMPLE_ROWS, in_width)
        oa, st = _hgrn_sample(u, lbf_l, oml_l, hgrn_norm[l], state, l, ld)
        ob, kn, vn = _swa_sample(u, ck, cv, l, sinks[l], qg2[l:l + 1], kg2[l:l + 1], bias_s, ld)
        oa, ob = oa.reshape(tm_s, A_WIDTH), ob.reshape(tm_s, B_WIDTH)
        if w_out_b is None:
            xs, w_out_b = _outproj(xs, oa, ob, w_out, l, tm_s, CAST_COLS)
            xs, w_gate_b, w_up_b, w_down_b = _ffn(xs, norm_ffn[l], w_gate, w_up, w_down, l, tm_s, CAST_COLS)
        else:
            xs = _outproj(xs, oa, ob, w_out_b, None, tm_s, d)
            xs = _ffn(xs, norm_ffn[l], w_gate_b, w_up_b, w_down_b, None, tm_s, FFN_TILE[1])
        sk.append(kn)
        sv.append(vn)
        ss.append(st)
        u = _norm_matmul(xp, norm_mix[l], w_in_b, None, *W_IN_TILE).reshape(bp, seq, in_width)
        oa, st, next_b = _hgrn_prompt(u, lbf_l, oml_l, hgrn_norm[l],
                                      stacked if l + 1 < depth else (), l + 1)
        ob, kn, vn = _swa_prompt(u, sinks[l], qg2[l:l + 1], kg2[l:l + 1], bias_p)
        xp = _outproj(xp, oa.reshape(bp * seq, A_WIDTH), ob.reshape(bp * seq, B_WIDTH), w_out_b, None,
                      W_OUT_ROWS, d)
        xp = _ffn(xp, norm_ffn[l], w_gate_b, w_up_b, w_down_b, None, *FFN_TILE)
        pk.append(kn)
        pv.append(vn)
        ps.append(st)
        if next_b:
            w_in_b, w_out_b, w_gate_b, w_up_b, w_down_b = next_b

    kv_shape = lambda n: (depth, n, WINDOW, B_KV_HEADS, B_HD)
    return (xp.reshape(bp, seq, d),
            xs.reshape(bd, SAMPLE_ROWS, d)[:, :ld],
            jnp.stack(pk).reshape(kv_shape(bp)), jnp.stack(pv).reshape(kv_shape(bp)), jnp.stack(ps),
            jnp.stack(sk).reshape(kv_shape(bd)), jnp.stack(sv).reshape(kv_shape(bd)), jnp.stack(ss))
```

```python
import functools
import math

import jax
import jax.numpy as jnp
import numpy as np
from jax import lax
from jax.experimental import pallas as pl
from jax.experimental.pallas import tpu as pltpu

F32 = jnp.float32
BF16 = jnp.bfloat16

LANES = 128
SUBLANES = 8
VMEM_BYTES_V7X = 64 * 1024 * 1024
VMEM_CAP = VMEM_BYTES_V7X - 8 * 1024 * 1024

A_HEADS = 8
A_DK = 128
A_DV = 128
A_KEY_WIDTH = A_HEADS * A_DK
A_WIDTH = A_HEADS * A_DV
B_HEADS = 16
B_KV_HEADS = 4
B_HD = 64
B_WIDTH = B_HEADS * B_HD
B_KV_WIDTH = B_KV_HEADS * B_HD
WINDOW = 128
ATTN_SCALE = 1.0 / math.sqrt(B_HD)
LOG2E = math.log2(math.e)
NEG_LOGIT = -1e30
LB_FLOOR = 1e-20
N_BUCKETS = 32
MAX_DISTANCE = 128
RMS_EPS = 1e-6

Q_A, F_A, I_A, G_A = 0, A_KEY_WIDTH, 2 * A_KEY_WIDTH, 2 * A_KEY_WIDTH + A_WIDTH
Q_B = 2 * A_KEY_WIDTH + 2 * A_WIDTH
K_B = Q_B + B_WIDTH
V_B = K_B + B_KV_WIDTH

W_IN_TILE = (512, 2816)
W_OUT_ROWS = 512
FFN_TILE = (1024, 512)
CAST_COLS = 512

HGRN_CHUNK = 128
HGRN_BASE = 4
HGRN_HEADS_PER_STEP = 8
SAMPLE_ROWS = SUBLANES
SWA_PROMPT_BLOCKS = 2
SWA_SAMPLE_SEQS = 4
HGRN_SAMPLE_SEQS = 2
KEY_SLOTS = 2 * WINDOW


def _params(sem, nbytes):
    assert nbytes <= VMEM_CAP, nbytes
    return pltpu.CompilerParams(dimension_semantics=sem, vmem_limit_bytes=VMEM_CAP)


def _sig_pair(z):
    th = 0.5 * jnp.tanh(0.5 * z)
    return 0.5 + th, 0.5 - th


def _silu(x):
    return x * _sig_pair(x)[0]


def _rms_rows(x, g):
    ms = jnp.mean(x * x, axis=-1, keepdims=True)
    return x * lax.rsqrt(ms + RMS_EPS) * g


def _split_bf16(x):
    hi = x.astype(BF16)
    lo = (x - hi.astype(F32)).astype(BF16)
    return hi, lo


def _dot(a, b):
    return jnp.dot(a, b, preferred_element_type=F32)


def _dot_nt(a, b):
    return lax.dot_general(a, b, (((1,), (1,)), ((), ())), preferred_element_type=F32)


def _lb_kernel(lb_ref, lbf_ref, oml_ref):
    depth = lb_ref.shape[0]
    rows = [lb_ref[i:i + 1, :] for i in range(depth)]
    m = functools.reduce(jnp.maximum, rows)
    e = [jnp.exp(r - m) for r in rows]
    s = functools.reduce(lambda a, b: a + b, e)
    p = [ei / s for ei in e]
    cum = p[0]
    for i in range(depth):
        if i > 0:
            cum = cum + p[i]
        lb = cum - p[0]
        lbf_ref[i:i + 1, :] = jnp.maximum(lb, LB_FLOOR)
        oml_ref[i:i + 1, :] = 1.0 - lb


def _lower_bounds(lower_bounds):
    shp = jax.ShapeDtypeStruct(lower_bounds.shape, F32)
    return pl.pallas_call(_lb_kernel, out_shape=(shp, shp), name="lb_prep")(lower_bounds.astype(F32))


def _weight_spec(w, layer, block, index):
    if w.ndim == 2:
        return pl.BlockSpec(block, index)
    return pl.BlockSpec((None,) + block, lambda *g: (layer,) + index(*g))


def _bf16_weight(w_ref, copy_refs):
    w = w_ref[...]
    if copy_refs:
        w = w.astype(BF16)
        copy_refs[0][...] = w
    return w


def _norm_matmul_kernel(x_ref, g_ref, w_ref, o_ref, *copy_refs):
    h = _rms_rows(x_ref[...], g_ref[...]).astype(BF16)
    o_ref[...] = _dot(h, _bf16_weight(w_ref, copy_refs))


def _norm_matmul(x, g, w, layer, tm, tn):
    m, d = x.shape
    n = w.shape[-1]
    cast = w.ndim == 3
    assert not cast or m == tm
    wbytes = 4 + 2 if cast else 2
    nbytes = 2 * tm * d * 4 + tm * d * 2 + 2 * d * tn * wbytes + 3 * tm * tn * 4
    u_spec = pl.BlockSpec((tm, tn), lambda j, i: (i, j))
    u_shape = jax.ShapeDtypeStruct((m, n), F32)
    w_block, w_index = (d, tn), lambda j, i: (0, j)
    return pl.pallas_call(
        _norm_matmul_kernel,
        grid=(n // tn, m // tm),
        in_specs=[pl.BlockSpec((tm, d), lambda j, i: (i, 0)),
                  pl.BlockSpec((1, d), lambda j, i: (0, 0)),
                  _weight_spec(w, layer, w_block, w_index)],
        out_specs=(u_spec, pl.BlockSpec(w_block, w_index)) if cast else u_spec,
        out_shape=(u_shape, jax.ShapeDtypeStruct((d, n), BF16)) if cast else u_shape,
        compiler_params=_params(("parallel", "parallel"), nbytes),
        name="norm_matmul",
    )(x, g.reshape(1, d), w)


def _outproj_kernel(x_ref, oa_ref, ob_ref, w_ref, o_ref, *rest):
    ka = oa_ref.shape[1]
    w = _bf16_weight(w_ref, rest)
    o_ref[...] = x_ref[...] + _dot(oa_ref[...], w[0:ka]) + _dot(ob_ref[...], w[ka:])


def _outproj(x, oa, ob, w, layer, tm, tn):
    m, d = x.shape
    ka, kb = oa.shape[1], ob.shape[1]
    cast = w.ndim == 3
    assert not cast or m == tm
    wbytes = 4 + 2 if cast else 2
    nbytes = 4 * tm * tn * 4 + 2 * tm * (ka + kb) * 2 + 2 * (ka + kb) * tn * wbytes
    x_spec = pl.BlockSpec((tm, tn), lambda i, j: (i, j))
    x_shape = jax.ShapeDtypeStruct((m, d), F32)
    w_block, w_index = (ka + kb, tn), lambda i, j: (0, j)
    return pl.pallas_call(
        _outproj_kernel,
        grid=(m // tm, d // tn),
        in_specs=[x_spec,
                  pl.BlockSpec((tm, ka), lambda i, j: (i, 0)),
                  pl.BlockSpec((tm, kb), lambda i, j: (i, 0)),
                  _weight_spec(w, layer, w_block, w_index)],
        out_specs=(x_spec, pl.BlockSpec(w_block, w_index)) if cast else x_spec,
        out_shape=(x_shape, jax.ShapeDtypeStruct((ka + kb, d), BF16)) if cast else x_shape,
        compiler_params=_params(("parallel", "arbitrary"), nbytes),
        name="outproj",
    )(x, oa, ob, w)


def _ffn_kernel(x_ref, g_ref, wg_ref, wu_ref, wd_ref, o_ref, *rest):
    h_ref = rest[-1]
    copies = rest[:-1]

    @pl.when(pl.program_id(1) == 0)
    def _():
        x = x_ref[...]
        h_ref[...] = _rms_rows(x, g_ref[...]).astype(BF16)
        o_ref[...] = x

    h = h_ref[...]
    gate = _dot(h, _bf16_weight(wg_ref, copies[0:1]))
    up = _dot(h, _bf16_weight(wu_ref, copies[1:2]))
    act = (_silu(gate) * up).astype(BF16)
    o_ref[...] += _dot(act, _bf16_weight(wd_ref, copies[2:3]))


def _ffn(x, g, wg, wu, wd, layer, tm, tf):
    m, d = x.shape
    f = wg.shape[-1]
    cast = wg.ndim == 3
    assert not cast or m == tm
    wbytes = 4 + 2 if cast else 2
    nbytes = 4 * tm * d * 4 + tm * d * 2 + 2 * 3 * d * tf * wbytes + 3 * tm * tf * 4
    x_spec = pl.BlockSpec((tm, d), lambda i, j: (i, 0))
    x_shape = jax.ShapeDtypeStruct((m, d), F32)
    col_block, col_index = (d, tf), lambda i, j: (0, j)
    row_block, row_index = (tf, d), lambda i, j: (j, 0)
    copy_specs = (pl.BlockSpec(col_block, col_index), pl.BlockSpec(col_block, col_index),
                  pl.BlockSpec(row_block, row_index))
    copy_shapes = (jax.ShapeDtypeStruct((d, f), BF16), jax.ShapeDtypeStruct((d, f), BF16),
                   jax.ShapeDtypeStruct((f, d), BF16))
    return pl.pallas_call(
        _ffn_kernel,
        grid=(m // tm, f // tf),
        in_specs=[x_spec,
                  pl.BlockSpec((1, d), lambda i, j: (0, 0)),
                  _weight_spec(wg, layer, col_block, col_index),
                  _weight_spec(wu, layer, col_block, col_index),
                  _weight_spec(wd, layer, row_block, row_index)],
        out_specs=(x_spec,) + copy_specs if cast else x_spec,
        out_shape=(x_shape,) + copy_shapes if cast else x_shape,
        scratch_shapes=[pltpu.VMEM((tm, d), BF16)],
        compiler_params=_params(("parallel", "arbitrary"), nbytes),
        name="ffn",
    )(x, g.reshape(1, d), wg, wu, wd)


def _hgrn_static(c):
    assert (HGRN_BASE - 1) * -math.log(LB_FLOOR) / 2 < 80.0
    t = np.arange(c)
    tri = (t[None, :] <= t[:, None]).astype(np.float32)
    first = HGRN_BASE * (t // HGRN_BASE)
    base = tri - 0.5 * (tri[first] + tri[first + HGRN_BASE - 1])
    masks = []
    h = c // 2
    while h >= HGRN_BASE:
        grp = t // (2 * h)
        upper = (t % (2 * h)) >= h
        masks.append((grp[:, None] == grp[None, :]) & upper[:, None] & ~upper[None, :])
        h //= 2
    masks.append((first[:, None] == first[None, :]) & (t[None, :] <= t[:, None]))
    p_all = np.concatenate([tri, base], axis=0)
    p2 = np.concatenate([p_all, p_all], axis=1)
    masks = np.stack(masks).astype(np.float32)
    return p2, np.concatenate([masks, masks], axis=2)


def _hgrn_prompt_kernel(q_ref, f_ref, i_ref, g_ref, lbf_ref, oml_ref, gn_ref, p2_ref, m_ref, *rest,
                        chunk, heads, cast_spans):
    n_cast = len(cast_spans)
    w_refs, (oa_ref, st_ref), wb_refs, s_scr = (rest[:n_cast], rest[n_cast:n_cast + 2],
                                                rest[n_cast + 2:2 * n_cast + 2], rest[-1])
    c_idx = pl.program_id(2)

    @pl.when(c_idx == 0)
    def _():
        s_scr[...] = jnp.zeros_like(s_scr)

    for w_ref, wb_ref, span in zip(w_refs, wb_refs, cast_spans):
        if span == 1:
            wb_ref[...] = w_ref[...].astype(BF16)
        else:
            @pl.when(c_idx % span == 0)
            def _(w_ref=w_ref, wb_ref=wb_ref):
                wb_ref[...] = w_ref[...].astype(BF16)

    width = heads * A_DK
    pw = 2 * A_DK
    sz, snz = _sig_pair(f_ref[0])
    oml = oml_ref[...]
    lf = jnp.log(lbf_ref[...] + oml * sz)
    ka = oml * snz
    hi, lo = _split_bf16(lf)
    expo = _dot(p2_ref[...], jnp.concatenate([hi, lo], axis=0)) * LOG2E
    b = expo[0:chunk]
    q = _silu(q_ref[0])
    v = i_ref[0]
    gate = _silu(g_ref[0])
    gn = gn_ref[...]

    row = lax.broadcasted_iota(jnp.int32, (chunk, pw), 0)
    left = lax.broadcasted_iota(jnp.int32, (chunk, pw), 1) < A_DK

    def bdiag(x):
        z = jnp.zeros_like(x)
        return jnp.concatenate([jnp.where(left, x, z), jnp.where(left, z, x)], axis=0)

    for p in range(heads // 2):
        sl = slice(p * pw, (p + 1) * pw)
        qp, kp, vp, bp = q[:, sl], ka[:, sl], v[:, sl], b[:, sl]
        qs, ks = [], []
        h = chunk // 2
        while h >= HGRN_BASE:
            groups = chunk // (2 * h)
            mid = jnp.broadcast_to(bp.reshape(groups, 2 * h, pw)[:, h - 1:h, :], (groups, 2 * h, pw))
            d = bp - mid.reshape(chunk, pw)
            e = jnp.exp2(jnp.where((row & h) != 0, d, -d))
            qs.append((qp * e).astype(BF16))
            ks.append((kp * e).astype(BF16))
            h //= 2
        e_base = expo[chunk:2 * chunk, sl]
        qs.append((qp * jnp.exp2(e_base)).astype(BF16))
        ks.append((kp * jnp.exp2(-e_base)).astype(BF16))
        b_last = bp[chunk - 1:chunk, :]
        st = s_scr[p]
        o = _dot_nt((qp * jnp.exp2(bp)).astype(BF16), bdiag(st.astype(BF16)))
        a = jnp.zeros((chunk, pw), BF16)
        for li in range(len(qs)):
            a = jnp.where(m_ref[li] != 0, _dot_nt(qs[li], bdiag(ks[li])).astype(BF16), a)
        o = o + _dot(a, bdiag(vp.astype(BF16)))
        vt = jnp.concatenate([vp[:, :A_DV].T, vp[:, A_DV:].T], axis=1).astype(BF16)
        k_tail = (kp * jnp.exp2(b_last - bp)).astype(BF16)
        s_scr[p] = st * jnp.exp2(b_last) + _dot(vt, bdiag(k_tail))
        for hh in range(2):
            hs = slice(p * pw + hh * A_DV, p * pw + (hh + 1) * A_DV)
            oa_ref[0, :, hs] = (_rms_rows(o[:, hh * A_DV:(hh + 1) * A_DV], gn) * gate[:, hs]).astype(BF16)

    @pl.when(c_idx == pl.num_programs(2) - 1)
    def _():
        for hh in range(heads):
            st_ref[0, hh] = s_scr[hh // 2][:, (hh % 2) * A_DK:(hh % 2 + 1) * A_DK].T


def _hgrn_prompt(u3, lbf, oml, gn, cast_weights=(), cast_layer=None):
    bsz, length, _ = u3.shape
    chunk, heads = HGRN_CHUNK, HGRN_HEADS_PER_STEP
    cols = heads * A_DK
    assert length % chunk == 0 and A_HEADS % heads == 0 and heads % 2 == 0 and chunk == A_DV
    p2, masks = _hgrn_static(chunk)
    p2 = jnp.asarray(p2, BF16)
    masks = jnp.asarray(masks, BF16)
    n_chunks = length // chunk
    assert not cast_weights or A_HEADS == heads

    def col_spec(offset):
        base = offset // cols
        return pl.BlockSpec((1, chunk, cols), lambda b, h, c: (b, c, base + h))

    cast_in, cast_out, cast_shape, cast_spans, cast_bytes = [], [], [], [], 0
    for w in cast_weights:
        span = next(s for s in (1, 2, 4) if n_chunks % s == 0 and w.shape[1] % (bsz * n_chunks // s) == 0
                    and (w.shape[1] // (bsz * n_chunks // s)) % (2 * SUBLANES) == 0)
        rows, width = w.shape[1] // (bsz * n_chunks // span), w.shape[2]

        def block(b, h, c, span=span):
            return (b * n_chunks + c) // span

        cast_in.append(pl.BlockSpec((None, rows, width), lambda b, h, c, block=block: (cast_layer, block(b, h, c), 0)))
        cast_out.append(pl.BlockSpec((rows, width), lambda b, h, c, block=block: (block(b, h, c), 0)))
        cast_shape.append(jax.ShapeDtypeStruct(w.shape[1:], BF16))
        cast_spans.append(span)
        cast_bytes += 2 * rows * width * (4 + 2)

    nbytes = (2 * 5 * chunk * cols * 4 + 2 * (p2.size * 2 + masks.size * 4)
              + 6 * p2.shape[0] * cols * 4 + cast_bytes)
    outs = pl.pallas_call(
        functools.partial(_hgrn_prompt_kernel, chunk=chunk, heads=heads, cast_spans=tuple(cast_spans)),
        grid=(bsz, A_HEADS // heads, n_chunks),
        in_specs=[col_spec(Q_A), col_spec(F_A), col_spec(I_A), col_spec(G_A),
                  pl.BlockSpec((1, cols), lambda b, h, c: (0, h)),
                  pl.BlockSpec((1, cols), lambda b, h, c: (0, h)),
                  pl.BlockSpec((1, A_DV), lambda b, h, c: (0, 0)),
                  pl.BlockSpec(p2.shape, lambda b, h, c: (0, 0)),
                  pl.BlockSpec(masks.shape, lambda b, h, c: (0, 0, 0))] + cast_in,
        out_specs=[pl.BlockSpec((1, chunk, cols), lambda b, h, c: (b, c, h)),
                   pl.BlockSpec((1, heads, A_DK, A_DV), lambda b, h, c: (b, h, 0, 0))] + cast_out,
        out_shape=[jax.ShapeDtypeStruct((bsz, length, A_WIDTH), BF16),
                   jax.ShapeDtypeStruct((bsz, A_HEADS, A_DK, A_DV), F32)] + cast_shape,
        scratch_shapes=[pltpu.VMEM((heads // 2, A_DV, 2 * A_DK), F32)],
        compiler_params=_params(("parallel", "parallel", "arbitrary"), nbytes),
        name="hgrn_prompt",
    )(u3, u3, u3, u3, lbf, oml, gn.reshape(1, A_DV), p2, masks, *cast_weights)
    return outs[0], outs[1], outs[2:]


def _hgrn_sample_kernel(u_ref, lbf_ref, oml_ref, gn_ref, s_ref, oa_ref, so_ref, *, steps):
    for i in range(u_ref.shape[0]):
        _hgrn_sample_one(i, u_ref, lbf_ref, oml_ref, gn_ref, s_ref, oa_ref, so_ref, steps)


def _hgrn_sample_one(i, u_ref, lbf_ref, oml_ref, gn_ref, s_ref, oa_ref, so_ref, steps):
    u = u_ref[i]
    q = _silu(u[:, Q_A:Q_A + A_KEY_WIDTH])
    sz, snz = _sig_pair(u[:, F_A:F_A + A_KEY_WIDTH])
    oml = oml_ref[...]
    f = lbf_ref[...] + oml * sz
    ka = oml * snz
    v = u[:, I_A:I_A + A_WIDTH]
    gate = _silu(u[:, G_A:G_A + A_WIDTH])
    gn = gn_ref[...]
    rows = u.shape[0]
    row = lax.broadcasted_iota(jnp.int32, f.shape, 0)
    live = row < steps
    f = jnp.where(live, f, 1.0)
    ka = jnp.where(live, ka, 0.0)

    def down(x, d, fill):
        return x if d == 0 else jnp.where(row >= d, pltpu.roll(x, d, axis=0), fill)

    def up(x, d, fill):
        return jnp.where(row < rows - d, pltpu.roll(x, rows - d, axis=0), fill)

    decay = [jnp.ones_like(f)]
    for d in range(1, steps + 1):
        decay.append(decay[-1] * down(f, d - 1, 1.0))
    head = decay[steps]
    tail = jnp.ones_like(f)
    for d in range(1, steps):
        tail = tail * up(f, d, 1.0)
    q_in = (q * head).astype(BF16)
    k_tail = ka * tail
    total = head[steps - 1:steps, :]
    t_hi = total.astype(BF16).astype(F32)
    t_mid = (total - t_hi).astype(BF16).astype(F32)
    t_lo = (total - t_hi - t_mid).astype(BF16).astype(F32)
    t_rows = jnp.where(row == 0, t_hi, jnp.where(row == 1, t_mid, jnp.where(row == 2, t_lo, 0.0)))
    prods = [q * down(ka, d, 0.0) * decay[d] for d in range(steps)]
    v_sh = [down(v, d, 0.0) for d in range(steps)]

    zsq = jnp.zeros((A_DK, A_DV), BF16)
    zrows = jnp.zeros((rows, A_DV), F32)
    r2 = lax.broadcasted_iota(jnp.int32, (rows, 2 * A_DV), 0)
    l2 = lax.broadcasted_iota(jnp.int32, (rows, 2 * A_DV), 1)
    ones_right = jnp.where((r2 < 3) & (l2 >= A_DV), 1.0, 0.0)

    for p in range(A_HEADS // 2):
        ps = slice(2 * p * A_DK, (2 * p + 2) * A_DK)
        s0 = [s_ref[i, 2 * p + hh] for hh in range(2)]
        s_bd = jnp.concatenate([jnp.concatenate([s0[0].astype(BF16), zsq], axis=1),
                                jnp.concatenate([zsq, s0[1].astype(BF16)], axis=1)], axis=0)
        o_pair = _dot(q_in[:, ps], s_bd)
        for hh in range(2):
            head_i = 2 * p + hh
            sl = slice(head_i * A_DK, (head_i + 1) * A_DK)
            o = o_pair[:, hh * A_DV:(hh + 1) * A_DV]
            for d in range(steps):
                o = o + jnp.sum(prods[d][:, sl], axis=-1, keepdims=True) * v_sh[d][:, sl]
            lhs = jnp.concatenate([k_tail[:, sl], t_rows[:, sl]], axis=0).astype(BF16)
            rhs = jnp.concatenate([jnp.concatenate([v[:, sl], zrows], axis=1), ones_right],
                                  axis=0).astype(BF16)
            upd = lax.dot_general(lhs, rhs, (((0,), (0,)), ((), ())), preferred_element_type=F32)
            so_ref[i, head_i] = s0[hh] * upd[:, A_DV:] + upd[:, :A_DV]
            oa_ref[i, :, sl] = (_rms_rows(o, gn) * gate[:, sl]).astype(BF16)


def _hgrn_sample(u3, lbf, oml, gn, state, layer, steps):
    bsz, rows, _ = u3.shape
    width = G_A + A_WIDTH
    bt = HGRN_SAMPLE_SEQS
    assert bsz % bt == 0
    nbytes = bt * (2 * rows * width * 4 + 4 * A_HEADS * A_DK * A_DV * 4)
    return pl.pallas_call(
        functools.partial(_hgrn_sample_kernel, steps=steps),
        grid=(bsz // bt,),
        in_specs=[pl.BlockSpec((bt, rows, width), lambda b: (b, 0, 0)),
                  pl.BlockSpec((1, A_KEY_WIDTH), lambda b: (0, 0)),
                  pl.BlockSpec((1, A_KEY_WIDTH), lambda b: (0, 0)),
                  pl.BlockSpec((1, A_DV), lambda b: (0, 0)),
                  pl.BlockSpec((None, bt, A_HEADS, A_DK, A_DV), lambda b: (layer, b, 0, 0, 0))],
        out_specs=(pl.BlockSpec((bt, rows, A_WIDTH), lambda b: (b, 0, 0)),
                   pl.BlockSpec((bt, A_HEADS, A_DK, A_DV), lambda b: (b, 0, 0, 0))),
        out_shape=(jax.ShapeDtypeStruct((bsz, rows, A_WIDTH), BF16),
                   jax.ShapeDtypeStruct(state.shape[1:], F32)),
        compiler_params=_params(("parallel",), nbytes),
        name="hgrn_sample",
    )(u3, lbf, oml, gn.reshape(1, A_DV), state)


def _t5_bucket(dist):
    n = np.maximum(dist, 0)
    max_exact = N_BUCKETS // 2
    nf = np.maximum(n, 1).astype(np.float32)
    large = max_exact + (np.log(nf / max_exact) / math.log(MAX_DISTANCE / max_exact)
                         * (N_BUCKETS - max_exact)).astype(np.int32)
    large = np.minimum(large, N_BUCKETS - 1)
    return np.where(n < max_exact, n, large).astype(np.int32)


def _bias_kernel(tab_ref, bucket_ref, valid_ref, o_ref):
    rows = bucket_ref.shape[0]
    bucket = bucket_ref[...]
    hits = [bucket == b for b in range(N_BUCKETS)]
    for h in range(B_HEADS):
        acc = jnp.zeros(bucket.shape, F32)
        for b in range(N_BUCKETS):
            acc = jnp.where(hits[b], tab_ref[b, h], acc)
        acc = acc * LOG2E
        j, r, c = h // 4, (h % 4) // 2, h % 2
        for m in range(valid_ref.shape[0]):
            o_ref[m, j, r * rows:(r + 1) * rows, c * KEY_SLOTS:(c + 1) * KEY_SLOTS] = jnp.where(
                valid_ref[m] != 0, acc, NEG_LOGIT * LOG2E)


def _bias_layout(table, dist, valids):
    rows = dist.shape[0]
    return pl.pallas_call(
        _bias_kernel,
        in_specs=[pl.BlockSpec(memory_space=pltpu.SMEM),
                  pl.BlockSpec(memory_space=pltpu.VMEM),
                  pl.BlockSpec(memory_space=pltpu.VMEM)],
        out_shape=jax.ShapeDtypeStruct((valids.shape[0], B_KV_HEADS, 2 * rows, 2 * KEY_SLOTS), F32),
        name="bias_prep",
    )(table.astype(F32), jnp.asarray(_t5_bucket(dist), jnp.int32), jnp.asarray(valids, jnp.int32))


def _block_diag_ones():
    r = lax.broadcasted_iota(jnp.int32, (LANES, LANES), 0) // B_HD
    c = lax.broadcasted_iota(jnp.int32, (LANES, LANES), 1) // B_HD
    return (r == c).astype(BF16)


def _head_norm(x, g2, bd):
    hi, lo = _split_bf16(x * x)
    ss = _dot(hi, bd) + _dot(lo, bd)
    return x * lax.rsqrt(ss * (1.0 / B_HD) + RMS_EPS) * g2


def _place(tile, half, lo_mask):
    rolled = pltpu.roll(tile, B_HD, axis=1)
    zero = jnp.zeros_like(tile)
    if half == 0:
        return jnp.where(lo_mask, tile, zero), jnp.where(lo_mask, zero, rolled)
    return jnp.where(lo_mask, rolled, zero), jnp.where(lo_mask, zero, tile)


def _swa_core(problems, bias_at, sink_ref):
    rows = problems[0][0][0].shape[0]
    lo_mask = lax.broadcasted_iota(jnp.int32, problems[0][1][0].shape, 1) < B_HD
    out_lo = lax.broadcasted_iota(jnp.int32, (rows, LANES), 1) < B_HD
    units = [(p, j) for p in range(len(problems)) for j in range(B_KV_HEADS)]
    scores, values = {}, {}
    for p, j in units:
        qn_tiles, k_tiles, v_tiles, _ = problems[p]
        t, half = j // 2, j % 2
        k_lo, k_hi = _place(k_tiles[t], half, lo_mask)
        v_lo, v_hi = _place(v_tiles[t], half, lo_mask)
        kk = jnp.concatenate([k_lo, k_hi], axis=0).astype(BF16)
        values[p, j] = jnp.concatenate([v_lo, v_hi], axis=0).astype(BF16)
        qq = jnp.concatenate([qn_tiles[2 * j], qn_tiles[2 * j + 1]], axis=0).astype(BF16)
        scores[p, j] = _dot_nt(qq, kk) + bias_at(p, j)
    weights, scales = {}, {}
    for p, j in units:
        e_rows, inv = [], []
        for r in range(2):
            e_cols, inv_r = [], []
            for c in range(2):
                sk = sink_ref[4 * j + 2 * r + c] * LOG2E
                sb = scores[p, j][r * rows:(r + 1) * rows, c * KEY_SLOTS:(c + 1) * KEY_SLOTS]
                m = jnp.maximum(jnp.max(sb, axis=-1, keepdims=True), sk)
                e = jnp.exp2(sb - m)
                inv_r.append(1.0 / (jnp.sum(e, axis=-1, keepdims=True) + jnp.exp2(sk - m)))
                e_cols.append(e.astype(BF16))
            e_rows.append(jnp.concatenate(e_cols, axis=1))
            inv.append(jnp.where(out_lo, inv_r[0], inv_r[1]))
        weights[p, j] = jnp.concatenate(e_rows, axis=0)
        scales[p, j] = inv
    for p, j in units:
        o = _dot(weights[p, j], values[p, j])
        for r in range(2):
            problems[p][3](2 * j + r, (o[r * rows:(r + 1) * rows] * scales[p, j][r]).astype(BF16))


def _swa_prompt_kernel(sink_ref, q_ref, kc_ref, kp_ref, vc_ref, vp_ref, qg_ref, kg_ref, bias_ref,
                       ob_ref, ko_ref, vo_ref):
    m = pl.program_id(1)
    blocks = q_ref.shape[1] // WINDOW
    bd = _block_diag_ones()
    k_all = jnp.concatenate([kp_ref[0], kc_ref[0]], axis=0)
    v_all = jnp.concatenate([vp_ref[0], vc_ref[0]], axis=0)
    kg = kg_ref[...]
    k_norm = [_head_norm(k_all[:, t * LANES:(t + 1) * LANES], kg, bd) for t in range(2)]
    q = q_ref[0]
    qg = qg_ref[...]
    qn = [_head_norm(q[:, t * LANES:(t + 1) * LANES], qg, bd) for t in range(B_WIDTH // LANES)]
    first = jnp.minimum(m, 1)

    problems, biases = [], []
    for i in range(blocks):
        rows = slice(i * WINDOW, (i + 1) * WINDOW)
        keys = slice(i * WINDOW, i * WINDOW + KEY_SLOTS)

        def store(tile, val, rows=rows):
            ob_ref[0, rows, tile * LANES:(tile + 1) * LANES] = val

        problems.append(([qt[rows] for qt in qn], [kt[keys] for kt in k_norm],
                         [v_all[keys, t * LANES:(t + 1) * LANES] for t in range(2)], store))
        biases.append(first if i == 0 else 1)
    _swa_core(problems, lambda p, j: bias_ref[biases[p], j], sink_ref)

    @pl.when(m == pl.num_programs(1) - 1)
    def _():
        ko_ref[0] = jnp.concatenate([kt[blocks * WINDOW:] for kt in k_norm], axis=1)
        vo_ref[0] = vc_ref[0, (blocks - 1) * WINDOW:, :]


def _prompt_dist():
    return WINDOW + np.arange(WINDOW)[:, None] - np.arange(KEY_SLOTS)[None, :]


def _prompt_valid():
    j = np.arange(KEY_SLOTS)[None, :]
    dist = _prompt_dist()
    cur = (j >= WINDOW) & (dist >= 0)
    prev = (j < WINDOW) & (dist < WINDOW)
    return np.stack([cur, cur | prev]).astype(np.int32)


def _swa_prompt(u3, sinks, qg, kg, bias):
    bsz, length, _ = u3.shape
    blocks = SWA_PROMPT_BLOCKS
    rows = blocks * WINDOW
    assert length % rows == 0
    qb, kb, vb = Q_B // B_WIDTH, K_B // B_KV_WIDTH, V_B // B_KV_WIDTH
    assert Q_B % B_WIDTH == 0 and K_B % B_KV_WIDTH == 0 and V_B % B_KV_WIDTH == 0

    def kv_spec(col, prev):
        if prev:
            return pl.BlockSpec((1, WINDOW, B_KV_WIDTH), lambda b, m: (b, jnp.maximum(m * blocks - 1, 0), col))
        return pl.BlockSpec((1, rows, B_KV_WIDTH), lambda b, m: (b, m, col))

    g_spec = pl.BlockSpec((1, LANES), lambda b, m: (0, 0))
    nbytes = (2 * rows * (B_WIDTH + 4 * B_KV_WIDTH) * 4 + 2 * bias.size * 4
              + 2 * rows * B_WIDTH * 2 + blocks * 24 * WINDOW * 2 * KEY_SLOTS * 4)
    return pl.pallas_call(
        _swa_prompt_kernel,
        grid=(bsz, length // rows),
        in_specs=[pl.BlockSpec(memory_space=pltpu.SMEM),
                  pl.BlockSpec((1, rows, B_WIDTH), lambda b, m: (b, m, qb)),
                  kv_spec(kb, False), kv_spec(kb, True), kv_spec(vb, False), kv_spec(vb, True),
                  g_spec, g_spec,
                  pl.BlockSpec(bias.shape, lambda b, m: (0, 0, 0, 0))],
        out_specs=(pl.BlockSpec((1, rows, B_WIDTH), lambda b, m: (b, m, 0)),
                   pl.BlockSpec((1, WINDOW, B_KV_WIDTH), lambda b, n: (b, 0, 0)),
                   pl.BlockSpec((1, WINDOW, B_KV_WIDTH), lambda b, n: (b, 0, 0))),
        out_shape=(jax.ShapeDtypeStruct((bsz, length, B_WIDTH), BF16),
                   jax.ShapeDtypeStruct((bsz, WINDOW, B_KV_WIDTH), F32),
                   jax.ShapeDtypeStruct((bsz, WINDOW, B_KV_WIDTH), F32)),
        compiler_params=_params(("parallel", "arbitrary"), nbytes),
        name="swa_prompt",
    )(sinks, u3, u3, u3, u3, u3, qg, kg, bias)


def _swa_sample_kernel(sink_ref, q_ref, kv_ref, ck_ref, cv_ref, qg_ref, kg_ref, bias_ref,
                       ob_ref, ko_ref, vo_ref, *, steps):
    bd = _block_diag_ones()
    rows = q_ref.shape[1]
    kg = kg_ref[...]
    qg = qg_ref[...]
    pad = jnp.zeros((KEY_SLOTS - WINDOW - rows, LANES), F32)
    problems = []
    for i in range(q_ref.shape[0]):
        kv = kv_ref[i]
        k_new = [_head_norm(kv[:, t * LANES:(t + 1) * LANES], kg, bd) for t in range(2)]
        v_new = [kv[:, B_KV_WIDTH + t * LANES:B_KV_WIDTH + (t + 1) * LANES] for t in range(2)]
        ck, cv = ck_ref[i], cv_ref[i]
        k_tiles = [jnp.concatenate([ck[:, t * LANES:(t + 1) * LANES], k_new[t], pad], axis=0)
                   for t in range(2)]
        v_tiles = [jnp.concatenate([cv[:, t * LANES:(t + 1) * LANES], v_new[t], pad], axis=0)
                   for t in range(2)]
        q = q_ref[i]
        qn_tiles = [_head_norm(q[:, t * LANES:(t + 1) * LANES], qg, bd) for t in range(B_WIDTH // LANES)]

        def store(tile, val, i=i):
            ob_ref[i, :, tile * LANES:(tile + 1) * LANES] = val

        problems.append((qn_tiles, k_tiles, v_tiles, store))
        ko_ref[i, 0:WINDOW - steps, :] = ck_ref[i, steps:WINDOW, :]
        vo_ref[i, 0:WINDOW - steps, :] = cv_ref[i, steps:WINDOW, :]
        ko_ref[i, WINDOW - steps:WINDOW, :] = jnp.concatenate([kt[0:steps] for kt in k_new], axis=1)
        vo_ref[i, WINDOW - steps:WINDOW, :] = kv[0:steps, B_KV_WIDTH:]
    _swa_core(problems, lambda p, j: bias_ref[0, j], sink_ref)


def _sample_dist(rows):
    return WINDOW + np.arange(rows)[:, None] - np.arange(KEY_SLOTS)[None, :]


def _sample_valid(rows, steps):
    dist = _sample_dist(rows)
    j = np.arange(KEY_SLOTS)[None, :]
    return ((dist >= 0) & (dist < WINDOW) & (j < WINDOW + steps)).astype(np.int32)[None]


def _swa_sample(u3, cache_k, cache_v, layer, sinks, qg, kg, bias, steps):
    bsz, rows, _ = u3.shape
    w = cache_k.shape[2]
    bt = SWA_SAMPLE_SEQS
    assert w == WINDOW and bsz % bt == 0
    qb, kvb = Q_B // B_WIDTH, K_B // (2 * B_KV_WIDTH)
    assert K_B % (2 * B_KV_WIDTH) == 0
    c_in_spec = pl.BlockSpec((None, bt, w, B_KV_WIDTH), lambda b: (layer, b, 0, 0))
    c_spec = pl.BlockSpec((bt, w, B_KV_WIDTH), lambda b: (b, 0, 0))
    g_spec = pl.BlockSpec((1, LANES), lambda b: (0, 0))
    nbytes = bt * 8 * w * B_KV_WIDTH * 4 + 2 * bias.size * 4 + bt * 16 * KEY_SLOTS * LANES * 4
    return pl.pallas_call(
        functools.partial(_swa_sample_kernel, steps=steps),
        grid=(bsz // bt,),
        in_specs=[pl.BlockSpec(memory_space=pltpu.SMEM),
                  pl.BlockSpec((bt, rows, B_WIDTH), lambda b: (b, 0, qb)),
                  pl.BlockSpec((bt, rows, 2 * B_KV_WIDTH), lambda b: (b, 0, kvb)),
                  c_in_spec, c_in_spec, g_spec, g_spec,
                  pl.BlockSpec(bias.shape, lambda b: (0, 0, 0, 0))],
        out_specs=(pl.BlockSpec((bt, rows, B_WIDTH), lambda b: (b, 0, 0)), c_spec, c_spec),
        out_shape=(jax.ShapeDtypeStruct((bsz, rows, B_WIDTH), BF16),
                   jax.ShapeDtypeStruct((bsz, w, B_KV_WIDTH), F32),
                   jax.ShapeDtypeStruct((bsz, w, B_KV_WIDTH), F32)),
        compiler_params=_params(("parallel",), nbytes),
        name="swa_sample",
    )(sinks, u3, u3, cache_k, cache_v, qg, kg, bias)


def kernel(x_prompt, x_sample, cache_k, cache_v, state_hgrn, norm_mix, w_in, lower_bounds, hgrn_norm,
           q_norm, k_norm, attn_sinks, rel_bias_table, w_out, norm_ffn, w_gate, w_up, w_down):
    depth = w_in.shape[0]
    bp, seq, d = x_prompt.shape
    bd, ld, _ = x_sample.shape
    w = cache_k.shape[2]
    in_width = w_in.shape[2]

    lbf, oml = _lower_bounds(lower_bounds)
    sinks = attn_sinks.astype(F32)
    qg2 = jnp.tile(q_norm.astype(F32), (1, LANES // B_HD)) * (ATTN_SCALE * LOG2E)
    kg2 = jnp.tile(k_norm.astype(F32), (1, LANES // B_HD))
    ck = cache_k.astype(F32).reshape(depth, bd, w, B_KV_WIDTH)
    cv = cache_v.astype(F32).reshape(depth, bd, w, B_KV_WIDTH)

    xp = x_prompt.reshape(bp * seq, d)
    xs = jnp.pad(x_sample, ((0, 0), (0, SAMPLE_ROWS - ld), (0, 0))).reshape(bd * SAMPLE_ROWS, d)
    tm_s = bd * SAMPLE_ROWS
    bias_p = _bias_layout(rel_bias_table, _prompt_dist(), _prompt_valid())
    bias_s = _bias_layout(rel_bias_table, _sample_dist(SAMPLE_ROWS), _sample_valid(SAMPLE_ROWS, ld))
    state = state_hgrn.astype(F32)

    stacked = (w_in, w_out, w_gate, w_up, w_down)
    w_in_b = w_out_b = w_gate_b = w_up_b = w_down_b = None
    pk, pv, ps, sk, sv, ss = [], [], [], [], [], []
    for l in range(depth):
        lbf_l, oml_l = lbf[l:l + 1], oml[l:l + 1]
        if w_in_b is None:
            u, w_in_b = _norm_matmul(xs, norm_mix[l], w_in, l, tm_s, CAST_COLS)
        else:
            u = _norm_matmul(xs, norm_mix[l], w_in_b, None, tm_s, W_IN_TILE[1])
        u = u.reshape(bd, SAMPLE_ROWS, in_width)
        oa, st = _hgrn_sample(u, lbf_l, oml_l, hgrn_norm[l], state, l, ld)
        ob, kn, vn = _swa_sample(u, ck, cv, l, sinks[l], qg2[l:l + 1], kg2[l:l + 1], bias_s, ld)
        oa, ob = oa.reshape(tm_s, A_WIDTH), ob.reshape(tm_s, B_WIDTH)
        if w_out_b is None:
            xs, w_out_b = _outproj(xs, oa, ob, w_out, l, tm_s, CAST_COLS)
            xs, w_gate_b, w_up_b, w_down_b = _ffn(xs, norm_ffn[l], w_gate, w_up, w_down, l, tm_s, CAST_COLS)
        else:
            xs = _outproj(xs, oa, ob, w_out_b, None, tm_s, d)
            xs = _ffn(xs, norm_ffn[l], w_gate_b, w_up_b, w_down_b, None, tm_s, FFN_TILE[1])
        sk.append(kn)
        sv.append(vn)
        ss.append(st)
        u = _norm_matmul(xp, norm_mix[l], w_in_b, None, *W_IN_TILE).reshape(bp, seq, in_width)
        oa, st, next_b = _hgrn_prompt(u, lbf_l, oml_l, hgrn_norm[l],
                                      stacked if l + 1 < depth else (), l + 1)
        ob, kn, vn = _swa_prompt(u, sinks[l], qg2[l:l + 1], kg2[l:l + 1], bias_p)
        xp = _outproj(xp, oa.reshape(bp * seq, A_WIDTH), ob.reshape(bp * seq, B_WIDTH), w_out_b, None,
                      W_OUT_ROWS, d)
        xp = _ffn(xp, norm_ffn[l], w_gate_b, w_up_b, w_down_b, None, *FFN_TILE)
        pk.append(kn)
        pv.append(vn)
        ps.append(st)
        if next_b:
            w_in_b, w_out_b, w_gate_b, w_up_b, w_down_b = next_b

    kv_shape = lambda n: (depth, n, WINDOW, B_KV_HEADS, B_HD)
    return (xp.reshape(bp, seq, d),
            xs.reshape(bd, SAMPLE_ROWS, d)[:, :ld],
            jnp.stack(pk).reshape(kv_shape(bp)), jnp.stack(pv).reshape(kv_shape(bp)), jnp.stack(ps),
            jnp.stack(sk).reshape(kv_shape(bd)), jnp.stack(sv).reshape(kv_shape(bd)), jnp.stack(ss))
```
